```python
import math
import jax
import jax.numpy as jnp
from jax import lax
import numpy as np

D_MODEL = 1024
BATCH = 2
SEQ = 8192
DEPTH = 4
DEC_BATCH = 32
DEC_SEQ = 4
PAST_LEN = 8192
PAGE_SIZE = 128

HEAD_DIM = 64
CONV_CH = D_MODEL // 2
CONV_WIDTH = 31
WINDOWS = (128, 512, 2048)
DILATIONS = (1, 4, 16)
B_HPG = 4
B_HEADS = B_HPG * len(WINDOWS)
SW_BLOCK = 128
GLA_HEADS = 4
GLA_DK = D_MODEL // 16
GLA_DV = D_MODEL // 8
GLA_GATE_RANK = 16
GLA_TAU = 16.0
GLA_CHUNK = 64
DSA_HEADS = D_MODEL // 128
IDX_HEADS = 4
IDX_DIM = 64
DSA_TOPK_MAX = 256
DSA_QBLOCK = 128
REL_BUCKETS = 32
REL_MAX_DIST = 2048
REL_HEADS = B_HEADS + DSA_HEADS
FFN_HIDDEN = ((8 * D_MODEL + 3 * 256 - 1) // (3 * 256)) * 256
EPS = 1e-6

N_AB = (DEPTH + 1) // 2
N_CD = DEPTH // 2
AB_SPLITS = (CONV_CH, CONV_CH, B_HEADS * HEAD_DIM, B_HEADS * HEAD_DIM, B_HEADS * HEAD_DIM)
AB_IN = sum(AB_SPLITS)
AB_OUT = CONV_CH + B_HPG * HEAD_DIM
CD_SPLITS = (GLA_HEADS * GLA_DK, GLA_HEADS * GLA_DK, GLA_HEADS * GLA_DV, GLA_GATE_RANK, GLA_HEADS * GLA_DV,
             DSA_HEADS * HEAD_DIM, DSA_HEADS * HEAD_DIM, DSA_HEADS * HEAD_DIM,
             IDX_HEADS * IDX_DIM, IDX_DIM, IDX_HEADS)
CD_IN = sum(CD_SPLITS)
CD_OUT = GLA_HEADS * GLA_DV + DSA_HEADS * HEAD_DIM

kernel_name = 'hybrid_conv_dilated_gla_dsa_decoder_step'

F32 = jnp.float32


def split_cols(z, sizes):
    return jnp.split(z, np.cumsum(sizes)[:-1].tolist(), axis=-1)


def rmsnorm(x, g):
    xf = x.astype(F32)
    y = xf * lax.rsqrt(jnp.mean(xf * xf, axis=-1, keepdims=True) + EPS)
    return (y * g.astype(F32)).astype(x.dtype)


def rel_bucket(dist):
    n = jnp.maximum(dist, 0)
    max_exact = REL_BUCKETS // 2
    nf = jnp.maximum(n, max_exact).astype(F32)
    large = max_exact + (jnp.log(nf / max_exact) / math.log(REL_MAX_DIST / max_exact)
                         * (REL_BUCKETS - max_exact)).astype(jnp.int32)
    large = jnp.minimum(large, REL_BUCKETS - 1)
    return jnp.where(n < max_exact, n, large)


def conv_module(u, hist, conv_w, conv_b, ln_g, ln_b):
    c = u.shape[-1]
    uh = jnp.concatenate([hist.astype(u.dtype), u], axis=1)
    y = lax.conv_general_dilated(uh, conv_w[:, None, :].astype(u.dtype), window_strides=(1,), padding='VALID',
                                 dimension_numbers=('NWC', 'WIO', 'NWC'), feature_group_count=c)
    yf = y.astype(F32) + conv_b.astype(F32)
    mu = jnp.mean(yf, axis=-1, keepdims=True)
    var = jnp.mean(jnp.square(yf - mu), axis=-1, keepdims=True)
    yn = (yf - mu) * lax.rsqrt(var + EPS) * ln_g.astype(F32) + ln_b.astype(F32)
    return jax.nn.silu(yn).astype(u.dtype), uh[:, -(CONV_WIDTH - 1):]


def dilated_window_prompt(q, k, v, dil, window, tab):
    n_b, s_len, h, dh = q.shape
    n = s_len // dil
    reach = window // dil
    nb = -(-n // SW_BLOCK)
    n_pad = nb * SW_BLOCK

    def to_blocks(t):
        t = t.reshape(n_b, n, dil, h, dh).transpose(0, 2, 1, 3, 4)
        t = jnp.pad(t, ((0, 0), (0, 0), (0, n_pad - n), (0, 0), (0, 0)))
        return t.reshape(n_b, dil, nb, SW_BLOCK, h, dh)

    def with_prev(t):
        prev = jnp.pad(t, ((0, 0), (0, 0), (1, 0), (0, 0), (0, 0), (0, 0)))[:, :, :-1]
        return jnp.concatenate([prev, t], axis=3)

    qb = to_blocks(q)
    kb = with_prev(to_blocks(k))
    vb = with_prev(to_blocks(v))
    ql = jnp.arange(SW_BLOCK)[:, None]
    kl = jnp.arange(2 * SW_BLOCK)[None, :] - SW_BLOCK
    rel = ql - kl
    key_sub = (jnp.arange(nb) * SW_BLOCK)[:, None, None] + kl[None]
    mask = ((rel >= 0) & (rel <= reach))[None] & (key_sub >= 0)
    bias = jnp.take(tab, rel_bucket(rel * dil), axis=0).transpose(2, 0, 1).astype(F32)
    s = jnp.einsum('nrbqhd,nrbkhd->nrbhqk', qb, kb).astype(F32) * HEAD_DIM ** -0.5 + bias
    s = jnp.where(mask[:, None], s, -jnp.inf)
    lse = jax.nn.logsumexp(s, axis=-1)
    o = jnp.einsum('nrbhqk,nrbkhd->nrbqhd', jnp.exp(s - lse[..., None]), vb.astype(F32))
    o = o.reshape(n_b, dil, n_pad, h, dh)[:, :, :n].transpose(0, 2, 1, 3, 4).reshape(n_b, s_len, h, dh)
    lse = lse.transpose(0, 1, 2, 4, 3).reshape(n_b, dil, n_pad, h)[:, :, :n]
    lse = lse.transpose(0, 2, 1, 3).reshape(n_b, s_len, h)
    return o, lse


def dilated_window_step(q, k, v, buf, dil, window, tab):
    n_b, l_new, h, dh = q.shape
    wb = buf.shape[1]
    kv = jnp.concatenate([buf, jnp.stack([k, v], axis=2).astype(buf.dtype)], axis=1)
    m = jnp.arange(window // dil + 1)
    idx = wb + jnp.arange(l_new)[:, None] - dil * m[None, :]
    valid = idx >= 0
    sel = kv[:, jnp.maximum(idx, 0)]
    bias = jnp.take(tab, rel_bucket(dil * m), axis=0).T.astype(F32)
    s = jnp.einsum('nlhd,nlmhd->nhlm', q, sel[:, :, :, 0]).astype(F32) * HEAD_DIM ** -0.5 + bias[None, :, None, :]
    s = jnp.where(valid[None, None], s, -jnp.inf)
    lse = jax.nn.logsumexp(s, axis=-1)
    o = jnp.einsum('nhlm,nlmhd->nlhd', jnp.exp(s - lse[..., None]), sel[:, :, :, 1].astype(F32))
    keep = min(window, wb + l_new)
    return o, lse.transpose(0, 2, 1), kv[:, -keep:]


def ab_mix(h, w_in, conv_w, conv_b, ln_g, ln_b, qn, kn, w_out, tab, conv_hist, win_bufs):
    n_b, l_len, _ = h.shape
    a_lin, a_gate, q, k, v = split_cols(h @ w_in, AB_SPLITS)
    a_out, conv_st = conv_module(a_lin * jax.nn.sigmoid(a_gate), conv_hist, conv_w, conv_b, ln_g, ln_b)
    shp = (n_b, l_len, B_HEADS, HEAD_DIM)
    q = rmsnorm(q.reshape(shp), qn)
    k = rmsnorm(k.reshape(shp), kn)
    v = v.reshape(shp)
    outs, lses, win_st = [], [], []
    for g, (window, dil) in enumerate(zip(WINDOWS, DILATIONS)):
        hs = slice(g * B_HPG, (g + 1) * B_HPG)
        tg = tab[:, hs]
        if win_bufs is None:
            o, lse = dilated_window_prompt(q[:, :, hs], k[:, :, hs], v[:, :, hs], dil, window, tg)
            st = jnp.stack([k[:, :, hs], v[:, :, hs]], axis=2)[:, -min(window, l_len):]
        else:
            o, lse, st = dilated_window_step(q[:, :, hs], k[:, :, hs], v[:, :, hs], win_bufs[g], dil, window, tg)
        outs.append(o)
        lses.append(lse)
        win_st.append(st)
    wts = jax.nn.softmax(jnp.stack(lses), axis=0)
    b_out = jnp.einsum('gnlh,gnlhd->nlhd', wts, jnp.stack(outs))
    merged = jnp.concatenate([a_out, b_out.reshape(n_b, l_len, -1).astype(h.dtype)], axis=-1)
    return merged @ w_out, conv_st, win_st


def gla_chunked(q, k, v, log_a, s0):
    n_b, l_len, h, dk = q.shape
    c = min(GLA_CHUNK, l_len)
    nc = -(-l_len // c)
    lp = nc * c

    def chunks(t):
        t = jnp.pad(t.astype(F32), ((0, 0), (0, lp - l_len), (0, 0), (0, 0)))
        return t.reshape(n_b, nc, c, h, t.shape[-1]).transpose(1, 0, 3, 2, 4)

    causal = jnp.tril(jnp.ones((c, c), bool))

    def step(s, inp):
        qi, ki, vi, gi = inp
        b = jnp.cumsum(gi, axis=2)
        o_inter = jnp.einsum('nhck,nhkv->nhcv', qi * jnp.exp(b), s)
        diff = b[:, :, :, None, :] - b[:, :, None, :, :]
        decay = jnp.exp(jnp.where(causal[:, :, None], diff, -jnp.inf))
        a_mat = jnp.einsum('nhik,nhjk,nhijk->nhij', qi, ki, decay)
        o_intra = jnp.einsum('nhij,nhjv->nhiv', a_mat, vi)
        b_last = b[:, :, -1:, :]
        s_new = jnp.exp(b_last[:, :, 0, :, None]) * s + jnp.einsum('nhck,nhcv->nhkv', ki * jnp.exp(b_last - b), vi)
        return s_new, o_inter + o_intra

    s_fin, o = lax.scan(step, s0.astype(F32), (chunks(q), chunks(k), chunks(v), chunks(log_a)))
    o = o.transpose(1, 0, 3, 2, 4).reshape(n_b, lp, h, v.shape[-1])[:, :l_len]
    return o, s_fin


def indexer_scores(qi, wi, ki):
    dots = jnp.einsum('nqhd,nkd->nqhk', qi, ki).astype(F32) * IDX_DIM ** -0.5
    return jnp.einsum('nqh,nqhk->nqk', wi.astype(F32) * IDX_HEADS ** -0.5, jax.nn.relu(dots))


def select_keys(scores, q_pos, topk):
    key_pos = jnp.arange(scores.shape[-1])
    s = jnp.where(key_pos[None, None, :] <= q_pos[None, :, None], scores, -jnp.inf)
    vals, idx = lax.top_k(s, topk)
    return idx, jnp.isfinite(vals)


def attend_selected(q, k_sel, v_sel, q_pos, key_idx, valid, tab):
    s = jnp.einsum('nqhd,nqkhd->nhqk', q, k_sel).astype(F32) * HEAD_DIM ** -0.5
    bias = jnp.take(tab, rel_bucket(q_pos[None, :, None] - key_idx), axis=0)
    s = s + bias.transpose(0, 3, 1, 2).astype(F32)
    s = jnp.where(valid[:, None], s, -jnp.inf)
    p = jax.nn.softmax(s, axis=-1)
    return jnp.einsum('nhqk,nqkhd->nqhd', p, v_sel.astype(F32))


def dsa_prompt(q, kv, qi, ki, wi, tab):
    n_b, s_len, h, dh = q.shape
    topk = min(DSA_TOPK_MAX, s_len // 4)
    nblk = s_len // DSA_QBLOCK
    gather = jax.vmap(lambda rows, i: rows[i])

    def block(bi):
        start = bi * DSA_QBLOCK
        q_pos = start + jnp.arange(DSA_QBLOCK)
        qb = lax.dynamic_slice_in_dim(q, start, DSA_QBLOCK, axis=1)
        qib = lax.dynamic_slice_in_dim(qi, start, DSA_QBLOCK, axis=1)
        wib = lax.dynamic_slice_in_dim(wi, start, DSA_QBLOCK, axis=1)
        idx, valid = select_keys(indexer_scores(qib, wib, ki), q_pos, topk)
        sel = gather(kv, idx)
        return attend_selected(qb, sel[:, :, :, 0], sel[:, :, :, 1], q_pos, idx, valid, tab)

    o = lax.map(block, jnp.arange(nblk))
    return o.transpose(1, 0, 2, 3, 4).reshape(n_b, s_len, h, dh)


def dsa_step(q, kv_new, qi, ki, wi, kv_pool, kidx_pool, page_table, tab):
    n_b, l_new = q.shape[:2]
    past = page_table.shape[1] * PAGE_SIZE
    topk = min(DSA_TOPK_MAX, (past + l_new) // 4)
    gather = jax.vmap(lambda rows, i: rows[i])
    ki_past = kidx_pool[page_table].reshape(n_b, past, IDX_DIM)
    ki_all = jnp.concatenate([ki_past, ki.astype(ki_past.dtype)], axis=1)
    q_pos = past + jnp.arange(l_new)
    idx, valid = select_keys(indexer_scores(qi, wi, ki_all), q_pos, topk)
    p_idx = jnp.minimum(idx, past - 1)
    phys = gather(page_table, p_idx // PAGE_SIZE) * PAGE_SIZE + p_idx % PAGE_SIZE
    past_sel = kv_pool.reshape((-1,) + kv_pool.shape[2:])[phys]
    new_sel = gather(kv_new.astype(kv_pool.dtype), jnp.clip(idx - past, 0, l_new - 1))
    sel = jnp.where((idx < past)[..., None, None, None], past_sel, new_sel)
    return attend_selected(q, sel[:, :, :, 0], sel[:, :, :, 1], q_pos, idx, valid, tab)


def cd_mix(h, w_in, wa2, ba, gla_g, qn, kn, w_out, tab, s0, pools):
    n_b, l_len, _ = h.shape
    cq, ck, cv, c_lr, c_gate, dq, dk, dv, iq, ik, iw = split_cols(h @ w_in, CD_SPLITS)
    gshp = (n_b, l_len, GLA_HEADS, GLA_DK)
    log_a = jax.nn.log_sigmoid((c_lr @ wa2 + ba).astype(F32)) / GLA_TAU
    o_c, s_c = gla_chunked(cq.reshape(gshp) * GLA_DK ** -0.5, ck.reshape(gshp),
                           cv.reshape(n_b, l_len, GLA_HEADS, GLA_DV), log_a.reshape(gshp), s0)
    o_c = rmsnorm(o_c, gla_g) * jax.nn.silu(c_gate.astype(F32).reshape(n_b, l_len, GLA_HEADS, GLA_DV))
    dshp = (n_b, l_len, DSA_HEADS, HEAD_DIM)
    dq = rmsnorm(dq.reshape(dshp), qn)
    dk = rmsnorm(dk.reshape(dshp), kn)
    kv_rows = jnp.stack([dk, dv.reshape(dshp)], axis=2)
    iq = iq.reshape(n_b, l_len, IDX_HEADS, IDX_DIM)
    if pools is None:
        o_d = dsa_prompt(dq, kv_rows, iq, ik, iw, tab)
        n_pg = l_len // PAGE_SIZE
        kv_st = kv_rows.reshape(n_b, n_pg, PAGE_SIZE, 2, DSA_HEADS, HEAD_DIM)
        ki_st = ik.reshape(n_b, n_pg, PAGE_SIZE, IDX_DIM)
    else:
        kv_pool, kidx_pool, page_table = pools
        o_d = dsa_step(dq, kv_rows, iq, ik, iw, kv_pool, kidx_pool, page_table, tab)
        kv_st, ki_st = kv_rows, ik
    merged = jnp.concatenate([o_c.reshape(n_b, l_len, -1).astype(h.dtype),
                              o_d.reshape(n_b, l_len, -1).astype(h.dtype)], axis=-1)
    return merged @ w_out, s_c.astype(h.dtype), kv_st, ki_st


def trunk(x, is_step, conv_state, win_states, gla_state, dsa_kv, dsa_kidx, page_table,
          norm_mix, norm_ffn, w_in_ab, conv_w, conv_b, conv_ln_g, conv_ln_b, qn_ab, kn_ab, w_out_ab,
          w_in_cd, gla_wa2, gla_ba, gla_norm, qn_cd, kn_cd, w_out_cd, rel_bias,
          w_ffn_gate, w_ffn_up, w_ffn_down):
    n_b = x.shape[0]
    conv_new, gla_new, kv_new, kidx_new = [], [], [], []
    win_new = [[] for _ in WINDOWS]
    for layer in range(DEPTH):
        i = layer // 2
        h = rmsnorm(x, norm_mix[layer])
        if layer % 2 == 0:
            hist = conv_state[i] if is_step else jnp.zeros((n_b, CONV_WIDTH - 1, CONV_CH), x.dtype)
            bufs = [ws[i] for ws in win_states] if is_step else None
            mix, c_st, w_st = ab_mix(h, w_in_ab[i], conv_w[i], conv_b[i], conv_ln_g[i], conv_ln_b[i],
                                     qn_ab[i], kn_ab[i], w_out_ab[i], rel_bias[:, :B_HEADS], hist, bufs)
            conv_new.append(c_st)
            for g in range(len(WINDOWS)):
                win_new[g].append(w_st[g])
        else:
            s0 = gla_state[i] if is_step else jnp.zeros((n_b, GLA_HEADS, GLA_DK, GLA_DV), F32)
            pools = (dsa_kv[i], dsa_kidx[i], page_table) if is_step else None
            mix, g_st, kv_st, ki_st = cd_mix(h, w_in_cd[i], gla_wa2[i], gla_ba[i], gla_norm[i], qn_cd[i], kn_cd[i],
                                             w_out_cd[i], rel_bias[:, B_HEADS:], s0, pools)
            gla_new.append(g_st)
            kv_new.append(kv_st)
            kidx_new.append(ki_st)
        x = x + mix
        hf = rmsnorm(x, norm_ffn[layer])
        x = x + (jax.nn.silu(hf @ w_ffn_gate[layer]) * (hf @ w_ffn_up[layer])) @ w_ffn_down[layer]
    states = (jnp.stack(conv_new), jnp.stack(win_new[0]), jnp.stack(win_new[1]), jnp.stack(win_new[2]),
              jnp.stack(gla_new), jnp.stack(kv_new), jnp.stack(kidx_new))
    return x, states


def setup_inputs(seed: int = 0) -> dict:
    key = jax.random.key(seed)
    ks = iter(jax.random.split(key, 48))

    def nrm(shape, scale=1.0):
        return scale * jax.random.normal(next(ks), shape, F32)

    def gain(shape):
        return 1.0 + nrm(shape, 0.02)

    n_pages = PAST_LEN // PAGE_SIZE
    n_pool = (5 * DEC_BATCH * n_pages + 3) // 4
    page_table = jax.random.permutation(next(ks), n_pool)[: DEC_BATCH * n_pages]
    page_table = page_table.reshape(DEC_BATCH, n_pages).astype(jnp.int32)
    return {
        'x_prompt': nrm((BATCH, SEQ, D_MODEL)),
        'x_sample': nrm((DEC_BATCH, DEC_SEQ, D_MODEL)),
        'state_conv': nrm((N_AB, DEC_BATCH, CONV_WIDTH - 1, CONV_CH), 0.5),
        'cache_win128': nrm((N_AB, DEC_BATCH, min(WINDOWS[0], PAST_LEN), 2, B_HPG, HEAD_DIM)),
        'cache_win512': nrm((N_AB, DEC_BATCH, min(WINDOWS[1], PAST_LEN), 2, B_HPG, HEAD_DIM)),
        'cache_win2048': nrm((N_AB, DEC_BATCH, min(WINDOWS[2], PAST_LEN), 2, B_HPG, HEAD_DIM)),
        'state_gla': nrm((N_CD, DEC_BATCH, GLA_HEADS, GLA_DK, GLA_DV), 0.3),
        'cache_dsa_kv': nrm((N_CD, n_pool, PAGE_SIZE, 2, DSA_HEADS, HEAD_DIM)),
        'cache_dsa_kidx': nrm((N_CD, n_pool, PAGE_SIZE, IDX_DIM)),
        'page_table': page_table,
        'norm_mix': gain((DEPTH, D_MODEL)),
        'norm_ffn': gain((DEPTH, D_MODEL)),
        'w_in_ab': nrm((N_AB, D_MODEL, AB_IN), D_MODEL ** -0.5),
        'conv_w': nrm((N_AB, CONV_WIDTH, CONV_CH), CONV_WIDTH ** -0.5),
        'conv_b': nrm((N_AB, CONV_CH), 0.02),
        'conv_ln_g': gain((N_AB, CONV_CH)),
        'conv_ln_b': nrm((N_AB, CONV_CH), 0.02),
        'qn_ab': gain((N_AB, HEAD_DIM)),
        'kn_ab': gain((N_AB, HEAD_DIM)),
        'w_out_ab': nrm((N_AB, AB_OUT, D_MODEL), AB_OUT ** -0.5),
        'w_in_cd': nrm((N_CD, D_MODEL, CD_IN), D_MODEL ** -0.5),
        'gla_wa2': nrm((N_CD, GLA_GATE_RANK, GLA_HEADS * GLA_DK), GLA_GATE_RANK ** -0.5),
        'gla_ba': nrm((N_CD, GLA_HEADS * GLA_DK), 0.02),
        'gla_norm': gain((N_CD, GLA_DV)),
        'qn_cd': gain((N_CD, HEAD_DIM)),
        'kn_cd': gain((N_CD, HEAD_DIM)),
        'w_out_cd': nrm((N_CD, CD_OUT, D_MODEL), CD_OUT ** -0.5),
        'rel_bias': nrm((REL_BUCKETS, REL_HEADS), 0.5),
        'w_ffn_gate': nrm((DEPTH, D_MODEL, FFN_HIDDEN), D_MODEL ** -0.5),
        'w_ffn_up': nrm((DEPTH, D_MODEL, FFN_HIDDEN), D_MODEL ** -0.5),
        'w_ffn_down': nrm((DEPTH, FFN_HIDDEN, D_MODEL), FFN_HIDDEN ** -0.5),
    }


def reference(x_prompt, x_sample, state_conv, cache_win128, cache_win512, cache_win2048, state_gla,
              cache_dsa_kv, cache_dsa_kidx, page_table,
              norm_mix, norm_ffn, w_in_ab, conv_w, conv_b, conv_ln_g, conv_ln_b, qn_ab, kn_ab, w_out_ab,
              w_in_cd, gla_wa2, gla_ba, gla_norm, qn_cd, kn_cd, w_out_cd, rel_bias,
              w_ffn_gate, w_ffn_up, w_ffn_down):
    weights = (norm_mix, norm_ffn, w_in_ab, conv_w, conv_b, conv_ln_g, conv_ln_b, qn_ab, kn_ab, w_out_ab,
               w_in_cd, gla_wa2, gla_ba, gla_norm, qn_cd, kn_cd, w_out_cd, rel_bias,
               w_ffn_gate, w_ffn_up, w_ffn_down)
    y_prompt, p_states = trunk(x_prompt, False, None, None, None, None, None, None, *weights)
    y_sample, s_states = trunk(x_sample, True, state_conv, (cache_win128, cache_win512, cache_win2048),
                               state_gla, cache_dsa_kv, cache_dsa_kidx, page_table, *weights)
    conv_p, win128_p, win512_p, win2048_p, gla_p, dsa_kv_p, dsa_kidx_p = p_states
    conv_s, win128_s, win512_s, win2048_s, gla_s, dsa_kv_s, dsa_kidx_s = s_states
    return (y_prompt, y_sample, conv_p, conv_s, win128_p, win128_s, win512_p, win512_s, win2048_p, win2048_s,
            gla_p, gla_s, dsa_kv_p, dsa_kv_s, dsa_kidx_p, dsa_kidx_s)
```

```python
import functools
import math

import numpy as np
import jax
import jax.numpy as jnp
from jax import lax
from jax.experimental import pallas as pl
from jax.experimental.pallas import tpu as pltpu

F32 = jnp.float32
BF16 = jnp.bfloat16
I32 = jnp.int32

EPS = 1e-6
NEG = -1e30
INT_MIN = -(2 ** 31)

V7X_VMEM_BYTES = 64 * 1024 * 1024
VMEM_LIMIT = V7X_VMEM_BYTES - 12 * 1024 * 1024
LANES = 128

HEAD_DIM = 64
CONV_WIDTH = 31
WINDOWS = (128, 512, 2048)
DILATIONS = (1, 4, 16)
B_HPG = 4
B_HEADS = B_HPG * len(WINDOWS)
SW_BLOCK = 128
GLA_HEADS = 4
GLA_DK = 64
GLA_DV = 128
GLA_GATE_RANK = 16
GLA_TAU = 16.0
DSA_HEADS = 8
IDX_HEADS = 4
IDX_DIM = 64
DSA_TOPK_MAX = 256
PAGE_SIZE = 128
REL_BUCKETS = 32
REL_MAX_DIST = 2048

GLA_CHUNK = 64
DSA_QB = 128
DSA_CK = 512
HIST_PAD = 32


def _cparams(*sem):
    return pltpu.CompilerParams(dimension_semantics=sem, vmem_limit_bytes=VMEM_LIMIT)


def _nt(a, b):
    return lax.dot_general(a, b, (((1,), (1,)), ((), ())), preferred_element_type=F32)


def _dot(a, b):
    return jnp.dot(a, b, preferred_element_type=F32)


def _split2(x):
    hi = x.astype(BF16)
    lo = (x - hi.astype(F32)).astype(BF16)
    return hi, lo


def _split3(x):
    hi = x.astype(BF16)
    r = x - hi.astype(F32)
    mid = r.astype(BF16)
    lo = (r - mid.astype(F32)).astype(BF16)
    return hi, mid, lo


def _rel_bucket_np(dist):
    n = np.maximum(np.asarray(dist, np.int64), 0)
    max_exact = REL_BUCKETS // 2
    nf = np.maximum(n, max_exact).astype(np.float32)
    large = max_exact + (np.log(nf / np.float32(max_exact)) / np.float32(math.log(REL_MAX_DIST / max_exact))
                         * np.float32(REL_BUCKETS - max_exact)).astype(np.int32)
    large = np.minimum(large, REL_BUCKETS - 1)
    return np.where(n < max_exact, n, large).astype(np.int32)


def _norm_matmul_kernel(x_ref, g_ref, w_ref, o_ref):
    x = x_ref[...]
    y = x * lax.rsqrt(jnp.mean(x * x, axis=-1, keepdims=True) + EPS) * g_ref[...]
    o_ref[...] = _dot(y.astype(BF16), w_ref[...])


def norm_matmul(x, g, w):
    t, d = x.shape
    n = w.shape[1]
    tm = min(t, 256)
    return pl.pallas_call(
        _norm_matmul_kernel,
        grid=(t // tm,),
        in_specs=[pl.BlockSpec((tm, d), lambda i: (i, 0)),
                  pl.BlockSpec((1, d), lambda i: (0, 0)),
                  pl.BlockSpec((d, n), lambda i: (0, 0))],
        out_specs=pl.BlockSpec((tm, n), lambda i: (i, 0)),
        out_shape=jax.ShapeDtypeStruct((t, n), F32),
        compiler_params=_cparams("parallel"),
        name="norm_matmul",
    )(x, g.reshape(1, d), w)


def _mix_ffn_kernel(x_ref, m1_ref, m2_ref, wo1_ref, wo2_ref, g_ref, wg_ref, wu_ref, wd_ref, o_ref,
                    x1_ref, hf_ref, acc_ref):
    j = pl.program_id(1)

    @pl.when(j == 0)
    def _():
        x1 = x_ref[...] + _dot(m1_ref[...], wo1_ref[...]) + _dot(m2_ref[...], wo2_ref[...])
        x1_ref[...] = x1
        hf = x1 * lax.rsqrt(jnp.mean(x1 * x1, axis=-1, keepdims=True) + EPS) * g_ref[...]
        hf_ref[...] = hf.astype(BF16)
        acc_ref[...] = jnp.zeros_like(acc_ref)

    hf = hf_ref[...]
    a = _dot(hf, wg_ref[...])
    u = _dot(hf, wu_ref[...])
    act = (a * jax.nn.sigmoid(a) * u).astype(BF16)
    acc_ref[...] += _dot(act, wd_ref[...])

    @pl.when(j == pl.num_programs(1) - 1)
    def _():
        o_ref[...] = x1_ref[...] + acc_ref[...]


def mix_ffn(x, m1, m2, wo1, wo2, g, wg, wu, wd):
    t, d = x.shape
    hid = wg.shape[1]
    tm = min(t, 512)
    th = 256
    d1, d2 = m1.shape[1], m2.shape[1]
    return pl.pallas_call(
        _mix_ffn_kernel,
        grid=(t // tm, hid // th),
        in_specs=[pl.BlockSpec((tm, d), lambda i, j: (i, 0)),
                  pl.BlockSpec((tm, d1), lambda i, j: (i, 0)),
                  pl.BlockSpec((tm, d2), lambda i, j: (i, 0)),
                  pl.BlockSpec((d1, d), lambda i, j: (0, 0)),
                  pl.BlockSpec((d2, d), lambda i, j: (0, 0)),
                  pl.BlockSpec((1, d), lambda i, j: (0, 0)),
                  pl.BlockSpec((d, th), lambda i, j: (0, j)),
                  pl.BlockSpec((d, th), lambda i, j: (0, j)),
                  pl.BlockSpec((th, d), lambda i, j: (j, 0))],
        out_specs=pl.BlockSpec((tm, d), lambda i, j: (i, 0)),
        out_shape=jax.ShapeDtypeStruct((t, d), F32),
        scratch_shapes=[pltpu.VMEM((tm, d), F32), pltpu.VMEM((tm, d), BF16), pltpu.VMEM((tm, d), F32)],
        compiler_params=_cparams("parallel", "arbitrary"),
        name="mix_ffn",
    )(x, m1, m2, wo1, wo2, g.reshape(1, d), wg, wu, wd)


def _conv_kernel(z_ref, hist_ref, w_ref, b_ref, lg_ref, lb_ref, o_ref, tail_ref, uh_ref, *, ts, c):
    t = pl.program_id(1)

    @pl.when(t == 0)
    def _():
        uh_ref[0:HIST_PAD, :] = hist_ref[0]

    z = z_ref[0]
    u = z[:, 0:c] * jax.nn.sigmoid(z[:, c:2 * c])
    uh_ref[HIST_PAD:HIST_PAD + ts, :] = u
    acc = jnp.zeros((ts, c), F32) + b_ref[...]
    off = HIST_PAD - (CONV_WIDTH - 1)
    for j in range(CONV_WIDTH):
        acc = acc + w_ref[j:j + 1, :] * uh_ref[off + j:off + j + ts, :]
    mu = jnp.mean(acc, axis=-1, keepdims=True)
    var = jnp.mean(jnp.square(acc - mu), axis=-1, keepdims=True)
    yn = (acc - mu) * lax.rsqrt(var + EPS) * lg_ref[...] + lb_ref[...]
    o_ref[0] = (yn * jax.nn.sigmoid(yn)).astype(BF16)
    tail = uh_ref[ts:ts + HIST_PAD, :]
    uh_ref[0:HIST_PAD, :] = tail
    tail_ref[0] = tail


def conv_module(z3, hist, conv_w, conv_b, ln_g, ln_b):
    nb, l, _ = z3.shape
    c = conv_w.shape[1]
    ts = min(l, 512)
    hist_p = jnp.pad(hist, ((0, 0), (HIST_PAD - (CONV_WIDTH - 1), 0), (0, 0)))
    w_p = jnp.pad(conv_w, ((0, HIST_PAD - CONV_WIDTH), (0, 0)))
    a_out, tail = pl.pallas_call(
        functools.partial(_conv_kernel, ts=ts, c=c),
        grid=(nb, l // ts),
        in_specs=[pl.BlockSpec((1, ts, 2 * c), lambda n, t: (n, t, 0)),
                  pl.BlockSpec((1, HIST_PAD, c), lambda n, t: (n, 0, 0)),
                  pl.BlockSpec((HIST_PAD, c), lambda n, t: (0, 0)),
                  pl.BlockSpec((1, c), lambda n, t: (0, 0)),
                  pl.BlockSpec((1, c), lambda n, t: (0, 0)),
                  pl.BlockSpec((1, c), lambda n, t: (0, 0))],
        out_specs=[pl.BlockSpec((1, ts, c), lambda n, t: (n, t, 0)),
                   pl.BlockSpec((1, HIST_PAD, c), lambda n, t: (n, 0, 0))],
        out_shape=[jax.ShapeDtypeStruct((nb, l, c), BF16),
                   jax.ShapeDtypeStruct((nb, HIST_PAD, c), F32)],
        scratch_shapes=[pltpu.VMEM((HIST_PAD + ts, c), F32)],
        compiler_params=_cparams("parallel", "arbitrary"),
        name="conv_module",
    )(z3, hist_p, w_p, conv_b.reshape(1, c), ln_g.reshape(1, c), ln_b.reshape(1, c))
    return a_out, tail[:, HIST_PAD - (CONV_WIDTH - 1):]


def _seg_mean_sq(x, seg_ref):
    hi, lo = _split2(x * x)
    seg = seg_ref[...]
    return (_dot(hi, seg) + _dot(lo, seg)) * (1.0 / HEAD_DIM)


def _head_norm(x, g, seg_ref):
    return x * lax.rsqrt(_seg_mean_sq(x, seg_ref) + EPS) * g


def _seg_matrix(width):
    idx = np.arange(width) // HEAD_DIM
    return jnp.asarray((idx[:, None] == idx[None, :]).astype(np.float32), BF16)


def _ab_qkv_kernel(q_ref, k_ref, v_ref, qg_ref, kg_ref, seg_ref, qo_ref, kvo_ref, *, w):
    qo_ref[...] = (_head_norm(q_ref[...], qg_ref[...], seg_ref) * HEAD_DIM ** -0.5).astype(BF16)
    kvo_ref[0, :, 0:w] = _head_norm(k_ref[...], kg_ref[...], seg_ref)
    kvo_ref[0, :, w:2 * w] = v_ref[...]


def ab_qkv(z, qn, kn, col0):
    t = z.shape[0]
    w = B_HPG * HEAD_DIM
    ng = len(WINDOWS)
    tm = min(t, 512)
    cb = col0 // w
    return pl.pallas_call(
        functools.partial(_ab_qkv_kernel, w=w),
        grid=(t // tm, ng),
        in_specs=[pl.BlockSpec((tm, w), lambda i, g: (i, cb + g)),
                  pl.BlockSpec((tm, w), lambda i, g: (i, cb + ng + g)),
                  pl.BlockSpec((tm, w), lambda i, g: (i, cb + 2 * ng + g)),
                  pl.BlockSpec((1, w), lambda i, g: (0, 0)),
                  pl.BlockSpec((1, w), lambda i, g: (0, 0)),
                  pl.BlockSpec((w, w), lambda i, g: (0, 0))],
        out_specs=[pl.BlockSpec((tm, w), lambda i, g: (i, g)),
                   pl.BlockSpec((1, tm, 2 * w), lambda i, g: (g, i, 0))],
        out_shape=[jax.ShapeDtypeStruct((t, ng * w), BF16),
                   jax.ShapeDtypeStruct((ng, t, 2 * w), F32)],
        compiler_params=_cparams("parallel", "parallel"),
        name="ab_qkv",
    )(z, z, z, jnp.tile(qn, B_HPG).reshape(1, w), jnp.tile(kn, B_HPG).reshape(1, w), _seg_matrix(w))


def _win_prompt_kernel(q_ref, kvp_ref, kvc_ref, bias_ref, o_ref, lse_ref, *, w):
    blk = pl.program_id(2)
    q = q_ref[0]
    kvp = kvp_ref[0, 0]
    kvc = kvc_ref[0, 0]
    col = lax.broadcasted_iota(I32, (SW_BLOCK, 2 * SW_BLOCK), 1)
    first = jnp.logical_and(blk == 0, col < SW_BLOCK)
    for h in range(B_HPG):
        hs = slice(h * HEAD_DIM, (h + 1) * HEAD_DIM)
        vs = slice(w + h * HEAD_DIM, w + (h + 1) * HEAD_DIM)
        k2 = jnp.concatenate([kvp[:, hs], kvc[:, hs]], axis=0).astype(BF16)
        v2 = jnp.concatenate([kvp[:, vs], kvc[:, vs]], axis=0).astype(BF16)
        s = _nt(q[:, hs], k2) + bias_ref[h]
        s = jnp.where(first, NEG, s)
        m = jnp.max(s, axis=-1, keepdims=True)
        p = jnp.exp(s - m)
        l = jnp.sum(p, axis=-1, keepdims=True)
        o_ref[0, :, hs] = _dot(p.astype(BF16), v2) / l
        lse_ref[0, :, hs] = jnp.broadcast_to(m + jnp.log(l), (SW_BLOCK, HEAD_DIM))


def _win_prompt_bias(tab_g, dil, reach):
    ql = np.arange(SW_BLOCK)[:, None]
    kl = np.arange(2 * SW_BLOCK)[None, :] - SW_BLOCK
    rel = ql - kl
    ok = (rel >= 0) & (rel <= reach)
    bias = jnp.take(tab_g, jnp.asarray(_rel_bucket_np(rel * dil)), axis=0)
    return jnp.where(jnp.asarray(ok)[None], bias.transpose(2, 0, 1), NEG).astype(F32)


def win_prompt(q, kv, g, tab_g):
    b, s, _ = q.shape
    w = B_HPG * HEAD_DIM
    ng = len(WINDOWS)
    dil = DILATIONS[g]
    n = s // dil
    nb = n // SW_BLOCK
    assert nb * SW_BLOCK * dil == s
    qv = q.reshape(b, n, dil * ng * w)
    kvv = kv.reshape(ng, b, n, dil * 2 * w)
    bias = _win_prompt_bias(tab_g, dil, WINDOWS[g] // dil)
    o, lse = pl.pallas_call(
        functools.partial(_win_prompt_kernel, w=w),
        grid=(b, dil, nb),
        in_specs=[pl.BlockSpec((1, SW_BLOCK, w), lambda n_, r, k: (n_, k, ng * r + g)),
                  pl.BlockSpec((1, 1, SW_BLOCK, 2 * w), lambda n_, r, k: (g, n_, jnp.maximum(k - 1, 0), r)),
                  pl.BlockSpec((1, 1, SW_BLOCK, 2 * w), lambda n_, r, k: (g, n_, k, r)),
                  pl.BlockSpec((B_HPG, SW_BLOCK, 2 * SW_BLOCK), lambda n_, r, k: (0, 0, 0))],
        out_specs=[pl.BlockSpec((1, SW_BLOCK, w), lambda n_, r, k: (n_, k, r)),
                   pl.BlockSpec((1, SW_BLOCK, w), lambda n_, r, k: (n_, k, r))],
        out_shape=[jax.ShapeDtypeStruct((b, n, dil * w), F32),
                   jax.ShapeDtypeStruct((b, n, dil * w), F32)],
        compiler_params=_cparams("parallel", "parallel", "arbitrary"),
        name=f"win_prompt_g{g}",
    )(qv, kvv, kvv, bias)
    return o.reshape(b, s, w), lse.reshape(b, s, w)


def _win_step_kernel(q_ref, kvn_ref, buf_ref, tb_ref, tn_ref, o_ref, lse_ref, st_ref, *, w, wb, l):
    q = q_ref[0]
    kvn = kvn_ref[0, 0]
    buf = buf_ref[0]
    for h in range(B_HPG):
        hs = slice(h * HEAD_DIM, (h + 1) * HEAD_DIM)
        vs = slice(w + h * HEAD_DIM, w + (h + 1) * HEAD_DIM)
        s1 = _nt(q[:, hs], buf[:, hs].astype(BF16)) + tb_ref[h]
        s2 = _nt(q[:, hs], kvn[:, hs].astype(BF16)) + tn_ref[h]
        m = jnp.maximum(jnp.max(s1, axis=-1, keepdims=True), jnp.max(s2, axis=-1, keepdims=True))
        p1 = jnp.exp(s1 - m)
        p2 = jnp.exp(s2 - m)
        den = jnp.sum(p1, axis=-1, keepdims=True) + jnp.sum(p2, axis=-1, keepdims=True)
        num = _dot(p1.astype(BF16), buf[:, vs].astype(BF16)) + _dot(p2.astype(BF16), kvn[:, vs].astype(BF16))
        o_ref[0, :, hs] = num / den
        lse_ref[0, :, hs] = jnp.broadcast_to(m + jnp.log(den), (l, HEAD_DIM))
    st_ref[0, 0:wb - l, :] = buf[l:wb, :]
    st_ref[0, wb - l:wb, :] = kvn


def _win_step_bias(tab_g, dil, window, wb, l):
    lq = np.arange(l)[:, None]
    dist_b = wb + lq - np.arange(wb)[None, :]
    ok_b = (dist_b % dil == 0) & (dist_b // dil <= window // dil)
    dist_n = lq - np.arange(l)[None, :]
    ok_n = (dist_n >= 0) & (dist_n % dil == 0) & (dist_n // dil <= window // dil)

    def table(dist, ok):
        bias = jnp.take(tab_g, jnp.asarray(_rel_bucket_np(dist)), axis=0)
        return jnp.where(jnp.asarray(ok)[None], bias.transpose(2, 0, 1), NEG).astype(F32)

    return table(dist_b, ok_b), table(dist_n, ok_n)


def win_step(q3, kv_new, buf, g, tab_g):
    nb, l, _ = q3.shape
    w = B_HPG * HEAD_DIM
    wb = buf.shape[1]
    window, dil = WINDOWS[g], DILATIONS[g]
    assert wb == window, "the step kernel keeps a full window of rows"
    tb, tn = _win_step_bias(tab_g, dil, window, wb, l)
    o, lse, st = pl.pallas_call(
        functools.partial(_win_step_kernel, w=w, wb=wb, l=l),
        grid=(nb,),
        in_specs=[pl.BlockSpec((1, l, w), lambda n: (n, 0, g)),
                  pl.BlockSpec((1, 1, l, 2 * w), lambda n: (g, n, 0, 0)),
                  pl.BlockSpec((1, wb, 2 * w), lambda n: (n, 0, 0)),
                  pl.BlockSpec((B_HPG, l, wb), lambda n: (0, 0, 0)),
                  pl.BlockSpec((B_HPG, l, l), lambda n: (0, 0, 0))],
        out_specs=[pl.BlockSpec((1, l, w), lambda n: (n, 0, 0)),
                   pl.BlockSpec((1, l, w), lambda n: (n, 0, 0)),
                   pl.BlockSpec((1, wb, 2 * w), lambda n: (n, 0, 0))],
        out_shape=[jax.ShapeDtypeStruct((nb, l, w), F32),
                   jax.ShapeDtypeStruct((nb, l, w), F32),
                   jax.ShapeDtypeStruct((nb, wb, 2 * w), F32)],
        compiler_params=_cparams("parallel"),
        name=f"win_step_g{g}",
    )(q3, kv_new, buf.reshape(nb, wb, 2 * w), tb, tn)
    return o, lse, st.reshape(buf.shape)


def _ab_merge_kernel(o0, o1, o2, l0, l1, l2, out_ref):
    a, b, c = l0[...], l1[...], l2[...]
    m = jnp.maximum(jnp.maximum(a, b), c)
    ea, eb, ec = jnp.exp(a - m), jnp.exp(b - m), jnp.exp(c - m)
    out_ref[...] = ((ea * o0[...] + eb * o1[...] + ec * o2[...]) / (ea + eb + ec)).astype(BF16)


def ab_merge(outs, lses):
    t, w = outs[0].shape
    tm = min(t, 1024)
    spec = pl.BlockSpec((tm, w), lambda i: (i, 0))
    return pl.pallas_call(
        _ab_merge_kernel,
        grid=(t // tm,),
        in_specs=[spec] * 6,
        out_specs=spec,
        out_shape=jax.ShapeDtypeStruct((t, w), BF16),
        compiler_params=_cparams("parallel"),
        name="ab_merge",
    )(*outs, *lses)


CD_COLS = dict(cq=0, ck=256, cv=512, cgate=1024, dq=1536, dk=2048, dv=2560, iq=3072, ik=3328, clr=3392, iw=3408)
CD_PAD = 3456
SMALL_BLOCK = CD_COLS["ik"] // LANES
SM_IK, SM_CLR, SM_IW = 0, CD_COLS["clr"] - CD_COLS["ik"], CD_COLS["iw"] - CD_COLS["ik"]


def cd_reorder_w(w_in):
    sizes = (256, 256, 512, 16, 512, 512, 512, 512, 256, 64, 4)
    names = ("cq", "ck", "cv", "clr", "cgate", "dq", "dk", "dv", "iq", "ik", "iw")
    starts = np.concatenate([[0], np.cumsum(sizes)[:-1]])
    out = jnp.zeros((w_in.shape[0], CD_PAD), w_in.dtype)
    for nm, st, sz in zip(names, starts, sizes):
        out = lax.dynamic_update_slice(out, w_in[:, st:st + sz], (0, CD_COLS[nm]))
    return out


def _cd_pre_kernel(d_ref, sm_ref, wa_ref, ba_ref, qg_ref, kg_ref, seg_ref, g_ref, dq_ref, kv_ref, kb_ref, vb_ref, *, w):
    d = d_ref[...]
    sm = sm_ref[...]
    clr = sm[:, SM_CLR:SM_CLR + GLA_GATE_RANK]
    wa = wa_ref[...]
    pre = ba_ref[...]
    for part in _split3(clr):
        pre = pre + _dot(part, wa[0]) + _dot(part, wa[1])
    lsig = jnp.minimum(pre, 0.0) - jnp.log(1.0 + jnp.exp(-jnp.abs(pre)))
    g_ref[...] = lsig * (1.0 / GLA_TAU)
    dq_ref[...] = (_head_norm(d[:, 0:w], qg_ref[...], seg_ref) * HEAD_DIM ** -0.5).astype(BF16)
    kn = _head_norm(d[:, w:2 * w], kg_ref[...], seg_ref)
    v = d[:, 2 * w:3 * w]
    kv_ref[:, 0:w] = kn
    kv_ref[:, w:2 * w] = v
    kb_ref[...] = kn.astype(BF16)
    vb_ref[...] = v.astype(BF16)


def cd_pre(z, wa2, ba, qn, kn):
    t = z.shape[0]
    w = DSA_HEADS * HEAD_DIM
    gw = GLA_HEADS * GLA_DK
    tm = min(t, 512)
    wa_hi, wa_lo = _split2(wa2)
    return pl.pallas_call(
        functools.partial(_cd_pre_kernel, w=w),
        grid=(t // tm,),
        in_specs=[pl.BlockSpec((tm, 3 * w), lambda i: (i, CD_COLS["dq"] // (3 * w))),
                  pl.BlockSpec((tm, LANES), lambda i: (i, SMALL_BLOCK)),
                  pl.BlockSpec((2, GLA_GATE_RANK, gw), lambda i: (0, 0, 0)),
                  pl.BlockSpec((1, gw), lambda i: (0, 0)),
                  pl.BlockSpec((1, w), lambda i: (0, 0)),
                  pl.BlockSpec((1, w), lambda i: (0, 0)),
                  pl.BlockSpec((w, w), lambda i: (0, 0))],
        out_specs=[pl.BlockSpec((tm, gw), lambda i: (i, 0)),
                   pl.BlockSpec((tm, w), lambda i: (i, 0)),
                   pl.BlockSpec((tm, 2 * w), lambda i: (i, 0)),
                   pl.BlockSpec((tm, w), lambda i: (i, 0)),
                   pl.BlockSpec((tm, w), lambda i: (i, 0))],
        out_shape=[jax.ShapeDtypeStruct((t, gw), F32),
                   jax.ShapeDtypeStruct((t, w), BF16),
                   jax.ShapeDtypeStruct((t, 2 * w), F32),
                   jax.ShapeDtypeStruct((t, w), BF16),
                   jax.ShapeDtypeStruct((t, w), BF16)],
        compiler_params=_cparams("parallel"),
        name="cd_pre",
    )(z, z, jnp.stack([wa_hi, wa_lo]), ba.reshape(1, gw), jnp.tile(qn, DSA_HEADS).reshape(1, w),
      jnp.tile(kn, DSA_HEADS).reshape(1, w), _seg_matrix(w))


def _gla_consts(c):
    levels = []
    s = c // 2
    while s >= 1:
        levels.append(s)
        s //= 2
    i = np.arange(c)
    tri = (i[None, :] <= i[:, None]).astype(np.float32)
    mats = [tri]
    masks = []
    for s in levels:
        ref = (i // (2 * s)) * (2 * s) + s - 1
        r = (i[None, :] <= ref[:, None]).astype(np.float32)
        mats.append(tri - r)
        same = (i[:, None] // (2 * s)) == (i[None, :] // (2 * s))
        masks.append(same & ((i[:, None] % (2 * s)) >= s) & ((i[None, :] % (2 * s)) < s))
    masks.append(i[:, None] == i[None, :])
    mstack = np.concatenate(mats, axis=0)
    mask = np.stack([np.tile(m.astype(np.float32), (1, GLA_HEADS)) for m in masks])
    return jnp.asarray(mstack, BF16), jnp.asarray(mask, F32), len(levels)


def _gla_kernel(qk_ref, v_ref, gate_ref, g_ref, s0_ref, mstack_ref, mask_ref, gn_ref, eye_ref,
                o_ref, sf_ref, st_ref, *, c, nl):
    t = pl.program_id(1)
    kw = GLA_HEADS * GLA_DK
    vw = GLA_HEADS * GLA_DV
    lane_k = lax.broadcasted_iota(I32, (1, kw), 1) // GLA_DK
    lane_v = lax.broadcasted_iota(I32, (1, vw), 1) // GLA_DV

    @pl.when(t == 0)
    def _():
        rows = []
        for h in range(GLA_HEADS):
            z = jnp.zeros((GLA_DK, GLA_DV), F32)
            rows.append(jnp.concatenate([s0_ref[0, h] if hh == h else z for hh in range(GLA_HEADS)], axis=1))
        st_ref[...] = jnp.concatenate(rows, axis=0).T

    qk = qk_ref[0]
    q = qk[:, 0:kw] * GLA_DK ** -0.5
    k = qk[:, kw:2 * kw]
    v = v_ref[0]
    mstack = mstack_ref[...]
    r = None
    for part in _split3(g_ref[0]):
        d = _dot(mstack, part)
        r = d if r is None else r + d
    b = r[0:c]

    def expand_k(x):
        return jnp.concatenate([jnp.where(lane_k == h, x, 0.0) for h in range(GLA_HEADS)], axis=0).astype(BF16)

    st = st_ref[...]
    o = _nt((q * jnp.exp(b)).astype(BF16), st.astype(BF16))
    a = mask_ref[nl] * _nt(q.astype(BF16), expand_k(k))
    for lv in range(nl):
        sc = jnp.exp(-jnp.abs(r[(lv + 1) * c:(lv + 2) * c]))
        a = a + mask_ref[lv] * _nt((q * sc).astype(BF16), expand_k(k * sc))
    vexp = jnp.concatenate([jnp.where(lane_v == h, v, 0.0) for h in range(GLA_HEADS)], axis=0).astype(BF16)
    o = o + _dot(a.astype(BF16), vexp)

    blast = b[c - 1:c]
    kt = (k * jnp.exp(blast - b)).astype(BF16)
    vt = _nt(eye_ref[...], v.astype(BF16)).astype(BF16)
    upd = _dot(vt, kt)
    row_h = lax.broadcasted_iota(I32, (vw, 1), 0) // GLA_DV
    st_new = st * jnp.exp(blast) + jnp.where(row_h == lane_k, upd, 0.0)
    st_ref[...] = st_new

    gate = gate_ref[0]
    gn = gn_ref[...]
    for h in range(GLA_HEADS):
        hs = slice(h * GLA_DV, (h + 1) * GLA_DV)
        oh = o[:, hs]
        y = oh * lax.rsqrt(jnp.mean(oh * oh, axis=-1, keepdims=True) + EPS) * gn
        gh = gate[:, hs]
        o_ref[0, :, hs] = (y * (gh * jax.nn.sigmoid(gh))).astype(BF16)

    @pl.when(t == pl.num_programs(1) - 1)
    def _():
        s_t = st_new.T
        for h in range(GLA_HEADS):
            sf_ref[0, h] = s_t[h * GLA_DK:(h + 1) * GLA_DK, h * GLA_DV:(h + 1) * GLA_DV]


def gla(z3, g3, s0, gla_norm):
    nb, l, _ = z3.shape
    c = GLA_CHUNK
    assert l % c == 0
    kw = GLA_HEADS * GLA_DK
    vw = GLA_HEADS * GLA_DV
    mstack, mask, nl = _gla_consts(c)
    eye = jnp.eye(vw, dtype=BF16)
    o, sf = pl.pallas_call(
        functools.partial(_gla_kernel, c=c, nl=nl),
        grid=(nb, l // c),
        in_specs=[pl.BlockSpec((1, c, 2 * kw), lambda n, t: (n, t, 0)),
                  pl.BlockSpec((1, c, vw), lambda n, t: (n, t, CD_COLS["cv"] // vw)),
                  pl.BlockSpec((1, c, vw), lambda n, t: (n, t, CD_COLS["cgate"] // vw)),
                  pl.BlockSpec((1, c, kw), lambda n, t: (n, t, 0)),
                  pl.BlockSpec((1, GLA_HEADS, GLA_DK, GLA_DV), lambda n, t: (n, 0, 0, 0)),
                  pl.BlockSpec(mstack.shape, lambda n, t: (0, 0)),
                  pl.BlockSpec(mask.shape, lambda n, t: (0, 0, 0)),
                  pl.BlockSpec((1, GLA_DV), lambda n, t: (0, 0)),
                  pl.BlockSpec((vw, vw), lambda n, t: (0, 0))],
        out_specs=[pl.BlockSpec((1, c, vw), lambda n, t: (n, t, 0)),
                   pl.BlockSpec((1, GLA_HEADS, GLA_DK, GLA_DV), lambda n, t: (n, 0, 0, 0))],
        out_shape=[jax.ShapeDtypeStruct((nb, l, vw), BF16),
                   jax.ShapeDtypeStruct((nb, GLA_HEADS, GLA_DK, GLA_DV), F32)],
        scratch_shapes=[pltpu.VMEM((vw, kw), F32)],
        compiler_params=_cparams("parallel", "arbitrary"),
        name="gla",
    )(z3, z3, z3, g3, s0, mstack, mask, gla_norm.reshape(1, GLA_DV), eye)
    return o, sf


def _sort_key(x):
    bits = pltpu.bitcast(x, I32)
    return jnp.where(bits < 0, (bits ^ 0x7FFFFFFF) + 1, bits)


def _idx_lhs(iq, h):
    hi, lo = _split2(iq[:, h * IDX_DIM:(h + 1) * IDX_DIM])
    return jnp.concatenate([hi, hi, lo, jnp.zeros_like(hi)], axis=1)


def _idx_rhs(ik):
    hi, lo = _split2(ik)
    return jnp.concatenate([hi, lo, hi, jnp.zeros_like(hi)], axis=1)


def _idx_scores(lhs, rhs, wcol):
    sc = None
    for h in range(IDX_HEADS):
        d = jnp.maximum(_nt(lhs[h], rhs) * IDX_DIM ** -0.5, 0.0) * (wcol[h] * IDX_HEADS ** -0.5)
        sc = d if sc is None else sc + d
    return sc


def _dsa_select_kernel(iq_ref, smq_ref, smk_ref, triu_ref, mask_ref, kb3_ref, sc_ref, *, nc, topk):
    i = pl.program_id(1)
    ck = DSA_CK
    qb = DSA_QB
    bpc = ck // qb

    @pl.when(i == 0)
    def _():
        for c in range(nc):
            kb3_ref[c] = _idx_rhs(smk_ref[0, c * ck:(c + 1) * ck, SM_IK:SM_IK + IDX_DIM])

    iq = iq_ref[0]
    smq = smq_ref[0]
    lhs = [_idx_lhs(iq, h) for h in range(IDX_HEADS)]
    wcol = [smq[:, SM_IW + h:SM_IW + h + 1] for h in range(IDX_HEADS)]
    nch = (i + bpc) // bpc
    dcr = lax.broadcasted_iota(I32, (qb, ck), 1) - lax.broadcasted_iota(I32, (qb, ck), 0)

    def score_body(c, carry):
        key = _sort_key(_idx_scores(lhs, kb3_ref[c], wcol))
        sc_ref[c] = jnp.where(dcr <= i * qb - c * ck, key, INT_MIN)
        return carry

    lax.fori_loop(0, nch, score_body, 0)

    def count(pred):
        def body(c, acc):
            m = jnp.where(pred(sc_ref[c]), 1, 0)
            for j in range(bpc):
                acc = acc + m[:, j * LANES:(j + 1) * LANES]
            return acc
        acc = lax.fori_loop(0, nch, body, jnp.zeros((qb, LANES), I32))
        return jnp.sum(acc, axis=1, keepdims=True)

    qpos = i * qb + lax.broadcasted_iota(I32, (qb, 1), 0)
    kk = jnp.minimum(topk, qpos + 1)

    def bit_body(t, carry):
        thr, cge = carry
        cand = thr + lax.shift_left(jnp.int32(1), 31 - t)
        cnt = count(lambda blk: blk >= cand)
        ok = cnt >= kk
        return jnp.where(ok, cand, thr), jnp.where(ok, cnt, cge)

    thr, cge = lax.fori_loop(0, 32, bit_body, (jnp.full((qb, 1), INT_MIN, I32), jnp.zeros((qb, 1), I32)))
    cgt = count(lambda blk: blk > thr)
    need = kk - cgt
    tied = jnp.max(jnp.where(need < cge - cgt, 1, 0))

    @pl.when(tied > 0)
    def _():
        needf = need.astype(F32)

        def tie_body(c, off):
            blk = sc_ref[c]
            eq = blk == thr
            pref = _dot(jnp.where(eq, 1.0, 0.0).astype(BF16), triu_ref[...])
            drop = jnp.logical_and(eq, pref + off > needf)
            sc_ref[c] = jnp.where(drop, INT_MIN, blk)
            return off + pref[:, ck - 1:ck]

        lax.fori_loop(0, nch, tie_body, jnp.zeros((qb, 1), F32))

    def live_body(c, carry):
        mask_ref[0, 0, c] = jnp.where(sc_ref[c] >= thr, 0.0, NEG).astype(BF16)
        return carry

    def dead_body(c, carry):
        mask_ref[0, 0, c] = jnp.full((qb, ck), NEG, BF16)
        return carry

    lax.fori_loop(0, nch, live_body, 0)
    lax.fori_loop(nch, nc, dead_body, 0)


def dsa_select(z3):
    b, s, _ = z3.shape
    nq, nc = s // DSA_QB, s // DSA_CK
    topk = min(DSA_TOPK_MAX, s // 4)
    triu = jnp.asarray(np.triu(np.ones((DSA_CK, DSA_CK), np.float32)), BF16)
    return pl.pallas_call(
        functools.partial(_dsa_select_kernel, nc=nc, topk=topk),
        grid=(b, nq),
        in_specs=[pl.BlockSpec((1, DSA_QB, IDX_HEADS * IDX_DIM), lambda n, i: (n, i, CD_COLS["iq"] // (IDX_HEADS * IDX_DIM))),
                  pl.BlockSpec((1, DSA_QB, LANES), lambda n, i: (n, i, SMALL_BLOCK)),
                  pl.BlockSpec((1, s, LANES), lambda n, i: (n, 0, SMALL_BLOCK)),
                  pl.BlockSpec((DSA_CK, DSA_CK), lambda n, i: (0, 0))],
        out_specs=pl.BlockSpec((1, 1, nc, DSA_QB, DSA_CK), lambda n, i: (n, i, 0, 0, 0)),
        out_shape=jax.ShapeDtypeStruct((b, nq, nc, DSA_QB, DSA_CK), BF16),
        scratch_shapes=[pltpu.VMEM((nc, DSA_CK, 4 * IDX_DIM), BF16), pltpu.VMEM((nc, DSA_QB, DSA_CK), I32)],
        compiler_params=_cparams("parallel", "arbitrary"),
        name="dsa_select",
    )(z3, z3, z3, triu)


def _dsa_bias_tiles(tab_d):
    o = 0
    while _rel_bucket_np(max(o * DSA_QB - (DSA_QB - 1), 0)) < REL_BUCKETS - 1:
        o += 1
    offs = np.arange(o + 1)[:, None, None] * DSA_QB
    d = offs + np.arange(DSA_QB)[None, :, None] - np.arange(DSA_QB)[None, None, :]
    tiles = jnp.take(tab_d, jnp.asarray(_rel_bucket_np(d)), axis=0)
    return tiles.transpose(0, 3, 1, 2).astype(F32)


def _dsa_attn_kernel(qi_ref, kc_ref, q_ref, k_ref, v_ref, mask_ref, bt_ref, o_ref, m_ref, l_ref, acc_ref, *, n_off):
    s_id = pl.program_id(1)
    i = qi_ref[s_id]
    c = kc_ref[s_id]
    qb, ck = DSA_QB, DSA_CK
    bpc = ck // qb

    @pl.when(c == 0)
    def _():
        m_ref[...] = jnp.full(m_ref.shape, NEG, F32)
        l_ref[...] = jnp.zeros(l_ref.shape, F32)
        acc_ref[...] = jnp.zeros(acc_ref.shape, F32)

    madd = mask_ref[0, 0, 0].astype(F32)
    q = q_ref[0]
    k = k_ref[0]
    v = v_ref[0]
    offs = [jnp.clip(i - (bpc * c + t), 0, n_off - 1) for t in range(bpc)]
    for h in range(DSA_HEADS):
        hs = slice(h * HEAD_DIM, (h + 1) * HEAD_DIM)
        bias = jnp.concatenate([bt_ref[offs[t], h] for t in range(bpc)], axis=1)
        s = _nt(q[:, hs], k[:, hs]) + bias + madd
        m_old = m_ref[h][:, 0:1]
        m_new = jnp.maximum(m_old, jnp.max(s, axis=-1, keepdims=True))
        alpha = jnp.exp(m_old - m_new)
        p = jnp.exp(s - m_new)
        l_ref[h] = jnp.broadcast_to(alpha * l_ref[h][:, 0:1] + jnp.sum(p, axis=-1, keepdims=True), (qb, LANES))
        m_ref[h] = jnp.broadcast_to(m_new, (qb, LANES))
        acc_ref[:, hs] = alpha * acc_ref[:, hs] + _dot(p.astype(BF16), v[:, hs])

    @pl.when(c == i // bpc)
    def _():
        for h in range(DSA_HEADS):
            hs = slice(h * HEAD_DIM, (h + 1) * HEAD_DIM)
            o_ref[0, :, hs] = (acc_ref[:, hs] / l_ref[h][:, 0:1]).astype(BF16)


def dsa_attend(dq, kb, vb, mask, tab_d):
    b, s, w = dq.shape
    nq = s // DSA_QB
    bpc = DSA_CK // DSA_QB
    bt = _dsa_bias_tiles(tab_d)
    n_off = bt.shape[0]
    qi = np.concatenate([np.full(i // bpc + 1, i) for i in range(nq)]).astype(np.int32)
    kc = np.concatenate([np.arange(i // bpc + 1) for i in range(nq)]).astype(np.int32)
    grid_spec = pltpu.PrefetchScalarGridSpec(
        num_scalar_prefetch=2,
        grid=(b, len(qi)),
        in_specs=[pl.BlockSpec((1, DSA_QB, w), lambda n, t, qi_, kc_: (n, qi_[t], 0)),
                  pl.BlockSpec((1, DSA_CK, w), lambda n, t, qi_, kc_: (n, kc_[t], 0)),
                  pl.BlockSpec((1, DSA_CK, w), lambda n, t, qi_, kc_: (n, kc_[t], 0)),
                  pl.BlockSpec((1, 1, 1, DSA_QB, DSA_CK), lambda n, t, qi_, kc_: (n, qi_[t], kc_[t], 0, 0)),
                  pl.BlockSpec(bt.shape, lambda n, t, qi_, kc_: (0, 0, 0, 0))],
        out_specs=pl.BlockSpec((1, DSA_QB, w), lambda n, t, qi_, kc_: (n, qi_[t], 0)),
        scratch_shapes=[pltpu.VMEM((DSA_HEADS, DSA_QB, LANES), F32), pltpu.VMEM((DSA_HEADS, DSA_QB, LANES), F32),
                        pltpu.VMEM((DSA_QB, w), F32)],
    )
    return pl.pallas_call(
        functools.partial(_dsa_attn_kernel, n_off=n_off),
        grid_spec=grid_spec,
        out_shape=jax.ShapeDtypeStruct((b, s, w), BF16),
        compiler_params=_cparams("parallel", "arbitrary"),
        name="dsa_attend",
    )(jnp.asarray(qi), jnp.asarray(kc), dq, kb, vb, mask, bt)


ROWS8 = 8


def _dsa_step_scores_kernel(pt_ref, iq_ref, sm_ref, kidx_ref, o_ref):
    iq = iq_ref[0]
    sm = sm_ref[0]
    lhs = [_idx_lhs(iq, h) for h in range(IDX_HEADS)]
    wcol = [sm[:, SM_IW + h:SM_IW + h + 1] for h in range(IDX_HEADS)]
    o_ref[0] = _idx_scores(lhs, _idx_rhs(kidx_ref[0]), wcol)


def dsa_step_scores(z8, kidx_pool, page_table):
    nb = z8.shape[0]
    n_pages = page_table.shape[1]
    iqw = IDX_HEADS * IDX_DIM
    grid_spec = pltpu.PrefetchScalarGridSpec(
        num_scalar_prefetch=1,
        grid=(nb, n_pages),
        in_specs=[pl.BlockSpec((1, ROWS8, iqw), lambda n, p, pt: (n, 0, CD_COLS["iq"] // iqw)),
                  pl.BlockSpec((1, ROWS8, LANES), lambda n, p, pt: (n, 0, SMALL_BLOCK)),
                  pl.BlockSpec((1, PAGE_SIZE, IDX_DIM), lambda n, p, pt: (pt[n * n_pages + p], 0, 0))],
        out_specs=pl.BlockSpec((1, ROWS8, PAGE_SIZE), lambda n, p, pt: (n, 0, p)),
    )
    return pl.pallas_call(
        _dsa_step_scores_kernel,
        grid_spec=grid_spec,
        out_shape=jax.ShapeDtypeStruct((nb, ROWS8, n_pages * PAGE_SIZE), F32),
        compiler_params=_cparams("parallel", "arbitrary"),
        name="dsa_step_scores",
    )(page_table.reshape(-1), z8, z8, kidx_pool)


def _dsa_step_select_kernel(sc_ref, iq_ref, sm_ref, triu_ref, mp_ref, mn_ref, key_ref, *, past, l_new, topk):
    ck = DSA_CK
    nck = past // ck
    iq = iq_ref[0]
    sm = sm_ref[0]
    lhs = [_idx_lhs(iq, h) for h in range(IDX_HEADS)]
    wcol = [sm[:, SM_IW + h:SM_IW + h + 1] for h in range(IDX_HEADS)]
    rhs_new = _idx_rhs(jnp.concatenate([sm[:, SM_IK:SM_IK + IDX_DIM], jnp.zeros((LANES - ROWS8, IDX_DIM), F32)], axis=0))
    sc_new = _idx_scores(lhs, rhs_new, wcol)
    row = lax.broadcasted_iota(I32, (ROWS8, LANES), 0)
    col = lax.broadcasted_iota(I32, (ROWS8, LANES), 1)
    key_new = jnp.where(jnp.logical_and(col <= row, col < l_new), _sort_key(sc_new), INT_MIN)
    key_ref[...] = _sort_key(sc_ref[0])
    kk = jnp.minimum(topk, past + 1 + lax.broadcasted_iota(I32, (ROWS8, 1), 0))

    def count(pred):
        return (jnp.sum(jnp.where(pred(key_ref[...]), 1, 0), axis=1, keepdims=True)
                + jnp.sum(jnp.where(pred(key_new), 1, 0), axis=1, keepdims=True))

    def bit_body(t, carry):
        thr, cge = carry
        cand = thr + lax.shift_left(jnp.int32(1), 31 - t)
        cnt = count(lambda x: x >= cand)
        ok = cnt >= kk
        return jnp.where(ok, cand, thr), jnp.where(ok, cnt, cge)

    thr, cge = lax.fori_loop(0, 32, bit_body, (jnp.full((ROWS8, 1), INT_MIN, I32), jnp.zeros((ROWS8, 1), I32)))
    cgt = count(lambda x: x > thr)
    need = kk - cgt
    needf = need.astype(F32)
    tied = jnp.max(jnp.where(need < cge - cgt, 1, 0))
    mn_ref[0] = jnp.where(key_new >= thr, 0.0, NEG)

    @pl.when(tied == 0)
    def _():
        mp_ref[0] = jnp.where(key_ref[...] >= thr, 0.0, NEG)

    @pl.when(tied > 0)
    def _():
        off = jnp.zeros((ROWS8, 1), F32)
        triu = triu_ref[...]
        for c in range(nck):
            blk = key_ref[:, c * ck:(c + 1) * ck]
            eq = blk == thr
            pref = _dot(jnp.where(eq, 1.0, 0.0).astype(BF16), triu)
            keep = jnp.logical_or(blk > thr, jnp.logical_and(eq, pref + off <= needf))
            mp_ref[0, :, c * ck:(c + 1) * ck] = jnp.where(keep, 0.0, NEG)
            off = off + pref[:, ck - 1:ck]
        eq = key_new == thr
        pref = _dot(jnp.where(eq, 1.0, 0.0).astype(BF16), triu[0:LANES, 0:LANES])
        keep = jnp.logical_or(key_new > thr, jnp.logical_and(eq, pref + off <= needf))
        mn_ref[0] = jnp.where(keep, 0.0, NEG)


def dsa_step_select(scores, z8, l_new):
    nb, _, past = scores.shape
    topk = min(DSA_TOPK_MAX, (past + l_new) // 4)
    iqw = IDX_HEADS * IDX_DIM
    assert past % DSA_CK == 0
    triu = jnp.asarray(np.triu(np.ones((DSA_CK, DSA_CK), np.float32)), BF16)
    return pl.pallas_call(
        functools.partial(_dsa_step_select_kernel, past=past, l_new=l_new, topk=topk),
        grid=(nb,),
        in_specs=[pl.BlockSpec((1, ROWS8, past), lambda n: (n, 0, 0)),
                  pl.BlockSpec((1, ROWS8, iqw), lambda n: (n, 0, CD_COLS["iq"] // iqw)),
                  pl.BlockSpec((1, ROWS8, LANES), lambda n: (n, 0, SMALL_BLOCK)),
                  pl.BlockSpec((DSA_CK, DSA_CK), lambda n: (0, 0))],
        out_specs=[pl.BlockSpec((1, ROWS8, past), lambda n: (n, 0, 0)),
                   pl.BlockSpec((1, ROWS8, LANES), lambda n: (n, 0, 0))],
        out_shape=[jax.ShapeDtypeStruct((nb, ROWS8, past), F32),
                   jax.ShapeDtypeStruct((nb, ROWS8, LANES), F32)],
        scratch_shapes=[pltpu.VMEM((ROWS8, past), I32)],
        compiler_params=_cparams("parallel"),
        name="dsa_step_select",
    )(scores, z8, z8, triu)


def _dsa_step_attn_kernel(pt_ref, q_ref, kv_ref, kvn_ref, mp_ref, mn_ref, bp_ref, bn_ref, o_ref,
                          qbd_ref, newpage_ref, m_ref, l_ref, acc_ref, *, w):
    n = pl.program_id(0)
    p = pl.program_id(1)
    rows = DSA_HEADS * ROWS8
    lane_h = lax.broadcasted_iota(I32, (ROWS8, w), 1) // HEAD_DIM

    @pl.when(jnp.logical_and(n == 0, p == 0))
    def _():
        newpage_ref[...] = jnp.zeros(newpage_ref.shape, F32)

    @pl.when(p == 0)
    def _():
        q = q_ref[0]
        for h in range(DSA_HEADS):
            qbd_ref[h * ROWS8:(h + 1) * ROWS8, :] = jnp.where(lane_h == h, q, jnp.zeros_like(q))
        m_ref[...] = jnp.full(m_ref.shape, NEG, F32)
        l_ref[...] = jnp.zeros(l_ref.shape, F32)
        acc_ref[...] = jnp.zeros(acc_ref.shape, F32)

    def accumulate(kv, madd8, bias):
        s = _nt(qbd_ref[...], kv[:, 0:w].astype(BF16)) + bias + jnp.concatenate([madd8] * DSA_HEADS, axis=0)
        m_old = m_ref[:, 0:1]
        m_new = jnp.maximum(m_old, jnp.max(s, axis=-1, keepdims=True))
        alpha = jnp.exp(m_old - m_new)
        pr = jnp.exp(s - m_new)
        l_ref[...] = jnp.broadcast_to(alpha * l_ref[:, 0:1] + jnp.sum(pr, axis=-1, keepdims=True), (rows, LANES))
        m_ref[...] = jnp.broadcast_to(m_new, (rows, LANES))
        acc_ref[...] = alpha * acc_ref[...] + _dot(pr.astype(BF16), kv[:, w:2 * w].astype(BF16))

    accumulate(kv_ref[0], mp_ref[0], bp_ref[...])

    @pl.when(p == pl.num_programs(1) - 1)
    def _():
        newpage_ref[0:ROWS8, :] = kvn_ref[0]
        accumulate(newpage_ref[...], mn_ref[0], bn_ref[...])
        for h in range(DSA_HEADS):
            rs = slice(h * ROWS8, (h + 1) * ROWS8)
            hs = slice(h * HEAD_DIM, (h + 1) * HEAD_DIM)
            o_ref[0, :, hs] = (acc_ref[rs, hs] / l_ref[rs, 0:1]).astype(BF16)


def _dsa_step_bias(tab_d, past, l_new):
    lq = np.minimum(np.arange(ROWS8), l_new - 1)[:, None]
    d_past = past + lq - np.arange(past)[None, :]
    d_new = lq - np.arange(LANES)[None, :]

    def table(d):
        t = jnp.take(tab_d, jnp.asarray(_rel_bucket_np(d)), axis=0)
        return t.transpose(2, 0, 1).reshape(DSA_HEADS * ROWS8, d.shape[1]).astype(F32)

    return table(d_past), table(d_new)


def dsa_step_attend(dq8, kv_new8, kv_pool, page_table, mask_past, mask_new, tab_d, l_new):
    nb, _, w = dq8.shape
    n_pages = page_table.shape[1]
    past = n_pages * PAGE_SIZE
    rows = DSA_HEADS * ROWS8
    bp, bn = _dsa_step_bias(tab_d, past, l_new)
    grid_spec = pltpu.PrefetchScalarGridSpec(
        num_scalar_prefetch=1,
        grid=(nb, n_pages),
        in_specs=[pl.BlockSpec((1, ROWS8, w), lambda n, p, pt: (n, 0, 0)),
                  pl.BlockSpec((1, PAGE_SIZE, 2 * w), lambda n, p, pt: (pt[n * n_pages + p], 0, 0)),
                  pl.BlockSpec((1, ROWS8, 2 * w), lambda n, p, pt: (n, 0, 0)),
                  pl.BlockSpec((1, ROWS8, PAGE_SIZE), lambda n, p, pt: (n, 0, p)),
                  pl.BlockSpec((1, ROWS8, LANES), lambda n, p, pt: (n, 0, 0)),
                  pl.BlockSpec((rows, PAGE_SIZE), lambda n, p, pt: (0, p)),
                  pl.BlockSpec((rows, LANES), lambda n, p, pt: (0, 0))],
        out_specs=pl.BlockSpec((1, ROWS8, w), lambda n, p, pt: (n, 0, 0)),
        scratch_shapes=[pltpu.VMEM((rows, w), BF16), pltpu.VMEM((PAGE_SIZE, 2 * w), F32),
                        pltpu.VMEM((rows, LANES), F32), pltpu.VMEM((rows, LANES), F32), pltpu.VMEM((rows, w), F32)],
    )
    return pl.pallas_call(
        functools.partial(_dsa_step_attn_kernel, w=w),
        grid_spec=grid_spec,
        out_shape=jax.ShapeDtypeStruct((nb, ROWS8, w), BF16),
        compiler_params=_cparams("arbitrary", "arbitrary"),
        name="dsa_step_attend",
    )(page_table.reshape(-1), dq8, kv_pool.reshape(kv_pool.shape[0], PAGE_SIZE, 2 * w), kv_new8,
      mask_past, mask_new, bp, bn)


def _pad_rows(x3, rows):
    return jnp.pad(x3, ((0, 0), (0, rows - x3.shape[1]), (0, 0)))


def _trunk(x, is_step, conv_state, win_states, gla_state, dsa_kv, dsa_kidx, page_table, wts):
    (norm_mix, norm_ffn, w_in_ab, conv_w, conv_b, conv_ln_g, conv_ln_b, qn_ab, kn_ab, w_out_ab,
     w_in_cd, gla_wa2, gla_ba, gla_norm, qn_cd, kn_cd, w_out_cd, rel_bias, w_g, w_u, w_d) = wts
    nb, l, d = x.shape
    t = nb * l
    depth = norm_mix.shape[0]
    c = conv_w.shape[2]
    wq = B_HPG * HEAD_DIM
    x2 = x.reshape(t, d)
    conv_new, gla_new, kv_new, kidx_new = [], [], [], []
    win_new = [[] for _ in WINDOWS]
    for layer in range(depth):
        i = layer // 2
        if layer % 2 == 0:
            z = norm_matmul(x2, norm_mix[layer], w_in_ab[i])
            hist = conv_state[i] if is_step else jnp.zeros((nb, CONV_WIDTH - 1, c), F32)
            a_out, c_st = conv_module(z.reshape(nb, l, -1), hist, conv_w[i], conv_b[i], conv_ln_g[i], conv_ln_b[i])
            conv_new.append(c_st)
            q, kv = ab_qkv(z, qn_ab[i], kn_ab[i], 2 * c)
            outs, lses = [], []
            for g, window in enumerate(WINDOWS):
                tab_g = rel_bias[:, g * B_HPG:(g + 1) * B_HPG]
                if is_step:
                    o, lse, st = win_step(q.reshape(nb, l, -1), kv.reshape(len(WINDOWS), nb, l, 2 * wq),
                                          win_states[g][i], g, tab_g)
                else:
                    o, lse = win_prompt(q.reshape(nb, l, -1), kv.reshape(len(WINDOWS), nb, l, 2 * wq), g, tab_g)
                    st = kv[g].reshape(nb, l, 2, B_HPG, HEAD_DIM)[:, -min(window, l):]
                outs.append(o.reshape(t, wq))
                lses.append(lse.reshape(t, wq))
                win_new[g].append(st)
            m1, m2 = a_out.reshape(t, c), ab_merge(outs, lses)
            wo = w_out_ab[i]
        else:
            z = norm_matmul(x2, norm_mix[layer], w_in_cd[i])
            gdec, dq, kv, kb, vb = cd_pre(z, gla_wa2[i], gla_ba[i], qn_cd[i], kn_cd[i])
            z3 = z.reshape(nb, l, -1)
            wd = DSA_HEADS * HEAD_DIM
            ik = z3[:, :, CD_COLS["ik"]:CD_COLS["ik"] + IDX_DIM]
            tab_d = rel_bias[:, B_HEADS:]
            if is_step:
                lp = GLA_CHUNK
                o_c, s_c = gla(_pad_rows(z3, lp), _pad_rows(gdec.reshape(nb, l, -1), lp), gla_state[i], gla_norm[i])
                o_c = o_c[:, :l]
                z8 = _pad_rows(z3, ROWS8)
                scores = dsa_step_scores(z8, dsa_kidx[i], page_table)
                mask_p, mask_n = dsa_step_select(scores, z8, l)
                o_d = dsa_step_attend(_pad_rows(dq.reshape(nb, l, wd), ROWS8), _pad_rows(kv.reshape(nb, l, 2 * wd), ROWS8),
                                      dsa_kv[i], page_table, mask_p, mask_n, tab_d, l)[:, :l]
                kv_st = kv.reshape(nb, l, 2, DSA_HEADS, HEAD_DIM)
                ki_st = ik
            else:
                s0 = jnp.zeros((nb, GLA_HEADS, GLA_DK, GLA_DV), F32)
                o_c, s_c = gla(z3, gdec.reshape(nb, l, -1), s0, gla_norm[i])
                mask = dsa_select(z3)
                o_d = dsa_attend(dq.reshape(nb, l, wd), kb.reshape(nb, l, wd), vb.reshape(nb, l, wd), mask, tab_d)
                n_pg = l // PAGE_SIZE
                kv_st = kv.reshape(nb, n_pg, PAGE_SIZE, 2, DSA_HEADS, HEAD_DIM)
                ki_st = ik.reshape(nb, n_pg, PAGE_SIZE, IDX_DIM)
            gla_new.append(s_c)
            kv_new.append(kv_st)
            kidx_new.append(ki_st)
            m1, m2 = o_c.reshape(t, -1), o_d.reshape(t, -1)
            wo = w_out_cd[i]
        d1 = m1.shape[1]
        x2 = mix_ffn(x2, m1, m2, wo[:d1], wo[d1:], norm_ffn[layer], w_g[layer], w_u[layer], w_d[layer])
    states = (jnp.stack(conv_new), jnp.stack(win_new[0]), jnp.stack(win_new[1]), jnp.stack(win_new[2]),
              jnp.stack(gla_new), jnp.stack(kv_new), jnp.stack(kidx_new))
    return x2.reshape(nb, l, d), states


def kernel(x_prompt, x_sample, state_conv, cache_win128, cache_win512, cache_win2048, state_gla, cache_dsa_kv, cache_dsa_kidx, page_table, norm_mix, norm_ffn, w_in_ab, conv_w, conv_b, conv_ln_g, conv_ln_b, qn_ab, kn_ab, w_out_ab, w_in_cd, gla_wa2, gla_ba, gla_norm, qn_cd, kn_cd, w_out_cd, rel_bias, w_ffn_gate, w_ffn_up, w_ffn_down):
    bf = lambda a: a.astype(BF16)
    w_in_cd_r = jnp.stack([cd_reorder_w(w_in_cd[i]) for i in range(w_in_cd.shape[0])])
    wts = (norm_mix, norm_ffn, bf(w_in_ab), conv_w, conv_b, conv_ln_g, conv_ln_b, qn_ab, kn_ab, bf(w_out_ab),
           bf(w_in_cd_r), gla_wa2, gla_ba, gla_norm, qn_cd, kn_cd, bf(w_out_cd), rel_bias,
           bf(w_ffn_gate), bf(w_ffn_up), bf(w_ffn_down))
    y_p, sp = _trunk(x_prompt, False, None, None, None, None, None, None, wts)
    y_s, ss = _trunk(x_sample, True, state_conv, (cache_win128, cache_win512, cache_win2048),
                     state_gla, cache_dsa_kv, cache_dsa_kidx, page_table, wts)
    conv_p, win128_p, win512_p, win2048_p, gla_p, dsa_kv_p, dsa_kidx_p = sp
    conv_s, win128_s, win512_s, win2048_s, gla_s, dsa_kv_s, dsa_kidx_s = ss
    return (y_p, y_s, conv_p, conv_s, win128_p, win128_s, win512_p, win512_s, win2048_p, win2048_s,
            gla_p, gla_s, dsa_kv_p, dsa_kv_s, dsa_kidx_p, dsa_kidx_s)
```

```python
import functools
import math

import numpy as np
import jax
import jax.numpy as jnp
from jax import lax
from jax.experimental import pallas as pl
from jax.experimental.pallas import tpu as pltpu

F32 = jnp.float32
BF16 = jnp.bfloat16
I32 = jnp.int32

EPS = 1e-6
NEG = -1e30
INT_MIN = -(2 ** 31)

V7X_VMEM_BYTES = 64 * 1024 * 1024
VMEM_LIMIT = V7X_VMEM_BYTES - 12 * 1024 * 1024
LANES = 128

HEAD_DIM = 64
CONV_WIDTH = 31
WINDOWS = (128, 512, 2048)
DILATIONS = (1, 4, 16)
B_HPG = 4
B_HEADS = B_HPG * len(WINDOWS)
SW_BLOCK = 128
GLA_HEADS = 4
GLA_DK = 64
GLA_DV = 128
GLA_GATE_RANK = 16
GLA_TAU = 16.0
DSA_HEADS = 8
IDX_HEADS = 4
IDX_DIM = 64
DSA_TOPK_MAX = 256
PAGE_SIZE = 128
REL_BUCKETS = 32
REL_MAX_DIST = 2048

GLA_CHUNK = 64
DSA_QB = 128
DSA_CK = 512
HIST_PAD = 32


def _cparams(*sem):
    return pltpu.CompilerParams(dimension_semantics=sem, vmem_limit_bytes=VMEM_LIMIT)


def _nt(a, b):
    return lax.dot_general(a, b, (((1,), (1,)), ((), ())), preferred_element_type=F32)


def _dot(a, b):
    return jnp.dot(a, b, preferred_element_type=F32)


def _split2(x):
    hi = x.astype(BF16)
    lo = (x - hi.astype(F32)).astype(BF16)
    return hi, lo


def _split3(x):
    hi = x.astype(BF16)
    r = x - hi.astype(F32)
    mid = r.astype(BF16)
    lo = (r - mid.astype(F32)).astype(BF16)
    return hi, mid, lo


def _rel_bucket_np(dist):
    n = np.maximum(np.asarray(dist, np.int64), 0)
    max_exact = REL_BUCKETS // 2
    nf = np.maximum(n, max_exact).astype(np.float32)
    large = max_exact + (np.log(nf / np.float32(max_exact)) / np.float32(math.log(REL_MAX_DIST / max_exact))
                         * np.float32(REL_BUCKETS - max_exact)).astype(np.int32)
    large = np.minimum(large, REL_BUCKETS - 1)
    return np.where(n < max_exact, n, large).astype(np.int32)


def _bias_lookup_kernel(tab_ref, idx_ref, o_ref, *, nh):
    idx = idx_ref[...]
    for h in range(nh):
        acc = jnp.full(idx.shape, NEG, F32)
        for b in range(REL_BUCKETS):
            acc = jnp.where(idx == b, tab_ref[b, h], acc)
        o_ref[h] = acc


def bias_lookup(tab, idx_np):
    r, c = idx_np.shape
    nh = tab.shape[1]
    tr = 8 if (c >= 2048 and r % 8 == 0) else (128 if r % 128 == 0 else r)
    return pl.pallas_call(
        functools.partial(_bias_lookup_kernel, nh=nh),
        grid=(r // tr,),
        in_specs=[pl.BlockSpec(memory_space=pltpu.SMEM),
                  pl.BlockSpec((tr, c), lambda i: (i, 0))],
        out_specs=pl.BlockSpec((nh, tr, c), lambda i: (0, i, 0)),
        out_shape=jax.ShapeDtypeStruct((nh, r, c), F32),
        compiler_params=_cparams("parallel"),
        name="bias_lookup",
    )(tab, jnp.asarray(idx_np.astype(np.int32)))


def _norm_matmul_kernel(x_ref, g_ref, w_ref, o_ref):
    x = x_ref[...]
    y = x * lax.rsqrt(jnp.mean(x * x, axis=-1, keepdims=True) + EPS) * g_ref[...]
    o_ref[...] = _dot(y.astype(BF16), w_ref[...])


def norm_matmul(x, g, w):
    t, d = x.shape
    n = w.shape[1]
    tm = min(t, 256)
    return pl.pallas_call(
        _norm_matmul_kernel,
        grid=(t // tm,),
        in_specs=[pl.BlockSpec((tm, d), lambda i: (i, 0)),
                  pl.BlockSpec((1, d), lambda i: (0, 0)),
                  pl.BlockSpec((d, n), lambda i: (0, 0))],
        out_specs=pl.BlockSpec((tm, n), lambda i: (i, 0)),
        out_shape=jax.ShapeDtypeStruct((t, n), F32),
        compiler_params=_cparams("parallel"),
        name="norm_matmul",
    )(x, g.reshape(1, d), w)


def _mix_ffn_kernel(x_ref, m1_ref, m2_ref, wo1_ref, wo2_ref, g_ref, wg_ref, wu_ref, wd_ref, o_ref,
                    x1_ref, hf_ref, acc_ref):
    j = pl.program_id(1)

    @pl.when(j == 0)
    def _():
        x1 = x_ref[...] + _dot(m1_ref[...], wo1_ref[...]) + _dot(m2_ref[...], wo2_ref[...])
        x1_ref[...] = x1
        hf = x1 * lax.rsqrt(jnp.mean(x1 * x1, axis=-1, keepdims=True) + EPS) * g_ref[...]
        hf_ref[...] = hf.astype(BF16)
        acc_ref[...] = jnp.zeros_like(acc_ref)

    hf = hf_ref[...]
    a = _dot(hf, wg_ref[...])
    u = _dot(hf, wu_ref[...])
    act = (a * jax.nn.sigmoid(a) * u).astype(BF16)
    acc_ref[...] += _dot(act, wd_ref[...])

    @pl.when(j == pl.num_programs(1) - 1)
    def _():
        o_ref[...] = x1_ref[...] + acc_ref[...]


def mix_ffn(x, m1, m2, wo1, wo2, g, wg, wu, wd):
    t, d = x.shape
    hid = wg.shape[1]
    tm = min(t, 512)
    th = 256
    d1, d2 = m1.shape[1], m2.shape[1]
    return pl.pallas_call(
        _mix_ffn_kernel,
        grid=(t // tm, hid // th),
        in_specs=[pl.BlockSpec((tm, d), lambda i, j: (i, 0)),
                  pl.BlockSpec((tm, d1), lambda i, j: (i, 0)),
                  pl.BlockSpec((tm, d2), lambda i, j: (i, 0)),
                  pl.BlockSpec((d1, d), lambda i, j: (0, 0)),
                  pl.BlockSpec((d2, d), lambda i, j: (0, 0)),
                  pl.BlockSpec((1, d), lambda i, j: (0, 0)),
                  pl.BlockSpec((d, th), lambda i, j: (0, j)),
                  pl.BlockSpec((d, th), lambda i, j: (0, j)),
                  pl.BlockSpec((th, d), lambda i, j: (j, 0))],
        out_specs=pl.BlockSpec((tm, d), lambda i, j: (i, 0)),
        out_shape=jax.ShapeDtypeStruct((t, d), F32),
        scratch_shapes=[pltpu.VMEM((tm, d), F32), pltpu.VMEM((tm, d), BF16), pltpu.VMEM((tm, d), F32)],
        compiler_params=_cparams("parallel", "arbitrary"),
        name="mix_ffn",
    )(x, m1, m2, wo1, wo2, g.reshape(1, d), wg, wu, wd)


def _conv_kernel(z_ref, hist_ref, w_ref, b_ref, lg_ref, lb_ref, o_ref, tail_ref, uh_ref, *, ts, c):
    t = pl.program_id(1)

    @pl.when(t == 0)
    def _():
        uh_ref[0:HIST_PAD, :] = hist_ref[0]

    z = z_ref[0]
    u = z[:, 0:c] * jax.nn.sigmoid(z[:, c:2 * c])
    uh_ref[HIST_PAD:HIST_PAD + ts, :] = u
    acc = jnp.zeros((ts, c), F32) + b_ref[...]
    off = HIST_PAD - (CONV_WIDTH - 1)
    for j in range(CONV_WIDTH):
        acc = acc + w_ref[j:j + 1, :] * uh_ref[off + j:off + j + ts, :]
    mu = jnp.mean(acc, axis=-1, keepdims=True)
    var = jnp.mean(jnp.square(acc - mu), axis=-1, keepdims=True)
    yn = (acc - mu) * lax.rsqrt(var + EPS) * lg_ref[...] + lb_ref[...]
    o_ref[0] = (yn * jax.nn.sigmoid(yn)).astype(BF16)
    tail = uh_ref[ts:ts + HIST_PAD, :]
    uh_ref[0:HIST_PAD, :] = tail
    tail_ref[0] = tail


def conv_module(z3, hist, conv_w, conv_b, ln_g, ln_b):
    nb, l, _ = z3.shape
    c = conv_w.shape[1]
    ts = min(l, 512)
    hist_p = jnp.pad(hist, ((0, 0), (HIST_PAD - (CONV_WIDTH - 1), 0), (0, 0)))
    w_p = jnp.pad(conv_w, ((0, HIST_PAD - CONV_WIDTH), (0, 0)))
    a_out, tail = pl.pallas_call(
        functools.partial(_conv_kernel, ts=ts, c=c),
        grid=(nb, l // ts),
        in_specs=[pl.BlockSpec((1, ts, 2 * c), lambda n, t: (n, t, 0)),
                  pl.BlockSpec((1, HIST_PAD, c), lambda n, t: (n, 0, 0)),
                  pl.BlockSpec((HIST_PAD, c), lambda n, t: (0, 0)),
                  pl.BlockSpec((1, c), lambda n, t: (0, 0)),
                  pl.BlockSpec((1, c), lambda n, t: (0, 0)),
                  pl.BlockSpec((1, c), lambda n, t: (0, 0))],
        out_specs=[pl.BlockSpec((1, ts, c), lambda n, t: (n, t, 0)),
                   pl.BlockSpec((1, HIST_PAD, c), lambda n, t: (n, 0, 0))],
        out_shape=[jax.ShapeDtypeStruct((nb, l, c), BF16),
                   jax.ShapeDtypeStruct((nb, HIST_PAD, c), F32)],
        scratch_shapes=[pltpu.VMEM((HIST_PAD + ts, c), F32)],
        compiler_params=_cparams("parallel", "arbitrary"),
        name="conv_module",
    )(z3, hist_p, w_p, conv_b.reshape(1, c), ln_g.reshape(1, c), ln_b.reshape(1, c))
    return a_out, tail[:, HIST_PAD - (CONV_WIDTH - 1):]


def _seg_mean_sq(x, seg_ref):
    hi, lo = _split2(x * x)
    seg = seg_ref[...]
    return (_dot(hi, seg) + _dot(lo, seg)) * (1.0 / HEAD_DIM)


def _head_norm(x, g, seg_ref):
    return x * lax.rsqrt(_seg_mean_sq(x, seg_ref) + EPS) * g


def _seg_matrix(width):
    idx = np.arange(width) // HEAD_DIM
    return jnp.asarray((idx[:, None] == idx[None, :]).astype(np.float32), BF16)


def _ab_qkv_kernel(q_ref, k_ref, v_ref, qg_ref, kg_ref, seg_ref, qo_ref, kvo_ref, *, w):
    qo_ref[...] = (_head_norm(q_ref[...], qg_ref[...], seg_ref) * HEAD_DIM ** -0.5).astype(BF16)
    kvo_ref[0, :, 0:w] = _head_norm(k_ref[...], kg_ref[...], seg_ref)
    kvo_ref[0, :, w:2 * w] = v_ref[...]


def ab_qkv(z, qn, kn, col0):
    t = z.shape[0]
    w = B_HPG * HEAD_DIM
    ng = len(WINDOWS)
    tm = min(t, 512)
    cb = col0 // w
    return pl.pallas_call(
        functools.partial(_ab_qkv_kernel, w=w),
        grid=(t // tm, ng),
        in_specs=[pl.BlockSpec((tm, w), lambda i, g: (i, cb + g)),
                  pl.BlockSpec((tm, w), lambda i, g: (i, cb + ng + g)),
                  pl.BlockSpec((tm, w), lambda i, g: (i, cb + 2 * ng + g)),
                  pl.BlockSpec((1, w), lambda i, g: (0, 0)),
                  pl.BlockSpec((1, w), lambda i, g: (0, 0)),
                  pl.BlockSpec((w, w), lambda i, g: (0, 0))],
        out_specs=[pl.BlockSpec((tm, w), lambda i, g: (i, g)),
                   pl.BlockSpec((1, tm, 2 * w), lambda i, g: (g, i, 0))],
        out_shape=[jax.ShapeDtypeStruct((t, ng * w), BF16),
                   jax.ShapeDtypeStruct((ng, t, 2 * w), F32)],
        compiler_params=_cparams("parallel", "parallel"),
        name="ab_qkv",
    )(z, z, z, jnp.tile(qn, B_HPG).reshape(1, w), jnp.tile(kn, B_HPG).reshape(1, w), _seg_matrix(w))


def _win_prompt_kernel(q_ref, kvp_ref, kvc_ref, bias_ref, o_ref, lse_ref, *, w):
    blk = pl.program_id(2)
    q = q_ref[0]
    kvp = kvp_ref[0, 0]
    kvc = kvc_ref[0, 0]
    col = lax.broadcasted_iota(I32, (SW_BLOCK, 2 * SW_BLOCK), 1)
    first = jnp.logical_and(blk == 0, col < SW_BLOCK)
    for h in range(B_HPG):
        hs = slice(h * HEAD_DIM, (h + 1) * HEAD_DIM)
        vs = slice(w + h * HEAD_DIM, w + (h + 1) * HEAD_DIM)
        k2 = jnp.concatenate([kvp[:, hs], kvc[:, hs]], axis=0).astype(BF16)
        v2 = jnp.concatenate([kvp[:, vs], kvc[:, vs]], axis=0).astype(BF16)
        s = _nt(q[:, hs], k2) + bias_ref[h]
        s = jnp.where(first, NEG, s)
        m = jnp.max(s, axis=-1, keepdims=True)
        p = jnp.exp(s - m)
        l = jnp.sum(p, axis=-1, keepdims=True)
        o_ref[0, :, hs] = _dot(p.astype(BF16), v2) / l
        lse_ref[0, :, hs] = jnp.broadcast_to(m + jnp.log(l), (SW_BLOCK, HEAD_DIM))


def _win_prompt_bias(tab_g, dil, reach):
    ql = np.arange(SW_BLOCK)[:, None]
    kl = np.arange(2 * SW_BLOCK)[None, :] - SW_BLOCK
    rel = ql - kl
    ok = (rel >= 0) & (rel <= reach)
    return bias_lookup(tab_g, np.where(ok, _rel_bucket_np(rel * dil), -1))


def win_prompt(q, kv, g, tab_g):
    b, s, _ = q.shape
    w = B_HPG * HEAD_DIM
    ng = len(WINDOWS)
    dil = DILATIONS[g]
    n = s // dil
    nb = n // SW_BLOCK
    assert nb * SW_BLOCK * dil == s
    qv = q.reshape(b, n, dil * ng * w)
    kvv = kv.reshape(ng, b, n, dil * 2 * w)
    bias = _win_prompt_bias(tab_g, dil, WINDOWS[g] // dil)
    o, lse = pl.pallas_call(
        functools.partial(_win_prompt_kernel, w=w),
        grid=(b, dil, nb),
        in_specs=[pl.BlockSpec((1, SW_BLOCK, w), lambda n_, r, k: (n_, k, ng * r + g)),
                  pl.BlockSpec((1, 1, SW_BLOCK, 2 * w), lambda n_, r, k: (g, n_, jnp.maximum(k - 1, 0), r)),
                  pl.BlockSpec((1, 1, SW_BLOCK, 2 * w), lambda n_, r, k: (g, n_, k, r)),
                  pl.BlockSpec((B_HPG, SW_BLOCK, 2 * SW_BLOCK), lambda n_, r, k: (0, 0, 0))],
        out_specs=[pl.BlockSpec((1, SW_BLOCK, w), lambda n_, r, k: (n_, k, r)),
                   pl.BlockSpec((1, SW_BLOCK, w), lambda n_, r, k: (n_, k, r))],
        out_shape=[jax.ShapeDtypeStruct((b, n, dil * w), F32),
                   jax.ShapeDtypeStruct((b, n, dil * w), F32)],
        compiler_params=_cparams("parallel", "parallel", "arbitrary"),
        name=f"win_prompt_g{g}",
    )(qv, kvv, kvv, bias)
    return o.reshape(b, s, w), lse.reshape(b, s, w)


ROWS8 = 8


def _win_step_kernel(q_ref, kvn_ref, newt_ref, buf_ref, tb_ref, tn_ref, o_ref, lse_ref, st_ref,
                     qbd_ref, newpage_ref, *, w, wb, l):
    n = pl.program_id(0)

    @pl.when(n == 0)
    def _():
        newpage_ref[...] = jnp.zeros(newpage_ref.shape, F32)

    q = q_ref[0]
    lane_h = lax.broadcasted_iota(I32, (ROWS8, w), 1) // HEAD_DIM
    for h in range(B_HPG):
        qbd_ref[h * ROWS8:(h + 1) * ROWS8, :] = jnp.where(lane_h == h, q, jnp.zeros_like(q))
    newpage_ref[0:l, :] = kvn_ref[0, 0]
    qbd = qbd_ref[...]
    newp = newpage_ref[...]
    buf = buf_ref[0, 0]
    kt = buf[0].reshape(w, wb).astype(BF16)
    vt = buf[1].reshape(w, wb).astype(BF16)
    s1 = _dot(qbd, kt) + tb_ref[...]
    s2 = _nt(qbd, newp[:, 0:w].astype(BF16)) + tn_ref[...]
    m = jnp.maximum(jnp.max(s1, axis=-1, keepdims=True), jnp.max(s2, axis=-1, keepdims=True))
    p1 = jnp.exp(s1 - m)
    p2 = jnp.exp(s2 - m)
    den = jnp.sum(p1, axis=-1, keepdims=True) + jnp.sum(p2, axis=-1, keepdims=True)
    num = _nt(p1.astype(BF16), vt) + _dot(p2.astype(BF16), newp[:, w:2 * w].astype(BF16))
    o = num / den
    lse = m + jnp.log(den)
    for h in range(B_HPG):
        rs = slice(h * ROWS8, (h + 1) * ROWS8)
        hs = slice(h * HEAD_DIM, (h + 1) * HEAD_DIM)
        o_ref[0, :, hs] = o[rs, hs]
        lse_ref[0, :, hs] = jnp.broadcast_to(lse[rs], (ROWS8, HEAD_DIM))
    rolled = pltpu.roll(buf.reshape(2 * w, wb), wb - l, 1)
    lane = lax.broadcasted_iota(I32, (2 * w, LANES), 1)
    last = jnp.where(lane >= LANES - l, newt_ref[0], rolled[:, wb - LANES:wb])
    if wb > LANES:
        st_ref[0, :, :, :, 0:wb - LANES] = rolled[:, 0:wb - LANES].reshape(2, B_HPG, HEAD_DIM, wb - LANES)
    st_ref[0, :, :, :, wb - LANES:wb] = last.reshape(2, B_HPG, HEAD_DIM, LANES)


def _win_step_bias(tab_g, dil, window, wb, l):
    lq = np.minimum(np.arange(ROWS8), l - 1)[:, None]
    dist_b = wb + lq - np.arange(wb)[None, :]
    ok_b = (dist_b % dil == 0) & (dist_b // dil <= window // dil)
    cols = np.arange(LANES)[None, :]
    dist_n = lq - cols
    ok_n = (cols < l) & (dist_n >= 0) & (dist_n % dil == 0) & (dist_n // dil <= window // dil)
    rows = B_HPG * ROWS8
    tb = bias_lookup(tab_g, np.where(ok_b, _rel_bucket_np(dist_b), -1)).reshape(rows, wb)
    tn = bias_lookup(tab_g, np.where(ok_n, _rel_bucket_np(dist_n), -1)).reshape(rows, LANES)
    return tb, tn


def win_step(q8, kv_new, cache_t, layer, g, tab_g):
    nb = q8.shape[0]
    l = kv_new.shape[2]
    w = B_HPG * HEAD_DIM
    wb = cache_t.shape[-1]
    window, dil = WINDOWS[g], DILATIONS[g]
    assert wb == window, "the step kernel keeps a full window of rows"
    tb, tn = _win_step_bias(tab_g, dil, window, wb, l)
    rows = B_HPG * ROWS8
    new_t = jnp.pad(kv_new[g].transpose(0, 2, 1), ((0, 0), (0, 0), (LANES - l, 0)))
    o, lse, st = pl.pallas_call(
        functools.partial(_win_step_kernel, w=w, wb=wb, l=l),
        grid=(nb,),
        in_specs=[pl.BlockSpec((1, ROWS8, w), lambda n: (n, 0, g)),
                  pl.BlockSpec((1, 1, l, 2 * w), lambda n: (g, n, 0, 0)),
                  pl.BlockSpec((1, 2 * w, LANES), lambda n: (n, 0, 0)),
                  pl.BlockSpec((1, 1, 2, B_HPG, HEAD_DIM, wb), lambda n: (layer, n, 0, 0, 0, 0)),
                  pl.BlockSpec((rows, wb), lambda n: (0, 0)),
                  pl.BlockSpec((rows, LANES), lambda n: (0, 0))],
        out_specs=[pl.BlockSpec((1, ROWS8, w), lambda n: (n, 0, 0)),
                   pl.BlockSpec((1, ROWS8, w), lambda n: (n, 0, 0)),
                   pl.BlockSpec((1, 2, B_HPG, HEAD_DIM, wb), lambda n: (n, 0, 0, 0, 0))],
        out_shape=[jax.ShapeDtypeStruct((nb, ROWS8, w), F32),
                   jax.ShapeDtypeStruct((nb, ROWS8, w), F32),
                   jax.ShapeDtypeStruct((nb, 2, B_HPG, HEAD_DIM, wb), F32)],
        scratch_shapes=[pltpu.VMEM((rows, w), BF16), pltpu.VMEM((LANES, 2 * w), F32)],
        compiler_params=_cparams("arbitrary"),
        name=f"win_step_g{g}",
    )(q8, kv_new, new_t, cache_t, tb, tn)
    return o[:, :l], lse[:, :l], st


def _ab_merge_kernel(o0, o1, o2, l0, l1, l2, out_ref):
    a, b, c = l0[...], l1[...], l2[...]
    m = jnp.maximum(jnp.maximum(a, b), c)
    ea, eb, ec = jnp.exp(a - m), jnp.exp(b - m), jnp.exp(c - m)
    out_ref[...] = ((ea * o0[...] + eb * o1[...] + ec * o2[...]) / (ea + eb + ec)).astype(BF16)


def ab_merge(outs, lses):
    t, w = outs[0].shape
    tm = min(t, 1024)
    spec = pl.BlockSpec((tm, w), lambda i: (i, 0))
    return pl.pallas_call(
        _ab_merge_kernel,
        grid=(t // tm,),
        in_specs=[spec] * 6,
        out_specs=spec,
        out_shape=jax.ShapeDtypeStruct((t, w), BF16),
        compiler_params=_cparams("parallel"),
        name="ab_merge",
    )(*outs, *lses)


CD_COLS = dict(cq=0, ck=256, cv=512, cgate=1024, dq=1536, dk=2048, dv=2560, iq=3072, ik=3328, clr=3392, iw=3408)
CD_PAD = 3456
SMALL_BLOCK = CD_COLS["ik"] // LANES
SM_IK, SM_CLR, SM_IW = 0, CD_COLS["clr"] - CD_COLS["ik"], CD_COLS["iw"] - CD_COLS["ik"]


def cd_reorder_w(w_in):
    sizes = (256, 256, 512, 16, 512, 512, 512, 512, 256, 64, 4)
    names = ("cq", "ck", "cv", "clr", "cgate", "dq", "dk", "dv", "iq", "ik", "iw")
    starts = np.concatenate([[0], np.cumsum(sizes)[:-1]])
    out = jnp.zeros((w_in.shape[0], CD_PAD), w_in.dtype)
    for nm, st, sz in zip(names, starts, sizes):
        out = lax.dynamic_update_slice(out, w_in[:, st:st + sz], (0, CD_COLS[nm]))
    return out


def _cd_pre_kernel(d_ref, sm_ref, wa_ref, ba_ref, qg_ref, kg_ref, seg_ref, g_ref, dq_ref, kv_ref, kb_ref, vb_ref, *, w):
    d = d_ref[...]
    sm = sm_ref[...]
    clr = sm[:, SM_CLR:SM_CLR + GLA_GATE_RANK]
    wa = wa_ref[...]
    pre = ba_ref[...]
    for part in _split3(clr):
        pre = pre + _dot(part, wa[0]) + _dot(part, wa[1])
    lsig = jnp.minimum(pre, 0.0) - jnp.log(1.0 + jnp.exp(-jnp.abs(pre)))
    g_ref[...] = lsig * (1.0 / GLA_TAU)
    dq_ref[...] = (_head_norm(d[:, 0:w], qg_ref[...], seg_ref) * HEAD_DIM ** -0.5).astype(BF16)
    kn = _head_norm(d[:, w:2 * w], kg_ref[...], seg_ref)
    v = d[:, 2 * w:3 * w]
    kv_ref[:, 0:w] = kn
    kv_ref[:, w:2 * w] = v
    kb_ref[...] = kn.astype(BF16)
    vb_ref[...] = v.astype(BF16)


def cd_pre(z, wa2, ba, qn, kn):
    t = z.shape[0]
    w = DSA_HEADS * HEAD_DIM
    gw = GLA_HEADS * GLA_DK
    tm = min(t, 512)
    wa_hi, wa_lo = _split2(wa2)
    return pl.pallas_call(
        functools.partial(_cd_pre_kernel, w=w),
        grid=(t // tm,),
        in_specs=[pl.BlockSpec((tm, 3 * w), lambda i: (i, CD_COLS["dq"] // (3 * w))),
                  pl.BlockSpec((tm, LANES), lambda i: (i, SMALL_BLOCK)),
                  pl.BlockSpec((2, GLA_GATE_RANK, gw), lambda i: (0, 0, 0)),
                  pl.BlockSpec((1, gw), lambda i: (0, 0)),
                  pl.BlockSpec((1, w), lambda i: (0, 0)),
                  pl.BlockSpec((1, w), lambda i: (0, 0)),
                  pl.BlockSpec((w, w), lambda i: (0, 0))],
        out_specs=[pl.BlockSpec((tm, gw), lambda i: (i, 0)),
                   pl.BlockSpec((tm, w), lambda i: (i, 0)),
                   pl.BlockSpec((tm, 2 * w), lambda i: (i, 0)),
                   pl.BlockSpec((tm, w), lambda i: (i, 0)),
                   pl.BlockSpec((tm, w), lambda i: (i, 0))],
        out_shape=[jax.ShapeDtypeStruct((t, gw), F32),
                   jax.ShapeDtypeStruct((t, w), BF16),
                   jax.ShapeDtypeStruct((t, 2 * w), F32),
                   jax.ShapeDtypeStruct((t, w), BF16),
                   jax.ShapeDtypeStruct((t, w), BF16)],
        compiler_params=_cparams("parallel"),
        name="cd_pre",
    )(z, z, jnp.stack([wa_hi, wa_lo]), ba.reshape(1, gw), jnp.tile(qn, DSA_HEADS).reshape(1, w),
      jnp.tile(kn, DSA_HEADS).reshape(1, w), _seg_matrix(w))


def _gla_consts(c):
    levels = []
    s = c // 2
    while s >= 1:
        levels.append(s)
        s //= 2
    i = np.arange(c)
    tri = (i[None, :] <= i[:, None]).astype(np.float32)
    mats = [tri]
    masks = []
    for s in levels:
        ref = (i // (2 * s)) * (2 * s) + s - 1
        r = (i[None, :] <= ref[:, None]).astype(np.float32)
        mats.append(tri - r)
        same = (i[:, None] // (2 * s)) == (i[None, :] // (2 * s))
        masks.append(same & ((i[:, None] % (2 * s)) >= s) & ((i[None, :] % (2 * s)) < s))
    masks.append(i[:, None] == i[None, :])
    mstack = np.concatenate(mats, axis=0)
    mask = np.stack([np.tile(m.astype(np.float32), (1, GLA_HEADS)) for m in masks])
    return jnp.asarray(mstack, BF16), jnp.asarray(mask, F32), len(levels)


def _gla_kernel(qk_ref, v_ref, gate_ref, g_ref, s0_ref, mstack_ref, mask_ref, gn_ref, eye_ref,
                o_ref, sf_ref, st_ref, *, c, nl):
    t = pl.program_id(1)
    kw = GLA_HEADS * GLA_DK
    vw = GLA_HEADS * GLA_DV
    lane_k = lax.broadcasted_iota(I32, (1, kw), 1) // GLA_DK
    lane_v = lax.broadcasted_iota(I32, (1, vw), 1) // GLA_DV

    @pl.when(t == 0)
    def _():
        rows = []
        for h in range(GLA_HEADS):
            z = jnp.zeros((GLA_DK, GLA_DV), F32)
            rows.append(jnp.concatenate([s0_ref[0, h] if hh == h else z for hh in range(GLA_HEADS)], axis=1))
        st_ref[...] = jnp.concatenate(rows, axis=0).T

    qk = qk_ref[0]
    q = qk[:, 0:kw] * GLA_DK ** -0.5
    k = qk[:, kw:2 * kw]
    v = v_ref[0]
    mstack = mstack_ref[...]
    r = None
    for part in _split3(g_ref[0]):
        d = _dot(mstack, part)
        r = d if r is None else r + d
    b = r[0:c]

    def expand_k(x):
        return jnp.concatenate([jnp.where(lane_k == h, x, 0.0) for h in range(GLA_HEADS)], axis=0).astype(BF16)

    st = st_ref[...]
    o = _nt((q * jnp.exp(b)).astype(BF16), st.astype(BF16))
    a = mask_ref[nl] * _nt(q.astype(BF16), expand_k(k))
    for lv in range(nl):
        sc = jnp.exp(-jnp.abs(r[(lv + 1) * c:(lv + 2) * c]))
        a = a + mask_ref[lv] * _nt((q * sc).astype(BF16), expand_k(k * sc))
    vexp = jnp.concatenate([jnp.where(lane_v == h, v, 0.0) for h in range(GLA_HEADS)], axis=0).astype(BF16)
    o = o + _dot(a.astype(BF16), vexp)

    blast = b[c - 1:c]
    kt = (k * jnp.exp(blast - b)).astype(BF16)
    vt = _nt(eye_ref[...], v.astype(BF16)).astype(BF16)
    upd = _dot(vt, kt)
    row_h = lax.broadcasted_iota(I32, (vw, 1), 0) // GLA_DV
    st_new = st * jnp.exp(blast) + jnp.where(row_h == lane_k, upd, 0.0)
    st_ref[...] = st_new

    gate = gate_ref[0]
    gn = gn_ref[...]
    for h in range(GLA_HEADS):
        hs = slice(h * GLA_DV, (h + 1) * GLA_DV)
        oh = o[:, hs]
        y = oh * lax.rsqrt(jnp.mean(oh * oh, axis=-1, keepdims=True) + EPS) * gn
        gh = gate[:, hs]
        o_ref[0, :, hs] = (y * (gh * jax.nn.sigmoid(gh))).astype(BF16)

    @pl.when(t == pl.num_programs(1) - 1)
    def _():
        s_t = st_new.T
        for h in range(GLA_HEADS):
            sf_ref[0, h] = s_t[h * GLA_DK:(h + 1) * GLA_DK, h * GLA_DV:(h + 1) * GLA_DV]


def gla(z3, g3, s0, gla_norm):
    nb, l, _ = z3.shape
    c = GLA_CHUNK
    assert l % c == 0
    kw = GLA_HEADS * GLA_DK
    vw = GLA_HEADS * GLA_DV
    mstack, mask, nl = _gla_consts(c)
    eye = jnp.eye(vw, dtype=BF16)
    o, sf = pl.pallas_call(
        functools.partial(_gla_kernel, c=c, nl=nl),
        grid=(nb, l // c),
        in_specs=[pl.BlockSpec((1, c, 2 * kw), lambda n, t: (n, t, 0)),
                  pl.BlockSpec((1, c, vw), lambda n, t: (n, t, CD_COLS["cv"] // vw)),
                  pl.BlockSpec((1, c, vw), lambda n, t: (n, t, CD_COLS["cgate"] // vw)),
                  pl.BlockSpec((1, c, kw), lambda n, t: (n, t, 0)),
                  pl.BlockSpec((1, GLA_HEADS, GLA_DK, GLA_DV), lambda n, t: (n, 0, 0, 0)),
                  pl.BlockSpec(mstack.shape, lambda n, t: (0, 0)),
                  pl.BlockSpec(mask.shape, lambda n, t: (0, 0, 0)),
                  pl.BlockSpec((1, GLA_DV), lambda n, t: (0, 0)),
                  pl.BlockSpec((vw, vw), lambda n, t: (0, 0))],
        out_specs=[pl.BlockSpec((1, c, vw), lambda n, t: (n, t, 0)),
                   pl.BlockSpec((1, GLA_HEADS, GLA_DK, GLA_DV), lambda n, t: (n, 0, 0, 0))],
        out_shape=[jax.ShapeDtypeStruct((nb, l, vw), BF16),
                   jax.ShapeDtypeStruct((nb, GLA_HEADS, GLA_DK, GLA_DV), F32)],
        scratch_shapes=[pltpu.VMEM((vw, kw), F32)],
        compiler_params=_cparams("parallel", "arbitrary"),
        name="gla",
    )(z3, z3, z3, g3, s0, mstack, mask, gla_norm.reshape(1, GLA_DV), eye)
    return o, sf


def _sort_key(x):
    bits = pltpu.bitcast(x, I32)
    return jnp.where(bits < 0, (bits ^ 0x7FFFFFFF) + 1, bits)


def _idx_lhs(iq, h):
    hi, lo = _split2(iq[:, h * IDX_DIM:(h + 1) * IDX_DIM])
    return jnp.concatenate([hi, hi, lo, jnp.zeros_like(hi)], axis=1)


def _idx_rhs(ik):
    hi, lo = _split2(ik)
    return jnp.concatenate([hi, lo, hi, jnp.zeros_like(hi)], axis=1)


def _idx_scores(lhs, rhs, wcol):
    sc = None
    for h in range(IDX_HEADS):
        d = jnp.maximum(_nt(lhs[h], rhs) * IDX_DIM ** -0.5, 0.0) * (wcol[h] * IDX_HEADS ** -0.5)
        sc = d if sc is None else sc + d
    return sc


def _dsa_select_kernel(iq_ref, smq_ref, smk_ref, triu_ref, mask_ref, kb3_ref, sc_ref, *, nc, topk):
    i = pl.program_id(1)
    ck = DSA_CK
    qb = DSA_QB
    bpc = ck // qb

    @pl.when(i == 0)
    def _():
        for c in range(nc):
            kb3_ref[c] = _idx_rhs(smk_ref[0, c * ck:(c + 1) * ck, SM_IK:SM_IK + IDX_DIM])

    iq = iq_ref[0]
    smq = smq_ref[0]
    lhs = [_idx_lhs(iq, h) for h in range(IDX_HEADS)]
    wcol = [smq[:, SM_IW + h:SM_IW + h + 1] for h in range(IDX_HEADS)]
    nch = (i + bpc) // bpc
    dcr = lax.broadcasted_iota(I32, (qb, ck), 1) - lax.broadcasted_iota(I32, (qb, ck), 0)

    def score_body(c, carry):
        key = _sort_key(_idx_scores(lhs, kb3_ref[c], wcol))
        sc_ref[c] = jnp.where(dcr <= i * qb - c * ck, key, INT_MIN)
        return carry

    lax.fori_loop(0, nch, score_body, 0)

    def count(pred):
        def body(c, acc):
            m = jnp.where(pred(sc_ref[c]), 1, 0)
            for j in range(bpc):
                acc = acc + m[:, j * LANES:(j + 1) * LANES]
            return acc
        acc = lax.fori_loop(0, nch, body, jnp.zeros((qb, LANES), I32))
        return jnp.sum(acc, axis=1, keepdims=True)

    qpos = i * qb + lax.broadcasted_iota(I32, (qb, 1), 0)
    kk = jnp.minimum(topk, qpos + 1)

    def bit_body(t, carry):
        thr, cge = carry
        cand = thr + lax.shift_left(jnp.int32(1), 31 - t)
        cnt = count(lambda blk: blk >= cand)
        ok = cnt >= kk
        return jnp.where(ok, cand, thr), jnp.where(ok, cnt, cge)

    thr, cge = lax.fori_loop(0, 32, bit_body, (jnp.full((qb, 1), INT_MIN, I32), jnp.zeros((qb, 1), I32)))
    cgt = count(lambda blk: blk > thr)
    need = kk - cgt
    tied = jnp.max(jnp.where(need < cge - cgt, 1, 0))

    @pl.when(tied > 0)
    def _():
        needf = need.astype(F32)

        def tie_body(c, off):
            blk = sc_ref[c]
            eq = blk == thr
            pref = _dot(jnp.where(eq, 1.0, 0.0).astype(BF16), triu_ref[...])
            drop = jnp.logical_and(eq, pref + off > needf)
            sc_ref[c] = jnp.where(drop, INT_MIN, blk)
            return off + pref[:, ck - 1:ck]

        lax.fori_loop(0, nch, tie_body, jnp.zeros((qb, 1), F32))

    def live_body(c, carry):
        mask_ref[0, 0, c] = jnp.where(sc_ref[c] >= thr, 0.0, NEG).astype(BF16)
        return carry

    def dead_body(c, carry):
        mask_ref[0, 0, c] = jnp.full((qb, ck), NEG, BF16)
        return carry

    lax.fori_loop(0, nch, live_body, 0)
    lax.fori_loop(nch, nc, dead_body, 0)


def dsa_select(z3):
    b, s, _ = z3.shape
    nq, nc = s // DSA_QB, s // DSA_CK
    topk = min(DSA_TOPK_MAX, s // 4)
    triu = jnp.asarray(np.triu(np.ones((DSA_CK, DSA_CK), np.float32)), BF16)
    return pl.pallas_call(
        functools.partial(_dsa_select_kernel, nc=nc, topk=topk),
        grid=(b, nq),
        in_specs=[pl.BlockSpec((1, DSA_QB, IDX_HEADS * IDX_DIM), lambda n, i: (n, i, CD_COLS["iq"] // (IDX_HEADS * IDX_DIM))),
                  pl.BlockSpec((1, DSA_QB, LANES), lambda n, i: (n, i, SMALL_BLOCK)),
                  pl.BlockSpec((1, s, LANES), lambda n, i: (n, 0, SMALL_BLOCK)),
                  pl.BlockSpec((DSA_CK, DSA_CK), lambda n, i: (0, 0))],
        out_specs=pl.BlockSpec((1, 1, nc, DSA_QB, DSA_CK), lambda n, i: (n, i, 0, 0, 0)),
        out_shape=jax.ShapeDtypeStruct((b, nq, nc, DSA_QB, DSA_CK), BF16),
        scratch_shapes=[pltpu.VMEM((nc, DSA_CK, 4 * IDX_DIM), BF16), pltpu.VMEM((nc, DSA_QB, DSA_CK), I32)],
        compiler_params=_cparams("parallel", "arbitrary"),
        name="dsa_select",
    )(z3, z3, z3, triu)


def _dsa_bias_tiles(tab_d):
    o = 0
    while _rel_bucket_np(max(o * DSA_QB - (DSA_QB - 1), 0)) < REL_BUCKETS - 1:
        o += 1
    offs = np.arange(o + 1)[:, None, None] * DSA_QB
    d = offs + np.arange(DSA_QB)[None, :, None] - np.arange(DSA_QB)[None, None, :]
    tiles = bias_lookup(tab_d, _rel_bucket_np(d).reshape((o + 1) * DSA_QB, DSA_QB))
    return tiles.reshape(DSA_HEADS, o + 1, DSA_QB, DSA_QB)


def _dsa_attn_kernel(qi_ref, kc_ref, q_ref, k_ref, v_ref, mask_ref, bt_ref, o_ref, m_ref, l_ref, acc_ref, *, n_off):
    s_id = pl.program_id(1)
    i = qi_ref[s_id]
    c = kc_ref[s_id]
    qb, ck = DSA_QB, DSA_CK
    bpc = ck // qb

    @pl.when(c == 0)
    def _():
        m_ref[...] = jnp.full(m_ref.shape, NEG, F32)
        l_ref[...] = jnp.zeros(l_ref.shape, F32)
        acc_ref[...] = jnp.zeros(acc_ref.shape, F32)

    madd = mask_ref[0, 0, 0].astype(F32)
    q = q_ref[0]
    k = k_ref[0]
    v = v_ref[0]
    offs = [jnp.clip(i - (bpc * c + t), 0, n_off - 1) for t in range(bpc)]
    for h in range(DSA_HEADS):
        hs = slice(h * HEAD_DIM, (h + 1) * HEAD_DIM)
        bias = jnp.concatenate([bt_ref[h, offs[t]] for t in range(bpc)], axis=1)
        s = _nt(q[:, hs], k[:, hs]) + bias + madd
        m_old = m_ref[h][:, 0:1]
        m_new = jnp.maximum(m_old, jnp.max(s, axis=-1, keepdims=True))
        alpha = jnp.exp(m_old - m_new)
        p = jnp.exp(s - m_new)
        l_ref[h] = jnp.broadcast_to(alpha * l_ref[h][:, 0:1] + jnp.sum(p, axis=-1, keepdims=True), (qb, LANES))
        m_ref[h] = jnp.broadcast_to(m_new, (qb, LANES))
        acc_ref[:, hs] = alpha * acc_ref[:, hs] + _dot(p.astype(BF16), v[:, hs])

    @pl.when(c == i // bpc)
    def _():
        for h in range(DSA_HEADS):
            hs = slice(h * HEAD_DIM, (h + 1) * HEAD_DIM)
            o_ref[0, :, hs] = (acc_ref[:, hs] / l_ref[h][:, 0:1]).astype(BF16)


def dsa_attend(dq, kb, vb, mask, tab_d):
    b, s, w = dq.shape
    nq = s // DSA_QB
    bpc = DSA_CK // DSA_QB
    bt = _dsa_bias_tiles(tab_d)
    n_off = bt.shape[1]
    qi = np.concatenate([np.full(i // bpc + 1, i) for i in range(nq)]).astype(np.int32)
    kc = np.concatenate([np.arange(i // bpc + 1) for i in range(nq)]).astype(np.int32)
    grid_spec = pltpu.PrefetchScalarGridSpec(
        num_scalar_prefetch=2,
        grid=(b, len(qi)),
        in_specs=[pl.BlockSpec((1, DSA_QB, w), lambda n, t, qi_, kc_: (n, qi_[t], 0)),
                  pl.BlockSpec((1, DSA_CK, w), lambda n, t, qi_, kc_: (n, kc_[t], 0)),
                  pl.BlockSpec((1, DSA_CK, w), lambda n, t, qi_, kc_: (n, kc_[t], 0)),
                  pl.BlockSpec((1, 1, 1, DSA_QB, DSA_CK), lambda n, t, qi_, kc_: (n, qi_[t], kc_[t], 0, 0)),
                  pl.BlockSpec(bt.shape, lambda n, t, qi_, kc_: (0, 0, 0, 0))],
        out_specs=pl.BlockSpec((1, DSA_QB, w), lambda n, t, qi_, kc_: (n, qi_[t], 0)),
        scratch_shapes=[pltpu.VMEM((DSA_HEADS, DSA_QB, LANES), F32), pltpu.VMEM((DSA_HEADS, DSA_QB, LANES), F32),
                        pltpu.VMEM((DSA_QB, w), F32)],
    )
    return pl.pallas_call(
        functools.partial(_dsa_attn_kernel, n_off=n_off),
        grid_spec=grid_spec,
        out_shape=jax.ShapeDtypeStruct((b, s, w), BF16),
        compiler_params=_cparams("parallel", "arbitrary"),
        name="dsa_attend",
    )(jnp.asarray(qi), jnp.asarray(kc), dq, kb, vb, mask, bt)


SCORE_PAGES = 16
ATTN_PAGES = 8


def _dsa_step_scores_kernel(pt_ref, iq_ref, sm_ref, *rest, npg):
    k_refs, o_ref = rest[:npg], rest[npg]
    iq = iq_ref[0]
    sm = sm_ref[0]
    lhs = jnp.concatenate([_idx_lhs(iq, h) for h in range(IDX_HEADS)], axis=0)
    wcol = [sm[:, SM_IW + h:SM_IW + h + 1] * IDX_HEADS ** -0.5 for h in range(IDX_HEADS)]
    for j in range(npg):
        hi, lo = _split2(k_refs[j][0, 0])
        d = _dot(lhs, jnp.concatenate([hi, lo, hi, jnp.zeros_like(hi)], axis=0))
        sc = None
        for h in range(IDX_HEADS):
            t = jnp.maximum(d[h * ROWS8:(h + 1) * ROWS8] * IDX_DIM ** -0.5, 0.0) * wcol[h]
            sc = t if sc is None else sc + t
        o_ref[0, :, j * PAGE_SIZE:(j + 1) * PAGE_SIZE] = sc


def dsa_step_scores(z8, kidx_t, layer, page_table):
    nb = z8.shape[0]
    n_pages = page_table.shape[1]
    npg = math.gcd(SCORE_PAGES, n_pages)
    iqw = IDX_HEADS * IDX_DIM

    def page_spec(j):
        return pl.BlockSpec((1, 1, IDX_DIM, PAGE_SIZE),
                            lambda n, p, pt: (layer, pt[n * n_pages + p * npg + j], 0, 0))

    grid_spec = pltpu.PrefetchScalarGridSpec(
        num_scalar_prefetch=1,
        grid=(nb, n_pages // npg),
        in_specs=[pl.BlockSpec((1, ROWS8, iqw), lambda n, p, pt: (n, 0, CD_COLS["iq"] // iqw)),
                  pl.BlockSpec((1, ROWS8, LANES), lambda n, p, pt: (n, 0, SMALL_BLOCK))]
                 + [page_spec(j) for j in range(npg)],
        out_specs=pl.BlockSpec((1, ROWS8, npg * PAGE_SIZE), lambda n, p, pt: (n, 0, p)),
    )
    return pl.pallas_call(
        functools.partial(_dsa_step_scores_kernel, npg=npg),
        grid_spec=grid_spec,
        out_shape=jax.ShapeDtypeStruct((nb, ROWS8, n_pages * PAGE_SIZE), F32),
        compiler_params=_cparams("parallel", "arbitrary"),
        name="dsa_step_scores",
    )(page_table.reshape(-1), z8, z8, *([kidx_t] * npg))


def _dsa_step_select_kernel(sc_ref, iq_ref, sm_ref, triu_ref, mp_ref, mn_ref, key_ref, *, past, l_new, topk):
    ck = DSA_CK
    nck = past // ck
    iq = iq_ref[0]
    sm = sm_ref[0]
    lhs = [_idx_lhs(iq, h) for h in range(IDX_HEADS)]
    wcol = [sm[:, SM_IW + h:SM_IW + h + 1] for h in range(IDX_HEADS)]
    rhs_new = _idx_rhs(jnp.concatenate([sm[:, SM_IK:SM_IK + IDX_DIM], jnp.zeros((LANES - ROWS8, IDX_DIM), F32)], axis=0))
    sc_new = _idx_scores(lhs, rhs_new, wcol)
    row = lax.broadcasted_iota(I32, (ROWS8, LANES), 0)
    col = lax.broadcasted_iota(I32, (ROWS8, LANES), 1)
    key_new = jnp.where(jnp.logical_and(col <= row, col < l_new), _sort_key(sc_new), INT_MIN)
    key_ref[...] = _sort_key(sc_ref[0])
    kk = jnp.minimum(topk, past + 1 + lax.broadcasted_iota(I32, (ROWS8, 1), 0))

    def count(pred):
        return (jnp.sum(jnp.where(pred(key_ref[...]), 1, 0), axis=1, keepdims=True)
                + jnp.sum(jnp.where(pred(key_new), 1, 0), axis=1, keepdims=True))

    def bit_body(t, carry):
        thr, cge = carry
        cand = thr + lax.shift_left(jnp.int32(1), 31 - t)
        cnt = count(lambda x: x >= cand)
        ok = cnt >= kk
        return jnp.where(ok, cand, thr), jnp.where(ok, cnt, cge)

    thr, cge = lax.fori_loop(0, 32, bit_body, (jnp.full((ROWS8, 1), INT_MIN, I32), jnp.zeros((ROWS8, 1), I32)))
    cgt = count(lambda x: x > thr)
    need = kk - cgt
    needf = need.astype(F32)
    tied = jnp.max(jnp.where(need < cge - cgt, 1, 0))
    mn_ref[0] = jnp.where(key_new >= thr, 0.0, NEG)

    @pl.when(tied == 0)
    def _():
        mp_ref[0] = jnp.where(key_ref[...] >= thr, 0.0, NEG)

    @pl.when(tied > 0)
    def _():
        off = jnp.zeros((ROWS8, 1), F32)
        triu = triu_ref[...]
        for c in range(nck):
            blk = key_ref[:, c * ck:(c + 1) * ck]
            eq = blk == thr
            pref = _dot(jnp.where(eq, 1.0, 0.0).astype(BF16), triu)
            keep = jnp.logical_or(blk > thr, jnp.logical_and(eq, pref + off <= needf))
            mp_ref[0, :, c * ck:(c + 1) * ck] = jnp.where(keep, 0.0, NEG)
            off = off + pref[:, ck - 1:ck]
        eq = key_new == thr
        pref = _dot(jnp.where(eq, 1.0, 0.0).astype(BF16), triu[0:LANES, 0:LANES])
        keep = jnp.logical_or(key_new > thr, jnp.logical_and(eq, pref + off <= needf))
        mn_ref[0] = jnp.where(keep, 0.0, NEG)


def dsa_step_select(scores, z8, l_new):
    nb, _, past = scores.shape
    topk = min(DSA_TOPK_MAX, (past + l_new) // 4)
    iqw = IDX_HEADS * IDX_DIM
    assert past % DSA_CK == 0
    triu = jnp.asarray(np.triu(np.ones((DSA_CK, DSA_CK), np.float32)), BF16)
    return pl.pallas_call(
        functools.partial(_dsa_step_select_kernel, past=past, l_new=l_new, topk=topk),
        grid=(nb,),
        in_specs=[pl.BlockSpec((1, ROWS8, past), lambda n: (n, 0, 0)),
                  pl.BlockSpec((1, ROWS8, iqw), lambda n: (n, 0, CD_COLS["iq"] // iqw)),
                  pl.BlockSpec((1, ROWS8, LANES), lambda n: (n, 0, SMALL_BLOCK)),
                  pl.BlockSpec((DSA_CK, DSA_CK), lambda n: (0, 0))],
        out_specs=[pl.BlockSpec((1, ROWS8, past), lambda n: (n, 0, 0)),
                   pl.BlockSpec((1, ROWS8, LANES), lambda n: (n, 0, 0))],
        out_shape=[jax.ShapeDtypeStruct((nb, ROWS8, past), F32),
                   jax.ShapeDtypeStruct((nb, ROWS8, LANES), F32)],
        scratch_shapes=[pltpu.VMEM((ROWS8, past), I32)],
        compiler_params=_cparams("parallel"),
        name="dsa_step_select",
    )(scores, z8, z8, triu)


def _dsa_step_attn_kernel(pt_ref, q_ref, kvn_ref, mp_ref, mn_ref, bp_ref, bn_ref, *rest, w, npg):
    kv_refs, o_ref = rest[:npg], rest[npg]
    qbd_ref, newpage_ref, m_ref, l_ref, acc_ref = rest[npg + 1:]
    n = pl.program_id(0)
    p = pl.program_id(1)
    rows = DSA_HEADS * ROWS8
    lane_h = lax.broadcasted_iota(I32, (ROWS8, w), 1) // HEAD_DIM

    @pl.when(jnp.logical_and(n == 0, p == 0))
    def _():
        newpage_ref[...] = jnp.zeros(newpage_ref.shape, F32)

    @pl.when(p == 0)
    def _():
        q = q_ref[0]
        for h in range(DSA_HEADS):
            qbd_ref[h * ROWS8:(h + 1) * ROWS8, :] = jnp.where(lane_h == h, q, jnp.zeros_like(q))
        m_ref[...] = jnp.full(m_ref.shape, NEG, F32)
        l_ref[...] = jnp.zeros(l_ref.shape, F32)
        acc_ref[...] = jnp.zeros(acc_ref.shape, F32)

    def accumulate(scores, madd8, bias, pv):
        s = scores + bias + jnp.concatenate([madd8] * DSA_HEADS, axis=0)
        m_old = m_ref[:, 0:1]
        m_new = jnp.maximum(m_old, jnp.max(s, axis=-1, keepdims=True))
        alpha = jnp.exp(m_old - m_new)
        pr = jnp.exp(s - m_new)
        l_ref[...] = jnp.broadcast_to(alpha * l_ref[:, 0:1] + jnp.sum(pr, axis=-1, keepdims=True), (rows, LANES))
        m_ref[...] = jnp.broadcast_to(m_new, (rows, LANES))
        acc_ref[...] = alpha * acc_ref[...] + pv(pr.astype(BF16))

    qbd = qbd_ref[...]
    scores = jnp.concatenate([_dot(qbd, kv_refs[j][0, 0, 0].reshape(w, PAGE_SIZE).astype(BF16))
                              for j in range(npg)], axis=1)

    def pv_pages(pr):
        out = None
        for j in range(npg):
            t = _nt(pr[:, j * PAGE_SIZE:(j + 1) * PAGE_SIZE], kv_refs[j][0, 0, 1].reshape(w, PAGE_SIZE).astype(BF16))
            out = t if out is None else out + t
        return out

    accumulate(scores, mp_ref[0], bp_ref[...], pv_pages)

    @pl.when(p == pl.num_programs(1) - 1)
    def _():
        newpage_ref[0:ROWS8, :] = kvn_ref[0]
        newp = newpage_ref[...]
        accumulate(_nt(qbd, newp[:, 0:w].astype(BF16)), mn_ref[0], bn_ref[...],
                   lambda pr: _dot(pr, newp[:, w:2 * w].astype(BF16)))
        for h in range(DSA_HEADS):
            rs = slice(h * ROWS8, (h + 1) * ROWS8)
            hs = slice(h * HEAD_DIM, (h + 1) * HEAD_DIM)
            o_ref[0, :, hs] = (acc_ref[rs, hs] / l_ref[rs, 0:1]).astype(BF16)


def _dsa_step_bias(tab_d, past, l_new):
    lq = np.minimum(np.arange(ROWS8), l_new - 1)[:, None]
    d_past = past + lq - np.arange(past)[None, :]
    d_new = lq - np.arange(LANES)[None, :]

    def table(d):
        return bias_lookup(tab_d, _rel_bucket_np(d)).reshape(DSA_HEADS * ROWS8, d.shape[1])

    return table(d_past), table(d_new)


def dsa_step_attend(dq8, kv_new8, kv_t, layer, page_table, mask_past, mask_new, tab_d, l_new):
    nb, _, w = dq8.shape
    n_pages = page_table.shape[1]
    npg = math.gcd(ATTN_PAGES, n_pages)
    past = n_pages * PAGE_SIZE
    rows = DSA_HEADS * ROWS8
    bp, bn = _dsa_step_bias(tab_d, past, l_new)

    def page_spec(j):
        return pl.BlockSpec((1, 1, 2, DSA_HEADS, HEAD_DIM, PAGE_SIZE),
                            lambda n, p, pt: (layer, pt[n * n_pages + p * npg + j], 0, 0, 0, 0))

    grid_spec = pltpu.PrefetchScalarGridSpec(
        num_scalar_prefetch=1,
        grid=(nb, n_pages // npg),
        in_specs=[pl.BlockSpec((1, ROWS8, w), lambda n, p, pt: (n, 0, 0)),
                  pl.BlockSpec((1, ROWS8, 2 * w), lambda n, p, pt: (n, 0, 0)),
                  pl.BlockSpec((1, ROWS8, npg * PAGE_SIZE), lambda n, p, pt: (n, 0, p)),
                  pl.BlockSpec((1, ROWS8, LANES), lambda n, p, pt: (n, 0, 0)),
                  pl.BlockSpec((rows, npg * PAGE_SIZE), lambda n, p, pt: (0, p)),
                  pl.BlockSpec((rows, LANES), lambda n, p, pt: (0, 0))]
                 + [page_spec(j) for j in range(npg)],
        out_specs=pl.BlockSpec((1, ROWS8, w), lambda n, p, pt: (n, 0, 0)),
        scratch_shapes=[pltpu.VMEM((rows, w), BF16), pltpu.VMEM((PAGE_SIZE, 2 * w), F32),
                        pltpu.VMEM((rows, LANES), F32), pltpu.VMEM((rows, LANES), F32), pltpu.VMEM((rows, w), F32)],
    )
    return pl.pallas_call(
        functools.partial(_dsa_step_attn_kernel, w=w, npg=npg),
        grid_spec=grid_spec,
        out_shape=jax.ShapeDtypeStruct((nb, ROWS8, w), BF16),
        compiler_params=_cparams("arbitrary", "arbitrary"),
        name="dsa_step_attend",
    )(page_table.reshape(-1), dq8, kv_new8, mask_past, mask_new, bp, bn, *([kv_t] * npg))


def _pad_rows(x3, rows):
    return jnp.pad(x3, ((0, 0), (0, rows - x3.shape[1]), (0, 0)))


def _trunk(x, is_step, conv_state, win_states, gla_state, dsa_kv, dsa_kidx, page_table, wts):
    (norm_mix, norm_ffn, w_in_ab, conv_w, conv_b, conv_ln_g, conv_ln_b, qn_ab, kn_ab, w_out_ab,
     w_in_cd, gla_wa2, gla_ba, gla_norm, qn_cd, kn_cd, w_out_cd, rel_bias, w_g, w_u, w_d) = wts
    nb, l, d = x.shape
    t = nb * l
    depth = norm_mix.shape[0]
    c = conv_w.shape[2]
    wq = B_HPG * HEAD_DIM
    x2 = x.reshape(t, d)
    conv_new, gla_new, kv_new, kidx_new = [], [], [], []
    win_new = [[] for _ in WINDOWS]
    for layer in range(depth):
        i = layer // 2
        if layer % 2 == 0:
            z = norm_matmul(x2, norm_mix[layer], w_in_ab[i])
            hist = conv_state[i] if is_step else jnp.zeros((nb, CONV_WIDTH - 1, c), F32)
            a_out, c_st = conv_module(z.reshape(nb, l, -1), hist, conv_w[i], conv_b[i], conv_ln_g[i], conv_ln_b[i])
            conv_new.append(c_st)
            q, kv = ab_qkv(z, qn_ab[i], kn_ab[i], 2 * c)
            outs, lses = [], []
            for g, window in enumerate(WINDOWS):
                tab_g = rel_bias[:, g * B_HPG:(g + 1) * B_HPG]
                if is_step:
                    o, lse, st = win_step(_pad_rows(q.reshape(nb, l, -1), ROWS8),
                                          kv.reshape(len(WINDOWS), nb, l, 2 * wq), win_states[g], i, g, tab_g)
                else:
                    o, lse = win_prompt(q.reshape(nb, l, -1), kv.reshape(len(WINDOWS), nb, l, 2 * wq), g, tab_g)
                    st = kv[g].reshape(nb, l, 2, B_HPG, HEAD_DIM)[:, -min(window, l):]
                outs.append(o.reshape(t, wq))
                lses.append(lse.reshape(t, wq))
                win_new[g].append(st)
            m1, m2 = a_out.reshape(t, c), ab_merge(outs, lses)
            wo = w_out_ab[i]
        else:
            z = norm_matmul(x2, norm_mix[layer], w_in_cd[i])
            gdec, dq, kv, kb, vb = cd_pre(z, gla_wa2[i], gla_ba[i], qn_cd[i], kn_cd[i])
            z3 = z.reshape(nb, l, -1)
            wd = DSA_HEADS * HEAD_DIM
            ik = z3[:, :, CD_COLS["ik"]:CD_COLS["ik"] + IDX_DIM]
            tab_d = rel_bias[:, B_HEADS:]
            if is_step:
                lp = GLA_CHUNK
                o_c, s_c = gla(_pad_rows(z3, lp), _pad_rows(gdec.reshape(nb, l, -1), lp), gla_state[i], gla_norm[i])
                o_c = o_c[:, :l]
                z8 = _pad_rows(z3, ROWS8)
                scores = dsa_step_scores(z8, dsa_kidx, i, page_table)
                mask_p, mask_n = dsa_step_select(scores, z8, l)
                o_d = dsa_step_attend(_pad_rows(dq.reshape(nb, l, wd), ROWS8), _pad_rows(kv.reshape(nb, l, 2 * wd), ROWS8),
                                      dsa_kv, i, page_table, mask_p, mask_n, tab_d, l)[:, :l]
                kv_st = kv.reshape(nb, l, 2, DSA_HEADS, HEAD_DIM)
                ki_st = ik
            else:
                s0 = jnp.zeros((nb, GLA_HEADS, GLA_DK, GLA_DV), F32)
                o_c, s_c = gla(z3, gdec.reshape(nb, l, -1), s0, gla_norm[i])
                mask = dsa_select(z3)
                o_d = dsa_attend(dq.reshape(nb, l, wd), kb.reshape(nb, l, wd), vb.reshape(nb, l, wd), mask, tab_d)
                n_pg = l // PAGE_SIZE
                kv_st = kv.reshape(nb, n_pg, PAGE_SIZE, 2, DSA_HEADS, HEAD_DIM)
                ki_st = ik.reshape(nb, n_pg, PAGE_SIZE, IDX_DIM)
            gla_new.append(s_c)
            kv_new.append(kv_st)
            kidx_new.append(ki_st)
            m1, m2 = o_c.reshape(t, -1), o_d.reshape(t, -1)
            wo = w_out_cd[i]
        d1 = m1.shape[1]
        x2 = mix_ffn(x2, m1, m2, wo[:d1], wo[d1:], norm_ffn[layer], w_g[layer], w_u[layer], w_d[layer])
    wins = [jnp.stack(ws) for ws in win_new]
    if is_step:
        wins = [ws.transpose(0, 1, 5, 2, 3, 4) for ws in wins]
    states = (jnp.stack(conv_new), wins[0], wins[1], wins[2],
              jnp.stack(gla_new), jnp.stack(kv_new), jnp.stack(kidx_new))
    return x2.reshape(nb, l, d), states


def kernel(x_prompt, x_sample, state_conv, cache_win128, cache_win512, cache_win2048, state_gla, cache_dsa_kv, cache_dsa_kidx, page_table, norm_mix, norm_ffn, w_in_ab, conv_w, conv_b, conv_ln_g, conv_ln_b, qn_ab, kn_ab, w_out_ab, w_in_cd, gla_wa2, gla_ba, gla_norm, qn_cd, kn_cd, w_out_cd, rel_bias, w_ffn_gate, w_ffn_up, w_ffn_down):
    bf = lambda a: a.astype(BF16)
    w_in_cd_r = jnp.stack([cd_reorder_w(w_in_cd[i]) for i in range(w_in_cd.shape[0])])
    wts = (norm_mix, norm_ffn, bf(w_in_ab), conv_w, conv_b, conv_ln_g, conv_ln_b, qn_ab, kn_ab, bf(w_out_ab),
           bf(w_in_cd_r), gla_wa2, gla_ba, gla_norm, qn_cd, kn_cd, bf(w_out_cd), rel_bias,
           bf(w_ffn_gate), bf(w_ffn_up), bf(w_ffn_down))
    y_p, sp = _trunk(x_prompt, False, None, None, None, None, None, None, wts)
    wins_t = tuple(cw.transpose(0, 1, 3, 4, 5, 2) for cw in (cache_win128, cache_win512, cache_win2048))
    y_s, ss = _trunk(x_sample, True, state_conv, wins_t, state_gla,
                     cache_dsa_kv.transpose(0, 1, 3, 4, 5, 2), cache_dsa_kidx.transpose(0, 1, 3, 2), page_table, wts)
    conv_p, win128_p, win512_p, win2048_p, gla_p, dsa_kv_p, dsa_kidx_p = sp
    conv_s, win128_s, win512_s, win2048_s, gla_s, dsa_kv_s, dsa_kidx_s = ss
    return (y_p, y_s, conv_p, conv_s, win128_p, win128_s, win512_p, win512_s, win2048_p, win2048_s,
            gla_p, gla_s, dsa_kv_p, dsa_kv_s, dsa_kidx_p, dsa_kidx_s)
```

```python
import functools
import math

import numpy as np
import jax
import jax.numpy as jnp
from jax import lax
from jax.experimental import pallas as pl
from jax.experimental.pallas import tpu as pltpu

F32 = jnp.float32
BF16 = jnp.bfloat16
I32 = jnp.int32

EPS = 1e-6
NEG = -1e30
LOG2E = math.log2(math.e)
INT_MIN = -(2 ** 31)

V7X_VMEM_BYTES = 64 * 1024 * 1024
VMEM_LIMIT = V7X_VMEM_BYTES - 12 * 1024 * 1024
LANES = 128

HEAD_DIM = 64
CONV_WIDTH = 31
WINDOWS = (128, 512, 2048)
DILATIONS = (1, 4, 16)
B_HPG = 4
B_HEADS = B_HPG * len(WINDOWS)
SW_BLOCK = 128
GLA_HEADS = 4
GLA_DK = 64
GLA_DV = 128
GLA_GATE_RANK = 16
GLA_TAU = 16.0
DSA_HEADS = 8
IDX_HEADS = 4
IDX_DIM = 64
DSA_TOPK_MAX = 256
PAGE_SIZE = 128
REL_BUCKETS = 32
REL_MAX_DIST = 2048

GLA_CHUNK = 64
DSA_QB = 128
DSA_CK = 512
HIST_PAD = 32


def _cparams(*sem):
    return pltpu.CompilerParams(dimension_semantics=sem, vmem_limit_bytes=VMEM_LIMIT)


def _nt(a, b):
    return lax.dot_general(a, b, (((1,), (1,)), ((), ())), preferred_element_type=F32)


def _dot(a, b):
    return jnp.dot(a, b, preferred_element_type=F32)


def _split2(x):
    hi = x.astype(BF16)
    lo = (x - hi.astype(F32)).astype(BF16)
    return hi, lo


def _split3(x):
    hi = x.astype(BF16)
    r = x - hi.astype(F32)
    mid = r.astype(BF16)
    lo = (r - mid.astype(F32)).astype(BF16)
    return hi, mid, lo


def _rel_bucket_np(dist):
    n = np.maximum(np.asarray(dist, np.int64), 0)
    max_exact = REL_BUCKETS // 2
    nf = np.maximum(n, max_exact).astype(np.float32)
    large = max_exact + (np.log(nf / np.float32(max_exact)) / np.float32(math.log(REL_MAX_DIST / max_exact))
                         * np.float32(REL_BUCKETS - max_exact)).astype(np.int32)
    large = np.minimum(large, REL_BUCKETS - 1)
    return np.where(n < max_exact, n, large).astype(np.int32)


def _bias_lookup_kernel(tab_ref, idx_ref, o_ref, *, nh):
    idx = idx_ref[...]
    for h in range(nh):
        acc = jnp.full(idx.shape, NEG, F32)
        for b in range(REL_BUCKETS):
            acc = jnp.where(idx == b, tab_ref[b, h], acc)
        o_ref[h] = acc


def bias_lookup(tab, idx_np):
    r, c = idx_np.shape
    nh = tab.shape[1]
    tr = 8 if (c >= 2048 and r % 8 == 0) else (128 if r % 128 == 0 else r)
    return pl.pallas_call(
        functools.partial(_bias_lookup_kernel, nh=nh),
        grid=(r // tr,),
        in_specs=[pl.BlockSpec(memory_space=pltpu.SMEM),
                  pl.BlockSpec((tr, c), lambda i: (i, 0))],
        out_specs=pl.BlockSpec((nh, tr, c), lambda i: (0, i, 0)),
        out_shape=jax.ShapeDtypeStruct((nh, r, c), F32),
        compiler_params=_cparams("parallel"),
        name="bias_lookup",
    )(tab, jnp.asarray(idx_np.astype(np.int32)))


def _norm_matmul_kernel(x_ref, g_ref, w_ref, o_ref):
    x = x_ref[...]
    y = x * lax.rsqrt(jnp.mean(x * x, axis=-1, keepdims=True) + EPS) * g_ref[...]
    o_ref[...] = _dot(y.astype(BF16), w_ref[...])


def norm_matmul(x, g, w):
    t, d = x.shape
    n = w.shape[1]
    tm = min(t, 256)
    return pl.pallas_call(
        _norm_matmul_kernel,
        grid=(t // tm,),
        in_specs=[pl.BlockSpec((tm, d), lambda i: (i, 0)),
                  pl.BlockSpec((1, d), lambda i: (0, 0)),
                  pl.BlockSpec((d, n), lambda i: (0, 0))],
        out_specs=pl.BlockSpec((tm, n), lambda i: (i, 0)),
        out_shape=jax.ShapeDtypeStruct((t, n), F32),
        compiler_params=_cparams("parallel"),
        name="norm_matmul",
    )(x, g.reshape(1, d), w)


def _mix_ffn_kernel(x_ref, m1_ref, m2_ref, wo1_ref, wo2_ref, g_ref, wg_ref, wu_ref, wd_ref, o_ref,
                    x1_ref, hf_ref, acc_ref):
    j = pl.program_id(1)

    @pl.when(j == 0)
    def _():
        x1 = x_ref[...] + _dot(m1_ref[...], wo1_ref[...]) + _dot(m2_ref[...], wo2_ref[...])
        x1_ref[...] = x1
        hf = x1 * lax.rsqrt(jnp.mean(x1 * x1, axis=-1, keepdims=True) + EPS) * g_ref[...]
        hf_ref[...] = hf.astype(BF16)
        acc_ref[...] = jnp.zeros_like(acc_ref)

    hf = hf_ref[...]
    a = _dot(hf, wg_ref[...])
    u = _dot(hf, wu_ref[...])
    act = (a * jax.nn.sigmoid(a) * u).astype(BF16)
    acc_ref[...] += _dot(act, wd_ref[...])

    @pl.when(j == pl.num_programs(1) - 1)
    def _():
        o_ref[...] = x1_ref[...] + acc_ref[...]


def mix_ffn(x, m1, m2, wo1, wo2, g, wg, wu, wd):
    t, d = x.shape
    hid = wg.shape[1]
    tm = min(t, 512)
    th = 256
    d1, d2 = m1.shape[1], m2.shape[1]
    return pl.pallas_call(
        _mix_ffn_kernel,
        grid=(t // tm, hid // th),
        in_specs=[pl.BlockSpec((tm, d), lambda i, j: (i, 0)),
                  pl.BlockSpec((tm, d1), lambda i, j: (i, 0)),
                  pl.BlockSpec((tm, d2), lambda i, j: (i, 0)),
                  pl.BlockSpec((d1, d), lambda i, j: (0, 0)),
                  pl.BlockSpec((d2, d), lambda i, j: (0, 0)),
                  pl.BlockSpec((1, d), lambda i, j: (0, 0)),
                  pl.BlockSpec((d, th), lambda i, j: (0, j)),
                  pl.BlockSpec((d, th), lambda i, j: (0, j)),
                  pl.BlockSpec((th, d), lambda i, j: (j, 0))],
        out_specs=pl.BlockSpec((tm, d), lambda i, j: (i, 0)),
        out_shape=jax.ShapeDtypeStruct((t, d), F32),
        scratch_shapes=[pltpu.VMEM((tm, d), F32), pltpu.VMEM((tm, d), BF16), pltpu.VMEM((tm, d), F32)],
        compiler_params=_cparams("parallel", "arbitrary"),
        name="mix_ffn",
    )(x, m1, m2, wo1, wo2, g.reshape(1, d), wg, wu, wd)


def _conv_kernel(z_ref, hist_ref, w_ref, b_ref, lg_ref, lb_ref, o_ref, tail_ref, uh_ref, *, ts, c):
    t = pl.program_id(1)

    @pl.when(t == 0)
    def _():
        uh_ref[0:HIST_PAD, :] = hist_ref[0]

    z = z_ref[0]
    u = z[:, 0:c] * jax.nn.sigmoid(z[:, c:2 * c])
    uh_ref[HIST_PAD:HIST_PAD + ts, :] = u
    acc = jnp.zeros((ts, c), F32) + b_ref[...]
    off = HIST_PAD - (CONV_WIDTH - 1)
    for j in range(CONV_WIDTH):
        acc = acc + w_ref[j:j + 1, :] * uh_ref[off + j:off + j + ts, :]
    mu = jnp.mean(acc, axis=-1, keepdims=True)
    var = jnp.mean(jnp.square(acc - mu), axis=-1, keepdims=True)
    yn = (acc - mu) * lax.rsqrt(var + EPS) * lg_ref[...] + lb_ref[...]
    o_ref[0] = (yn * jax.nn.sigmoid(yn)).astype(BF16)
    tail = uh_ref[ts:ts + HIST_PAD, :]
    uh_ref[0:HIST_PAD, :] = tail
    tail_ref[0] = tail


def conv_module(z3, hist, conv_w, conv_b, ln_g, ln_b):
    nb, l, _ = z3.shape
    c = conv_w.shape[1]
    ts = min(l, 512)
    hist_p = jnp.pad(hist, ((0, 0), (HIST_PAD - (CONV_WIDTH - 1), 0), (0, 0)))
    w_p = jnp.pad(conv_w, ((0, HIST_PAD - CONV_WIDTH), (0, 0)))
    a_out, tail = pl.pallas_call(
        functools.partial(_conv_kernel, ts=ts, c=c),
        grid=(nb, l // ts),
        in_specs=[pl.BlockSpec((1, ts, 2 * c), lambda n, t: (n, t, 0)),
                  pl.BlockSpec((1, HIST_PAD, c), lambda n, t: (n, 0, 0)),
                  pl.BlockSpec((HIST_PAD, c), lambda n, t: (0, 0)),
                  pl.BlockSpec((1, c), lambda n, t: (0, 0)),
                  pl.BlockSpec((1, c), lambda n, t: (0, 0)),
                  pl.BlockSpec((1, c), lambda n, t: (0, 0))],
        out_specs=[pl.BlockSpec((1, ts, c), lambda n, t: (n, t, 0)),
                   pl.BlockSpec((1, HIST_PAD, c), lambda n, t: (n, 0, 0))],
        out_shape=[jax.ShapeDtypeStruct((nb, l, c), BF16),
                   jax.ShapeDtypeStruct((nb, HIST_PAD, c), F32)],
        scratch_shapes=[pltpu.VMEM((HIST_PAD + ts, c), F32)],
        compiler_params=_cparams("parallel", "arbitrary"),
        name="conv_module",
    )(z3, hist_p, w_p, conv_b.reshape(1, c), ln_g.reshape(1, c), ln_b.reshape(1, c))
    return a_out, tail[:, HIST_PAD - (CONV_WIDTH - 1):]


def _seg_mean_sq(x, seg_ref):
    hi, lo = _split2(x * x)
    seg = seg_ref[...]
    return (_dot(hi, seg) + _dot(lo, seg)) * (1.0 / HEAD_DIM)


def _head_norm(x, g, seg_ref):
    return x * lax.rsqrt(_seg_mean_sq(x, seg_ref) + EPS) * g


def _seg_matrix(width):
    idx = np.arange(width) // HEAD_DIM
    return jnp.asarray((idx[:, None] == idx[None, :]).astype(np.float32), BF16)


def _ab_qkv_kernel(q_ref, k_ref, v_ref, qg_ref, kg_ref, seg_ref, qo_ref, kvo_ref, *, w):
    qo_ref[...] = (_head_norm(q_ref[...], qg_ref[...], seg_ref) * HEAD_DIM ** -0.5).astype(BF16)
    kvo_ref[0, :, 0:w] = _head_norm(k_ref[...], kg_ref[...], seg_ref)
    kvo_ref[0, :, w:2 * w] = v_ref[...]


def ab_qkv(z, qn, kn, col0):
    t = z.shape[0]
    w = B_HPG * HEAD_DIM
    ng = len(WINDOWS)
    tm = min(t, 512)
    cb = col0 // w
    return pl.pallas_call(
        functools.partial(_ab_qkv_kernel, w=w),
        grid=(t // tm, ng),
        in_specs=[pl.BlockSpec((tm, w), lambda i, g: (i, cb + g)),
                  pl.BlockSpec((tm, w), lambda i, g: (i, cb + ng + g)),
                  pl.BlockSpec((tm, w), lambda i, g: (i, cb + 2 * ng + g)),
                  pl.BlockSpec((1, w), lambda i, g: (0, 0)),
                  pl.BlockSpec((1, w), lambda i, g: (0, 0)),
                  pl.BlockSpec((w, w), lambda i, g: (0, 0))],
        out_specs=[pl.BlockSpec((tm, w), lambda i, g: (i, g)),
                   pl.BlockSpec((1, tm, 2 * w), lambda i, g: (g, i, 0))],
        out_shape=[jax.ShapeDtypeStruct((t, ng * w), BF16),
                   jax.ShapeDtypeStruct((ng, t, 2 * w), F32)],
        compiler_params=_cparams("parallel", "parallel"),
        name="ab_qkv",
    )(z, z, z, jnp.tile(qn, B_HPG).reshape(1, w), jnp.tile(kn, B_HPG).reshape(1, w), _seg_matrix(w))


def _win_prompt_kernel(q_ref, kvp_ref, kvc_ref, bias_ref, o_ref, lse_ref, *, w):
    blk = pl.program_id(2)
    q = q_ref[0]
    kvp = kvp_ref[0, 0]
    kvc = kvc_ref[0, 0]
    col = lax.broadcasted_iota(I32, (SW_BLOCK, 2 * SW_BLOCK), 1)
    first = jnp.logical_and(blk == 0, col < SW_BLOCK)
    for h in range(B_HPG):
        hs = slice(h * HEAD_DIM, (h + 1) * HEAD_DIM)
        vs = slice(w + h * HEAD_DIM, w + (h + 1) * HEAD_DIM)
        k2 = jnp.concatenate([kvp[:, hs], kvc[:, hs]], axis=0).astype(BF16)
        v2 = jnp.concatenate([kvp[:, vs], kvc[:, vs]], axis=0).astype(BF16)
        s = _nt(q[:, hs], k2) + bias_ref[h]
        s = jnp.where(first, NEG, s)
        m = jnp.max(s, axis=-1, keepdims=True)
        p = jnp.exp(s - m)
        l = jnp.sum(p, axis=-1, keepdims=True)
        o_ref[0, :, hs] = _dot(p.astype(BF16), v2) / l
        lse_ref[0, :, hs] = jnp.broadcast_to(m + jnp.log(l), (SW_BLOCK, HEAD_DIM))


def _win_prompt_bias(tab_g, dil, reach):
    ql = np.arange(SW_BLOCK)[:, None]
    kl = np.arange(2 * SW_BLOCK)[None, :] - SW_BLOCK
    rel = ql - kl
    ok = (rel >= 0) & (rel <= reach)
    return bias_lookup(tab_g, np.where(ok, _rel_bucket_np(rel * dil), -1))


def win_prompt(q, kv, g, tab_g):
    b, s, _ = q.shape
    w = B_HPG * HEAD_DIM
    ng = len(WINDOWS)
    dil = DILATIONS[g]
    n = s // dil
    nb = n // SW_BLOCK
    assert nb * SW_BLOCK * dil == s
    qv = q.reshape(b, n, dil * ng * w)
    kvv = kv.reshape(ng, b, n, dil * 2 * w)
    bias = _win_prompt_bias(tab_g, dil, WINDOWS[g] // dil)
    o, lse = pl.pallas_call(
        functools.partial(_win_prompt_kernel, w=w),
        grid=(b, dil, nb),
        in_specs=[pl.BlockSpec((1, SW_BLOCK, w), lambda n_, r, k: (n_, k, ng * r + g)),
                  pl.BlockSpec((1, 1, SW_BLOCK, 2 * w), lambda n_, r, k: (g, n_, jnp.maximum(k - 1, 0), r)),
                  pl.BlockSpec((1, 1, SW_BLOCK, 2 * w), lambda n_, r, k: (g, n_, k, r)),
                  pl.BlockSpec((B_HPG, SW_BLOCK, 2 * SW_BLOCK), lambda n_, r, k: (0, 0, 0))],
        out_specs=[pl.BlockSpec((1, SW_BLOCK, w), lambda n_, r, k: (n_, k, r)),
                   pl.BlockSpec((1, SW_BLOCK, w), lambda n_, r, k: (n_, k, r))],
        out_shape=[jax.ShapeDtypeStruct((b, n, dil * w), F32),
                   jax.ShapeDtypeStruct((b, n, dil * w), F32)],
        compiler_params=_cparams("parallel", "parallel", "arbitrary"),
        name=f"win_prompt_g{g}",
    )(qv, kvv, kvv, bias)
    return o.reshape(b, s, w), lse.reshape(b, s, w)


ROWS8 = 8


def _win_step_kernel(q_ref, kvn_ref, newt_ref, buf_ref, tb_ref, tn_ref, o_ref, lse_ref, st_ref,
                     qbd_ref, newpage_ref, *, w, wb, l):
    n = pl.program_id(0)

    @pl.when(n == 0)
    def _():
        newpage_ref[...] = jnp.zeros(newpage_ref.shape, F32)

    q = q_ref[0]
    lane_h = lax.broadcasted_iota(I32, (ROWS8, w), 1) // HEAD_DIM
    for h in range(B_HPG):
        qbd_ref[h * ROWS8:(h + 1) * ROWS8, :] = jnp.where(lane_h == h, q, jnp.zeros_like(q))
    newpage_ref[0:l, :] = kvn_ref[0, 0]
    qbd = qbd_ref[...]
    newp = newpage_ref[...]
    buf = buf_ref[0, 0]
    kt = buf[0].reshape(w, wb).astype(BF16)
    vt = buf[1].reshape(w, wb).astype(BF16)
    s1 = _dot(qbd, kt) + tb_ref[...]
    s2 = _nt(qbd, newp[:, 0:w].astype(BF16)) + tn_ref[...]
    m = jnp.maximum(jnp.max(s1, axis=-1, keepdims=True), jnp.max(s2, axis=-1, keepdims=True))
    p1 = jnp.exp(s1 - m)
    p2 = jnp.exp(s2 - m)
    den = jnp.sum(p1, axis=-1, keepdims=True) + jnp.sum(p2, axis=-1, keepdims=True)
    num = _nt(p1.astype(BF16), vt) + _dot(p2.astype(BF16), newp[:, w:2 * w].astype(BF16))
    o = num / den
    lse = m + jnp.log(den)
    for h in range(B_HPG):
        rs = slice(h * ROWS8, (h + 1) * ROWS8)
        hs = slice(h * HEAD_DIM, (h + 1) * HEAD_DIM)
        o_ref[0, :, hs] = o[rs, hs]
        lse_ref[0, :, hs] = jnp.broadcast_to(lse[rs], (ROWS8, HEAD_DIM))
    rolled = pltpu.roll(buf.reshape(2 * w, wb), wb - l, 1)
    lane = lax.broadcasted_iota(I32, (2 * w, LANES), 1)
    last = jnp.where(lane >= LANES - l, newt_ref[0], rolled[:, wb - LANES:wb])
    if wb > LANES:
        st_ref[0, :, :, :, 0:wb - LANES] = rolled[:, 0:wb - LANES].reshape(2, B_HPG, HEAD_DIM, wb - LANES)
    st_ref[0, :, :, :, wb - LANES:wb] = last.reshape(2, B_HPG, HEAD_DIM, LANES)


def _win_step_bias(tab_g, dil, window, wb, l):
    lq = np.minimum(np.arange(ROWS8), l - 1)[:, None]
    dist_b = wb + lq - np.arange(wb)[None, :]
    ok_b = (dist_b % dil == 0) & (dist_b // dil <= window // dil)
    cols = np.arange(LANES)[None, :]
    dist_n = lq - cols
    ok_n = (cols < l) & (dist_n >= 0) & (dist_n % dil == 0) & (dist_n // dil <= window // dil)
    rows = B_HPG * ROWS8
    tb = bias_lookup(tab_g, np.where(ok_b, _rel_bucket_np(dist_b), -1)).reshape(rows, wb)
    tn = bias_lookup(tab_g, np.where(ok_n, _rel_bucket_np(dist_n), -1)).reshape(rows, LANES)
    return tb, tn


def win_step(q8, kv_new, cache_t, layer, g, tab_g):
    nb = q8.shape[0]
    l = kv_new.shape[2]
    w = B_HPG * HEAD_DIM
    wb = cache_t.shape[-1]
    window, dil = WINDOWS[g], DILATIONS[g]
    assert wb == window, "the step kernel keeps a full window of rows"
    tb, tn = _win_step_bias(tab_g, dil, window, wb, l)
    rows = B_HPG * ROWS8
    new_t = jnp.pad(kv_new[g].transpose(0, 2, 1), ((0, 0), (0, 0), (LANES - l, 0)))
    o, lse, st = pl.pallas_call(
        functools.partial(_win_step_kernel, w=w, wb=wb, l=l),
        grid=(nb,),
        in_specs=[pl.BlockSpec((1, ROWS8, w), lambda n: (n, 0, g)),
                  pl.BlockSpec((1, 1, l, 2 * w), lambda n: (g, n, 0, 0)),
                  pl.BlockSpec((1, 2 * w, LANES), lambda n: (n, 0, 0)),
                  pl.BlockSpec((1, 1, 2, B_HPG, HEAD_DIM, wb), lambda n: (layer, n, 0, 0, 0, 0)),
                  pl.BlockSpec((rows, wb), lambda n: (0, 0)),
                  pl.BlockSpec((rows, LANES), lambda n: (0, 0))],
        out_specs=[pl.BlockSpec((1, ROWS8, w), lambda n: (n, 0, 0)),
                   pl.BlockSpec((1, ROWS8, w), lambda n: (n, 0, 0)),
                   pl.BlockSpec((1, 2, B_HPG, HEAD_DIM, wb), lambda n: (n, 0, 0, 0, 0))],
        out_shape=[jax.ShapeDtypeStruct((nb, ROWS8, w), F32),
                   jax.ShapeDtypeStruct((nb, ROWS8, w), F32),
                   jax.ShapeDtypeStruct((nb, 2, B_HPG, HEAD_DIM, wb), F32)],
        scratch_shapes=[pltpu.VMEM((rows, w), BF16), pltpu.VMEM((LANES, 2 * w), F32)],
        compiler_params=_cparams("arbitrary"),
        name=f"win_step_g{g}",
    )(q8, kv_new, new_t, cache_t, tb, tn)
    return o[:, :l], lse[:, :l], st


def _ab_merge_kernel(o0, o1, o2, l0, l1, l2, out_ref):
    a, b, c = l0[...], l1[...], l2[...]
    m = jnp.maximum(jnp.maximum(a, b), c)
    ea, eb, ec = jnp.exp(a - m), jnp.exp(b - m), jnp.exp(c - m)
    out_ref[...] = ((ea * o0[...] + eb * o1[...] + ec * o2[...]) / (ea + eb + ec)).astype(BF16)


def ab_merge(outs, lses):
    t, w = outs[0].shape
    tm = min(t, 1024)
    spec = pl.BlockSpec((tm, w), lambda i: (i, 0))
    return pl.pallas_call(
        _ab_merge_kernel,
        grid=(t // tm,),
        in_specs=[spec] * 6,
        out_specs=spec,
        out_shape=jax.ShapeDtypeStruct((t, w), BF16),
        compiler_params=_cparams("parallel"),
        name="ab_merge",
    )(*outs, *lses)


CD_COLS = dict(cq=0, ck=256, cv=512, cgate=1024, dq=1536, dk=2048, dv=2560, iq=3072, ik=3328, clr=3392, iw=3408)
CD_PAD = 3456
SMALL_BLOCK = CD_COLS["ik"] // LANES
SM_IK, SM_CLR, SM_IW = 0, CD_COLS["clr"] - CD_COLS["ik"], CD_COLS["iw"] - CD_COLS["ik"]


def cd_reorder_w(w_in):
    sizes = (256, 256, 512, 16, 512, 512, 512, 512, 256, 64, 4)
    names = ("cq", "ck", "cv", "clr", "cgate", "dq", "dk", "dv", "iq", "ik", "iw")
    starts = np.concatenate([[0], np.cumsum(sizes)[:-1]])
    out = jnp.zeros((w_in.shape[0], CD_PAD), w_in.dtype)
    for nm, st, sz in zip(names, starts, sizes):
        out = lax.dynamic_update_slice(out, w_in[:, st:st + sz], (0, CD_COLS[nm]))
    return out


def _cd_pre_kernel(d_ref, sm_ref, wa_ref, ba_ref, qg_ref, kg_ref, seg_ref, g_ref, dq_ref, kv_ref, *t_refs, w):
    d = d_ref[...]
    sm = sm_ref[...]
    clr = sm[:, SM_CLR:SM_CLR + GLA_GATE_RANK]
    wa = wa_ref[...]
    pre = ba_ref[...]
    for part in _split3(clr):
        pre = pre + _dot(part, wa[0]) + _dot(part, wa[1])
    lsig = jnp.minimum(pre, 0.0) - jnp.log(1.0 + jnp.exp(-jnp.abs(pre)))
    g_ref[...] = lsig * (1.0 / GLA_TAU)
    q_scale = HEAD_DIM ** -0.5 * (LOG2E if t_refs else 1.0)
    dq_ref[...] = (_head_norm(d[:, 0:w], qg_ref[...], seg_ref) * q_scale).astype(BF16)
    kn = _head_norm(d[:, w:2 * w], kg_ref[...], seg_ref)
    v = d[:, 2 * w:3 * w]
    kv_ref[:, 0:w] = kn
    kv_ref[:, w:2 * w] = v
    if t_refs:
        kb_ref, vt_ref, smt_ref = t_refs
        kb_ref[...] = kn.astype(BF16)
        vt_ref[0] = v.T.astype(BF16)
        smt_ref[0] = sm.T


def cd_pre(z, wa2, ba, qn, kn, seq=None):
    t = z.shape[0]
    w = DSA_HEADS * HEAD_DIM
    gw = GLA_HEADS * GLA_DK
    tm = min(t, 512)
    wa_hi, wa_lo = _split2(wa2)
    out_specs = [pl.BlockSpec((tm, gw), lambda i: (i, 0)),
                 pl.BlockSpec((tm, w), lambda i: (i, 0)),
                 pl.BlockSpec((tm, 2 * w), lambda i: (i, 0))]
    out_shape = [jax.ShapeDtypeStruct((t, gw), F32),
                 jax.ShapeDtypeStruct((t, w), BF16),
                 jax.ShapeDtypeStruct((t, 2 * w), F32)]
    if seq is not None:
        nt = seq // tm
        out_specs += [pl.BlockSpec((tm, w), lambda i: (i, 0)),
                      pl.BlockSpec((1, w, tm), lambda i: (i // nt, 0, i % nt)),
                      pl.BlockSpec((1, LANES, tm), lambda i: (i // nt, 0, i % nt))]
        out_shape += [jax.ShapeDtypeStruct((t, w), BF16),
                      jax.ShapeDtypeStruct((t // seq, w, seq), BF16),
                      jax.ShapeDtypeStruct((t // seq, LANES, seq), F32)]
    return pl.pallas_call(
        functools.partial(_cd_pre_kernel, w=w),
        grid=(t // tm,),
        in_specs=[pl.BlockSpec((tm, 3 * w), lambda i: (i, CD_COLS["dq"] // (3 * w))),
                  pl.BlockSpec((tm, LANES), lambda i: (i, SMALL_BLOCK)),
                  pl.BlockSpec((2, GLA_GATE_RANK, gw), lambda i: (0, 0, 0)),
                  pl.BlockSpec((1, gw), lambda i: (0, 0)),
                  pl.BlockSpec((1, w), lambda i: (0, 0)),
                  pl.BlockSpec((1, w), lambda i: (0, 0)),
                  pl.BlockSpec((w, w), lambda i: (0, 0))],
        out_specs=out_specs,
        out_shape=out_shape,
        compiler_params=_cparams("parallel"),
        name="cd_pre",
    )(z, z, jnp.stack([wa_hi, wa_lo]), ba.reshape(1, gw), jnp.tile(qn, DSA_HEADS).reshape(1, w),
      jnp.tile(kn, DSA_HEADS).reshape(1, w), _seg_matrix(w))


def _gla_consts(c):
    levels = []
    s = c // 2
    while s >= 1:
        levels.append(s)
        s //= 2
    i = np.arange(c)
    tri = (i[None, :] <= i[:, None]).astype(np.float32)
    mats = [tri]
    masks = []
    for s in levels:
        ref = (i // (2 * s)) * (2 * s) + s - 1
        r = (i[None, :] <= ref[:, None]).astype(np.float32)
        mats.append(tri - r)
        same = (i[:, None] // (2 * s)) == (i[None, :] // (2 * s))
        masks.append(same & ((i[:, None] % (2 * s)) >= s) & ((i[None, :] % (2 * s)) < s))
    masks.append(i[:, None] == i[None, :])
    mstack = np.concatenate(mats, axis=0)
    mask = np.stack([np.tile(m.astype(np.float32), (1, GLA_HEADS)) for m in masks])
    return jnp.asarray(mstack, BF16), jnp.asarray(mask, F32), len(levels)


def _gla_kernel(qk_ref, v_ref, gate_ref, g_ref, s0_ref, mstack_ref, mask_ref, gn_ref, eye_ref,
                o_ref, sf_ref, st_ref, *, c, nl):
    t = pl.program_id(1)
    kw = GLA_HEADS * GLA_DK
    vw = GLA_HEADS * GLA_DV
    lane_k = lax.broadcasted_iota(I32, (1, kw), 1) // GLA_DK
    lane_v = lax.broadcasted_iota(I32, (1, vw), 1) // GLA_DV

    @pl.when(t == 0)
    def _():
        rows = []
        for h in range(GLA_HEADS):
            z = jnp.zeros((GLA_DK, GLA_DV), F32)
            rows.append(jnp.concatenate([s0_ref[0, h] if hh == h else z for hh in range(GLA_HEADS)], axis=1))
        st_ref[...] = jnp.concatenate(rows, axis=0).T

    qk = qk_ref[0]
    q = qk[:, 0:kw] * GLA_DK ** -0.5
    k = qk[:, kw:2 * kw]
    v = v_ref[0]
    mstack = mstack_ref[...]
    r = None
    for part in _split3(g_ref[0]):
        d = _dot(mstack, part)
        r = d if r is None else r + d
    b = r[0:c]

    def expand_k(x):
        return jnp.concatenate([jnp.where(lane_k == h, x, 0.0) for h in range(GLA_HEADS)], axis=0).astype(BF16)

    st = st_ref[...]
    o = _nt((q * jnp.exp(b)).astype(BF16), st.astype(BF16))
    a = mask_ref[nl] * _nt(q.astype(BF16), expand_k(k))
    for lv in range(nl):
        sc = jnp.exp(-jnp.abs(r[(lv + 1) * c:(lv + 2) * c]))
        a = a + mask_ref[lv] * _nt((q * sc).astype(BF16), expand_k(k * sc))
    vexp = jnp.concatenate([jnp.where(lane_v == h, v, 0.0) for h in range(GLA_HEADS)], axis=0).astype(BF16)
    o = o + _dot(a.astype(BF16), vexp)

    blast = b[c - 1:c]
    kt = (k * jnp.exp(blast - b)).astype(BF16)
    vt = _nt(eye_ref[...], v.astype(BF16)).astype(BF16)
    upd = _dot(vt, kt)
    row_h = lax.broadcasted_iota(I32, (vw, 1), 0) // GLA_DV
    st_new = st * jnp.exp(blast) + jnp.where(row_h == lane_k, upd, 0.0)
    st_ref[...] = st_new

    gate = gate_ref[0]
    gn = gn_ref[...]
    for h in range(GLA_HEADS):
        hs = slice(h * GLA_DV, (h + 1) * GLA_DV)
        oh = o[:, hs]
        y = oh * lax.rsqrt(jnp.mean(oh * oh, axis=-1, keepdims=True) + EPS) * gn
        gh = gate[:, hs]
        o_ref[0, :, hs] = (y * (gh * jax.nn.sigmoid(gh))).astype(BF16)

    @pl.when(t == pl.num_programs(1) - 1)
    def _():
        s_t = st_new.T
        for h in range(GLA_HEADS):
            sf_ref[0, h] = s_t[h * GLA_DK:(h + 1) * GLA_DK, h * GLA_DV:(h + 1) * GLA_DV]


def gla(z3, g3, s0, gla_norm):
    nb, l, _ = z3.shape
    c = GLA_CHUNK
    assert l % c == 0
    kw = GLA_HEADS * GLA_DK
    vw = GLA_HEADS * GLA_DV
    mstack, mask, nl = _gla_consts(c)
    eye = jnp.eye(vw, dtype=BF16)
    o, sf = pl.pallas_call(
        functools.partial(_gla_kernel, c=c, nl=nl),
        grid=(nb, l // c),
        in_specs=[pl.BlockSpec((1, c, 2 * kw), lambda n, t: (n, t, 0)),
                  pl.BlockSpec((1, c, vw), lambda n, t: (n, t, CD_COLS["cv"] // vw)),
                  pl.BlockSpec((1, c, vw), lambda n, t: (n, t, CD_COLS["cgate"] // vw)),
                  pl.BlockSpec((1, c, kw), lambda n, t: (n, t, 0)),
                  pl.BlockSpec((1, GLA_HEADS, GLA_DK, GLA_DV), lambda n, t: (n, 0, 0, 0)),
                  pl.BlockSpec(mstack.shape, lambda n, t: (0, 0)),
                  pl.BlockSpec(mask.shape, lambda n, t: (0, 0, 0)),
                  pl.BlockSpec((1, GLA_DV), lambda n, t: (0, 0)),
                  pl.BlockSpec((vw, vw), lambda n, t: (0, 0))],
        out_specs=[pl.BlockSpec((1, c, vw), lambda n, t: (n, t, 0)),
                   pl.BlockSpec((1, GLA_HEADS, GLA_DK, GLA_DV), lambda n, t: (n, 0, 0, 0))],
        out_shape=[jax.ShapeDtypeStruct((nb, l, vw), BF16),
                   jax.ShapeDtypeStruct((nb, GLA_HEADS, GLA_DK, GLA_DV), F32)],
        scratch_shapes=[pltpu.VMEM((vw, kw), F32)],
        compiler_params=_cparams("parallel", "arbitrary"),
        name="gla",
    )(z3, z3, z3, g3, s0, mstack, mask, gla_norm.reshape(1, GLA_DV), eye)
    return o, sf


def _sort_key(x):
    bits = pltpu.bitcast(x, I32)
    return jnp.where(bits < 0, (bits ^ 0x7FFFFFFF) + 1, bits)


def _idx_lhs(iq, h):
    hi, lo = _split2(iq[:, h * IDX_DIM:(h + 1) * IDX_DIM])
    return jnp.concatenate([hi, hi, lo, jnp.zeros_like(hi)], axis=1)


def _idx_rhs(ik):
    hi, lo = _split2(ik)
    return jnp.concatenate([hi, lo, hi, jnp.zeros_like(hi)], axis=1)


def _idx_scores(lhs, rhs, wcol):
    sc = None
    for h in range(IDX_HEADS):
        d = jnp.maximum(_nt(lhs[h], rhs) * IDX_DIM ** -0.5, 0.0) * (wcol[h] * IDX_HEADS ** -0.5)
        sc = d if sc is None else sc + d
    return sc


def _dsa_select_kernel(iq_ref, smt_ref, smk_ref, tril_ref, mask_ref, kb3_ref, sc_ref, *, nc, topk):
    i = pl.program_id(1)
    ck = DSA_CK
    qb = DSA_QB
    bpc = ck // qb
    sub = ck // 8

    @pl.when(i == 0)
    def _():
        for c in range(nc):
            kb3_ref[c] = _idx_rhs(smk_ref[0, c * ck:(c + 1) * ck, SM_IK:SM_IK + IDX_DIM])

    iq = iq_ref[0]
    smt = smt_ref[0]
    lhs = [_idx_lhs(iq, h) for h in range(IDX_HEADS)]
    wrow = [smt[SM_IW + h:SM_IW + h + 1, :] * IDX_HEADS ** -0.5 for h in range(IDX_HEADS)]
    nch = (i + bpc) // bpc
    drc = lax.broadcasted_iota(I32, (ck, qb), 0) - lax.broadcasted_iota(I32, (ck, qb), 1)

    def score_body(c, carry):
        rhs = kb3_ref[c]
        sc = None
        for h in range(IDX_HEADS):
            t = jnp.maximum(_nt(rhs, lhs[h]) * IDX_DIM ** -0.5, 0.0) * wrow[h]
            sc = t if sc is None else sc + t
        sc_ref[c] = jnp.where(drc <= i * qb - c * ck, _sort_key(sc), INT_MIN)
        return carry

    lax.fori_loop(0, nch, score_body, 0)

    def count(pred):
        def body(c, acc):
            m = jnp.where(pred(sc_ref[c]), 1, 0)
            return acc + jnp.sum(m.reshape(8, sub, qb), axis=0)
        acc = lax.fori_loop(0, nch, body, jnp.zeros((sub, qb), I32))
        return jnp.sum(acc, axis=0, keepdims=True)

    qpos = i * qb + lax.broadcasted_iota(I32, (1, qb), 1)
    kk = jnp.minimum(topk, qpos + 1)

    def bit_cond(carry):
        t, _, cge = carry
        return jnp.logical_and(t < 32, jnp.min(jnp.where(cge == kk, 1, 0)) == 0)

    def bit_step(j, carry):
        t, thr, cge = carry
        cand = thr + lax.shift_left(jnp.int32(1), 31 - (t + j))
        cnt = count(lambda blk: blk >= cand)
        ok = cnt >= kk
        return t, jnp.where(ok, cand, thr), jnp.where(ok, cnt, cge)

    def bit_body(carry):
        t, thr, cge = lax.fori_loop(0, 4, bit_step, carry)
        return t + 4, thr, cge

    _, thr, cge = lax.while_loop(bit_cond, bit_body,
                                 (jnp.int32(0), jnp.full((1, qb), INT_MIN, I32), jnp.zeros((1, qb), I32)))
    tied = jnp.max(jnp.where(cge > kk, 1, 0))

    @pl.when(tied > 0)
    def _():
        cgt = count(lambda blk: blk > thr)
        needf = (kk - cgt).astype(F32)

        def tie_body(c, off):
            blk = sc_ref[c]
            eq = blk == thr
            pref = _dot(tril_ref[...], jnp.where(eq, 1.0, 0.0).astype(BF16))
            drop = jnp.logical_and(eq, pref + off > needf)
            sc_ref[c] = jnp.where(drop, INT_MIN, blk)
            return off + pref[ck - 1:ck, :]

        lax.fori_loop(0, nch, tie_body, jnp.zeros((1, qb), F32))

    def live_body(c, carry):
        mask_ref[0, 0, c] = jnp.where(sc_ref[c] >= thr, 0.0, NEG).astype(BF16)
        return carry

    def dead_body(c, carry):
        mask_ref[0, 0, c] = jnp.full((ck, qb), NEG, BF16)
        return carry

    lax.fori_loop(0, nch, live_body, 0)
    lax.fori_loop(nch, nc, dead_body, 0)


def dsa_select(z3, smt):
    b, s, _ = z3.shape
    nq, nc = s // DSA_QB, s // DSA_CK
    topk = min(DSA_TOPK_MAX, s // 4)
    tril = jnp.asarray(np.tril(np.ones((DSA_CK, DSA_CK), np.float32)), BF16)
    return pl.pallas_call(
        functools.partial(_dsa_select_kernel, nc=nc, topk=topk),
        grid=(b, nq),
        in_specs=[pl.BlockSpec((1, DSA_QB, IDX_HEADS * IDX_DIM), lambda n, i: (n, i, CD_COLS["iq"] // (IDX_HEADS * IDX_DIM))),
                  pl.BlockSpec((1, LANES, DSA_QB), lambda n, i: (n, 0, i)),
                  pl.BlockSpec((1, s, LANES), lambda n, i: (n, 0, SMALL_BLOCK)),
                  pl.BlockSpec((DSA_CK, DSA_CK), lambda n, i: (0, 0))],
        out_specs=pl.BlockSpec((1, 1, nc, DSA_CK, DSA_QB), lambda n, i: (n, i, 0, 0, 0)),
        out_shape=jax.ShapeDtypeStruct((b, nq, nc, DSA_CK, DSA_QB), BF16),
        scratch_shapes=[pltpu.VMEM((nc, DSA_CK, 4 * IDX_DIM), BF16), pltpu.VMEM((nc, DSA_CK, DSA_QB), I32)],
        compiler_params=_cparams("parallel", "arbitrary"),
        name="dsa_select",
    )(z3, smt, z3, tril)


def _dsa_bias_tiles(tab_d):
    o = 0
    while _rel_bucket_np(max(o * DSA_QB - (DSA_QB - 1), 0)) < REL_BUCKETS - 1:
        o += 1
    offs = np.arange(o + 1)[:, None, None] * DSA_QB
    d = offs + np.arange(DSA_QB)[None, None, :] - np.arange(DSA_QB)[None, :, None]
    tiles = bias_lookup(tab_d, _rel_bucket_np(d).reshape((o + 1) * DSA_QB, DSA_QB))
    return tiles.reshape(DSA_HEADS, o + 1, DSA_QB, DSA_QB)


def _dsa_attn_kernel(qi_ref, kc_ref, q_ref, k_ref, vt_ref, mask_ref, bt_ref, o_ref, m_ref, l_ref, acc_ref, s_ref,
                     *, n_off):
    s_id = pl.program_id(1)
    i = qi_ref[s_id]
    c = kc_ref[s_id]
    qb, ck = DSA_QB, DSA_CK
    bpc = ck // qb

    @pl.when(c == 0)
    def _():
        m_ref[...] = jnp.full(m_ref.shape, NEG, F32)
        l_ref[...] = jnp.zeros(l_ref.shape, F32)
        acc_ref[...] = jnp.zeros(acc_ref.shape, F32)

    madd = mask_ref[0, 0, 0].astype(F32)
    q = q_ref[0]
    k = k_ref[0]
    vt = vt_ref[0]
    offs = [jnp.clip(i - (bpc * c + t), 0, n_off - 1) for t in range(bpc)]
    m_all = m_ref[...]
    l_all = l_ref[...]
    m_rows, l_rows = [], []
    for h in range(DSA_HEADS):
        hs = slice(h * HEAD_DIM, (h + 1) * HEAD_DIM)
        bias = jnp.concatenate([bt_ref[h, offs[t]] for t in range(bpc)], axis=0)
        s = _nt(k[:, hs], q[:, hs]) + bias + madd
        s_ref[h] = s
        m_rows.append(jnp.maximum(m_all[h:h + 1, :], jnp.max(s, axis=0, keepdims=True)))
    ones = jnp.ones((16, ck), BF16)
    for h in range(DSA_HEADS):
        hs = slice(h * HEAD_DIM, (h + 1) * HEAD_DIM)
        alpha = jnp.exp2(m_all[h:h + 1, :] - m_rows[h])
        p = jnp.exp2(s_ref[h] - m_rows[h]).astype(BF16)
        pv = _dot(jnp.concatenate([vt[hs, :], ones], axis=0), p)
        l_rows.append(alpha * l_all[h:h + 1, :] + pv[HEAD_DIM:HEAD_DIM + 1, :])
        acc_ref[hs, :] = alpha * acc_ref[hs, :] + pv[0:HEAD_DIM, :]
    m_ref[...] = jnp.concatenate(m_rows, axis=0)
    l_ref[...] = jnp.concatenate(l_rows, axis=0)

    @pl.when(c == i // bpc)
    def _():
        inv = 1.0 / l_ref[...]
        ot = jnp.concatenate([acc_ref[h * HEAD_DIM:(h + 1) * HEAD_DIM, :] * inv[h:h + 1, :]
                              for h in range(DSA_HEADS)], axis=0)
        o_ref[0] = ot.T.astype(BF16)


def dsa_attend(dq, kb, vt, mask, tab_d):
    b, s, w = dq.shape
    nq = s // DSA_QB
    bpc = DSA_CK // DSA_QB
    bt = _dsa_bias_tiles(tab_d) * LOG2E
    n_off = bt.shape[1]
    qi = np.concatenate([np.full(i // bpc + 1, i) for i in range(nq)]).astype(np.int32)
    kc = np.concatenate([np.arange(i // bpc + 1) for i in range(nq)]).astype(np.int32)
    grid_spec = pltpu.PrefetchScalarGridSpec(
        num_scalar_prefetch=2,
        grid=(b, len(qi)),
        in_specs=[pl.BlockSpec((1, DSA_QB, w), lambda n, t, qi_, kc_: (n, qi_[t], 0)),
                  pl.BlockSpec((1, DSA_CK, w), lambda n, t, qi_, kc_: (n, kc_[t], 0)),
                  pl.BlockSpec((1, w, DSA_CK), lambda n, t, qi_, kc_: (n, 0, kc_[t])),
                  pl.BlockSpec((1, 1, 1, DSA_CK, DSA_QB), lambda n, t, qi_, kc_: (n, qi_[t], kc_[t], 0, 0)),
                  pl.BlockSpec(bt.shape, lambda n, t, qi_, kc_: (0, 0, 0, 0))],
        out_specs=pl.BlockSpec((1, DSA_QB, w), lambda n, t, qi_, kc_: (n, qi_[t], 0)),
        scratch_shapes=[pltpu.VMEM((DSA_HEADS, DSA_QB), F32), pltpu.VMEM((DSA_HEADS, DSA_QB), F32),
                        pltpu.VMEM((w, DSA_QB), F32), pltpu.VMEM((DSA_HEADS, DSA_CK, DSA_QB), F32)],
    )
    return pl.pallas_call(
        functools.partial(_dsa_attn_kernel, n_off=n_off),
        grid_spec=grid_spec,
        out_shape=jax.ShapeDtypeStruct((b, s, w), BF16),
        compiler_params=_cparams("parallel", "arbitrary"),
        name="dsa_attend",
    )(jnp.asarray(qi), jnp.asarray(kc), dq, kb, vt, mask, bt)


SCORE_PAGES = 16
ATTN_PAGES = 8


def _dsa_step_scores_kernel(pt_ref, iq_ref, sm_ref, *rest, npg):
    k_refs, o_ref = rest[:npg], rest[npg]
    iq = iq_ref[0]
    sm = sm_ref[0]
    lhs = jnp.concatenate([_idx_lhs(iq, h) for h in range(IDX_HEADS)], axis=0)
    wcol = [sm[:, SM_IW + h:SM_IW + h + 1] * IDX_HEADS ** -0.5 for h in range(IDX_HEADS)]
    for j in range(npg):
        hi, lo = _split2(k_refs[j][0, 0])
        d = _dot(lhs, jnp.concatenate([hi, lo, hi, jnp.zeros_like(hi)], axis=0))
        sc = None
        for h in range(IDX_HEADS):
            t = jnp.maximum(d[h * ROWS8:(h + 1) * ROWS8] * IDX_DIM ** -0.5, 0.0) * wcol[h]
            sc = t if sc is None else sc + t
        o_ref[0, :, j * PAGE_SIZE:(j + 1) * PAGE_SIZE] = sc


def dsa_step_scores(z8, kidx_t, layer, page_table):
    nb = z8.shape[0]
    n_pages = page_table.shape[1]
    npg = math.gcd(SCORE_PAGES, n_pages)
    iqw = IDX_HEADS * IDX_DIM

    def page_spec(j):
        return pl.BlockSpec((1, 1, IDX_DIM, PAGE_SIZE),
                            lambda n, p, pt: (layer, pt[n * n_pages + p * npg + j], 0, 0))

    grid_spec = pltpu.PrefetchScalarGridSpec(
        num_scalar_prefetch=1,
        grid=(nb, n_pages // npg),
        in_specs=[pl.BlockSpec((1, ROWS8, iqw), lambda n, p, pt: (n, 0, CD_COLS["iq"] // iqw)),
                  pl.BlockSpec((1, ROWS8, LANES), lambda n, p, pt: (n, 0, SMALL_BLOCK))]
                 + [page_spec(j) for j in range(npg)],
        out_specs=pl.BlockSpec((1, ROWS8, npg * PAGE_SIZE), lambda n, p, pt: (n, 0, p)),
    )
    return pl.pallas_call(
        functools.partial(_dsa_step_scores_kernel, npg=npg),
        grid_spec=grid_spec,
        out_shape=jax.ShapeDtypeStruct((nb, ROWS8, n_pages * PAGE_SIZE), F32),
        compiler_params=_cparams("parallel", "arbitrary"),
        name="dsa_step_scores",
    )(page_table.reshape(-1), z8, z8, *([kidx_t] * npg))


def _dsa_step_select_kernel(sc_ref, iq_ref, sm_ref, triu_ref, mp_ref, mn_ref, key_ref, *, past, l_new, topk):
    ck = DSA_CK
    nck = past // ck
    iq = iq_ref[0]
    sm = sm_ref[0]
    lhs = [_idx_lhs(iq, h) for h in range(IDX_HEADS)]
    wcol = [sm[:, SM_IW + h:SM_IW + h + 1] for h in range(IDX_HEADS)]
    rhs_new = _idx_rhs(jnp.concatenate([sm[:, SM_IK:SM_IK + IDX_DIM], jnp.zeros((LANES - ROWS8, IDX_DIM), F32)], axis=0))
    sc_new = _idx_scores(lhs, rhs_new, wcol)
    row = lax.broadcasted_iota(I32, (ROWS8, LANES), 0)
    col = lax.broadcasted_iota(I32, (ROWS8, LANES), 1)
    key_new = jnp.where(jnp.logical_and(col <= row, col < l_new), _sort_key(sc_new), INT_MIN)
    key_ref[...] = _sort_key(sc_ref[0])
    kk = jnp.minimum(topk, past + 1 + lax.broadcasted_iota(I32, (ROWS8, 1), 0))

    def count(pred):
        return (jnp.sum(jnp.where(pred(key_ref[...]), 1, 0), axis=1, keepdims=True)
                + jnp.sum(jnp.where(pred(key_new), 1, 0), axis=1, keepdims=True))

    def bit_body(t, carry):
        thr, cge = carry
        cand = thr + lax.shift_left(jnp.int32(1), 31 - t)
        cnt = count(lambda x: x >= cand)
        ok = cnt >= kk
        return jnp.where(ok, cand, thr), jnp.where(ok, cnt, cge)

    thr, cge = lax.fori_loop(0, 32, bit_body, (jnp.full((ROWS8, 1), INT_MIN, I32), jnp.zeros((ROWS8, 1), I32)))
    cgt = count(lambda x: x > thr)
    need = kk - cgt
    needf = need.astype(F32)
    tied = jnp.max(jnp.where(need < cge - cgt, 1, 0))
    mn_ref[0] = jnp.where(key_new >= thr, 0.0, NEG)

    @pl.when(tied == 0)
    def _():
        mp_ref[0] = jnp.where(key_ref[...] >= thr, 0.0, NEG)

    @pl.when(tied > 0)
    def _():
        off = jnp.zeros((ROWS8, 1), F32)
        triu = triu_ref[...]
        for c in range(nck):
            blk = key_ref[:, c * ck:(c + 1) * ck]
            eq = blk == thr
            pref = _dot(jnp.where(eq, 1.0, 0.0).astype(BF16), triu)
            keep = jnp.logical_or(blk > thr, jnp.logical_and(eq, pref + off <= needf))
            mp_ref[0, :, c * ck:(c + 1) * ck] = jnp.where(keep, 0.0, NEG)
            off = off + pref[:, ck - 1:ck]
        eq = key_new == thr
        pref = _dot(jnp.where(eq, 1.0, 0.0).astype(BF16), triu[0:LANES, 0:LANES])
        keep = jnp.logical_or(key_new > thr, jnp.logical_and(eq, pref + off <= needf))
        mn_ref[0] = jnp.where(keep, 0.0, NEG)


def dsa_step_select(scores, z8, l_new):
    nb, _, past = scores.shape
    topk = min(DSA_TOPK_MAX, (past + l_new) // 4)
    iqw = IDX_HEADS * IDX_DIM
    assert past % DSA_CK == 0
    triu = jnp.asarray(np.triu(np.ones((DSA_CK, DSA_CK), np.float32)), BF16)
    return pl.pallas_call(
        functools.partial(_dsa_step_select_kernel, past=past, l_new=l_new, topk=topk),
        grid=(nb,),
        in_specs=[pl.BlockSpec((1, ROWS8, past), lambda n: (n, 0, 0)),
                  pl.BlockSpec((1, ROWS8, iqw), lambda n: (n, 0, CD_COLS["iq"] // iqw)),
                  pl.BlockSpec((1, ROWS8, LANES), lambda n: (n, 0, SMALL_BLOCK)),
                  pl.BlockSpec((DSA_CK, DSA_CK), lambda n: (0, 0))],
        out_specs=[pl.BlockSpec((1, ROWS8, past), lambda n: (n, 0, 0)),
                   pl.BlockSpec((1, ROWS8, LANES), lambda n: (n, 0, 0))],
        out_shape=[jax.ShapeDtypeStruct((nb, ROWS8, past), F32),
                   jax.ShapeDtypeStruct((nb, ROWS8, LANES), F32)],
        scratch_shapes=[pltpu.VMEM((ROWS8, past), I32)],
        compiler_params=_cparams("parallel"),
        name="dsa_step_select",
    )(scores, z8, z8, triu)


def _dsa_step_attn_kernel(pt_ref, q_ref, kvn_ref, mp_ref, mn_ref, bp_ref, bn_ref, *rest, w, npg):
    kv_refs, o_ref = rest[:npg], rest[npg]
    qbd_ref, newpage_ref, m_ref, l_ref, acc_ref = rest[npg + 1:]
    n = pl.program_id(0)
    p = pl.program_id(1)
    rows = DSA_HEADS * ROWS8
    lane_h = lax.broadcasted_iota(I32, (ROWS8, w), 1) // HEAD_DIM

    @pl.when(jnp.logical_and(n == 0, p == 0))
    def _():
        newpage_ref[...] = jnp.zeros(newpage_ref.shape, F32)

    @pl.when(p == 0)
    def _():
        q = q_ref[0]
        for h in range(DSA_HEADS):
            qbd_ref[h * ROWS8:(h + 1) * ROWS8, :] = jnp.where(lane_h == h, q, jnp.zeros_like(q))
        m_ref[...] = jnp.full(m_ref.shape, NEG, F32)
        l_ref[...] = jnp.zeros(l_ref.shape, F32)
        acc_ref[...] = jnp.zeros(acc_ref.shape, F32)

    def accumulate(scores, madd8, bias, pv):
        s = scores + bias + jnp.concatenate([madd8] * DSA_HEADS, axis=0)
        m_old = m_ref[:, 0:1]
        m_new = jnp.maximum(m_old, jnp.max(s, axis=-1, keepdims=True))
        alpha = jnp.exp(m_old - m_new)
        pr = jnp.exp(s - m_new)
        l_ref[...] = jnp.broadcast_to(alpha * l_ref[:, 0:1] + jnp.sum(pr, axis=-1, keepdims=True), (rows, LANES))
        m_ref[...] = jnp.broadcast_to(m_new, (rows, LANES))
        acc_ref[...] = alpha * acc_ref[...] + pv(pr.astype(BF16))

    qbd = qbd_ref[...]
    scores = jnp.concatenate([_dot(qbd, kv_refs[j][0, 0, 0].reshape(w, PAGE_SIZE).astype(BF16))
                              for j in range(npg)], axis=1)

    def pv_pages(pr):
        out = None
        for j in range(npg):
            t = _nt(pr[:, j * PAGE_SIZE:(j + 1) * PAGE_SIZE], kv_refs[j][0, 0, 1].reshape(w, PAGE_SIZE).astype(BF16))
            out = t if out is None else out + t
        return out

    accumulate(scores, mp_ref[0], bp_ref[...], pv_pages)

    @pl.when(p == pl.num_programs(1) - 1)
    def _():
        newpage_ref[0:ROWS8, :] = kvn_ref[0]
        newp = newpage_ref[...]
        accumulate(_nt(qbd, newp[:, 0:w].astype(BF16)), mn_ref[0], bn_ref[...],
                   lambda pr: _dot(pr, newp[:, w:2 * w].astype(BF16)))
        for h in range(DSA_HEADS):
            rs = slice(h * ROWS8, (h + 1) * ROWS8)
            hs = slice(h * HEAD_DIM, (h + 1) * HEAD_DIM)
            o_ref[0, :, hs] = (acc_ref[rs, hs] / l_ref[rs, 0:1]).astype(BF16)


def _dsa_step_bias(tab_d, past, l_new):
    lq = np.minimum(np.arange(ROWS8), l_new - 1)[:, None]
    d_past = past + lq - np.arange(past)[None, :]
    d_new = lq - np.arange(LANES)[None, :]

    def table(d):
        return bias_lookup(tab_d, _rel_bucket_np(d)).reshape(DSA_HEADS * ROWS8, d.shape[1])

    return table(d_past), table(d_new)


def dsa_step_attend(dq8, kv_new8, kv_t, layer, page_table, mask_past, mask_new, tab_d, l_new):
    nb, _, w = dq8.shape
    n_pages = page_table.shape[1]
    npg = math.gcd(ATTN_PAGES, n_pages)
    past = n_pages * PAGE_SIZE
    rows = DSA_HEADS * ROWS8
    bp, bn = _dsa_step_bias(tab_d, past, l_new)

    def page_spec(j):
        return pl.BlockSpec((1, 1, 2, DSA_HEADS, HEAD_DIM, PAGE_SIZE),
                            lambda n, p, pt: (layer, pt[n * n_pages + p * npg + j], 0, 0, 0, 0))

    grid_spec = pltpu.PrefetchScalarGridSpec(
        num_scalar_prefetch=1,
        grid=(nb, n_pages // npg),
        in_specs=[pl.BlockSpec((1, ROWS8, w), lambda n, p, pt: (n, 0, 0)),
                  pl.BlockSpec((1, ROWS8, 2 * w), lambda n, p, pt: (n, 0, 0)),
                  pl.BlockSpec((1, ROWS8, npg * PAGE_SIZE), lambda n, p, pt: (n, 0, p)),
                  pl.BlockSpec((1, ROWS8, LANES), lambda n, p, pt: (n, 0, 0)),
                  pl.BlockSpec((rows, npg * PAGE_SIZE), lambda n, p, pt: (0, p)),
                  pl.BlockSpec((rows, LANES), lambda n, p, pt: (0, 0))]
                 + [page_spec(j) for j in range(npg)],
        out_specs=pl.BlockSpec((1, ROWS8, w), lambda n, p, pt: (n, 0, 0)),
        scratch_shapes=[pltpu.VMEM((rows, w), BF16), pltpu.VMEM((PAGE_SIZE, 2 * w), F32),
                        pltpu.VMEM((rows, LANES), F32), pltpu.VMEM((rows, LANES), F32), pltpu.VMEM((rows, w), F32)],
    )
    return pl.pallas_call(
        functools.partial(_dsa_step_attn_kernel, w=w, npg=npg),
        grid_spec=grid_spec,
        out_shape=jax.ShapeDtypeStruct((nb, ROWS8, w), BF16),
        compiler_params=_cparams("arbitrary", "arbitrary"),
        name="dsa_step_attend",
    )(page_table.reshape(-1), dq8, kv_new8, mask_past, mask_new, bp, bn, *([kv_t] * npg))


def _pad_rows(x3, rows):
    return jnp.pad(x3, ((0, 0), (0, rows - x3.shape[1]), (0, 0)))


def _trunk(x, is_step, conv_state, win_states, gla_state, dsa_kv, dsa_kidx, page_table, wts):
    (norm_mix, norm_ffn, w_in_ab, conv_w, conv_b, conv_ln_g, conv_ln_b, qn_ab, kn_ab, w_out_ab,
     w_in_cd, gla_wa2, gla_ba, gla_norm, qn_cd, kn_cd, w_out_cd, rel_bias, w_g, w_u, w_d) = wts
    nb, l, d = x.shape
    t = nb * l
    depth = norm_mix.shape[0]
    c = conv_w.shape[2]
    wq = B_HPG * HEAD_DIM
    x2 = x.reshape(t, d)
    conv_new, gla_new, kv_new, kidx_new = [], [], [], []
    win_new = [[] for _ in WINDOWS]
    for layer in range(depth):
        i = layer // 2
        if layer % 2 == 0:
            z = norm_matmul(x2, norm_mix[layer], w_in_ab[i])
            hist = conv_state[i] if is_step else jnp.zeros((nb, CONV_WIDTH - 1, c), F32)
            a_out, c_st = conv_module(z.reshape(nb, l, -1), hist, conv_w[i], conv_b[i], conv_ln_g[i], conv_ln_b[i])
            conv_new.append(c_st)
            q, kv = ab_qkv(z, qn_ab[i], kn_ab[i], 2 * c)
            outs, lses = [], []
            for g, window in enumerate(WINDOWS):
                tab_g = rel_bias[:, g * B_HPG:(g + 1) * B_HPG]
                if is_step:
                    o, lse, st = win_step(_pad_rows(q.reshape(nb, l, -1), ROWS8),
                                          kv.reshape(len(WINDOWS), nb, l, 2 * wq), win_states[g], i, g, tab_g)
                else:
                    o, lse = win_prompt(q.reshape(nb, l, -1), kv.reshape(len(WINDOWS), nb, l, 2 * wq), g, tab_g)
                    st = kv[g].reshape(nb, l, 2, B_HPG, HEAD_DIM)[:, -min(window, l):]
                outs.append(o.reshape(t, wq))
                lses.append(lse.reshape(t, wq))
                win_new[g].append(st)
            m1, m2 = a_out.reshape(t, c), ab_merge(outs, lses)
            wo = w_out_ab[i]
        else:
            z = norm_matmul(x2, norm_mix[layer], w_in_cd[i])
            pre = cd_pre(z, gla_wa2[i], gla_ba[i], qn_cd[i], kn_cd[i], None if is_step else l)
            gdec, dq, kv = pre[:3]
            z3 = z.reshape(nb, l, -1)
            wd = DSA_HEADS * HEAD_DIM
            ik = z3[:, :, CD_COLS["ik"]:CD_COLS["ik"] + IDX_DIM]
            tab_d = rel_bias[:, B_HEADS:]
            if is_step:
                lp = GLA_CHUNK
                o_c, s_c = gla(_pad_rows(z3, lp), _pad_rows(gdec.reshape(nb, l, -1), lp), gla_state[i], gla_norm[i])
                o_c = o_c[:, :l]
                z8 = _pad_rows(z3, ROWS8)
                scores = dsa_step_scores(z8, dsa_kidx, i, page_table)
                mask_p, mask_n = dsa_step_select(scores, z8, l)
                o_d = dsa_step_attend(_pad_rows(dq.reshape(nb, l, wd), ROWS8), _pad_rows(kv.reshape(nb, l, 2 * wd), ROWS8),
                                      dsa_kv, i, page_table, mask_p, mask_n, tab_d, l)[:, :l]
                kv_st = kv.reshape(nb, l, 2, DSA_HEADS, HEAD_DIM)
                ki_st = ik
            else:
                s0 = jnp.zeros((nb, GLA_HEADS, GLA_DK, GLA_DV), F32)
                o_c, s_c = gla(z3, gdec.reshape(nb, l, -1), s0, gla_norm[i])
                kb, vt, smt = pre[3:]
                mask = dsa_select(z3, smt)
                o_d = dsa_attend(dq.reshape(nb, l, wd), kb.reshape(nb, l, wd), vt, mask, tab_d)
                n_pg = l // PAGE_SIZE
                kv_st = kv.reshape(nb, n_pg, PAGE_SIZE, 2, DSA_HEADS, HEAD_DIM)
                ki_st = ik.reshape(nb, n_pg, PAGE_SIZE, IDX_DIM)
            gla_new.append(s_c)
            kv_new.append(kv_st)
            kidx_new.append(ki_st)
            m1, m2 = o_c.reshape(t, -1), o_d.reshape(t, -1)
            wo = w_out_cd[i]
        d1 = m1.shape[1]
        x2 = mix_ffn(x2, m1, m2, wo[:d1], wo[d1:], norm_ffn[layer], w_g[layer], w_u[layer], w_d[layer])
    wins = [jnp.stack(ws) for ws in win_new]
    if is_step:
        wins = [ws.transpose(0, 1, 5, 2, 3, 4) for ws in wins]
    states = (jnp.stack(conv_new), wins[0], wins[1], wins[2],
              jnp.stack(gla_new), jnp.stack(kv_new), jnp.stack(kidx_new))
    return x2.reshape(nb, l, d), states


def kernel(x_prompt, x_sample, state_conv, cache_win128, cache_win512, cache_win2048, state_gla, cache_dsa_kv, cache_dsa_kidx, page_table, norm_mix, norm_ffn, w_in_ab, conv_w, conv_b, conv_ln_g, conv_ln_b, qn_ab, kn_ab, w_out_ab, w_in_cd, gla_wa2, gla_ba, gla_norm, qn_cd, kn_cd, w_out_cd, rel_bias, w_ffn_gate, w_ffn_up, w_ffn_down):
    bf = lambda a: a.astype(BF16)
    w_in_cd_r = jnp.stack([cd_reorder_w(w_in_cd[i]) for i in range(w_in_cd.shape[0])])
    wts = (norm_mix, norm_ffn, bf(w_in_ab), conv_w, conv_b, conv_ln_g, conv_ln_b, qn_ab, kn_ab, bf(w_out_ab),
           bf(w_in_cd_r), gla_wa2, gla_ba, gla_norm, qn_cd, kn_cd, bf(w_out_cd), rel_bias,
           bf(w_ffn_gate), bf(w_ffn_up), bf(w_ffn_down))
    y_p, sp = _trunk(x_prompt, False, None, None, None, None, None, None, wts)
    wins_t = tuple(cw.transpose(0, 1, 3, 4, 5, 2) for cw in (cache_win128, cache_win512, cache_win2048))
    y_s, ss = _trunk(x_sample, True, state_conv, wins_t, state_gla,
                     cache_dsa_kv.transpose(0, 1, 3, 4, 5, 2), cache_dsa_kidx.transpose(0, 1, 3, 2), page_table, wts)
    conv_p, win128_p, win512_p, win2048_p, gla_p, dsa_kv_p, dsa_kidx_p = sp
    conv_s, win128_s, win512_s, win2048_s, gla_s, dsa_kv_s, dsa_kidx_s = ss
    return (y_p, y_s, conv_p, conv_s, win128_p, win128_s, win512_p, win512_s, win2048_p, win2048_s,
            gla_p, gla_s, dsa_kv_p, dsa_kv_s, dsa_kidx_p, dsa_kidx_s)
```

```python
import functools
import math

import numpy as np
import jax
import jax.numpy as jnp
from jax import lax
from jax.experimental import pallas as pl
from jax.experimental.pallas import tpu as pltpu

F32 = jnp.float32
BF16 = jnp.bfloat16
I32 = jnp.int32

EPS = 1e-6
NEG = -1e30
LOG2E = math.log2(math.e)
INT_MIN = -(2 ** 31)

V7X_VMEM_BYTES = 64 * 1024 * 1024
VMEM_LIMIT = V7X_VMEM_BYTES - 12 * 1024 * 1024
LANES = 128

HEAD_DIM = 64
CONV_WIDTH = 31
WINDOWS = (128, 512, 2048)
DILATIONS = (1, 4, 16)
B_HPG = 4
B_HEADS = B_HPG * len(WINDOWS)
SW_BLOCK = 128
GLA_HEADS = 4
GLA_DK = 64
GLA_DV = 128
GLA_GATE_RANK = 16
GLA_TAU = 16.0
DSA_HEADS = 8
IDX_HEADS = 4
IDX_DIM = 64
DSA_TOPK_MAX = 256
PAGE_SIZE = 128
REL_BUCKETS = 32
REL_MAX_DIST = 2048

GLA_CHUNK = 64
DSA_QB = 128
DSA_SQ = 256
DSA_CK = 512
HIST_PAD = 32


def _cparams(*sem):
    return pltpu.CompilerParams(dimension_semantics=sem, vmem_limit_bytes=VMEM_LIMIT)


def _nt(a, b):
    return lax.dot_general(a, b, (((1,), (1,)), ((), ())), preferred_element_type=F32)


def _dot(a, b):
    return jnp.dot(a, b, preferred_element_type=F32)


def _split2(x):
    hi = x.astype(BF16)
    lo = (x - hi.astype(F32)).astype(BF16)
    return hi, lo


def _split3(x):
    hi = x.astype(BF16)
    r = x - hi.astype(F32)
    mid = r.astype(BF16)
    lo = (r - mid.astype(F32)).astype(BF16)
    return hi, mid, lo


def _rel_bucket_np(dist):
    n = np.maximum(np.asarray(dist, np.int64), 0)
    max_exact = REL_BUCKETS // 2
    nf = np.maximum(n, max_exact).astype(np.float32)
    large = max_exact + (np.log(nf / np.float32(max_exact)) / np.float32(math.log(REL_MAX_DIST / max_exact))
                         * np.float32(REL_BUCKETS - max_exact)).astype(np.int32)
    large = np.minimum(large, REL_BUCKETS - 1)
    return np.where(n < max_exact, n, large).astype(np.int32)


def _bias_lookup_kernel(tab_ref, idx_ref, o_ref, *, nh):
    idx = idx_ref[...]
    for h in range(nh):
        acc = jnp.full(idx.shape, NEG, F32)
        for b in range(REL_BUCKETS):
            acc = jnp.where(idx == b, tab_ref[b, h], acc)
        o_ref[h] = acc


def bias_lookup(tab, idx_np):
    r, c = idx_np.shape
    nh = tab.shape[1]
    tr = 8 if (c >= 2048 and r % 8 == 0) else (128 if r % 128 == 0 else r)
    return pl.pallas_call(
        functools.partial(_bias_lookup_kernel, nh=nh),
        grid=(r // tr,),
        in_specs=[pl.BlockSpec(memory_space=pltpu.SMEM),
                  pl.BlockSpec((tr, c), lambda i: (i, 0))],
        out_specs=pl.BlockSpec((nh, tr, c), lambda i: (0, i, 0)),
        out_shape=jax.ShapeDtypeStruct((nh, r, c), F32),
        compiler_params=_cparams("parallel"),
        name="bias_lookup",
    )(tab, jnp.asarray(idx_np.astype(np.int32)))


def _norm_matmul_kernel(x_ref, g_ref, w_ref, o_ref):
    x = x_ref[...]
    y = x * lax.rsqrt(jnp.mean(x * x, axis=-1, keepdims=True) + EPS) * g_ref[...]
    o_ref[...] = _dot(y.astype(BF16), w_ref[...])


def norm_matmul(x, g, w):
    t, d = x.shape
    n = w.shape[1]
    tm = min(t, 256)
    return pl.pallas_call(
        _norm_matmul_kernel,
        grid=(t // tm,),
        in_specs=[pl.BlockSpec((tm, d), lambda i: (i, 0)),
                  pl.BlockSpec((1, d), lambda i: (0, 0)),
                  pl.BlockSpec((d, n), lambda i: (0, 0))],
        out_specs=pl.BlockSpec((tm, n), lambda i: (i, 0)),
        out_shape=jax.ShapeDtypeStruct((t, n), F32),
        compiler_params=_cparams("parallel"),
        name="norm_matmul",
    )(x, g.reshape(1, d), w)


def _mix_ffn_kernel(x_ref, m1_ref, m2_ref, wo1_ref, wo2_ref, g_ref, wg_ref, wu_ref, wd_ref, o_ref,
                    x1_ref, hf_ref, acc_ref):
    j = pl.program_id(1)

    @pl.when(j == 0)
    def _():
        x1 = x_ref[...] + _dot(m1_ref[...], wo1_ref[...]) + _dot(m2_ref[...], wo2_ref[...])
        x1_ref[...] = x1
        hf = x1 * lax.rsqrt(jnp.mean(x1 * x1, axis=-1, keepdims=True) + EPS) * g_ref[...]
        hf_ref[...] = hf.astype(BF16)
        acc_ref[...] = jnp.zeros_like(acc_ref)

    hf = hf_ref[...]
    a = _dot(hf, wg_ref[...])
    u = _dot(hf, wu_ref[...])
    act = (a * jax.nn.sigmoid(a) * u).astype(BF16)
    acc_ref[...] += _dot(act, wd_ref[...])

    @pl.when(j == pl.num_programs(1) - 1)
    def _():
        o_ref[...] = x1_ref[...] + acc_ref[...]


def mix_ffn(x, m1, m2, wo1, wo2, g, wg, wu, wd):
    t, d = x.shape
    hid = wg.shape[1]
    tm = min(t, 512)
    th = hid // 2 if hid % 256 == 0 else hid
    d1, d2 = m1.shape[1], m2.shape[1]
    return pl.pallas_call(
        _mix_ffn_kernel,
        grid=(t // tm, hid // th),
        in_specs=[pl.BlockSpec((tm, d), lambda i, j: (i, 0)),
                  pl.BlockSpec((tm, d1), lambda i, j: (i, 0)),
                  pl.BlockSpec((tm, d2), lambda i, j: (i, 0)),
                  pl.BlockSpec((d1, d), lambda i, j: (0, 0)),
                  pl.BlockSpec((d2, d), lambda i, j: (0, 0)),
                  pl.BlockSpec((1, d), lambda i, j: (0, 0)),
                  pl.BlockSpec((d, th), lambda i, j: (0, j)),
                  pl.BlockSpec((d, th), lambda i, j: (0, j)),
                  pl.BlockSpec((th, d), lambda i, j: (j, 0))],
        out_specs=pl.BlockSpec((tm, d), lambda i, j: (i, 0)),
        out_shape=jax.ShapeDtypeStruct((t, d), F32),
        scratch_shapes=[pltpu.VMEM((tm, d), F32), pltpu.VMEM((tm, d), BF16), pltpu.VMEM((tm, d), F32)],
        compiler_params=_cparams("parallel", "arbitrary"),
        name="mix_ffn",
    )(x, m1, m2, wo1, wo2, g.reshape(1, d), wg, wu, wd)


def _conv_kernel(z_ref, hist_ref, w_ref, b_ref, lg_ref, lb_ref, o_ref, tail_ref, uh_ref, *, ts, c):
    t = pl.program_id(1)

    @pl.when(t == 0)
    def _():
        uh_ref[0:HIST_PAD, :] = hist_ref[0]

    z = z_ref[0]
    u = z[:, 0:c] * jax.nn.sigmoid(z[:, c:2 * c])
    uh_ref[HIST_PAD:HIST_PAD + ts, :] = u
    acc = jnp.zeros((ts, c), F32) + b_ref[...]
    off = HIST_PAD - (CONV_WIDTH - 1)
    for j in range(CONV_WIDTH):
        acc = acc + w_ref[j:j + 1, :] * uh_ref[off + j:off + j + ts, :]
    mu = jnp.mean(acc, axis=-1, keepdims=True)
    var = jnp.mean(jnp.square(acc - mu), axis=-1, keepdims=True)
    yn = (acc - mu) * lax.rsqrt(var + EPS) * lg_ref[...] + lb_ref[...]
    o_ref[0] = (yn * jax.nn.sigmoid(yn)).astype(BF16)
    tail = uh_ref[ts:ts + HIST_PAD, :]
    uh_ref[0:HIST_PAD, :] = tail
    tail_ref[0] = tail


def conv_module(z3, hist, conv_w, conv_b, ln_g, ln_b):
    nb, l, _ = z3.shape
    c = conv_w.shape[1]
    ts = min(l, 512)
    hist_p = jnp.pad(hist, ((0, 0), (HIST_PAD - (CONV_WIDTH - 1), 0), (0, 0)))
    w_p = jnp.pad(conv_w, ((0, HIST_PAD - CONV_WIDTH), (0, 0)))
    a_out, tail = pl.pallas_call(
        functools.partial(_conv_kernel, ts=ts, c=c),
        grid=(nb, l // ts),
        in_specs=[pl.BlockSpec((1, ts, 2 * c), lambda n, t: (n, t, 0)),
                  pl.BlockSpec((1, HIST_PAD, c), lambda n, t: (n, 0, 0)),
                  pl.BlockSpec((HIST_PAD, c), lambda n, t: (0, 0)),
                  pl.BlockSpec((1, c), lambda n, t: (0, 0)),
                  pl.BlockSpec((1, c), lambda n, t: (0, 0)),
                  pl.BlockSpec((1, c), lambda n, t: (0, 0))],
        out_specs=[pl.BlockSpec((1, ts, c), lambda n, t: (n, t, 0)),
                   pl.BlockSpec((1, HIST_PAD, c), lambda n, t: (n, 0, 0))],
        out_shape=[jax.ShapeDtypeStruct((nb, l, c), BF16),
                   jax.ShapeDtypeStruct((nb, HIST_PAD, c), F32)],
        scratch_shapes=[pltpu.VMEM((HIST_PAD + ts, c), F32)],
        compiler_params=_cparams("parallel", "arbitrary"),
        name="conv_module",
    )(z3, hist_p, w_p, conv_b.reshape(1, c), ln_g.reshape(1, c), ln_b.reshape(1, c))
    return a_out, tail[:, HIST_PAD - (CONV_WIDTH - 1):]


def _seg_mean_sq(x, seg_ref):
    hi, lo = _split2(x * x)
    seg = seg_ref[...]
    return (_dot(hi, seg) + _dot(lo, seg)) * (1.0 / HEAD_DIM)


def _head_norm(x, g, seg_ref):
    return x * lax.rsqrt(_seg_mean_sq(x, seg_ref) + EPS) * g


def _seg_matrix(width):
    idx = np.arange(width) // HEAD_DIM
    return jnp.asarray((idx[:, None] == idx[None, :]).astype(np.float32), BF16)


def _ab_qkv_kernel(*refs, w, tm, dils):
    ng = len(dils)
    qkv_refs = refs[:3 * ng]
    qg_ref, kg_ref, seg_ref = refs[3 * ng:3 * ng + 3]
    qo_refs = refs[3 * ng + 3:4 * ng + 3]
    kvo_refs = refs[4 * ng + 3:5 * ng + 3]
    qs_ref, kvs_ref = refs[5 * ng + 3:]
    for g, dil in enumerate(dils):
        qn = _head_norm(qkv_refs[g][...], qg_ref[...], seg_ref) * HEAD_DIM ** -0.5
        kn = _head_norm(qkv_refs[ng + g][...], kg_ref[...], seg_ref)
        v = qkv_refs[2 * ng + g][...]
        if dil == 1:
            qo_refs[g][0] = qn.astype(BF16)
            kvo_refs[g][0, :, 0:w] = kn
            kvo_refs[g][0, :, w:2 * w] = v
        else:
            nq, nkv = w // LANES, 2 * w // LANES
            kv = jnp.concatenate([kn, v], axis=1)
            for j in range(nq):
                qs_ref[j] = qn[:, j * LANES:(j + 1) * LANES]
            for j in range(nkv):
                kvs_ref[j] = kv[:, j * LANES:(j + 1) * LANES]
            for r in range(dil):
                rows = pl.ds(r, tm // dil, stride=dil)
                for j in range(nq):
                    qo_refs[g][0, :, r * w + j * LANES:r * w + (j + 1) * LANES] = qs_ref[j, rows, :].astype(BF16)
                for j in range(nkv):
                    kvo_refs[g][0, :, r * 2 * w + j * LANES:r * 2 * w + (j + 1) * LANES] = kvs_ref[j, rows, :]


def ab_qkv(z, qn, kn, col0, seq, dils):
    t = z.shape[0]
    w = B_HPG * HEAD_DIM
    ng = len(dils)
    tm = min(seq, 512)
    nt = seq // tm
    cb = col0 // w
    assert all(tm % (8 * d) == 0 or d == 1 for d in dils)
    col_spec = lambda j: pl.BlockSpec((tm, w), lambda i: (i, cb + j))
    out_specs = ([pl.BlockSpec((1, tm // d, d * w), lambda i: (i // nt, i % nt, 0)) for d in dils]
                 + [pl.BlockSpec((1, tm // d, d * 2 * w), lambda i: (i // nt, i % nt, 0)) for d in dils])
    out_shape = ([jax.ShapeDtypeStruct((t // seq, seq // d, d * w), BF16) for d in dils]
                 + [jax.ShapeDtypeStruct((t // seq, seq // d, d * 2 * w), F32) for d in dils])
    outs = pl.pallas_call(
        functools.partial(_ab_qkv_kernel, w=w, tm=tm, dils=tuple(dils)),
        grid=(t // tm,),
        in_specs=[col_spec(j) for j in range(3 * ng)]
                 + [pl.BlockSpec((1, w), lambda i: (0, 0)),
                    pl.BlockSpec((1, w), lambda i: (0, 0)),
                    pl.BlockSpec((w, w), lambda i: (0, 0))],
        out_specs=out_specs,
        out_shape=out_shape,
        scratch_shapes=[pltpu.VMEM((w // LANES, tm, LANES), F32), pltpu.VMEM((2 * w // LANES, tm, LANES), F32)],
        compiler_params=_cparams("parallel"),
        name="ab_qkv",
    )(*([z] * (3 * ng)), jnp.tile(qn, B_HPG).reshape(1, w), jnp.tile(kn, B_HPG).reshape(1, w), _seg_matrix(w))
    return outs[:ng], outs[ng:]


def _win_prompt_kernel(q_ref, kvp_ref, kvc_ref, bias_ref, o_ref, lse_ref, *, w):
    blk = pl.program_id(2)
    q = q_ref[0]
    kvp = kvp_ref[0]
    kvc = kvc_ref[0]
    col = lax.broadcasted_iota(I32, (SW_BLOCK, 2 * SW_BLOCK), 1)
    first = jnp.logical_and(blk == 0, col < SW_BLOCK)
    for h in range(B_HPG):
        hs = slice(h * HEAD_DIM, (h + 1) * HEAD_DIM)
        vs = slice(w + h * HEAD_DIM, w + (h + 1) * HEAD_DIM)
        k2 = jnp.concatenate([kvp[:, hs], kvc[:, hs]], axis=0).astype(BF16)
        v2 = jnp.concatenate([kvp[:, vs], kvc[:, vs]], axis=0).astype(BF16)
        s = _nt(q[:, hs], k2) + bias_ref[h]
        s = jnp.where(first, NEG, s)
        m = jnp.max(s, axis=-1, keepdims=True)
        p = jnp.exp(s - m)
        l = jnp.sum(p, axis=-1, keepdims=True)
        o_ref[0, :, hs] = _dot(p.astype(BF16), v2) / l
        lse_ref[0, :, hs] = jnp.broadcast_to(m + jnp.log(l), (SW_BLOCK, HEAD_DIM))


def _win_prompt_bias(tab_g, dil, reach):
    ql = np.arange(SW_BLOCK)[:, None]
    kl = np.arange(2 * SW_BLOCK)[None, :] - SW_BLOCK
    rel = ql - kl
    ok = (rel >= 0) & (rel <= reach)
    return bias_lookup(tab_g, np.where(ok, _rel_bucket_np(rel * dil), -1))


def win_prompt(qv, kvv, g, tab_g):
    w = B_HPG * HEAD_DIM
    dil = DILATIONS[g]
    b, n, _ = qv.shape
    s = n * dil
    nb = n // SW_BLOCK
    assert nb * SW_BLOCK * dil == s
    bias = _win_prompt_bias(tab_g, dil, WINDOWS[g] // dil)
    o, lse = pl.pallas_call(
        functools.partial(_win_prompt_kernel, w=w),
        grid=(b, dil, nb),
        in_specs=[pl.BlockSpec((1, SW_BLOCK, w), lambda n_, r, k: (n_, k, r)),
                  pl.BlockSpec((1, SW_BLOCK, 2 * w), lambda n_, r, k: (n_, jnp.maximum(k - 1, 0), r)),
                  pl.BlockSpec((1, SW_BLOCK, 2 * w), lambda n_, r, k: (n_, k, r)),
                  pl.BlockSpec((B_HPG, SW_BLOCK, 2 * SW_BLOCK), lambda n_, r, k: (0, 0, 0))],
        out_specs=[pl.BlockSpec((1, SW_BLOCK, w), lambda n_, r, k: (n_, k, r)),
                   pl.BlockSpec((1, SW_BLOCK, w), lambda n_, r, k: (n_, k, r))],
        out_shape=[jax.ShapeDtypeStruct((b, n, dil * w), F32),
                   jax.ShapeDtypeStruct((b, n, dil * w), F32)],
        compiler_params=_cparams("parallel", "parallel", "arbitrary"),
        name=f"win_prompt_g{g}",
    )(qv, kvv, kvv, bias)
    return o.reshape(b, s, w), lse.reshape(b, s, w)


ROWS8 = 8


def _win_step_kernel(q_ref, kvn_ref, newt_ref, buf_ref, tb_ref, tn_ref, o_ref, lse_ref, st_ref,
                     qbd_ref, newpage_ref, *, w, wb, l):
    n = pl.program_id(0)

    @pl.when(n == 0)
    def _():
        newpage_ref[...] = jnp.zeros(newpage_ref.shape, F32)

    q = q_ref[0]
    lane_h = lax.broadcasted_iota(I32, (ROWS8, w), 1) // HEAD_DIM
    for h in range(B_HPG):
        qbd_ref[h * ROWS8:(h + 1) * ROWS8, :] = jnp.where(lane_h == h, q, jnp.zeros_like(q))
    newpage_ref[0:l, :] = kvn_ref[0]
    qbd = qbd_ref[...]
    newp = newpage_ref[...]
    buf = buf_ref[0, 0]
    kt = buf[0].reshape(w, wb).astype(BF16)
    vt = buf[1].reshape(w, wb).astype(BF16)
    s1 = _dot(qbd, kt) + tb_ref[...]
    s2 = _nt(qbd, newp[:, 0:w].astype(BF16)) + tn_ref[...]
    m = jnp.maximum(jnp.max(s1, axis=-1, keepdims=True), jnp.max(s2, axis=-1, keepdims=True))
    p1 = jnp.exp(s1 - m)
    p2 = jnp.exp(s2 - m)
    den = jnp.sum(p1, axis=-1, keepdims=True) + jnp.sum(p2, axis=-1, keepdims=True)
    num = _nt(p1.astype(BF16), vt) + _dot(p2.astype(BF16), newp[:, w:2 * w].astype(BF16))
    o = num / den
    lse = m + jnp.log(den)
    for h in range(B_HPG):
        rs = slice(h * ROWS8, (h + 1) * ROWS8)
        hs = slice(h * HEAD_DIM, (h + 1) * HEAD_DIM)
        o_ref[0, :, hs] = o[rs, hs]
        lse_ref[0, :, hs] = jnp.broadcast_to(lse[rs], (ROWS8, HEAD_DIM))
    rolled = pltpu.roll(buf.reshape(2 * w, wb), wb - l, 1)
    lane = lax.broadcasted_iota(I32, (2 * w, LANES), 1)
    last = jnp.where(lane >= LANES - l, newt_ref[0], rolled[:, wb - LANES:wb])
    if wb > LANES:
        st_ref[0, :, :, :, 0:wb - LANES] = rolled[:, 0:wb - LANES].reshape(2, B_HPG, HEAD_DIM, wb - LANES)
    st_ref[0, :, :, :, wb - LANES:wb] = last.reshape(2, B_HPG, HEAD_DIM, LANES)


def _win_step_bias(tab_g, dil, window, wb, l):
    lq = np.minimum(np.arange(ROWS8), l - 1)[:, None]
    dist_b = wb + lq - np.arange(wb)[None, :]
    ok_b = (dist_b % dil == 0) & (dist_b // dil <= window // dil)
    cols = np.arange(LANES)[None, :]
    dist_n = lq - cols
    ok_n = (cols < l) & (dist_n >= 0) & (dist_n % dil == 0) & (dist_n // dil <= window // dil)
    rows = B_HPG * ROWS8
    tb = bias_lookup(tab_g, np.where(ok_b, _rel_bucket_np(dist_b), -1)).reshape(rows, wb)
    tn = bias_lookup(tab_g, np.where(ok_n, _rel_bucket_np(dist_n), -1)).reshape(rows, LANES)
    return tb, tn


def win_step(q8, kv_new, cache_t, layer, g, tab_g):
    nb = q8.shape[0]
    l = kv_new.shape[1]
    w = B_HPG * HEAD_DIM
    wb = cache_t.shape[-1]
    window, dil = WINDOWS[g], DILATIONS[g]
    assert wb == window, "the step kernel keeps a full window of rows"
    tb, tn = _win_step_bias(tab_g, dil, window, wb, l)
    rows = B_HPG * ROWS8
    new_t = jnp.pad(kv_new.transpose(0, 2, 1), ((0, 0), (0, 0), (LANES - l, 0)))
    o, lse, st = pl.pallas_call(
        functools.partial(_win_step_kernel, w=w, wb=wb, l=l),
        grid=(nb,),
        in_specs=[pl.BlockSpec((1, ROWS8, w), lambda n: (n, 0, 0)),
                  pl.BlockSpec((1, l, 2 * w), lambda n: (n, 0, 0)),
                  pl.BlockSpec((1, 2 * w, LANES), lambda n: (n, 0, 0)),
                  pl.BlockSpec((1, 1, 2, B_HPG, HEAD_DIM, wb), lambda n: (layer, n, 0, 0, 0, 0)),
                  pl.BlockSpec((rows, wb), lambda n: (0, 0)),
                  pl.BlockSpec((rows, LANES), lambda n: (0, 0))],
        out_specs=[pl.BlockSpec((1, ROWS8, w), lambda n: (n, 0, 0)),
                   pl.BlockSpec((1, ROWS8, w), lambda n: (n, 0, 0)),
                   pl.BlockSpec((1, 2, B_HPG, HEAD_DIM, wb), lambda n: (n, 0, 0, 0, 0))],
        out_shape=[jax.ShapeDtypeStruct((nb, ROWS8, w), F32),
                   jax.ShapeDtypeStruct((nb, ROWS8, w), F32),
                   jax.ShapeDtypeStruct((nb, 2, B_HPG, HEAD_DIM, wb), F32)],
        scratch_shapes=[pltpu.VMEM((rows, w), BF16), pltpu.VMEM((LANES, 2 * w), F32)],
        compiler_params=_cparams("arbitrary"),
        name=f"win_step_g{g}",
    )(q8, kv_new, new_t, cache_t, tb, tn)
    return o[:, :l], lse[:, :l], st


def _ab_merge_kernel(o0, o1, o2, l0, l1, l2, out_ref):
    a, b, c = l0[...], l1[...], l2[...]
    m = jnp.maximum(jnp.maximum(a, b), c)
    ea, eb, ec = jnp.exp(a - m), jnp.exp(b - m), jnp.exp(c - m)
    out_ref[...] = ((ea * o0[...] + eb * o1[...] + ec * o2[...]) / (ea + eb + ec)).astype(BF16)


def ab_merge(outs, lses):
    t, w = outs[0].shape
    tm = min(t, 1024)
    spec = pl.BlockSpec((tm, w), lambda i: (i, 0))
    return pl.pallas_call(
        _ab_merge_kernel,
        grid=(t // tm,),
        in_specs=[spec] * 6,
        out_specs=spec,
        out_shape=jax.ShapeDtypeStruct((t, w), BF16),
        compiler_params=_cparams("parallel"),
        name="ab_merge",
    )(*outs, *lses)


CD_COLS = dict(cq=0, ck=256, cv=512, cgate=1024, dq=1536, dk=2048, dv=2560, iq=3072, ik=3328, clr=3392, iw=3408)
CD_PAD = 3456
SMALL_BLOCK = CD_COLS["ik"] // LANES
SM_IK, SM_CLR, SM_IW = 0, CD_COLS["clr"] - CD_COLS["ik"], CD_COLS["iw"] - CD_COLS["ik"]


def cd_reorder_w(w_in):
    sizes = (256, 256, 512, 16, 512, 512, 512, 512, 256, 64, 4)
    names = ("cq", "ck", "cv", "clr", "cgate", "dq", "dk", "dv", "iq", "ik", "iw")
    starts = np.concatenate([[0], np.cumsum(sizes)[:-1]])
    out = jnp.zeros((w_in.shape[0], CD_PAD), w_in.dtype)
    for nm, st, sz in zip(names, starts, sizes):
        out = lax.dynamic_update_slice(out, w_in[:, st:st + sz], (0, CD_COLS[nm]))
    return out


def _cd_pre_kernel(d_ref, sm_ref, wa_ref, ba_ref, qg_ref, kg_ref, seg_ref, g_ref, dq_ref, kv_ref, *t_refs, w):
    d = d_ref[...]
    sm = sm_ref[...]
    clr = sm[:, SM_CLR:SM_CLR + GLA_GATE_RANK]
    wa = wa_ref[...]
    pre = ba_ref[...]
    for part in _split3(clr):
        pre = pre + _dot(part, wa[0]) + _dot(part, wa[1])
    lsig = jnp.minimum(pre, 0.0) - jnp.log(1.0 + jnp.exp(-jnp.abs(pre)))
    g_ref[...] = lsig * (1.0 / GLA_TAU)
    q_scale = HEAD_DIM ** -0.5 * (LOG2E if t_refs else 1.0)
    dq_ref[...] = (_head_norm(d[:, 0:w], qg_ref[...], seg_ref) * q_scale).astype(BF16)
    kn = _head_norm(d[:, w:2 * w], kg_ref[...], seg_ref)
    v = d[:, 2 * w:3 * w]
    kv_ref[:, 0:w] = kn
    kv_ref[:, w:2 * w] = v
    if t_refs:
        kb_ref, vt_ref, smt_ref = t_refs
        kb_ref[...] = kn.astype(BF16)
        vt_ref[0] = v.T.astype(BF16)
        smt_ref[0] = sm.T


def cd_pre(z, wa2, ba, qn, kn, seq=None):
    t = z.shape[0]
    w = DSA_HEADS * HEAD_DIM
    gw = GLA_HEADS * GLA_DK
    tm = min(t, 512)
    wa_hi, wa_lo = _split2(wa2)
    out_specs = [pl.BlockSpec((tm, gw), lambda i: (i, 0)),
                 pl.BlockSpec((tm, w), lambda i: (i, 0)),
                 pl.BlockSpec((tm, 2 * w), lambda i: (i, 0))]
    out_shape = [jax.ShapeDtypeStruct((t, gw), F32),
                 jax.ShapeDtypeStruct((t, w), BF16),
                 jax.ShapeDtypeStruct((t, 2 * w), F32)]
    if seq is not None:
        nt = seq // tm
        out_specs += [pl.BlockSpec((tm, w), lambda i: (i, 0)),
                      pl.BlockSpec((1, w, tm), lambda i: (i // nt, 0, i % nt)),
                      pl.BlockSpec((1, LANES, tm), lambda i: (i // nt, 0, i % nt))]
        out_shape += [jax.ShapeDtypeStruct((t, w), BF16),
                      jax.ShapeDtypeStruct((t // seq, w, seq), BF16),
                      jax.ShapeDtypeStruct((t // seq, LANES, seq), F32)]
    return pl.pallas_call(
        functools.partial(_cd_pre_kernel, w=w),
        grid=(t // tm,),
        in_specs=[pl.BlockSpec((tm, 3 * w), lambda i: (i, CD_COLS["dq"] // (3 * w))),
                  pl.BlockSpec((tm, LANES), lambda i: (i, SMALL_BLOCK)),
                  pl.BlockSpec((2, GLA_GATE_RANK, gw), lambda i: (0, 0, 0)),
                  pl.BlockSpec((1, gw), lambda i: (0, 0)),
                  pl.BlockSpec((1, w), lambda i: (0, 0)),
                  pl.BlockSpec((1, w), lambda i: (0, 0)),
                  pl.BlockSpec((w, w), lambda i: (0, 0))],
        out_specs=out_specs,
        out_shape=out_shape,
        compiler_params=_cparams("parallel"),
        name="cd_pre",
    )(z, z, jnp.stack([wa_hi, wa_lo]), ba.reshape(1, gw), jnp.tile(qn, DSA_HEADS).reshape(1, w),
      jnp.tile(kn, DSA_HEADS).reshape(1, w), _seg_matrix(w))


def _gla_consts(c):
    levels = []
    s = c // 2
    while s >= 1:
        levels.append(s)
        s //= 2
    i = np.arange(c)
    tri = (i[None, :] <= i[:, None]).astype(np.float32)
    mats = [tri]
    masks = []
    for s in levels:
        ref = (i // (2 * s)) * (2 * s) + s - 1
        r = (i[None, :] <= ref[:, None]).astype(np.float32)
        mats.append(tri - r)
        same = (i[:, None] // (2 * s)) == (i[None, :] // (2 * s))
        masks.append(same & ((i[:, None] % (2 * s)) >= s) & ((i[None, :] % (2 * s)) < s))
    masks.append(i[:, None] == i[None, :])
    mstack = np.concatenate(mats, axis=0)
    mask = np.stack([np.tile(m.astype(np.float32), (1, GLA_HEADS)) for m in masks])
    return jnp.asarray(mstack, BF16), jnp.asarray(mask, F32), len(levels)


def _gla_kernel(qk_ref, v_ref, gate_ref, g_ref, s0_ref, mstack_ref, mask_ref, gn_ref, eye_ref,
                o_ref, sf_ref, st_ref, *, c, nl):
    t = pl.program_id(1)
    kw = GLA_HEADS * GLA_DK
    vw = GLA_HEADS * GLA_DV
    lane_k = lax.broadcasted_iota(I32, (1, kw), 1) // GLA_DK
    lane_v = lax.broadcasted_iota(I32, (1, vw), 1) // GLA_DV

    @pl.when(t == 0)
    def _():
        rows = []
        for h in range(GLA_HEADS):
            z = jnp.zeros((GLA_DK, GLA_DV), F32)
            rows.append(jnp.concatenate([s0_ref[0, h] if hh == h else z for hh in range(GLA_HEADS)], axis=1))
        st_ref[...] = jnp.concatenate(rows, axis=0).T

    qk = qk_ref[0]
    q = qk[:, 0:kw] * GLA_DK ** -0.5
    k = qk[:, kw:2 * kw]
    v = v_ref[0]
    mstack = mstack_ref[...]
    r = None
    for part in _split3(g_ref[0]):
        d = _dot(mstack, part)
        r = d if r is None else r + d
    b = r[0:c]

    def expand_k(x):
        return jnp.concatenate([jnp.where(lane_k == h, x, 0.0) for h in range(GLA_HEADS)], axis=0).astype(BF16)

    st = st_ref[...]
    o = _nt((q * jnp.exp(b)).astype(BF16), st.astype(BF16))
    a = mask_ref[nl] * _nt(q.astype(BF16), expand_k(k))
    for lv in range(nl):
        sc = jnp.exp(-jnp.abs(r[(lv + 1) * c:(lv + 2) * c]))
        a = a + mask_ref[lv] * _nt((q * sc).astype(BF16), expand_k(k * sc))
    vexp = jnp.concatenate([jnp.where(lane_v == h, v, 0.0) for h in range(GLA_HEADS)], axis=0).astype(BF16)
    o = o + _dot(a.astype(BF16), vexp)

    blast = b[c - 1:c]
    kt = (k * jnp.exp(blast - b)).astype(BF16)
    vt = _nt(eye_ref[...], v.astype(BF16)).astype(BF16)
    upd = _dot(vt, kt)
    row_h = lax.broadcasted_iota(I32, (vw, 1), 0) // GLA_DV
    st_new = st * jnp.exp(blast) + jnp.where(row_h == lane_k, upd, 0.0)
    st_ref[...] = st_new

    gate = gate_ref[0]
    gn = gn_ref[...]
    for h in range(GLA_HEADS):
        hs = slice(h * GLA_DV, (h + 1) * GLA_DV)
        oh = o[:, hs]
        y = oh * lax.rsqrt(jnp.mean(oh * oh, axis=-1, keepdims=True) + EPS) * gn
        gh = gate[:, hs]
        o_ref[0, :, hs] = (y * (gh * jax.nn.sigmoid(gh))).astype(BF16)

    @pl.when(t == pl.num_programs(1) - 1)
    def _():
        s_t = st_new.T
        for h in range(GLA_HEADS):
            sf_ref[0, h] = s_t[h * GLA_DK:(h + 1) * GLA_DK, h * GLA_DV:(h + 1) * GLA_DV]


def gla(z3, g3, s0, gla_norm):
    nb, l, _ = z3.shape
    c = GLA_CHUNK
    assert l % c == 0
    kw = GLA_HEADS * GLA_DK
    vw = GLA_HEADS * GLA_DV
    mstack, mask, nl = _gla_consts(c)
    eye = jnp.eye(vw, dtype=BF16)
    o, sf = pl.pallas_call(
        functools.partial(_gla_kernel, c=c, nl=nl),
        grid=(nb, l // c),
        in_specs=[pl.BlockSpec((1, c, 2 * kw), lambda n, t: (n, t, 0)),
                  pl.BlockSpec((1, c, vw), lambda n, t: (n, t, CD_COLS["cv"] // vw)),
                  pl.BlockSpec((1, c, vw), lambda n, t: (n, t, CD_COLS["cgate"] // vw)),
                  pl.BlockSpec((1, c, kw), lambda n, t: (n, t, 0)),
                  pl.BlockSpec((1, GLA_HEADS, GLA_DK, GLA_DV), lambda n, t: (n, 0, 0, 0)),
                  pl.BlockSpec(mstack.shape, lambda n, t: (0, 0)),
                  pl.BlockSpec(mask.shape, lambda n, t: (0, 0, 0)),
                  pl.BlockSpec((1, GLA_DV), lambda n, t: (0, 0)),
                  pl.BlockSpec((vw, vw), lambda n, t: (0, 0))],
        out_specs=[pl.BlockSpec((1, c, vw), lambda n, t: (n, t, 0)),
                   pl.BlockSpec((1, GLA_HEADS, GLA_DK, GLA_DV), lambda n, t: (n, 0, 0, 0))],
        out_shape=[jax.ShapeDtypeStruct((nb, l, vw), BF16),
                   jax.ShapeDtypeStruct((nb, GLA_HEADS, GLA_DK, GLA_DV), F32)],
        scratch_shapes=[pltpu.VMEM((vw, kw), F32)],
        compiler_params=_cparams("parallel", "arbitrary"),
        name="gla",
    )(z3, z3, z3, g3, s0, mstack, mask, gla_norm.reshape(1, GLA_DV), eye)
    return o, sf


def _sort_key(x):
    bits = pltpu.bitcast(x, I32)
    return jnp.where(bits < 0, (bits ^ 0x7FFFFFFF) + 1, bits)


def _idx_lhs(iq, h):
    hi, lo = _split2(iq[:, h * IDX_DIM:(h + 1) * IDX_DIM])
    return jnp.concatenate([hi, hi, lo, jnp.zeros_like(hi)], axis=1)


def _idx_rhs(ik):
    hi, lo = _split2(ik)
    return jnp.concatenate([hi, lo, hi, jnp.zeros_like(hi)], axis=1)


def _idx_scores(lhs, rhs, wcol):
    sc = None
    for h in range(IDX_HEADS):
        d = jnp.maximum(_nt(lhs[h], rhs) * IDX_DIM ** -0.5, 0.0) * (wcol[h] * IDX_HEADS ** -0.5)
        sc = d if sc is None else sc + d
    return sc


def _dsa_select_kernel(iq_ref, smt_ref, smk_ref, tril_ref, mask_ref, kb3_ref, sc_ref, *, nc, topk):
    i = pl.program_id(1)
    ck = DSA_CK
    qb = DSA_SQ
    bpc = ck // qb
    sub = ck // 8

    @pl.when(i == 0)
    def _():
        for c in range(nc):
            kb3_ref[c] = _idx_rhs(smk_ref[0, c * ck:(c + 1) * ck, SM_IK:SM_IK + IDX_DIM])

    iq = iq_ref[0]
    smt = smt_ref[0]
    lhs = jnp.concatenate([_idx_lhs(iq, h) for h in range(IDX_HEADS)], axis=0)
    wrow = [smt[SM_IW + h:SM_IW + h + 1, :] * IDX_HEADS ** -0.5 for h in range(IDX_HEADS)]
    nch = (i + bpc) // bpc
    drc = lax.broadcasted_iota(I32, (ck, qb), 0) - lax.broadcasted_iota(I32, (ck, qb), 1)

    def score_body(c, carry):
        dots = _nt(kb3_ref[c], lhs)
        sc = None
        for h in range(IDX_HEADS):
            t = jnp.maximum(dots[:, h * qb:(h + 1) * qb] * IDX_DIM ** -0.5, 0.0) * wrow[h]
            sc = t if sc is None else sc + t
        sc_ref[c] = jnp.where(drc <= i * qb - c * ck, _sort_key(sc), INT_MIN)
        return carry

    lax.fori_loop(0, nch, score_body, 0)

    def count(pred):
        def body(c, acc):
            m = jnp.where(pred(sc_ref[c]), 1, 0)
            return acc + jnp.sum(m.reshape(8, sub, qb), axis=0)
        acc = lax.fori_loop(0, nch, body, jnp.zeros((sub, qb), I32))
        return jnp.sum(acc, axis=0, keepdims=True)

    qpos = i * qb + lax.broadcasted_iota(I32, (1, qb), 1)
    kk = jnp.minimum(topk, qpos + 1)

    def bit_cond(carry):
        t, _, cge = carry
        return jnp.logical_and(t < 32, jnp.min(jnp.where(cge == kk, 1, 0)) == 0)

    def bit_step(j, carry):
        t, thr, cge = carry
        cand = thr + lax.shift_left(jnp.int32(1), 31 - (t + j))
        cnt = count(lambda blk: blk >= cand)
        ok = cnt >= kk
        return t, jnp.where(ok, cand, thr), jnp.where(ok, cnt, cge)

    def bit_body(carry):
        t, thr, cge = lax.fori_loop(0, 4, bit_step, carry)
        return t + 4, thr, cge

    _, thr, cge = lax.while_loop(bit_cond, bit_body,
                                 (jnp.int32(0), jnp.full((1, qb), INT_MIN, I32), jnp.zeros((1, qb), I32)))
    tied = jnp.max(jnp.where(cge > kk, 1, 0))

    @pl.when(tied > 0)
    def _():
        cgt = count(lambda blk: blk > thr)
        needf = (kk - cgt).astype(F32)

        def tie_body(c, off):
            blk = sc_ref[c]
            eq = blk == thr
            pref = _dot(tril_ref[...], jnp.where(eq, 1.0, 0.0).astype(BF16))
            drop = jnp.logical_and(eq, pref + off > needf)
            sc_ref[c] = jnp.where(drop, INT_MIN, blk)
            return off + pref[ck - 1:ck, :]

        lax.fori_loop(0, nch, tie_body, jnp.zeros((1, qb), F32))

    def live_body(c, carry):
        mask_ref[0, 0, c] = jnp.where(sc_ref[c] >= thr, 0.0, NEG).astype(BF16)
        return carry

    def dead_body(c, carry):
        mask_ref[0, 0, c] = jnp.full((ck, qb), NEG, BF16)
        return carry

    lax.fori_loop(0, nch, live_body, 0)
    lax.fori_loop(nch, nc, dead_body, 0)


def dsa_select(z3, smt):
    b, s, _ = z3.shape
    nq, nc = s // DSA_SQ, s // DSA_CK
    topk = min(DSA_TOPK_MAX, s // 4)
    tril = jnp.asarray(np.tril(np.ones((DSA_CK, DSA_CK), np.float32)), BF16)
    return pl.pallas_call(
        functools.partial(_dsa_select_kernel, nc=nc, topk=topk),
        grid=(b, nq),
        in_specs=[pl.BlockSpec((1, DSA_SQ, IDX_HEADS * IDX_DIM), lambda n, i: (n, i, CD_COLS["iq"] // (IDX_HEADS * IDX_DIM))),
                  pl.BlockSpec((1, LANES, DSA_SQ), lambda n, i: (n, 0, i)),
                  pl.BlockSpec((1, s, LANES), lambda n, i: (n, 0, SMALL_BLOCK)),
                  pl.BlockSpec((DSA_CK, DSA_CK), lambda n, i: (0, 0))],
        out_specs=pl.BlockSpec((1, 1, nc, DSA_CK, DSA_SQ), lambda n, i: (n, i, 0, 0, 0)),
        out_shape=jax.ShapeDtypeStruct((b, nq, nc, DSA_CK, DSA_SQ), BF16),
        scratch_shapes=[pltpu.VMEM((nc, DSA_CK, 4 * IDX_DIM), BF16), pltpu.VMEM((nc, DSA_CK, DSA_SQ), I32)],
        compiler_params=_cparams("parallel", "arbitrary"),
        name="dsa_select",
    )(z3, smt, z3, tril)


def _dsa_bias_tiles(tab_d):
    o = 0
    while _rel_bucket_np(max(o * DSA_QB - (DSA_QB - 1), 0)) < REL_BUCKETS - 1:
        o += 1
    offs = np.arange(o + 1)[:, None, None] * DSA_QB
    d = offs + np.arange(DSA_QB)[None, None, :] - np.arange(DSA_QB)[None, :, None]
    tiles = bias_lookup(tab_d, _rel_bucket_np(d).reshape((o + 1) * DSA_QB, DSA_QB))
    return tiles.reshape(DSA_HEADS, o + 1, DSA_QB, DSA_QB)


def _dsa_attn_kernel(qi_ref, kc_ref, q_ref, k_ref, vt_ref, mask_ref, bt_ref, o_ref, m_ref, l_ref, acc_ref, s_ref,
                     *, n_off):
    s_id = pl.program_id(1)
    i = qi_ref[s_id]
    c = kc_ref[s_id]
    qb, ck = DSA_QB, DSA_CK
    bpc = ck // qb

    @pl.when(c == 0)
    def _():
        m_ref[...] = jnp.full(m_ref.shape, NEG, F32)
        l_ref[...] = jnp.zeros(l_ref.shape, F32)
        acc_ref[...] = jnp.zeros(acc_ref.shape, F32)

    madd = mask_ref[0, 0, 0].astype(F32)
    q = q_ref[0]
    k = k_ref[0]
    vt = vt_ref[0]
    offs = [jnp.clip(i - (bpc * c + t), 0, n_off - 1) for t in range(bpc)]
    m_all = m_ref[...]
    l_all = l_ref[...]
    m_rows, l_rows = [], []
    for h in range(DSA_HEADS):
        hs = slice(h * HEAD_DIM, (h + 1) * HEAD_DIM)
        bias = jnp.concatenate([bt_ref[h, offs[t]] for t in range(bpc)], axis=0)
        s = _nt(k[:, hs], q[:, hs]) + bias + madd
        s_ref[h] = s
        m_rows.append(jnp.maximum(m_all[h:h + 1, :], jnp.max(s, axis=0, keepdims=True)))
    ones = jnp.ones((16, ck), BF16)
    for h in range(DSA_HEADS):
        hs = slice(h * HEAD_DIM, (h + 1) * HEAD_DIM)
        alpha = jnp.exp2(m_all[h:h + 1, :] - m_rows[h])
        p = jnp.exp2(s_ref[h] - m_rows[h]).astype(BF16)
        pv = _dot(jnp.concatenate([vt[hs, :], ones], axis=0), p)
        l_rows.append(alpha * l_all[h:h + 1, :] + pv[HEAD_DIM:HEAD_DIM + 1, :])
        acc_ref[hs, :] = alpha * acc_ref[hs, :] + pv[0:HEAD_DIM, :]
    m_ref[...] = jnp.concatenate(m_rows, axis=0)
    l_ref[...] = jnp.concatenate(l_rows, axis=0)

    @pl.when(c == i // bpc)
    def _():
        inv = 1.0 / l_ref[...]
        ot = jnp.concatenate([acc_ref[h * HEAD_DIM:(h + 1) * HEAD_DIM, :] * inv[h:h + 1, :]
                              for h in range(DSA_HEADS)], axis=0)
        o_ref[0] = ot.T.astype(BF16)


def dsa_attend(dq, kb, vt, mask, tab_d):
    b, s, w = dq.shape
    nq = s // DSA_QB
    bpc = DSA_CK // DSA_QB
    qps = DSA_SQ // DSA_QB
    bt = _dsa_bias_tiles(tab_d) * LOG2E
    n_off = bt.shape[1]
    qi = np.concatenate([np.full(i // bpc + 1, i) for i in range(nq)]).astype(np.int32)
    kc = np.concatenate([np.arange(i // bpc + 1) for i in range(nq)]).astype(np.int32)
    grid_spec = pltpu.PrefetchScalarGridSpec(
        num_scalar_prefetch=2,
        grid=(b, len(qi)),
        in_specs=[pl.BlockSpec((1, DSA_QB, w), lambda n, t, qi_, kc_: (n, qi_[t], 0)),
                  pl.BlockSpec((1, DSA_CK, w), lambda n, t, qi_, kc_: (n, kc_[t], 0)),
                  pl.BlockSpec((1, w, DSA_CK), lambda n, t, qi_, kc_: (n, 0, kc_[t])),
                  pl.BlockSpec((1, 1, 1, DSA_CK, DSA_QB),
                               lambda n, t, qi_, kc_: (n, qi_[t] // qps, kc_[t], 0, qi_[t] % qps)),
                  pl.BlockSpec(bt.shape, lambda n, t, qi_, kc_: (0, 0, 0, 0))],
        out_specs=pl.BlockSpec((1, DSA_QB, w), lambda n, t, qi_, kc_: (n, qi_[t], 0)),
        scratch_shapes=[pltpu.VMEM((DSA_HEADS, DSA_QB), F32), pltpu.VMEM((DSA_HEADS, DSA_QB), F32),
                        pltpu.VMEM((w, DSA_QB), F32), pltpu.VMEM((DSA_HEADS, DSA_CK, DSA_QB), F32)],
    )
    return pl.pallas_call(
        functools.partial(_dsa_attn_kernel, n_off=n_off),
        grid_spec=grid_spec,
        out_shape=jax.ShapeDtypeStruct((b, s, w), BF16),
        compiler_params=_cparams("parallel", "arbitrary"),
        name="dsa_attend",
    )(jnp.asarray(qi), jnp.asarray(kc), dq, kb, vt, mask, bt)


SCORE_PAGES = 16
ATTN_PAGES = 8


def _dsa_step_scores_kernel(pt_ref, iq_ref, sm_ref, *rest, npg):
    k_refs, o_ref = rest[:npg], rest[npg]
    iq = iq_ref[0]
    sm = sm_ref[0]
    lhs = jnp.concatenate([_idx_lhs(iq, h) for h in range(IDX_HEADS)], axis=0)
    wcol = [sm[:, SM_IW + h:SM_IW + h + 1] * IDX_HEADS ** -0.5 for h in range(IDX_HEADS)]
    for j in range(npg):
        hi, lo = _split2(k_refs[j][0, 0])
        d = _dot(lhs, jnp.concatenate([hi, lo, hi, jnp.zeros_like(hi)], axis=0))
        sc = None
        for h in range(IDX_HEADS):
            t = jnp.maximum(d[h * ROWS8:(h + 1) * ROWS8] * IDX_DIM ** -0.5, 0.0) * wcol[h]
            sc = t if sc is None else sc + t
        o_ref[0, :, j * PAGE_SIZE:(j + 1) * PAGE_SIZE] = sc


def dsa_step_scores(z8, kidx_t, layer, page_table):
    nb = z8.shape[0]
    n_pages = page_table.shape[1]
    npg = math.gcd(SCORE_PAGES, n_pages)
    iqw = IDX_HEADS * IDX_DIM

    def page_spec(j):
        return pl.BlockSpec((1, 1, IDX_DIM, PAGE_SIZE),
                            lambda n, p, pt: (layer, pt[n * n_pages + p * npg + j], 0, 0))

    grid_spec = pltpu.PrefetchScalarGridSpec(
        num_scalar_prefetch=1,
        grid=(nb, n_pages // npg),
        in_specs=[pl.BlockSpec((1, ROWS8, iqw), lambda n, p, pt: (n, 0, CD_COLS["iq"] // iqw)),
                  pl.BlockSpec((1, ROWS8, LANES), lambda n, p, pt: (n, 0, SMALL_BLOCK))]
                 + [page_spec(j) for j in range(npg)],
        out_specs=pl.BlockSpec((1, ROWS8, npg * PAGE_SIZE), lambda n, p, pt: (n, 0, p)),
    )
    return pl.pallas_call(
        functools.partial(_dsa_step_scores_kernel, npg=npg),
        grid_spec=grid_spec,
        out_shape=jax.ShapeDtypeStruct((nb, ROWS8, n_pages * PAGE_SIZE), F32),
        compiler_params=_cparams("parallel", "arbitrary"),
        name="dsa_step_scores",
    )(page_table.reshape(-1), z8, z8, *([kidx_t] * npg))


def _dsa_step_select_kernel(sc_ref, iq_ref, sm_ref, triu_ref, mp_ref, mn_ref, key_ref, *, past, l_new, topk):
    ck = DSA_CK
    nck = past // ck
    iq = iq_ref[0]
    sm = sm_ref[0]
    lhs = [_idx_lhs(iq, h) for h in range(IDX_HEADS)]
    wcol = [sm[:, SM_IW + h:SM_IW + h + 1] for h in range(IDX_HEADS)]
    rhs_new = _idx_rhs(jnp.concatenate([sm[:, SM_IK:SM_IK + IDX_DIM], jnp.zeros((LANES - ROWS8, IDX_DIM), F32)], axis=0))
    sc_new = _idx_scores(lhs, rhs_new, wcol)
    row = lax.broadcasted_iota(I32, (ROWS8, LANES), 0)
    col = lax.broadcasted_iota(I32, (ROWS8, LANES), 1)
    key_new = jnp.where(jnp.logical_and(col <= row, col < l_new), _sort_key(sc_new), INT_MIN)
    key_ref[...] = _sort_key(sc_ref[0])
    kk = jnp.minimum(topk, past + 1 + lax.broadcasted_iota(I32, (ROWS8, 1), 0))

    def count(pred):
        return (jnp.sum(jnp.where(pred(key_ref[...]), 1, 0), axis=1, keepdims=True)
                + jnp.sum(jnp.where(pred(key_new), 1, 0), axis=1, keepdims=True))

    def bit_body(t, carry):
        thr, cge = carry
        cand = thr + lax.shift_left(jnp.int32(1), 31 - t)
        cnt = count(lambda x: x >= cand)
        ok = cnt >= kk
        return jnp.where(ok, cand, thr), jnp.where(ok, cnt, cge)

    thr, cge = lax.fori_loop(0, 32, bit_body, (jnp.full((ROWS8, 1), INT_MIN, I32), jnp.zeros((ROWS8, 1), I32)))
    cgt = count(lambda x: x > thr)
    need = kk - cgt
    needf = need.astype(F32)
    tied = jnp.max(jnp.where(need < cge - cgt, 1, 0))
    mn_ref[0] = jnp.where(key_new >= thr, 0.0, NEG)

    @pl.when(tied == 0)
    def _():
        mp_ref[0] = jnp.where(key_ref[...] >= thr, 0.0, NEG)

    @pl.when(tied > 0)
    def _():
        off = jnp.zeros((ROWS8, 1), F32)
        triu = triu_ref[...]
        for c in range(nck):
            blk = key_ref[:, c * ck:(c + 1) * ck]
            eq = blk == thr
            pref = _dot(jnp.where(eq, 1.0, 0.0).astype(BF16), triu)
            keep = jnp.logical_or(blk > thr, jnp.logical_and(eq, pref + off <= needf))
            mp_ref[0, :, c * ck:(c + 1) * ck] = jnp.where(keep, 0.0, NEG)
            off = off + pref[:, ck - 1:ck]
        eq = key_new == thr
        pref = _dot(jnp.where(eq, 1.0, 0.0).astype(BF16), triu[0:LANES, 0:LANES])
        keep = jnp.logical_or(key_new > thr, jnp.logical_and(eq, pref + off <= needf))
        mn_ref[0] = jnp.where(keep, 0.0, NEG)


def dsa_step_select(scores, z8, l_new):
    nb, _, past = scores.shape
    topk = min(DSA_TOPK_MAX, (past + l_new) // 4)
    iqw = IDX_HEADS * IDX_DIM
    assert past % DSA_CK == 0
    triu = jnp.asarray(np.triu(np.ones((DSA_CK, DSA_CK), np.float32)), BF16)
    return pl.pallas_call(
        functools.partial(_dsa_step_select_kernel, past=past, l_new=l_new, topk=topk),
        grid=(nb,),
        in_specs=[pl.BlockSpec((1, ROWS8, past), lambda n: (n, 0, 0)),
                  pl.BlockSpec((1, ROWS8, iqw), lambda n: (n, 0, CD_COLS["iq"] // iqw)),
                  pl.BlockSpec((1, ROWS8, LANES), lambda n: (n, 0, SMALL_BLOCK)),
                  pl.BlockSpec((DSA_CK, DSA_CK), lambda n: (0, 0))],
        out_specs=[pl.BlockSpec((1, ROWS8, past), lambda n: (n, 0, 0)),
                   pl.BlockSpec((1, ROWS8, LANES), lambda n: (n, 0, 0))],
        out_shape=[jax.ShapeDtypeStruct((nb, ROWS8, past), F32),
                   jax.ShapeDtypeStruct((nb, ROWS8, LANES), F32)],
        scratch_shapes=[pltpu.VMEM((ROWS8, past), I32)],
        compiler_params=_cparams("parallel"),
        name="dsa_step_select",
    )(scores, z8, z8, triu)


def _dsa_step_attn_kernel(pt_ref, q_ref, kvn_ref, mp_ref, mn_ref, bp_ref, bn_ref, *rest, w, npg):
    kv_refs, o_ref = rest[:npg], rest[npg]
    qbd_ref, newpage_ref, m_ref, l_ref, acc_ref = rest[npg + 1:]
    n = pl.program_id(0)
    p = pl.program_id(1)
    rows = DSA_HEADS * ROWS8
    lane_h = lax.broadcasted_iota(I32, (ROWS8, w), 1) // HEAD_DIM

    @pl.when(jnp.logical_and(n == 0, p == 0))
    def _():
        newpage_ref[...] = jnp.zeros(newpage_ref.shape, F32)

    @pl.when(p == 0)
    def _():
        q = q_ref[0]
        for h in range(DSA_HEADS):
            qbd_ref[h * ROWS8:(h + 1) * ROWS8, :] = jnp.where(lane_h == h, q, jnp.zeros_like(q))
        m_ref[...] = jnp.full(m_ref.shape, NEG, F32)
        l_ref[...] = jnp.zeros(l_ref.shape, F32)
        acc_ref[...] = jnp.zeros(acc_ref.shape, F32)

    def accumulate(scores, madd8, bias, pv):
        s = scores + bias + jnp.concatenate([madd8] * DSA_HEADS, axis=0)
        m_old = m_ref[:, 0:1]
        m_new = jnp.maximum(m_old, jnp.max(s, axis=-1, keepdims=True))
        alpha = jnp.exp(m_old - m_new)
        pr = jnp.exp(s - m_new)
        l_ref[...] = jnp.broadcast_to(alpha * l_ref[:, 0:1] + jnp.sum(pr, axis=-1, keepdims=True), (rows, LANES))
        m_ref[...] = jnp.broadcast_to(m_new, (rows, LANES))
        acc_ref[...] = alpha * acc_ref[...] + pv(pr.astype(BF16))

    qbd = qbd_ref[...]
    scores = jnp.concatenate([_dot(qbd, kv_refs[j][0, 0, 0].reshape(w, PAGE_SIZE).astype(BF16))
                              for j in range(npg)], axis=1)

    def pv_pages(pr):
        out = None
        for j in range(npg):
            t = _nt(pr[:, j * PAGE_SIZE:(j + 1) * PAGE_SIZE], kv_refs[j][0, 0, 1].reshape(w, PAGE_SIZE).astype(BF16))
            out = t if out is None else out + t
        return out

    accumulate(scores, mp_ref[0], bp_ref[...], pv_pages)

    @pl.when(p == pl.num_programs(1) - 1)
    def _():
        newpage_ref[0:ROWS8, :] = kvn_ref[0]
        newp = newpage_ref[...]
        accumulate(_nt(qbd, newp[:, 0:w].astype(BF16)), mn_ref[0], bn_ref[...],
                   lambda pr: _dot(pr, newp[:, w:2 * w].astype(BF16)))
        for h in range(DSA_HEADS):
            rs = slice(h * ROWS8, (h + 1) * ROWS8)
            hs = slice(h * HEAD_DIM, (h + 1) * HEAD_DIM)
            o_ref[0, :, hs] = (acc_ref[rs, hs] / l_ref[rs, 0:1]).astype(BF16)


def _dsa_step_bias(tab_d, past, l_new):
    lq = np.minimum(np.arange(ROWS8), l_new - 1)[:, None]
    d_past = past + lq - np.arange(past)[None, :]
    d_new = lq - np.arange(LANES)[None, :]

    def table(d):
        return bias_lookup(tab_d, _rel_bucket_np(d)).reshape(DSA_HEADS * ROWS8, d.shape[1])

    return table(d_past), table(d_new)


def dsa_step_attend(dq8, kv_new8, kv_t, layer, page_table, mask_past, mask_new, tab_d, l_new):
    nb, _, w = dq8.shape
    n_pages = page_table.shape[1]
    npg = math.gcd(ATTN_PAGES, n_pages)
    past = n_pages * PAGE_SIZE
    rows = DSA_HEADS * ROWS8
    bp, bn = _dsa_step_bias(tab_d, past, l_new)

    def page_spec(j):
        return pl.BlockSpec((1, 1, 2, DSA_HEADS, HEAD_DIM, PAGE_SIZE),
                            lambda n, p, pt: (layer, pt[n * n_pages + p * npg + j], 0, 0, 0, 0))

    grid_spec = pltpu.PrefetchScalarGridSpec(
        num_scalar_prefetch=1,
        grid=(nb, n_pages // npg),
        in_specs=[pl.BlockSpec((1, ROWS8, w), lambda n, p, pt: (n, 0, 0)),
                  pl.BlockSpec((1, ROWS8, 2 * w), lambda n, p, pt: (n, 0, 0)),
                  pl.BlockSpec((1, ROWS8, npg * PAGE_SIZE), lambda n, p, pt: (n, 0, p)),
                  pl.BlockSpec((1, ROWS8, LANES), lambda n, p, pt: (n, 0, 0)),
                  pl.BlockSpec((rows, npg * PAGE_SIZE), lambda n, p, pt: (0, p)),
                  pl.BlockSpec((rows, LANES), lambda n, p, pt: (0, 0))]
                 + [page_spec(j) for j in range(npg)],
        out_specs=pl.BlockSpec((1, ROWS8, w), lambda n, p, pt: (n, 0, 0)),
        scratch_shapes=[pltpu.VMEM((rows, w), BF16), pltpu.VMEM((PAGE_SIZE, 2 * w), F32),
                        pltpu.VMEM((rows, LANES), F32), pltpu.VMEM((rows, LANES), F32), pltpu.VMEM((rows, w), F32)],
    )
    return pl.pallas_call(
        functools.partial(_dsa_step_attn_kernel, w=w, npg=npg),
        grid_spec=grid_spec,
        out_shape=jax.ShapeDtypeStruct((nb, ROWS8, w), BF16),
        compiler_params=_cparams("arbitrary", "arbitrary"),
        name="dsa_step_attend",
    )(page_table.reshape(-1), dq8, kv_new8, mask_past, mask_new, bp, bn, *([kv_t] * npg))


def _pad_rows(x3, rows):
    return jnp.pad(x3, ((0, 0), (0, rows - x3.shape[1]), (0, 0)))


def _trunk(x, is_step, conv_state, win_states, gla_state, dsa_kv, dsa_kidx, page_table, wts):
    (norm_mix, norm_ffn, w_in_ab, conv_w, conv_b, conv_ln_g, conv_ln_b, qn_ab, kn_ab, w_out_ab,
     w_in_cd, gla_wa2, gla_ba, gla_norm, qn_cd, kn_cd, w_out_cd, rel_bias, w_g, w_u, w_d) = wts
    nb, l, d = x.shape
    t = nb * l
    depth = norm_mix.shape[0]
    c = conv_w.shape[2]
    wq = B_HPG * HEAD_DIM
    x2 = x.reshape(t, d)
    conv_new, gla_new, kv_new, kidx_new = [], [], [], []
    win_new = [[] for _ in WINDOWS]
    for layer in range(depth):
        i = layer // 2
        if layer % 2 == 0:
            z = norm_matmul(x2, norm_mix[layer], w_in_ab[i])
            hist = conv_state[i] if is_step else jnp.zeros((nb, CONV_WIDTH - 1, c), F32)
            a_out, c_st = conv_module(z.reshape(nb, l, -1), hist, conv_w[i], conv_b[i], conv_ln_g[i], conv_ln_b[i])
            conv_new.append(c_st)
            if is_step:
                qs, kvs = ab_qkv(z, qn_ab[i], kn_ab[i], 2 * c, t, (1,) * len(WINDOWS))
            else:
                qs, kvs = ab_qkv(z, qn_ab[i], kn_ab[i], 2 * c, l, DILATIONS)
            outs, lses = [], []
            for g, window in enumerate(WINDOWS):
                tab_g = rel_bias[:, g * B_HPG:(g + 1) * B_HPG]
                if is_step:
                    o, lse, st = win_step(_pad_rows(qs[g].reshape(nb, l, wq), ROWS8),
                                          kvs[g].reshape(nb, l, 2 * wq), win_states[g], i, g, tab_g)
                else:
                    o, lse = win_prompt(qs[g], kvs[g], g, tab_g)
                    keep = min(window, l)
                    st = kvs[g][:, -(keep // DILATIONS[g]):].reshape(nb, keep, 2, B_HPG, HEAD_DIM)
                outs.append(o.reshape(t, wq))
                lses.append(lse.reshape(t, wq))
                win_new[g].append(st)
            m1, m2 = a_out.reshape(t, c), ab_merge(outs, lses)
            wo = w_out_ab[i]
        else:
            z = norm_matmul(x2, norm_mix[layer], w_in_cd[i])
            pre = cd_pre(z, gla_wa2[i], gla_ba[i], qn_cd[i], kn_cd[i], None if is_step else l)
            gdec, dq, kv = pre[:3]
            z3 = z.reshape(nb, l, -1)
            wd = DSA_HEADS * HEAD_DIM
            ik = z3[:, :, CD_COLS["ik"]:CD_COLS["ik"] + IDX_DIM]
            tab_d = rel_bias[:, B_HEADS:]
            if is_step:
                lp = GLA_CHUNK
                o_c, s_c = gla(_pad_rows(z3, lp), _pad_rows(gdec.reshape(nb, l, -1), lp), gla_state[i], gla_norm[i])
                o_c = o_c[:, :l]
                z8 = _pad_rows(z3, ROWS8)
                scores = dsa_step_scores(z8, dsa_kidx, i, page_table)
                mask_p, mask_n = dsa_step_select(scores, z8, l)
                o_d = dsa_step_attend(_pad_rows(dq.reshape(nb, l, wd), ROWS8), _pad_rows(kv.reshape(nb, l, 2 * wd), ROWS8),
                                      dsa_kv, i, page_table, mask_p, mask_n, tab_d, l)[:, :l]
                kv_st = kv.reshape(nb, l, 2, DSA_HEADS, HEAD_DIM)
                ki_st = ik
            else:
                s0 = jnp.zeros((nb, GLA_HEADS, GLA_DK, GLA_DV), F32)
                o_c, s_c = gla(z3, gdec.reshape(nb, l, -1), s0, gla_norm[i])
                kb, vt, smt = pre[3:]
                mask = dsa_select(z3, smt)
                o_d = dsa_attend(dq.reshape(nb, l, wd), kb.reshape(nb, l, wd), vt, mask, tab_d)
                n_pg = l // PAGE_SIZE
                kv_st = kv.reshape(nb, n_pg, PAGE_SIZE, 2, DSA_HEADS, HEAD_DIM)
                ki_st = ik.reshape(nb, n_pg, PAGE_SIZE, IDX_DIM)
            gla_new.append(s_c)
            kv_new.append(kv_st)
            kidx_new.append(ki_st)
            m1, m2 = o_c.reshape(t, -1), o_d.reshape(t, -1)
            wo = w_out_cd[i]
        d1 = m1.shape[1]
        x2 = mix_ffn(x2, m1, m2, wo[:d1], wo[d1:], norm_ffn[layer], w_g[layer], w_u[layer], w_d[layer])
    wins = [jnp.stack(ws) for ws in win_new]
    if is_step:
        wins = [ws.transpose(0, 1, 5, 2, 3, 4) for ws in wins]
    states = (jnp.stack(conv_new), wins[0], wins[1], wins[2],
              jnp.stack(gla_new), jnp.stack(kv_new), jnp.stack(kidx_new))
    return x2.reshape(nb, l, d), states


def kernel(x_prompt, x_sample, state_conv, cache_win128, cache_win512, cache_win2048, state_gla, cache_dsa_kv, cache_dsa_kidx, page_table, norm_mix, norm_ffn, w_in_ab, conv_w, conv_b, conv_ln_g, conv_ln_b, qn_ab, kn_ab, w_out_ab, w_in_cd, gla_wa2, gla_ba, gla_norm, qn_cd, kn_cd, w_out_cd, rel_bias, w_ffn_gate, w_ffn_up, w_ffn_down):
    bf = lambda a: a.astype(BF16)
    w_in_cd_r = jnp.stack([cd_reorder_w(w_in_cd[i]) for i in range(w_in_cd.shape[0])])
    wts = (norm_mix, norm_ffn, bf(w_in_ab), conv_w, conv_b, conv_ln_g, conv_ln_b, qn_ab, kn_ab, bf(w_out_ab),
           bf(w_in_cd_r), gla_wa2, gla_ba, gla_norm, qn_cd, kn_cd, bf(w_out_cd), rel_bias,
           bf(w_ffn_gate), bf(w_ffn_up), bf(w_ffn_down))
    y_p, sp = _trunk(x_prompt, False, None, None, None, None, None, None, wts)
    wins_t = tuple(cw.transpose(0, 1, 3, 4, 5, 2) for cw in (cache_win128, cache_win512, cache_win2048))
    y_s, ss = _trunk(x_sample, True, state_conv, wins_t, state_gla,
                     cache_dsa_kv.transpose(0, 1, 3, 4, 5, 2), cache_dsa_kidx.transpose(0, 1, 3, 2), page_table, wts)
    conv_p, win128_p, win512_p, win2048_p, gla_p, dsa_kv_p, dsa_kidx_p = sp
    conv_s, win128_s, win512_s, win2048_s, gla_s, dsa_kv_s, dsa_kidx_s = ss
    return (y_p, y_s, conv_p, conv_s, win128_p, win128_s, win512_p, win512_s, win2048_p, win2048_s,
            gla_p, gla_s, dsa_kv_p, dsa_kv_s, dsa_kidx_p, dsa_kidx_s)
```

```python
import functools
import math

import numpy as np
import jax
import jax.numpy as jnp
from jax import lax
from jax.experimental import pallas as pl
from jax.experimental.pallas import tpu as pltpu

F32 = jnp.float32
BF16 = jnp.bfloat16
I32 = jnp.int32

EPS = 1e-6
NEG = -1e30
LOG2E = math.log2(math.e)
INT_MIN = -(2 ** 31)

V7X_VMEM_BYTES = 64 * 1024 * 1024
VMEM_LIMIT = V7X_VMEM_BYTES - 12 * 1024 * 1024
LANES = 128

HEAD_DIM = 64
CONV_WIDTH = 31
WINDOWS = (128, 512, 2048)
DILATIONS = (1, 4, 16)
B_HPG = 4
B_HEADS = B_HPG * len(WINDOWS)
SW_BLOCK = 128
GLA_HEADS = 4
GLA_DK = 64
GLA_DV = 128
GLA_GATE_RANK = 16
GLA_TAU = 16.0
DSA_HEADS = 8
IDX_HEADS = 4
IDX_DIM = 64
DSA_TOPK_MAX = 256
PAGE_SIZE = 128
REL_BUCKETS = 32
REL_MAX_DIST = 2048

GLA_CHUNK = 128
GLA_STEP_ROWS = 64
DSA_QB = 128
DSA_SQ = 256
DSA_CK = 512
HIST_PAD = 32


def _cparams(*sem):
    return pltpu.CompilerParams(dimension_semantics=sem, vmem_limit_bytes=VMEM_LIMIT)


def _nt(a, b):
    return lax.dot_general(a, b, (((1,), (1,)), ((), ())), preferred_element_type=F32)


def _dot(a, b):
    return jnp.dot(a, b, preferred_element_type=F32)


def _split2(x):
    hi = x.astype(BF16)
    lo = (x - hi.astype(F32)).astype(BF16)
    return hi, lo


def _split3(x):
    hi = x.astype(BF16)
    r = x - hi.astype(F32)
    mid = r.astype(BF16)
    lo = (r - mid.astype(F32)).astype(BF16)
    return hi, mid, lo


def _rel_bucket_np(dist):
    n = np.maximum(np.asarray(dist, np.int64), 0)
    max_exact = REL_BUCKETS // 2
    nf = np.maximum(n, max_exact).astype(np.float32)
    large = max_exact + (np.log(nf / np.float32(max_exact)) / np.float32(math.log(REL_MAX_DIST / max_exact))
                         * np.float32(REL_BUCKETS - max_exact)).astype(np.int32)
    large = np.minimum(large, REL_BUCKETS - 1)
    return np.where(n < max_exact, n, large).astype(np.int32)


def _bias_lookup_kernel(tab_ref, idx_ref, o_ref, *, nh):
    idx = idx_ref[...]
    for h in range(nh):
        acc = jnp.full(idx.shape, NEG, F32)
        for b in range(REL_BUCKETS):
            acc = jnp.where(idx == b, tab_ref[b, h], acc)
        o_ref[h] = acc


def bias_lookup(tab, idx_np):
    r, c = idx_np.shape
    nh = tab.shape[1]
    tr = 8 if (c >= 2048 and r % 8 == 0) else (128 if r % 128 == 0 else r)
    return pl.pallas_call(
        functools.partial(_bias_lookup_kernel, nh=nh),
        grid=(r // tr,),
        in_specs=[pl.BlockSpec(memory_space=pltpu.SMEM),
                  pl.BlockSpec((tr, c), lambda i: (i, 0))],
        out_specs=pl.BlockSpec((nh, tr, c), lambda i: (0, i, 0)),
        out_shape=jax.ShapeDtypeStruct((nh, r, c), F32),
        compiler_params=_cparams("parallel"),
        name="bias_lookup",
    )(tab, jnp.asarray(idx_np.astype(np.int32)))


def _norm_matmul_kernel(x_ref, g_ref, w_ref, o_ref):
    x = x_ref[...]
    y = x * lax.rsqrt(jnp.mean(x * x, axis=-1, keepdims=True) + EPS) * g_ref[...]
    o_ref[...] = _dot(y.astype(BF16), w_ref[...])


def norm_matmul(x, g, w):
    t, d = x.shape
    n = w.shape[1]
    tm = min(t, 256)
    return pl.pallas_call(
        _norm_matmul_kernel,
        grid=(t // tm,),
        in_specs=[pl.BlockSpec((tm, d), lambda i: (i, 0)),
                  pl.BlockSpec((1, d), lambda i: (0, 0)),
                  pl.BlockSpec((d, n), lambda i: (0, 0))],
        out_specs=pl.BlockSpec((tm, n), lambda i: (i, 0)),
        out_shape=jax.ShapeDtypeStruct((t, n), F32),
        compiler_params=_cparams("parallel"),
        name="norm_matmul",
    )(x, g.reshape(1, d), w)


def _mix_ffn_kernel(x_ref, m1_ref, m2_ref, wo1_ref, wo2_ref, g_ref, wg_ref, wu_ref, wd_ref, o_ref,
                    x1_ref, hf_ref, acc_ref):
    j = pl.program_id(1)

    @pl.when(j == 0)
    def _():
        x1 = x_ref[...] + _dot(m1_ref[...], wo1_ref[...]) + _dot(m2_ref[...], wo2_ref[...])
        x1_ref[...] = x1
        hf = x1 * lax.rsqrt(jnp.mean(x1 * x1, axis=-1, keepdims=True) + EPS) * g_ref[...]
        hf_ref[...] = hf.astype(BF16)
        acc_ref[...] = jnp.zeros_like(acc_ref)

    hf = hf_ref[...]
    a = _dot(hf, wg_ref[...])
    u = _dot(hf, wu_ref[...])
    act = (a * jax.nn.sigmoid(a) * u).astype(BF16)
    acc_ref[...] += _dot(act, wd_ref[...])

    @pl.when(j == pl.num_programs(1) - 1)
    def _():
        o_ref[...] = x1_ref[...] + acc_ref[...]


def mix_ffn(x, m1, m2, wo1, wo2, g, wg, wu, wd):
    t, d = x.shape
    hid = wg.shape[1]
    tm = min(t, 512)
    th = hid // 2 if hid % 256 == 0 else hid
    d1, d2 = m1.shape[1], m2.shape[1]
    return pl.pallas_call(
        _mix_ffn_kernel,
        grid=(t // tm, hid // th),
        in_specs=[pl.BlockSpec((tm, d), lambda i, j: (i, 0)),
                  pl.BlockSpec((tm, d1), lambda i, j: (i, 0)),
                  pl.BlockSpec((tm, d2), lambda i, j: (i, 0)),
                  pl.BlockSpec((d1, d), lambda i, j: (0, 0)),
                  pl.BlockSpec((d2, d), lambda i, j: (0, 0)),
                  pl.BlockSpec((1, d), lambda i, j: (0, 0)),
                  pl.BlockSpec((d, th), lambda i, j: (0, j)),
                  pl.BlockSpec((d, th), lambda i, j: (0, j)),
                  pl.BlockSpec((th, d), lambda i, j: (j, 0))],
        out_specs=pl.BlockSpec((tm, d), lambda i, j: (i, 0)),
        out_shape=jax.ShapeDtypeStruct((t, d), F32),
        scratch_shapes=[pltpu.VMEM((tm, d), F32), pltpu.VMEM((tm, d), BF16), pltpu.VMEM((tm, d), F32)],
        compiler_params=_cparams("parallel", "arbitrary"),
        name="mix_ffn",
    )(x, m1, m2, wo1, wo2, g.reshape(1, d), wg, wu, wd)


def _conv_kernel(z_ref, hist_ref, w_ref, b_ref, lg_ref, lb_ref, o_ref, tail_ref, uh_ref, *, ts, c):
    t = pl.program_id(1)

    @pl.when(t == 0)
    def _():
        uh_ref[0:HIST_PAD, :] = hist_ref[0]

    z = z_ref[0]
    u = z[:, 0:c] * jax.nn.sigmoid(z[:, c:2 * c])
    uh_ref[HIST_PAD:HIST_PAD + ts, :] = u
    acc = jnp.zeros((ts, c), F32) + b_ref[...]
    off = HIST_PAD - (CONV_WIDTH - 1)
    for j in range(CONV_WIDTH):
        acc = acc + w_ref[j:j + 1, :] * uh_ref[off + j:off + j + ts, :]
    mu = jnp.mean(acc, axis=-1, keepdims=True)
    var = jnp.mean(jnp.square(acc - mu), axis=-1, keepdims=True)
    yn = (acc - mu) * lax.rsqrt(var + EPS) * lg_ref[...] + lb_ref[...]
    o_ref[0] = (yn * jax.nn.sigmoid(yn)).astype(BF16)
    tail = uh_ref[ts:ts + HIST_PAD, :]
    uh_ref[0:HIST_PAD, :] = tail
    tail_ref[0] = tail


def conv_module(z3, hist, conv_w, conv_b, ln_g, ln_b):
    nb, l, _ = z3.shape
    c = conv_w.shape[1]
    ts = min(l, 512)
    hist_p = jnp.pad(hist, ((0, 0), (HIST_PAD - (CONV_WIDTH - 1), 0), (0, 0)))
    w_p = jnp.pad(conv_w, ((0, HIST_PAD - CONV_WIDTH), (0, 0)))
    a_out, tail = pl.pallas_call(
        functools.partial(_conv_kernel, ts=ts, c=c),
        grid=(nb, l // ts),
        in_specs=[pl.BlockSpec((1, ts, 2 * c), lambda n, t: (n, t, 0)),
                  pl.BlockSpec((1, HIST_PAD, c), lambda n, t: (n, 0, 0)),
                  pl.BlockSpec((HIST_PAD, c), lambda n, t: (0, 0)),
                  pl.BlockSpec((1, c), lambda n, t: (0, 0)),
                  pl.BlockSpec((1, c), lambda n, t: (0, 0)),
                  pl.BlockSpec((1, c), lambda n, t: (0, 0))],
        out_specs=[pl.BlockSpec((1, ts, c), lambda n, t: (n, t, 0)),
                   pl.BlockSpec((1, HIST_PAD, c), lambda n, t: (n, 0, 0))],
        out_shape=[jax.ShapeDtypeStruct((nb, l, c), BF16),
                   jax.ShapeDtypeStruct((nb, HIST_PAD, c), F32)],
        scratch_shapes=[pltpu.VMEM((HIST_PAD + ts, c), F32)],
        compiler_params=_cparams("parallel", "arbitrary"),
        name="conv_module",
    )(z3, hist_p, w_p, conv_b.reshape(1, c), ln_g.reshape(1, c), ln_b.reshape(1, c))
    return a_out, tail[:, HIST_PAD - (CONV_WIDTH - 1):]


def _seg_mean_sq(x, seg_ref):
    hi, lo = _split2(x * x)
    seg = seg_ref[...]
    return (_dot(hi, seg) + _dot(lo, seg)) * (1.0 / HEAD_DIM)


def _head_norm(x, g, seg_ref):
    return x * lax.rsqrt(_seg_mean_sq(x, seg_ref) + EPS) * g


def _seg_matrix(width):
    idx = np.arange(width) // HEAD_DIM
    return jnp.asarray((idx[:, None] == idx[None, :]).astype(np.float32), BF16)


def _ab_qkv_kernel(*refs, w, tm, dils):
    ng = len(dils)
    qkv_refs = refs[:3 * ng]
    qg_ref, kg_ref, seg_ref = refs[3 * ng:3 * ng + 3]
    qo_refs = refs[3 * ng + 3:4 * ng + 3]
    kvo_refs = refs[4 * ng + 3:5 * ng + 3]
    qs_ref, kvs_ref = refs[5 * ng + 3:]
    for g, dil in enumerate(dils):
        qn = _head_norm(qkv_refs[g][...], qg_ref[...], seg_ref) * HEAD_DIM ** -0.5
        kn = _head_norm(qkv_refs[ng + g][...], kg_ref[...], seg_ref)
        v = qkv_refs[2 * ng + g][...]
        if dil == 1:
            qo_refs[g][0] = qn.astype(BF16)
            kvo_refs[g][0, :, 0:w] = kn
            kvo_refs[g][0, :, w:2 * w] = v
        else:
            nq, nkv = w // LANES, 2 * w // LANES
            kv = jnp.concatenate([kn, v], axis=1)
            for j in range(nq):
                qs_ref[j] = qn[:, j * LANES:(j + 1) * LANES]
            for j in range(nkv):
                kvs_ref[j] = kv[:, j * LANES:(j + 1) * LANES]
            for r in range(dil):
                rows = pl.ds(r, tm // dil, stride=dil)
                for j in range(nq):
                    qo_refs[g][0, :, r * w + j * LANES:r * w + (j + 1) * LANES] = qs_ref[j, rows, :].astype(BF16)
                for j in range(nkv):
                    kvo_refs[g][0, :, r * 2 * w + j * LANES:r * 2 * w + (j + 1) * LANES] = kvs_ref[j, rows, :]


def ab_qkv(z, qn, kn, col0, seq, dils):
    t = z.shape[0]
    w = B_HPG * HEAD_DIM
    ng = len(dils)
    tm = min(seq, 512)
    nt = seq // tm
    cb = col0 // w
    assert all(tm % (8 * d) == 0 or d == 1 for d in dils)
    col_spec = lambda j: pl.BlockSpec((tm, w), lambda i: (i, cb + j))
    out_specs = ([pl.BlockSpec((1, tm // d, d * w), lambda i: (i // nt, i % nt, 0)) for d in dils]
                 + [pl.BlockSpec((1, tm // d, d * 2 * w), lambda i: (i // nt, i % nt, 0)) for d in dils])
    out_shape = ([jax.ShapeDtypeStruct((t // seq, seq // d, d * w), BF16) for d in dils]
                 + [jax.ShapeDtypeStruct((t // seq, seq // d, d * 2 * w), F32) for d in dils])
    outs = pl.pallas_call(
        functools.partial(_ab_qkv_kernel, w=w, tm=tm, dils=tuple(dils)),
        grid=(t // tm,),
        in_specs=[col_spec(j) for j in range(3 * ng)]
                 + [pl.BlockSpec((1, w), lambda i: (0, 0)),
                    pl.BlockSpec((1, w), lambda i: (0, 0)),
                    pl.BlockSpec((w, w), lambda i: (0, 0))],
        out_specs=out_specs,
        out_shape=out_shape,
        scratch_shapes=[pltpu.VMEM((w // LANES, tm, LANES), F32), pltpu.VMEM((2 * w // LANES, tm, LANES), F32)],
        compiler_params=_cparams("parallel"),
        name="ab_qkv",
    )(*([z] * (3 * ng)), jnp.tile(qn, B_HPG).reshape(1, w), jnp.tile(kn, B_HPG).reshape(1, w), _seg_matrix(w))
    return outs[:ng], outs[ng:]


WIN_BLOCKS = 4


def _win_prompt_kernel(q_ref, kvp_ref, kvc_ref, bias_ref, o_ref, lse_ref, *, w, nblk):
    step = pl.program_id(2)
    col = lax.broadcasted_iota(I32, (SW_BLOCK, 2 * SW_BLOCK), 1)
    first = jnp.logical_and(step == 0, col < SW_BLOCK)
    for j in range(nblk):
        rows = slice(j * SW_BLOCK, (j + 1) * SW_BLOCK)
        q = q_ref[0, rows, :]
        kvc = kvc_ref[0, rows, :]
        kvp = kvp_ref[0] if j == 0 else kvc_ref[0, (j - 1) * SW_BLOCK:j * SW_BLOCK, :]
        for h in range(B_HPG):
            hs = slice(h * HEAD_DIM, (h + 1) * HEAD_DIM)
            vs = slice(w + h * HEAD_DIM, w + (h + 1) * HEAD_DIM)
            k2 = jnp.concatenate([kvp[:, hs], kvc[:, hs]], axis=0).astype(BF16)
            v2 = jnp.concatenate([kvp[:, vs], kvc[:, vs]], axis=0).astype(BF16)
            s = _nt(q[:, hs], k2) + bias_ref[h]
            if j == 0:
                s = jnp.where(first, NEG, s)
            m = jnp.max(s, axis=-1, keepdims=True)
            p = jnp.exp(s - m)
            l = jnp.sum(p, axis=-1, keepdims=True)
            o_ref[0, rows, hs] = _dot(p.astype(BF16), v2) / l
            lse_ref[0, rows, hs] = jnp.broadcast_to(m + jnp.log(l), (SW_BLOCK, HEAD_DIM))


def _win_prompt_bias(tab_g, dil, reach):
    ql = np.arange(SW_BLOCK)[:, None]
    kl = np.arange(2 * SW_BLOCK)[None, :] - SW_BLOCK
    rel = ql - kl
    ok = (rel >= 0) & (rel <= reach)
    return bias_lookup(tab_g, np.where(ok, _rel_bucket_np(rel * dil), -1))


def win_prompt(qv, kvv, g, tab_g):
    w = B_HPG * HEAD_DIM
    dil = DILATIONS[g]
    b, n, _ = qv.shape
    s = n * dil
    nb = n // SW_BLOCK
    assert nb * SW_BLOCK * dil == s
    bias = _win_prompt_bias(tab_g, dil, WINDOWS[g] // dil)
    nblk = math.gcd(WIN_BLOCKS, nb)
    rows = nblk * SW_BLOCK
    o, lse = pl.pallas_call(
        functools.partial(_win_prompt_kernel, w=w, nblk=nblk),
        grid=(b, dil, nb // nblk),
        in_specs=[pl.BlockSpec((1, rows, w), lambda n_, r, k: (n_, k, r)),
                  pl.BlockSpec((1, SW_BLOCK, 2 * w), lambda n_, r, k: (n_, jnp.maximum(k * nblk - 1, 0), r)),
                  pl.BlockSpec((1, rows, 2 * w), lambda n_, r, k: (n_, k, r)),
                  pl.BlockSpec((B_HPG, SW_BLOCK, 2 * SW_BLOCK), lambda n_, r, k: (0, 0, 0))],
        out_specs=[pl.BlockSpec((1, rows, w), lambda n_, r, k: (n_, k, r)),
                   pl.BlockSpec((1, rows, w), lambda n_, r, k: (n_, k, r))],
        out_shape=[jax.ShapeDtypeStruct((b, n, dil * w), F32),
                   jax.ShapeDtypeStruct((b, n, dil * w), F32)],
        compiler_params=_cparams("parallel", "parallel", "arbitrary"),
        name=f"win_prompt_g{g}",
    )(qv, kvv, kvv, bias)
    return o.reshape(b, s, w), lse.reshape(b, s, w)


ROWS8 = 8


def _win_step_kernel(q_ref, kvn_ref, newt_ref, buf_ref, tb_ref, tn_ref, o_ref, lse_ref, st_ref,
                     qbd_ref, newpage_ref, *, w, wb, l):
    n = pl.program_id(0)

    @pl.when(n == 0)
    def _():
        newpage_ref[...] = jnp.zeros(newpage_ref.shape, F32)

    q = q_ref[0]
    lane_h = lax.broadcasted_iota(I32, (ROWS8, w), 1) // HEAD_DIM
    for h in range(B_HPG):
        qbd_ref[h * ROWS8:(h + 1) * ROWS8, :] = jnp.where(lane_h == h, q, jnp.zeros_like(q))
    newpage_ref[0:l, :] = kvn_ref[0]
    qbd = qbd_ref[...]
    newp = newpage_ref[...]
    buf = buf_ref[0, 0]
    kt = buf[0].reshape(w, wb).astype(BF16)
    vt = buf[1].reshape(w, wb).astype(BF16)
    s1 = _dot(qbd, kt) + tb_ref[...]
    s2 = _nt(qbd, newp[:, 0:w].astype(BF16)) + tn_ref[...]
    m = jnp.maximum(jnp.max(s1, axis=-1, keepdims=True), jnp.max(s2, axis=-1, keepdims=True))
    p1 = jnp.exp(s1 - m)
    p2 = jnp.exp(s2 - m)
    den = jnp.sum(p1, axis=-1, keepdims=True) + jnp.sum(p2, axis=-1, keepdims=True)
    num = _nt(p1.astype(BF16), vt) + _dot(p2.astype(BF16), newp[:, w:2 * w].astype(BF16))
    o = num / den
    lse = m + jnp.log(den)
    for h in range(B_HPG):
        rs = slice(h * ROWS8, (h + 1) * ROWS8)
        hs = slice(h * HEAD_DIM, (h + 1) * HEAD_DIM)
        o_ref[0, :, hs] = o[rs, hs]
        lse_ref[0, :, hs] = jnp.broadcast_to(lse[rs], (ROWS8, HEAD_DIM))
    rolled = pltpu.roll(buf.reshape(2 * w, wb), wb - l, 1)
    lane = lax.broadcasted_iota(I32, (2 * w, LANES), 1)
    last = jnp.where(lane >= LANES - l, newt_ref[0], rolled[:, wb - LANES:wb])
    if wb > LANES:
        st_ref[0, :, :, :, 0:wb - LANES] = rolled[:, 0:wb - LANES].reshape(2, B_HPG, HEAD_DIM, wb - LANES)
    st_ref[0, :, :, :, wb - LANES:wb] = last.reshape(2, B_HPG, HEAD_DIM, LANES)


def _win_step_bias(tab_g, dil, window, wb, l):
    lq = np.minimum(np.arange(ROWS8), l - 1)[:, None]
    dist_b = wb + lq - np.arange(wb)[None, :]
    ok_b = (dist_b % dil == 0) & (dist_b // dil <= window // dil)
    cols = np.arange(LANES)[None, :]
    dist_n = lq - cols
    ok_n = (cols < l) & (dist_n >= 0) & (dist_n % dil == 0) & (dist_n // dil <= window // dil)
    rows = B_HPG * ROWS8
    tb = bias_lookup(tab_g, np.where(ok_b, _rel_bucket_np(dist_b), -1)).reshape(rows, wb)
    tn = bias_lookup(tab_g, np.where(ok_n, _rel_bucket_np(dist_n), -1)).reshape(rows, LANES)
    return tb, tn


def win_step(q8, kv_new, cache_t, layer, g, tab_g):
    nb = q8.shape[0]
    l = kv_new.shape[1]
    w = B_HPG * HEAD_DIM
    wb = cache_t.shape[-1]
    window, dil = WINDOWS[g], DILATIONS[g]
    assert wb == window, "the step kernel keeps a full window of rows"
    tb, tn = _win_step_bias(tab_g, dil, window, wb, l)
    rows = B_HPG * ROWS8
    new_t = jnp.pad(kv_new.transpose(0, 2, 1), ((0, 0), (0, 0), (LANES - l, 0)))
    o, lse, st = pl.pallas_call(
        functools.partial(_win_step_kernel, w=w, wb=wb, l=l),
        grid=(nb,),
        in_specs=[pl.BlockSpec((1, ROWS8, w), lambda n: (n, 0, 0)),
                  pl.BlockSpec((1, l, 2 * w), lambda n: (n, 0, 0)),
                  pl.BlockSpec((1, 2 * w, LANES), lambda n: (n, 0, 0)),
                  pl.BlockSpec((1, 1, 2, B_HPG, HEAD_DIM, wb), lambda n: (layer, n, 0, 0, 0, 0)),
                  pl.BlockSpec((rows, wb), lambda n: (0, 0)),
                  pl.BlockSpec((rows, LANES), lambda n: (0, 0))],
        out_specs=[pl.BlockSpec((1, ROWS8, w), lambda n: (n, 0, 0)),
                   pl.BlockSpec((1, ROWS8, w), lambda n: (n, 0, 0)),
                   pl.BlockSpec((1, 2, B_HPG, HEAD_DIM, wb), lambda n: (n, 0, 0, 0, 0))],
        out_shape=[jax.ShapeDtypeStruct((nb, ROWS8, w), F32),
                   jax.ShapeDtypeStruct((nb, ROWS8, w), F32),
                   jax.ShapeDtypeStruct((nb, 2, B_HPG, HEAD_DIM, wb), F32)],
        scratch_shapes=[pltpu.VMEM((rows, w), BF16), pltpu.VMEM((LANES, 2 * w), F32)],
        compiler_params=_cparams("arbitrary"),
        name=f"win_step_g{g}",
    )(q8, kv_new, new_t, cache_t, tb, tn)
    return o[:, :l], lse[:, :l], st


def _ab_merge_kernel(o0, o1, o2, l0, l1, l2, out_ref):
    a, b, c = l0[...], l1[...], l2[...]
    m = jnp.maximum(jnp.maximum(a, b), c)
    ea, eb, ec = jnp.exp(a - m), jnp.exp(b - m), jnp.exp(c - m)
    out_ref[...] = ((ea * o0[...] + eb * o1[...] + ec * o2[...]) / (ea + eb + ec)).astype(BF16)


def ab_merge(outs, lses):
    t, w = outs[0].shape
    tm = min(t, 1024)
    spec = pl.BlockSpec((tm, w), lambda i: (i, 0))
    return pl.pallas_call(
        _ab_merge_kernel,
        grid=(t // tm,),
        in_specs=[spec] * 6,
        out_specs=spec,
        out_shape=jax.ShapeDtypeStruct((t, w), BF16),
        compiler_params=_cparams("parallel"),
        name="ab_merge",
    )(*outs, *lses)


CD_COLS = dict(cq=0, ck=256, cv=512, cgate=1024, dq=1536, dk=2048, dv=2560, iq=3072, ik=3328, clr=3392, iw=3408)
CD_PAD = 3456
SMALL_BLOCK = CD_COLS["ik"] // LANES
SM_IK, SM_CLR, SM_IW = 0, CD_COLS["clr"] - CD_COLS["ik"], CD_COLS["iw"] - CD_COLS["ik"]


def cd_reorder_w(w_in):
    sizes = (256, 256, 512, 16, 512, 512, 512, 512, 256, 64, 4)
    names = ("cq", "ck", "cv", "clr", "cgate", "dq", "dk", "dv", "iq", "ik", "iw")
    starts = np.concatenate([[0], np.cumsum(sizes)[:-1]])
    out = jnp.zeros((w_in.shape[0], CD_PAD), w_in.dtype)
    for nm, st, sz in zip(names, starts, sizes):
        out = lax.dynamic_update_slice(out, w_in[:, st:st + sz], (0, CD_COLS[nm]))
    return out


def _cd_pre_kernel(d_ref, sm_ref, wa_ref, ba_ref, qg_ref, kg_ref, seg_ref, g_ref, dq_ref, kv_ref, *t_refs, w):
    d = d_ref[...]
    sm = sm_ref[...]
    clr = sm[:, SM_CLR:SM_CLR + GLA_GATE_RANK]
    wa = wa_ref[...]
    pre = ba_ref[...]
    for part in _split3(clr):
        pre = pre + _dot(part, wa[0]) + _dot(part, wa[1])
    lsig = jnp.minimum(pre, 0.0) - jnp.log(1.0 + jnp.exp(-jnp.abs(pre)))
    g_ref[...] = lsig * (1.0 / GLA_TAU)
    q_scale = HEAD_DIM ** -0.5 * (LOG2E if t_refs else 1.0)
    dq_ref[...] = (_head_norm(d[:, 0:w], qg_ref[...], seg_ref) * q_scale).astype(BF16)
    kn = _head_norm(d[:, w:2 * w], kg_ref[...], seg_ref)
    v = d[:, 2 * w:3 * w]
    kv_ref[:, 0:w] = kn
    kv_ref[:, w:2 * w] = v
    if t_refs:
        kb_ref, vt_ref, smt_ref = t_refs
        kb_ref[...] = kn.astype(BF16)
        vt_ref[0] = v.T.astype(BF16)
        smt_ref[0] = sm.T


def cd_pre(z, wa2, ba, qn, kn, seq=None):
    t = z.shape[0]
    w = DSA_HEADS * HEAD_DIM
    gw = GLA_HEADS * GLA_DK
    tm = min(t, 512)
    wa_hi, wa_lo = _split2(wa2)
    out_specs = [pl.BlockSpec((tm, gw), lambda i: (i, 0)),
                 pl.BlockSpec((tm, w), lambda i: (i, 0)),
                 pl.BlockSpec((tm, 2 * w), lambda i: (i, 0))]
    out_shape = [jax.ShapeDtypeStruct((t, gw), F32),
                 jax.ShapeDtypeStruct((t, w), BF16),
                 jax.ShapeDtypeStruct((t, 2 * w), F32)]
    if seq is not None:
        nt = seq // tm
        out_specs += [pl.BlockSpec((tm, w), lambda i: (i, 0)),
                      pl.BlockSpec((1, w, tm), lambda i: (i // nt, 0, i % nt)),
                      pl.BlockSpec((1, LANES, tm), lambda i: (i // nt, 0, i % nt))]
        out_shape += [jax.ShapeDtypeStruct((t, w), BF16),
                      jax.ShapeDtypeStruct((t // seq, w, seq), BF16),
                      jax.ShapeDtypeStruct((t // seq, LANES, seq), F32)]
    return pl.pallas_call(
        functools.partial(_cd_pre_kernel, w=w),
        grid=(t // tm,),
        in_specs=[pl.BlockSpec((tm, 3 * w), lambda i: (i, CD_COLS["dq"] // (3 * w))),
                  pl.BlockSpec((tm, LANES), lambda i: (i, SMALL_BLOCK)),
                  pl.BlockSpec((2, GLA_GATE_RANK, gw), lambda i: (0, 0, 0)),
                  pl.BlockSpec((1, gw), lambda i: (0, 0)),
                  pl.BlockSpec((1, w), lambda i: (0, 0)),
                  pl.BlockSpec((1, w), lambda i: (0, 0)),
                  pl.BlockSpec((w, w), lambda i: (0, 0))],
        out_specs=out_specs,
        out_shape=out_shape,
        compiler_params=_cparams("parallel"),
        name="cd_pre",
    )(z, z, jnp.stack([wa_hi, wa_lo]), ba.reshape(1, gw), jnp.tile(qn, DSA_HEADS).reshape(1, w),
      jnp.tile(kn, DSA_HEADS).reshape(1, w), _seg_matrix(w))


def _gla_consts(c):
    levels = []
    s = c // 2
    while s >= 1:
        levels.append(s)
        s //= 2
    i = np.arange(c)
    tri = (i[None, :] <= i[:, None]).astype(np.float32)
    mats = [tri]
    masks = []
    for s in levels:
        ref = (i // (2 * s)) * (2 * s) + s - 1
        r = (i[None, :] <= ref[:, None]).astype(np.float32)
        mats.append(tri - r)
        same = (i[:, None] // (2 * s)) == (i[None, :] // (2 * s))
        masks.append(same & ((i[:, None] % (2 * s)) >= s) & ((i[None, :] % (2 * s)) < s))
    masks.append(i[:, None] == i[None, :])
    mstack = np.concatenate(mats, axis=0)
    mask = np.stack([np.tile(m.astype(np.float32), (1, GLA_HEADS)) for m in masks])
    return jnp.asarray(mstack, BF16), jnp.asarray(mask, F32), len(levels)


def _gla_kernel(qk_ref, v_ref, gate_ref, g_ref, s0_ref, mstack_ref, mask_ref, gn_ref, eye_ref,
                o_ref, sf_ref, st_ref, *, c, nl):
    t = pl.program_id(1)
    kw = GLA_HEADS * GLA_DK
    vw = GLA_HEADS * GLA_DV
    lane_k = lax.broadcasted_iota(I32, (1, kw), 1) // GLA_DK
    lane_v = lax.broadcasted_iota(I32, (1, vw), 1) // GLA_DV

    @pl.when(t == 0)
    def _():
        rows = []
        for h in range(GLA_HEADS):
            z = jnp.zeros((GLA_DK, GLA_DV), F32)
            rows.append(jnp.concatenate([s0_ref[0, h] if hh == h else z for hh in range(GLA_HEADS)], axis=1))
        st_ref[...] = jnp.concatenate(rows, axis=0).T

    qk = qk_ref[0]
    q = qk[:, 0:kw] * GLA_DK ** -0.5
    k = qk[:, kw:2 * kw]
    v = v_ref[0]
    mstack = mstack_ref[...]
    r = None
    for part in _split3(g_ref[0]):
        d = _dot(mstack, part)
        r = d if r is None else r + d
    b = r[0:c]

    def expand_k(x):
        return jnp.concatenate([jnp.where(lane_k == h, x, 0.0) for h in range(GLA_HEADS)], axis=0).astype(BF16)

    st = st_ref[...]
    o = _nt((q * jnp.exp(b)).astype(BF16), st.astype(BF16))
    a = mask_ref[nl] * _nt(q.astype(BF16), expand_k(k))
    for lv in range(nl):
        sc = jnp.exp(-jnp.abs(r[(lv + 1) * c:(lv + 2) * c]))
        a = a + mask_ref[lv] * _nt((q * sc).astype(BF16), expand_k(k * sc))
    vexp = jnp.concatenate([jnp.where(lane_v == h, v, 0.0) for h in range(GLA_HEADS)], axis=0).astype(BF16)
    o = o + _dot(a.astype(BF16), vexp)

    blast = b[c - 1:c]
    kt = (k * jnp.exp(blast - b)).astype(BF16)
    vt = _nt(eye_ref[...], v.astype(BF16)).astype(BF16)
    upd = _dot(vt, kt)
    row_h = lax.broadcasted_iota(I32, (vw, 1), 0) // GLA_DV
    st_new = st * jnp.exp(blast) + jnp.where(row_h == lane_k, upd, 0.0)
    st_ref[...] = st_new

    gate = gate_ref[0]
    gn = gn_ref[...]
    for h in range(GLA_HEADS):
        hs = slice(h * GLA_DV, (h + 1) * GLA_DV)
        oh = o[:, hs]
        y = oh * lax.rsqrt(jnp.mean(oh * oh, axis=-1, keepdims=True) + EPS) * gn
        gh = gate[:, hs]
        o_ref[0, :, hs] = (y * (gh * jax.nn.sigmoid(gh))).astype(BF16)

    @pl.when(t == pl.num_programs(1) - 1)
    def _():
        s_t = st_new.T
        for h in range(GLA_HEADS):
            sf_ref[0, h] = s_t[h * GLA_DK:(h + 1) * GLA_DK, h * GLA_DV:(h + 1) * GLA_DV]


def gla(z3, g3, s0, gla_norm):
    nb, l, _ = z3.shape
    c = min(l, GLA_CHUNK)
    assert l % c == 0 and c & (c - 1) == 0
    kw = GLA_HEADS * GLA_DK
    vw = GLA_HEADS * GLA_DV
    mstack, mask, nl = _gla_consts(c)
    eye = jnp.eye(vw, dtype=BF16)
    o, sf = pl.pallas_call(
        functools.partial(_gla_kernel, c=c, nl=nl),
        grid=(nb, l // c),
        in_specs=[pl.BlockSpec((1, c, 2 * kw), lambda n, t: (n, t, 0)),
                  pl.BlockSpec((1, c, vw), lambda n, t: (n, t, CD_COLS["cv"] // vw)),
                  pl.BlockSpec((1, c, vw), lambda n, t: (n, t, CD_COLS["cgate"] // vw)),
                  pl.BlockSpec((1, c, kw), lambda n, t: (n, t, 0)),
                  pl.BlockSpec((1, GLA_HEADS, GLA_DK, GLA_DV), lambda n, t: (n, 0, 0, 0)),
                  pl.BlockSpec(mstack.shape, lambda n, t: (0, 0)),
                  pl.BlockSpec(mask.shape, lambda n, t: (0, 0, 0)),
                  pl.BlockSpec((1, GLA_DV), lambda n, t: (0, 0)),
                  pl.BlockSpec((vw, vw), lambda n, t: (0, 0))],
        out_specs=[pl.BlockSpec((1, c, vw), lambda n, t: (n, t, 0)),
                   pl.BlockSpec((1, GLA_HEADS, GLA_DK, GLA_DV), lambda n, t: (n, 0, 0, 0))],
        out_shape=[jax.ShapeDtypeStruct((nb, l, vw), BF16),
                   jax.ShapeDtypeStruct((nb, GLA_HEADS, GLA_DK, GLA_DV), F32)],
        scratch_shapes=[pltpu.VMEM((vw, kw), F32)],
        compiler_params=_cparams("parallel", "arbitrary"),
        name="gla",
    )(z3, z3, z3, g3, s0, mstack, mask, gla_norm.reshape(1, GLA_DV), eye)
    return o, sf


def _sort_key(x):
    bits = pltpu.bitcast(x, I32)
    return jnp.where(bits < 0, (bits ^ 0x7FFFFFFF) + 1, bits)


def _idx_lhs(iq, h):
    hi, lo = _split2(iq[:, h * IDX_DIM:(h + 1) * IDX_DIM])
    return jnp.concatenate([hi, hi, lo, jnp.zeros_like(hi)], axis=1)


def _idx_rhs(ik):
    hi, lo = _split2(ik)
    return jnp.concatenate([hi, lo, hi, jnp.zeros_like(hi)], axis=1)


def _idx_scores(lhs, rhs, wcol):
    sc = None
    for h in range(IDX_HEADS):
        d = jnp.maximum(_nt(lhs[h], rhs) * IDX_DIM ** -0.5, 0.0) * (wcol[h] * IDX_HEADS ** -0.5)
        sc = d if sc is None else sc + d
    return sc


def _dsa_select_kernel(iq_ref, smt_ref, smk_ref, tril_ref, mask_ref, kb3_ref, sc_ref, *, nc, topk):
    i = pl.program_id(1)
    ck = DSA_CK
    qb = DSA_SQ
    bpc = ck // qb
    sub = ck // 8

    @pl.when(i == 0)
    def _():
        for c in range(nc):
            kb3_ref[c] = _idx_rhs(smk_ref[0, c * ck:(c + 1) * ck, SM_IK:SM_IK + IDX_DIM])

    iq = iq_ref[0]
    smt = smt_ref[0]
    lhs = jnp.concatenate([_idx_lhs(iq, h) for h in range(IDX_HEADS)], axis=0)
    wrow = [smt[SM_IW + h:SM_IW + h + 1, :] * (IDX_HEADS ** -0.5 * IDX_DIM ** -0.5) for h in range(IDX_HEADS)]
    nch = (i + bpc) // bpc

    def chunk_keys(c):
        dots = _nt(kb3_ref[c], lhs)
        sc = None
        for h in range(IDX_HEADS):
            t = jnp.maximum(dots[:, h * qb:(h + 1) * qb], 0.0) * wrow[h]
            sc = t if sc is None else sc + t
        return _sort_key(sc)

    def score_body(c, carry):
        sc_ref[c] = chunk_keys(c)
        return carry

    lax.fori_loop(0, nch - 1, score_body, 0)
    drc = lax.broadcasted_iota(I32, (ck, qb), 0) - lax.broadcasted_iota(I32, (ck, qb), 1)
    sc_ref[nch - 1] = jnp.where(drc <= i * qb - (nch - 1) * ck, chunk_keys(nch - 1), INT_MIN)

    def count(pred):
        def body(c, acc):
            m = jnp.where(pred(sc_ref[c]), 1, 0)
            return acc + jnp.sum(m.reshape(8, sub, qb), axis=0)
        acc = lax.fori_loop(0, nch, body, jnp.zeros((sub, qb), I32))
        return jnp.sum(acc, axis=0, keepdims=True)

    qpos = i * qb + lax.broadcasted_iota(I32, (1, qb), 1)
    kk = jnp.minimum(topk, qpos + 1)

    def bit_cond(carry):
        t, _, cge = carry
        return jnp.logical_and(t < 32, jnp.min(jnp.where(cge == kk, 1, 0)) == 0)

    def bit_step(j, carry):
        t, thr, cge = carry
        cand = thr + lax.shift_left(jnp.int32(1), 31 - (t + j))
        cnt = count(lambda blk: blk >= cand)
        ok = cnt >= kk
        return t, jnp.where(ok, cand, thr), jnp.where(ok, cnt, cge)

    def bit_body(carry):
        t, thr, cge = lax.fori_loop(0, 4, bit_step, carry)
        return t + 4, thr, cge

    _, thr, cge = lax.while_loop(bit_cond, bit_body,
                                 (jnp.int32(0), jnp.full((1, qb), INT_MIN, I32), jnp.zeros((1, qb), I32)))
    tied = jnp.max(jnp.where(cge > kk, 1, 0))

    @pl.when(tied > 0)
    def _():
        cgt = count(lambda blk: blk > thr)
        needf = (kk - cgt).astype(F32)

        def tie_body(c, off):
            blk = sc_ref[c]
            eq = blk == thr
            pref = _dot(tril_ref[...], jnp.where(eq, 1.0, 0.0).astype(BF16))
            drop = jnp.logical_and(eq, pref + off > needf)
            sc_ref[c] = jnp.where(drop, INT_MIN, blk)
            return off + pref[ck - 1:ck, :]

        lax.fori_loop(0, nch, tie_body, jnp.zeros((1, qb), F32))

    def live_body(c, carry):
        mask_ref[0, 0, c] = jnp.where(sc_ref[c] >= thr, 0.0, NEG).astype(BF16)
        return carry

    def dead_body(c, carry):
        mask_ref[0, 0, c] = jnp.full((ck, qb), NEG, BF16)
        return carry

    lax.fori_loop(0, nch, live_body, 0)
    lax.fori_loop(nch, nc, dead_body, 0)


def dsa_select(z3, smt):
    b, s, _ = z3.shape
    nq, nc = s // DSA_SQ, s // DSA_CK
    topk = min(DSA_TOPK_MAX, s // 4)
    tril = jnp.asarray(np.tril(np.ones((DSA_CK, DSA_CK), np.float32)), BF16)
    return pl.pallas_call(
        functools.partial(_dsa_select_kernel, nc=nc, topk=topk),
        grid=(b, nq),
        in_specs=[pl.BlockSpec((1, DSA_SQ, IDX_HEADS * IDX_DIM), lambda n, i: (n, i, CD_COLS["iq"] // (IDX_HEADS * IDX_DIM))),
                  pl.BlockSpec((1, LANES, DSA_SQ), lambda n, i: (n, 0, i)),
                  pl.BlockSpec((1, s, LANES), lambda n, i: (n, 0, SMALL_BLOCK)),
                  pl.BlockSpec((DSA_CK, DSA_CK), lambda n, i: (0, 0))],
        out_specs=pl.BlockSpec((1, 1, nc, DSA_CK, DSA_SQ), lambda n, i: (n, i, 0, 0, 0)),
        out_shape=jax.ShapeDtypeStruct((b, nq, nc, DSA_CK, DSA_SQ), BF16),
        scratch_shapes=[pltpu.VMEM((nc, DSA_CK, 4 * IDX_DIM), BF16), pltpu.VMEM((nc, DSA_CK, DSA_SQ), I32)],
        compiler_params=_cparams("parallel", "arbitrary"),
        name="dsa_select",
    )(z3, smt, z3, tril)


def _dsa_bias_tiles(tab_d):
    o = 0
    while _rel_bucket_np(max(o * DSA_QB - (DSA_QB - 1), 0)) < REL_BUCKETS - 1:
        o += 1
    offs = np.arange(o + 1)[:, None, None] * DSA_QB
    d = offs + np.arange(DSA_QB)[None, None, :] - np.arange(DSA_QB)[None, :, None]
    tiles = bias_lookup(tab_d, _rel_bucket_np(d).reshape((o + 1) * DSA_QB, DSA_QB))
    return tiles.reshape(DSA_HEADS, o + 1, DSA_QB, DSA_QB)


def _dsa_attn_kernel(qi_ref, kc_ref, q_ref, k_ref, vt_ref, mask_ref, bt_ref, o_ref, m_ref, l_ref, acc_ref, s_ref,
                     *, n_off):
    s_id = pl.program_id(1)
    i = qi_ref[s_id]
    c = kc_ref[s_id]
    qb, ck = DSA_QB, DSA_CK
    bpc = ck // qb

    @pl.when(c == 0)
    def _():
        m_ref[...] = jnp.full(m_ref.shape, NEG, F32)
        l_ref[...] = jnp.zeros(l_ref.shape, F32)
        acc_ref[...] = jnp.zeros(acc_ref.shape, F32)

    madd = mask_ref[0, 0, 0].astype(F32)
    q = q_ref[0]
    k = k_ref[0]
    vt = vt_ref[0]
    offs = [jnp.clip(i - (bpc * c + t), 0, n_off - 1) for t in range(bpc)]
    m_all = m_ref[...]
    l_all = l_ref[...]
    m_rows, l_rows = [], []
    for h in range(DSA_HEADS):
        hs = slice(h * HEAD_DIM, (h + 1) * HEAD_DIM)
        bias = jnp.concatenate([bt_ref[h, offs[t]] for t in range(bpc)], axis=0)
        s = _nt(k[:, hs], q[:, hs]) + bias + madd
        s_ref[h] = s
        m_rows.append(jnp.maximum(m_all[h:h + 1, :], jnp.max(s, axis=0, keepdims=True)))
    ones = jnp.ones((16, ck), BF16)
    for h in range(DSA_HEADS):
        hs = slice(h * HEAD_DIM, (h + 1) * HEAD_DIM)
        alpha = jnp.exp2(m_all[h:h + 1, :] - m_rows[h])
        p = jnp.exp2(s_ref[h] - m_rows[h]).astype(BF16)
        pv = _dot(jnp.concatenate([vt[hs, :], ones], axis=0), p)
        l_rows.append(alpha * l_all[h:h + 1, :] + pv[HEAD_DIM:HEAD_DIM + 1, :])
        acc_ref[hs, :] = alpha * acc_ref[hs, :] + pv[0:HEAD_DIM, :]
    m_ref[...] = jnp.concatenate(m_rows, axis=0)
    l_ref[...] = jnp.concatenate(l_rows, axis=0)

    @pl.when(c == i // bpc)
    def _():
        inv = 1.0 / l_ref[...]
        ot = jnp.concatenate([acc_ref[h * HEAD_DIM:(h + 1) * HEAD_DIM, :] * inv[h:h + 1, :]
                              for h in range(DSA_HEADS)], axis=0)
        o_ref[0] = ot.T.astype(BF16)


def dsa_attend(dq, kb, vt, mask, tab_d):
    b, s, w = dq.shape
    nq = s // DSA_QB
    bpc = DSA_CK // DSA_QB
    qps = DSA_SQ // DSA_QB
    bt = _dsa_bias_tiles(tab_d) * LOG2E
    n_off = bt.shape[1]
    qi = np.concatenate([np.full(i // bpc + 1, i) for i in range(nq)]).astype(np.int32)
    kc = np.concatenate([np.arange(i // bpc + 1) for i in range(nq)]).astype(np.int32)
    grid_spec = pltpu.PrefetchScalarGridSpec(
        num_scalar_prefetch=2,
        grid=(b, len(qi)),
        in_specs=[pl.BlockSpec((1, DSA_QB, w), lambda n, t, qi_, kc_: (n, qi_[t], 0)),
                  pl.BlockSpec((1, DSA_CK, w), lambda n, t, qi_, kc_: (n, kc_[t], 0)),
                  pl.BlockSpec((1, w, DSA_CK), lambda n, t, qi_, kc_: (n, 0, kc_[t])),
                  pl.BlockSpec((1, 1, 1, DSA_CK, DSA_QB),
                               lambda n, t, qi_, kc_: (n, qi_[t] // qps, kc_[t], 0, qi_[t] % qps)),
                  pl.BlockSpec(bt.shape, lambda n, t, qi_, kc_: (0, 0, 0, 0))],
        out_specs=pl.BlockSpec((1, DSA_QB, w), lambda n, t, qi_, kc_: (n, qi_[t], 0)),
        scratch_shapes=[pltpu.VMEM((DSA_HEADS, DSA_QB), F32), pltpu.VMEM((DSA_HEADS, DSA_QB), F32),
                        pltpu.VMEM((w, DSA_QB), F32), pltpu.VMEM((DSA_HEADS, DSA_CK, DSA_QB), F32)],
    )
    return pl.pallas_call(
        functools.partial(_dsa_attn_kernel, n_off=n_off),
        grid_spec=grid_spec,
        out_shape=jax.ShapeDtypeStruct((b, s, w), BF16),
        compiler_params=_cparams("parallel", "arbitrary"),
        name="dsa_attend",
    )(jnp.asarray(qi), jnp.asarray(kc), dq, kb, vt, mask, bt)


SCORE_PAGES = 16
ATTN_PAGES = 16


def _dsa_step_scores_kernel(pt_ref, iq_ref, sm_ref, *rest, npg):
    k_refs, o_ref = rest[:npg], rest[npg]
    iq = iq_ref[0]
    sm = sm_ref[0]
    lhs = jnp.concatenate([_idx_lhs(iq, h) for h in range(IDX_HEADS)], axis=0)
    wcol = [sm[:, SM_IW + h:SM_IW + h + 1] * IDX_HEADS ** -0.5 for h in range(IDX_HEADS)]
    for j in range(npg):
        hi, lo = _split2(k_refs[j][0, 0])
        d = _dot(lhs, jnp.concatenate([hi, lo, hi, jnp.zeros_like(hi)], axis=0))
        sc = None
        for h in range(IDX_HEADS):
            t = jnp.maximum(d[h * ROWS8:(h + 1) * ROWS8] * IDX_DIM ** -0.5, 0.0) * wcol[h]
            sc = t if sc is None else sc + t
        o_ref[0, :, j * PAGE_SIZE:(j + 1) * PAGE_SIZE] = sc


def dsa_step_scores(z8, kidx_t, layer, page_table):
    nb = z8.shape[0]
    n_pages = page_table.shape[1]
    npg = math.gcd(SCORE_PAGES, n_pages)
    iqw = IDX_HEADS * IDX_DIM

    def page_spec(j):
        return pl.BlockSpec((1, 1, IDX_DIM, PAGE_SIZE),
                            lambda n, p, pt: (layer, pt[n * n_pages + p * npg + j], 0, 0))

    grid_spec = pltpu.PrefetchScalarGridSpec(
        num_scalar_prefetch=1,
        grid=(nb, n_pages // npg),
        in_specs=[pl.BlockSpec((1, ROWS8, iqw), lambda n, p, pt: (n, 0, CD_COLS["iq"] // iqw)),
                  pl.BlockSpec((1, ROWS8, LANES), lambda n, p, pt: (n, 0, SMALL_BLOCK))]
                 + [page_spec(j) for j in range(npg)],
        out_specs=pl.BlockSpec((1, ROWS8, npg * PAGE_SIZE), lambda n, p, pt: (n, 0, p)),
    )
    return pl.pallas_call(
        functools.partial(_dsa_step_scores_kernel, npg=npg),
        grid_spec=grid_spec,
        out_shape=jax.ShapeDtypeStruct((nb, ROWS8, n_pages * PAGE_SIZE), F32),
        compiler_params=_cparams("parallel", "arbitrary"),
        name="dsa_step_scores",
    )(page_table.reshape(-1), z8, z8, *([kidx_t] * npg))


def _dsa_step_select_kernel(sc_ref, iq_ref, sm_ref, triu_ref, mp_ref, mn_ref, key_ref, *, past, l_new, topk):
    ck = DSA_CK
    nck = past // ck
    iq = iq_ref[0]
    sm = sm_ref[0]
    lhs = [_idx_lhs(iq, h) for h in range(IDX_HEADS)]
    wcol = [sm[:, SM_IW + h:SM_IW + h + 1] for h in range(IDX_HEADS)]
    rhs_new = _idx_rhs(jnp.concatenate([sm[:, SM_IK:SM_IK + IDX_DIM], jnp.zeros((LANES - ROWS8, IDX_DIM), F32)], axis=0))
    sc_new = _idx_scores(lhs, rhs_new, wcol)
    row = lax.broadcasted_iota(I32, (ROWS8, LANES), 0)
    col = lax.broadcasted_iota(I32, (ROWS8, LANES), 1)
    key_new = jnp.where(jnp.logical_and(col <= row, col < l_new), _sort_key(sc_new), INT_MIN)
    key_ref[...] = _sort_key(sc_ref[0])
    kk = jnp.minimum(topk, past + 1 + lax.broadcasted_iota(I32, (ROWS8, 1), 0))

    def count(pred):
        return (jnp.sum(jnp.where(pred(key_ref[...]), 1, 0), axis=1, keepdims=True)
                + jnp.sum(jnp.where(pred(key_new), 1, 0), axis=1, keepdims=True))

    def bit_body(t, carry):
        thr, cge = carry
        cand = thr + lax.shift_left(jnp.int32(1), 31 - t)
        cnt = count(lambda x: x >= cand)
        ok = cnt >= kk
        return jnp.where(ok, cand, thr), jnp.where(ok, cnt, cge)

    thr, cge = lax.fori_loop(0, 32, bit_body, (jnp.full((ROWS8, 1), INT_MIN, I32), jnp.zeros((ROWS8, 1), I32)))
    cgt = count(lambda x: x > thr)
    need = kk - cgt
    needf = need.astype(F32)
    tied = jnp.max(jnp.where(need < cge - cgt, 1, 0))
    mn_ref[0] = jnp.where(key_new >= thr, 0.0, NEG)

    @pl.when(tied == 0)
    def _():
        mp_ref[0] = jnp.where(key_ref[...] >= thr, 0.0, NEG)

    @pl.when(tied > 0)
    def _():
        off = jnp.zeros((ROWS8, 1), F32)
        triu = triu_ref[...]
        for c in range(nck):
            blk = key_ref[:, c * ck:(c + 1) * ck]
            eq = blk == thr
            pref = _dot(jnp.where(eq, 1.0, 0.0).astype(BF16), triu)
            keep = jnp.logical_or(blk > thr, jnp.logical_and(eq, pref + off <= needf))
            mp_ref[0, :, c * ck:(c + 1) * ck] = jnp.where(keep, 0.0, NEG)
            off = off + pref[:, ck - 1:ck]
        eq = key_new == thr
        pref = _dot(jnp.where(eq, 1.0, 0.0).astype(BF16), triu[0:LANES, 0:LANES])
        keep = jnp.logical_or(key_new > thr, jnp.logical_and(eq, pref + off <= needf))
        mn_ref[0] = jnp.where(keep, 0.0, NEG)


def dsa_step_select(scores, z8, l_new):
    nb, _, past = scores.shape
    topk = min(DSA_TOPK_MAX, (past + l_new) // 4)
    iqw = IDX_HEADS * IDX_DIM
    assert past % DSA_CK == 0
    triu = jnp.asarray(np.triu(np.ones((DSA_CK, DSA_CK), np.float32)), BF16)
    return pl.pallas_call(
        functools.partial(_dsa_step_select_kernel, past=past, l_new=l_new, topk=topk),
        grid=(nb,),
        in_specs=[pl.BlockSpec((1, ROWS8, past), lambda n: (n, 0, 0)),
                  pl.BlockSpec((1, ROWS8, iqw), lambda n: (n, 0, CD_COLS["iq"] // iqw)),
                  pl.BlockSpec((1, ROWS8, LANES), lambda n: (n, 0, SMALL_BLOCK)),
                  pl.BlockSpec((DSA_CK, DSA_CK), lambda n: (0, 0))],
        out_specs=[pl.BlockSpec((1, ROWS8, past), lambda n: (n, 0, 0)),
                   pl.BlockSpec((1, ROWS8, LANES), lambda n: (n, 0, 0))],
        out_shape=[jax.ShapeDtypeStruct((nb, ROWS8, past), F32),
                   jax.ShapeDtypeStruct((nb, ROWS8, LANES), F32)],
        scratch_shapes=[pltpu.VMEM((ROWS8, past), I32)],
        compiler_params=_cparams("parallel"),
        name="dsa_step_select",
    )(scores, z8, z8, triu)


def _dsa_step_attn_kernel(pt_ref, q_ref, kvn_ref, mp_ref, mn_ref, bp_ref, bn_ref, *rest, w, npg):
    kv_refs, o_ref = rest[:npg], rest[npg]
    qbd_ref, newpage_ref, m_ref, l_ref, acc_ref = rest[npg + 1:]
    n = pl.program_id(0)
    p = pl.program_id(1)
    rows = DSA_HEADS * ROWS8
    lane_h = lax.broadcasted_iota(I32, (ROWS8, w), 1) // HEAD_DIM

    @pl.when(jnp.logical_and(n == 0, p == 0))
    def _():
        newpage_ref[...] = jnp.zeros(newpage_ref.shape, F32)

    @pl.when(p == 0)
    def _():
        q = q_ref[0]
        for h in range(DSA_HEADS):
            qbd_ref[h * ROWS8:(h + 1) * ROWS8, :] = jnp.where(lane_h == h, q, jnp.zeros_like(q))
        m_ref[...] = jnp.full(m_ref.shape, NEG, F32)
        l_ref[...] = jnp.zeros(l_ref.shape, F32)
        acc_ref[...] = jnp.zeros(acc_ref.shape, F32)

    def accumulate(scores, madd8, bias, pv):
        s = scores + bias + jnp.concatenate([madd8] * DSA_HEADS, axis=0)
        m_old = m_ref[:, 0:1]
        m_new = jnp.maximum(m_old, jnp.max(s, axis=-1, keepdims=True))
        alpha = jnp.exp(m_old - m_new)
        pr = jnp.exp(s - m_new)
        l_ref[...] = jnp.broadcast_to(alpha * l_ref[:, 0:1] + jnp.sum(pr, axis=-1, keepdims=True), (rows, LANES))
        m_ref[...] = jnp.broadcast_to(m_new, (rows, LANES))
        acc_ref[...] = alpha * acc_ref[...] + pv(pr.astype(BF16))

    qbd = qbd_ref[...]
    scores = jnp.concatenate([_dot(qbd, kv_refs[j][0, 0, 0].reshape(w, PAGE_SIZE).astype(BF16))
                              for j in range(npg)], axis=1)

    def pv_pages(pr):
        out = None
        for j in range(npg):
            t = _nt(pr[:, j * PAGE_SIZE:(j + 1) * PAGE_SIZE], kv_refs[j][0, 0, 1].reshape(w, PAGE_SIZE).astype(BF16))
            out = t if out is None else out + t
        return out

    accumulate(scores, mp_ref[0], bp_ref[...], pv_pages)

    @pl.when(p == pl.num_programs(1) - 1)
    def _():
        newpage_ref[0:ROWS8, :] = kvn_ref[0]
        newp = newpage_ref[...]
        accumulate(_nt(qbd, newp[:, 0:w].astype(BF16)), mn_ref[0], bn_ref[...],
                   lambda pr: _dot(pr, newp[:, w:2 * w].astype(BF16)))
        for h in range(DSA_HEADS):
            rs = slice(h * ROWS8, (h + 1) * ROWS8)
            hs = slice(h * HEAD_DIM, (h + 1) * HEAD_DIM)
            o_ref[0, :, hs] = (acc_ref[rs, hs] / l_ref[rs, 0:1]).astype(BF16)


def _dsa_step_bias(tab_d, past, l_new):
    lq = np.minimum(np.arange(ROWS8), l_new - 1)[:, None]
    d_past = past + lq - np.arange(past)[None, :]
    d_new = lq - np.arange(LANES)[None, :]

    def table(d):
        return bias_lookup(tab_d, _rel_bucket_np(d)).reshape(DSA_HEADS * ROWS8, d.shape[1])

    return table(d_past), table(d_new)


def dsa_step_attend(dq8, kv_new8, kv_t, layer, page_table, mask_past, mask_new, tab_d, l_new):
    nb, _, w = dq8.shape
    n_pages = page_table.shape[1]
    npg = math.gcd(ATTN_PAGES, n_pages)
    past = n_pages * PAGE_SIZE
    rows = DSA_HEADS * ROWS8
    bp, bn = _dsa_step_bias(tab_d, past, l_new)

    def page_spec(j):
        return pl.BlockSpec((1, 1, 2, DSA_HEADS, HEAD_DIM, PAGE_SIZE),
                            lambda n, p, pt: (layer, pt[n * n_pages + p * npg + j], 0, 0, 0, 0))

    grid_spec = pltpu.PrefetchScalarGridSpec(
        num_scalar_prefetch=1,
        grid=(nb, n_pages // npg),
        in_specs=[pl.BlockSpec((1, ROWS8, w), lambda n, p, pt: (n, 0, 0)),
                  pl.BlockSpec((1, ROWS8, 2 * w), lambda n, p, pt: (n, 0, 0)),
                  pl.BlockSpec((1, ROWS8, npg * PAGE_SIZE), lambda n, p, pt: (n, 0, p)),
                  pl.BlockSpec((1, ROWS8, LANES), lambda n, p, pt: (n, 0, 0)),
                  pl.BlockSpec((rows, npg * PAGE_SIZE), lambda n, p, pt: (0, p)),
                  pl.BlockSpec((rows, LANES), lambda n, p, pt: (0, 0))]
                 + [page_spec(j) for j in range(npg)],
        out_specs=pl.BlockSpec((1, ROWS8, w), lambda n, p, pt: (n, 0, 0)),
        scratch_shapes=[pltpu.VMEM((rows, w), BF16), pltpu.VMEM((PAGE_SIZE, 2 * w), F32),
                        pltpu.VMEM((rows, LANES), F32), pltpu.VMEM((rows, LANES), F32), pltpu.VMEM((rows, w), F32)],
    )
    return pl.pallas_call(
        functools.partial(_dsa_step_attn_kernel, w=w, npg=npg),
        grid_spec=grid_spec,
        out_shape=jax.ShapeDtypeStruct((nb, ROWS8, w), BF16),
        compiler_params=_cparams("arbitrary", "arbitrary"),
        name="dsa_step_attend",
    )(page_table.reshape(-1), dq8, kv_new8, mask_past, mask_new, bp, bn, *([kv_t] * npg))


def _pad_rows(x3, rows):
    return jnp.pad(x3, ((0, 0), (0, rows - x3.shape[1]), (0, 0)))


def _trunk(x, is_step, conv_state, win_states, gla_state, dsa_kv, dsa_kidx, page_table, wts):
    (norm_mix, norm_ffn, w_in_ab, conv_w, conv_b, conv_ln_g, conv_ln_b, qn_ab, kn_ab, w_out_ab,
     w_in_cd, gla_wa2, gla_ba, gla_norm, qn_cd, kn_cd, w_out_cd, rel_bias, w_g, w_u, w_d) = wts
    nb, l, d = x.shape
    t = nb * l
    depth = norm_mix.shape[0]
    c = conv_w.shape[2]
    wq = B_HPG * HEAD_DIM
    x2 = x.reshape(t, d)
    conv_new, gla_new, kv_new, kidx_new = [], [], [], []
    win_new = [[] for _ in WINDOWS]
    for layer in range(depth):
        i = layer // 2
        if layer % 2 == 0:
            z = norm_matmul(x2, norm_mix[layer], w_in_ab[i])
            hist = conv_state[i] if is_step else jnp.zeros((nb, CONV_WIDTH - 1, c), F32)
            a_out, c_st = conv_module(z.reshape(nb, l, -1), hist, conv_w[i], conv_b[i], conv_ln_g[i], conv_ln_b[i])
            conv_new.append(c_st)
            if is_step:
                qs, kvs = ab_qkv(z, qn_ab[i], kn_ab[i], 2 * c, t, (1,) * len(WINDOWS))
            else:
                qs, kvs = ab_qkv(z, qn_ab[i], kn_ab[i], 2 * c, l, DILATIONS)
            outs, lses = [], []
            for g, window in enumerate(WINDOWS):
                tab_g = rel_bias[:, g * B_HPG:(g + 1) * B_HPG]
                if is_step:
                    o, lse, st = win_step(_pad_rows(qs[g].reshape(nb, l, wq), ROWS8),
                                          kvs[g].reshape(nb, l, 2 * wq), win_states[g], i, g, tab_g)
                else:
                    o, lse = win_prompt(qs[g], kvs[g], g, tab_g)
                    keep = min(window, l)
                    st = kvs[g][:, -(keep // DILATIONS[g]):].reshape(nb, keep, 2, B_HPG, HEAD_DIM)
                outs.append(o.reshape(t, wq))
                lses.append(lse.reshape(t, wq))
                win_new[g].append(st)
            m1, m2 = a_out.reshape(t, c), ab_merge(outs, lses)
            wo = w_out_ab[i]
        else:
            z = norm_matmul(x2, norm_mix[layer], w_in_cd[i])
            pre = cd_pre(z, gla_wa2[i], gla_ba[i], qn_cd[i], kn_cd[i], None if is_step else l)
            gdec, dq, kv = pre[:3]
            z3 = z.reshape(nb, l, -1)
            wd = DSA_HEADS * HEAD_DIM
            ik = z3[:, :, CD_COLS["ik"]:CD_COLS["ik"] + IDX_DIM]
            tab_d = rel_bias[:, B_HEADS:]
            if is_step:
                lp = GLA_STEP_ROWS
                o_c, s_c = gla(_pad_rows(z3, lp), _pad_rows(gdec.reshape(nb, l, -1), lp), gla_state[i], gla_norm[i])
                o_c = o_c[:, :l]
                z8 = _pad_rows(z3, ROWS8)
                scores = dsa_step_scores(z8, dsa_kidx, i, page_table)
                mask_p, mask_n = dsa_step_select(scores, z8, l)
                o_d = dsa_step_attend(_pad_rows(dq.reshape(nb, l, wd), ROWS8), _pad_rows(kv.reshape(nb, l, 2 * wd), ROWS8),
                                      dsa_kv, i, page_table, mask_p, mask_n, tab_d, l)[:, :l]
                kv_st = kv.reshape(nb, l, 2, DSA_HEADS, HEAD_DIM)
                ki_st = ik
            else:
                s0 = jnp.zeros((nb, GLA_HEADS, GLA_DK, GLA_DV), F32)
                o_c, s_c = gla(z3, gdec.reshape(nb, l, -1), s0, gla_norm[i])
                kb, vt, smt = pre[3:]
                mask = dsa_select(z3, smt)
                o_d = dsa_attend(dq.reshape(nb, l, wd), kb.reshape(nb, l, wd), vt, mask, tab_d)
                n_pg = l // PAGE_SIZE
                kv_st = kv.reshape(nb, n_pg, PAGE_SIZE, 2, DSA_HEADS, HEAD_DIM)
                ki_st = ik.reshape(nb, n_pg, PAGE_SIZE, IDX_DIM)
            gla_new.append(s_c)
            kv_new.append(kv_st)
            kidx_new.append(ki_st)
            m1, m2 = o_c.reshape(t, -1), o_d.reshape(t, -1)
            wo = w_out_cd[i]
        d1 = m1.shape[1]
        x2 = mix_ffn(x2, m1, m2, wo[:d1], wo[d1:], norm_ffn[layer], w_g[layer], w_u[layer], w_d[layer])
    wins = [jnp.stack(ws) for ws in win_new]
    if is_step:
        wins = [ws.transpose(0, 1, 5, 2, 3, 4) for ws in wins]
    states = (jnp.stack(conv_new), wins[0], wins[1], wins[2],
              jnp.stack(gla_new), jnp.stack(kv_new), jnp.stack(kidx_new))
    return x2.reshape(nb, l, d), states


def kernel(x_prompt, x_sample, state_conv, cache_win128, cache_win512, cache_win2048, state_gla, cache_dsa_kv, cache_dsa_kidx, page_table, norm_mix, norm_ffn, w_in_ab, conv_w, conv_b, conv_ln_g, conv_ln_b, qn_ab, kn_ab, w_out_ab, w_in_cd, gla_wa2, gla_ba, gla_norm, qn_cd, kn_cd, w_out_cd, rel_bias, w_ffn_gate, w_ffn_up, w_ffn_down):
    bf = lambda a: a.astype(BF16)
    w_in_cd_r = jnp.stack([cd_reorder_w(w_in_cd[i]) for i in range(w_in_cd.shape[0])])
    wts = (norm_mix, norm_ffn, bf(w_in_ab), conv_w, conv_b, conv_ln_g, conv_ln_b, qn_ab, kn_ab, bf(w_out_ab),
           bf(w_in_cd_r), gla_wa2, gla_ba, gla_norm, qn_cd, kn_cd, bf(w_out_cd), rel_bias,
           bf(w_ffn_gate), bf(w_ffn_up), bf(w_ffn_down))
    y_p, sp = _trunk(x_prompt, False, None, None, None, None, None, None, wts)
    wins_t = tuple(cw.transpose(0, 1, 3, 4, 5, 2) for cw in (cache_win128, cache_win512, cache_win2048))
    y_s, ss = _trunk(x_sample, True, state_conv, wins_t, state_gla,
                     cache_dsa_kv.transpose(0, 1, 3, 4, 5, 2), cache_dsa_kidx.transpose(0, 1, 3, 2), page_table, wts)
    conv_p, win128_p, win512_p, win2048_p, gla_p, dsa_kv_p, dsa_kidx_p = sp
    conv_s, win128_s, win512_s, win2048_s, gla_s, dsa_kv_s, dsa_kidx_s = ss
    return (y_p, y_s, conv_p, conv_s, win128_p, win128_s, win512_p, win512_s, win2048_p, win2048_s,
            gla_p, gla_s, dsa_kv_p, dsa_kv_s, dsa_kidx_p, dsa_kidx_s)
```

```python
import functools
import math

import numpy as np
import jax
import jax.numpy as jnp
from jax import lax
from jax.experimental import pallas as pl
from jax.experimental.pallas import tpu as pltpu

F32 = jnp.float32
BF16 = jnp.bfloat16
I32 = jnp.int32

EPS = 1e-6
NEG = -1e30
LOG2E = math.log2(math.e)
F32_MIN_NORMAL = 2.0 ** -126
INT_MIN = -(2 ** 31)

V7X_VMEM_BYTES = 64 * 1024 * 1024
VMEM_LIMIT = V7X_VMEM_BYTES - 12 * 1024 * 1024
LANES = 128

HEAD_DIM = 64
CONV_WIDTH = 31
WINDOWS = (128, 512, 2048)
DILATIONS = (1, 4, 16)
B_HPG = 4
B_HEADS = B_HPG * len(WINDOWS)
SW_BLOCK = 128
GLA_HEADS = 4
GLA_DK = 64
GLA_DV = 128
GLA_GATE_RANK = 16
GLA_TAU = 16.0
DSA_HEADS = 8
IDX_HEADS = 4
IDX_DIM = 64
DSA_TOPK_MAX = 256
PAGE_SIZE = 128
REL_BUCKETS = 32
REL_MAX_DIST = 2048

GLA_CHUNK = 128
GLA_STEP_ROWS = 64
DSA_QB = 128
DSA_SQ = 256
DSA_CK = 512
HIST_PAD = 32


def _cparams(*sem):
    return pltpu.CompilerParams(dimension_semantics=sem, vmem_limit_bytes=VMEM_LIMIT)


def _nt(a, b):
    return lax.dot_general(a, b, (((1,), (1,)), ((), ())), preferred_element_type=F32)


def _dot(a, b):
    return jnp.dot(a, b, preferred_element_type=F32)


def _split2(x):
    hi = x.astype(BF16)
    lo = (x - hi.astype(F32)).astype(BF16)
    return hi, lo


def _split3(x):
    hi = x.astype(BF16)
    r = x - hi.astype(F32)
    mid = r.astype(BF16)
    lo = (r - mid.astype(F32)).astype(BF16)
    return hi, mid, lo


def _rel_bucket_np(dist):
    n = np.maximum(np.asarray(dist, np.int64), 0)
    max_exact = REL_BUCKETS // 2
    nf = np.maximum(n, max_exact).astype(np.float32)
    large = max_exact + (np.log(nf / np.float32(max_exact)) / np.float32(math.log(REL_MAX_DIST / max_exact))
                         * np.float32(REL_BUCKETS - max_exact)).astype(np.int32)
    large = np.minimum(large, REL_BUCKETS - 1)
    return np.where(n < max_exact, n, large).astype(np.int32)


def _bias_lookup_kernel(tab_ref, idx_ref, o_ref, *, nh):
    idx = idx_ref[...]
    for h in range(nh):
        acc = jnp.full(idx.shape, NEG, F32)
        for b in range(REL_BUCKETS):
            acc = jnp.where(idx == b, tab_ref[b, h], acc)
        o_ref[h] = acc


def bias_lookup(tab, idx_np):
    r, c = idx_np.shape
    nh = tab.shape[1]
    tr = 8 if (c >= 2048 and r % 8 == 0) else (128 if r % 128 == 0 else r)
    return pl.pallas_call(
        functools.partial(_bias_lookup_kernel, nh=nh),
        grid=(r // tr,),
        in_specs=[pl.BlockSpec(memory_space=pltpu.SMEM),
                  pl.BlockSpec((tr, c), lambda i: (i, 0))],
        out_specs=pl.BlockSpec((nh, tr, c), lambda i: (0, i, 0)),
        out_shape=jax.ShapeDtypeStruct((nh, r, c), F32),
        compiler_params=_cparams("parallel"),
        name="bias_lookup",
    )(tab, jnp.asarray(idx_np.astype(np.int32)))


def _norm_matmul_kernel(x_ref, g_ref, w_ref, o_ref):
    x = x_ref[...]
    y = x * lax.rsqrt(jnp.mean(x * x, axis=-1, keepdims=True) + EPS) * g_ref[...]
    o_ref[...] = _dot(y.astype(BF16), w_ref[...])


def norm_matmul(x, g, w):
    t, d = x.shape
    n = w.shape[1]
    tm = min(t, 256)
    return pl.pallas_call(
        _norm_matmul_kernel,
        grid=(t // tm,),
        in_specs=[pl.BlockSpec((tm, d), lambda i: (i, 0)),
                  pl.BlockSpec((1, d), lambda i: (0, 0)),
                  pl.BlockSpec((d, n), lambda i: (0, 0))],
        out_specs=pl.BlockSpec((tm, n), lambda i: (i, 0)),
        out_shape=jax.ShapeDtypeStruct((t, n), F32),
        compiler_params=_cparams("parallel"),
        name="norm_matmul",
    )(x, g.reshape(1, d), w)


def _mix_ffn_kernel(x_ref, m1_ref, m2_ref, wo1_ref, wo2_ref, g_ref, wg_ref, wu_ref, wd_ref, o_ref,
                    x1_ref, hf_ref, acc_ref):
    j = pl.program_id(1)

    @pl.when(j == 0)
    def _():
        x1 = x_ref[...] + _dot(m1_ref[...], wo1_ref[...]) + _dot(m2_ref[...], wo2_ref[...])
        x1_ref[...] = x1
        hf = x1 * lax.rsqrt(jnp.mean(x1 * x1, axis=-1, keepdims=True) + EPS) * g_ref[...]
        hf_ref[...] = hf.astype(BF16)
        acc_ref[...] = jnp.zeros_like(acc_ref)

    hf = hf_ref[...]
    a = _dot(hf, wg_ref[...])
    u = _dot(hf, wu_ref[...])
    act = (a * jax.nn.sigmoid(a) * u).astype(BF16)
    acc_ref[...] += _dot(act, wd_ref[...])

    @pl.when(j == pl.num_programs(1) - 1)
    def _():
        o_ref[...] = x1_ref[...] + acc_ref[...]


def mix_ffn(x, m1, m2, wo1, wo2, g, wg, wu, wd):
    t, d = x.shape
    hid = wg.shape[1]
    tm = min(t, 512)
    th = hid // 2 if hid % 256 == 0 else hid
    d1, d2 = m1.shape[1], m2.shape[1]
    return pl.pallas_call(
        _mix_ffn_kernel,
        grid=(t // tm, hid // th),
        in_specs=[pl.BlockSpec((tm, d), lambda i, j: (i, 0)),
                  pl.BlockSpec((tm, d1), lambda i, j: (i, 0)),
                  pl.BlockSpec((tm, d2), lambda i, j: (i, 0)),
                  pl.BlockSpec((d1, d), lambda i, j: (0, 0)),
                  pl.BlockSpec((d2, d), lambda i, j: (0, 0)),
                  pl.BlockSpec((1, d), lambda i, j: (0, 0)),
                  pl.BlockSpec((d, th), lambda i, j: (0, j)),
                  pl.BlockSpec((d, th), lambda i, j: (0, j)),
                  pl.BlockSpec((th, d), lambda i, j: (j, 0))],
        out_specs=pl.BlockSpec((tm, d), lambda i, j: (i, 0)),
        out_shape=jax.ShapeDtypeStruct((t, d), F32),
        scratch_shapes=[pltpu.VMEM((tm, d), F32), pltpu.VMEM((tm, d), BF16), pltpu.VMEM((tm, d), F32)],
        compiler_params=_cparams("parallel", "arbitrary"),
        name="mix_ffn",
    )(x, m1, m2, wo1, wo2, g.reshape(1, d), wg, wu, wd)


def _conv_kernel(z_ref, hist_ref, w_ref, b_ref, lg_ref, lb_ref, o_ref, tail_ref, uh_ref, *, ts, c):
    t = pl.program_id(1)

    @pl.when(t == 0)
    def _():
        uh_ref[0:HIST_PAD, :] = hist_ref[0]

    z = z_ref[0]
    u = z[:, 0:c] * jax.nn.sigmoid(z[:, c:2 * c])
    uh_ref[HIST_PAD:HIST_PAD + ts, :] = u
    acc = jnp.zeros((ts, c), F32) + b_ref[...]
    off = HIST_PAD - (CONV_WIDTH - 1)
    for j in range(CONV_WIDTH):
        acc = acc + w_ref[j:j + 1, :] * uh_ref[off + j:off + j + ts, :]
    mu = jnp.mean(acc, axis=-1, keepdims=True)
    var = jnp.mean(jnp.square(acc - mu), axis=-1, keepdims=True)
    yn = (acc - mu) * lax.rsqrt(var + EPS) * lg_ref[...] + lb_ref[...]
    o_ref[0] = (yn * jax.nn.sigmoid(yn)).astype(BF16)
    tail = uh_ref[ts:ts + HIST_PAD, :]
    uh_ref[0:HIST_PAD, :] = tail
    tail_ref[0] = tail


def conv_module(z3, hist, conv_w, conv_b, ln_g, ln_b):
    nb, l, _ = z3.shape
    c = conv_w.shape[1]
    ts = min(l, 512)
    hist_p = jnp.pad(hist, ((0, 0), (HIST_PAD - (CONV_WIDTH - 1), 0), (0, 0)))
    w_p = jnp.pad(conv_w, ((0, HIST_PAD - CONV_WIDTH), (0, 0)))
    a_out, tail = pl.pallas_call(
        functools.partial(_conv_kernel, ts=ts, c=c),
        grid=(nb, l // ts),
        in_specs=[pl.BlockSpec((1, ts, 2 * c), lambda n, t: (n, t, 0)),
                  pl.BlockSpec((1, HIST_PAD, c), lambda n, t: (n, 0, 0)),
                  pl.BlockSpec((HIST_PAD, c), lambda n, t: (0, 0)),
                  pl.BlockSpec((1, c), lambda n, t: (0, 0)),
                  pl.BlockSpec((1, c), lambda n, t: (0, 0)),
                  pl.BlockSpec((1, c), lambda n, t: (0, 0))],
        out_specs=[pl.BlockSpec((1, ts, c), lambda n, t: (n, t, 0)),
                   pl.BlockSpec((1, HIST_PAD, c), lambda n, t: (n, 0, 0))],
        out_shape=[jax.ShapeDtypeStruct((nb, l, c), BF16),
                   jax.ShapeDtypeStruct((nb, HIST_PAD, c), F32)],
        scratch_shapes=[pltpu.VMEM((HIST_PAD + ts, c), F32)],
        compiler_params=_cparams("parallel", "arbitrary"),
        name="conv_module",
    )(z3, hist_p, w_p, conv_b.reshape(1, c), ln_g.reshape(1, c), ln_b.reshape(1, c))
    return a_out, tail[:, HIST_PAD - (CONV_WIDTH - 1):]


def _seg_mean_sq(x, seg_ref):
    hi, lo = _split2(x * x)
    seg = seg_ref[...]
    return (_dot(hi, seg) + _dot(lo, seg)) * (1.0 / HEAD_DIM)


def _head_norm(x, g, seg_ref):
    return x * lax.rsqrt(_seg_mean_sq(x, seg_ref) + EPS) * g


def _seg_matrix(width):
    idx = np.arange(width) // HEAD_DIM
    return jnp.asarray((idx[:, None] == idx[None, :]).astype(np.float32), BF16)


def _ab_qkv_kernel(*refs, w, tm, dils):
    ng = len(dils)
    qkv_refs = refs[:3 * ng]
    qg_ref, kg_ref, seg_ref = refs[3 * ng:3 * ng + 3]
    qo_refs = refs[3 * ng + 3:4 * ng + 3]
    kvo_refs = refs[4 * ng + 3:5 * ng + 3]
    qs_ref, kvs_ref = refs[5 * ng + 3:]
    for g, dil in enumerate(dils):
        qn = _head_norm(qkv_refs[g][...], qg_ref[...], seg_ref) * HEAD_DIM ** -0.5
        kn = _head_norm(qkv_refs[ng + g][...], kg_ref[...], seg_ref)
        v = qkv_refs[2 * ng + g][...]
        if dil == 1:
            qo_refs[g][0] = qn.astype(BF16)
            kvo_refs[g][0, :, 0:w] = kn
            kvo_refs[g][0, :, w:2 * w] = v
        else:
            nq, nkv = w // LANES, 2 * w // LANES
            kv = jnp.concatenate([kn, v], axis=1)
            for j in range(nq):
                qs_ref[j] = qn[:, j * LANES:(j + 1) * LANES]
            for j in range(nkv):
                kvs_ref[j] = kv[:, j * LANES:(j + 1) * LANES]
            for r in range(dil):
                rows = pl.ds(r, tm // dil, stride=dil)
                for j in range(nq):
                    qo_refs[g][0, :, r * w + j * LANES:r * w + (j + 1) * LANES] = qs_ref[j, rows, :].astype(BF16)
                for j in range(nkv):
                    kvo_refs[g][0, :, r * 2 * w + j * LANES:r * 2 * w + (j + 1) * LANES] = kvs_ref[j, rows, :]


def ab_qkv(z, qn, kn, col0, seq, dils):
    t = z.shape[0]
    w = B_HPG * HEAD_DIM
    ng = len(dils)
    tm = min(seq, 512)
    nt = seq // tm
    cb = col0 // w
    assert all(tm % (8 * d) == 0 or d == 1 for d in dils)
    col_spec = lambda j: pl.BlockSpec((tm, w), lambda i: (i, cb + j))
    out_specs = ([pl.BlockSpec((1, tm // d, d * w), lambda i: (i // nt, i % nt, 0)) for d in dils]
                 + [pl.BlockSpec((1, tm // d, d * 2 * w), lambda i: (i // nt, i % nt, 0)) for d in dils])
    out_shape = ([jax.ShapeDtypeStruct((t // seq, seq // d, d * w), BF16) for d in dils]
                 + [jax.ShapeDtypeStruct((t // seq, seq // d, d * 2 * w), F32) for d in dils])
    outs = pl.pallas_call(
        functools.partial(_ab_qkv_kernel, w=w, tm=tm, dils=tuple(dils)),
        grid=(t // tm,),
        in_specs=[col_spec(j) for j in range(3 * ng)]
                 + [pl.BlockSpec((1, w), lambda i: (0, 0)),
                    pl.BlockSpec((1, w), lambda i: (0, 0)),
                    pl.BlockSpec((w, w), lambda i: (0, 0))],
        out_specs=out_specs,
        out_shape=out_shape,
        scratch_shapes=[pltpu.VMEM((w // LANES, tm, LANES), F32), pltpu.VMEM((2 * w // LANES, tm, LANES), F32)],
        compiler_params=_cparams("parallel"),
        name="ab_qkv",
    )(*([z] * (3 * ng)), jnp.tile(qn, B_HPG).reshape(1, w), jnp.tile(kn, B_HPG).reshape(1, w), _seg_matrix(w))
    return outs[:ng], outs[ng:]


WIN_BLOCKS = 4


def _win_prompt_kernel(q_ref, kvp_ref, kvc_ref, bias_ref, o_ref, lse_ref, *, w, nblk):
    step = pl.program_id(2)
    col = lax.broadcasted_iota(I32, (SW_BLOCK, 2 * SW_BLOCK), 1)
    first = jnp.logical_and(step == 0, col < SW_BLOCK)
    for j in range(nblk):
        rows = slice(j * SW_BLOCK, (j + 1) * SW_BLOCK)
        q = q_ref[0, rows, :]
        kvc = kvc_ref[0, rows, :]
        kvp = kvp_ref[0] if j == 0 else kvc_ref[0, (j - 1) * SW_BLOCK:j * SW_BLOCK, :]
        for h in range(B_HPG):
            hs = slice(h * HEAD_DIM, (h + 1) * HEAD_DIM)
            vs = slice(w + h * HEAD_DIM, w + (h + 1) * HEAD_DIM)
            k2 = jnp.concatenate([kvp[:, hs], kvc[:, hs]], axis=0).astype(BF16)
            v2 = jnp.concatenate([kvp[:, vs], kvc[:, vs]], axis=0).astype(BF16)
            s = _nt(q[:, hs], k2) + bias_ref[h]
            if j == 0:
                s = jnp.where(first, NEG, s)
            m = jnp.max(s, axis=-1, keepdims=True)
            p = jnp.exp(s - m)
            l = jnp.sum(p, axis=-1, keepdims=True)
            o_ref[0, rows, hs] = _dot(p.astype(BF16), v2) / l
            lse_ref[0, rows, hs] = jnp.broadcast_to(m + jnp.log(l), (SW_BLOCK, HEAD_DIM))


def _win_prompt_bias(tab_g, dil, reach):
    ql = np.arange(SW_BLOCK)[:, None]
    kl = np.arange(2 * SW_BLOCK)[None, :] - SW_BLOCK
    rel = ql - kl
    ok = (rel >= 0) & (rel <= reach)
    return bias_lookup(tab_g, np.where(ok, _rel_bucket_np(rel * dil), -1))


def win_prompt(qv, kvv, g, tab_g):
    w = B_HPG * HEAD_DIM
    dil = DILATIONS[g]
    b, n, _ = qv.shape
    s = n * dil
    nb = n // SW_BLOCK
    assert nb * SW_BLOCK * dil == s
    bias = _win_prompt_bias(tab_g, dil, WINDOWS[g] // dil)
    nblk = math.gcd(WIN_BLOCKS, nb)
    rows = nblk * SW_BLOCK
    o, lse = pl.pallas_call(
        functools.partial(_win_prompt_kernel, w=w, nblk=nblk),
        grid=(b, dil, nb // nblk),
        in_specs=[pl.BlockSpec((1, rows, w), lambda n_, r, k: (n_, k, r)),
                  pl.BlockSpec((1, SW_BLOCK, 2 * w), lambda n_, r, k: (n_, jnp.maximum(k * nblk - 1, 0), r)),
                  pl.BlockSpec((1, rows, 2 * w), lambda n_, r, k: (n_, k, r)),
                  pl.BlockSpec((B_HPG, SW_BLOCK, 2 * SW_BLOCK), lambda n_, r, k: (0, 0, 0))],
        out_specs=[pl.BlockSpec((1, rows, w), lambda n_, r, k: (n_, k, r)),
                   pl.BlockSpec((1, rows, w), lambda n_, r, k: (n_, k, r))],
        out_shape=[jax.ShapeDtypeStruct((b, n, dil * w), F32),
                   jax.ShapeDtypeStruct((b, n, dil * w), F32)],
        compiler_params=_cparams("parallel", "parallel", "arbitrary"),
        name=f"win_prompt_g{g}",
    )(qv, kvv, kvv, bias)
    return o.reshape(b, s, w), lse.reshape(b, s, w)


ROWS8 = 8


def _win_step_kernel(q_ref, kvn_ref, newt_ref, buf_ref, tb_ref, tn_ref, o_ref, lse_ref, st_ref,
                     qbd_ref, newpage_ref, *, w, wb, l):
    n = pl.program_id(0)

    @pl.when(n == 0)
    def _():
        newpage_ref[...] = jnp.zeros(newpage_ref.shape, F32)

    q = q_ref[0]
    lane_h = lax.broadcasted_iota(I32, (ROWS8, w), 1) // HEAD_DIM
    for h in range(B_HPG):
        qbd_ref[h * ROWS8:(h + 1) * ROWS8, :] = jnp.where(lane_h == h, q, jnp.zeros_like(q))
    newpage_ref[0:l, :] = kvn_ref[0]
    qbd = qbd_ref[...]
    newp = newpage_ref[...]
    buf = buf_ref[0, 0]
    kt = buf[0].reshape(w, wb).astype(BF16)
    vt = buf[1].reshape(w, wb).astype(BF16)
    s1 = _dot(qbd, kt) + tb_ref[...]
    s2 = _nt(qbd, newp[:, 0:w].astype(BF16)) + tn_ref[...]
    m = jnp.maximum(jnp.max(s1, axis=-1, keepdims=True), jnp.max(s2, axis=-1, keepdims=True))
    p1 = jnp.exp(s1 - m)
    p2 = jnp.exp(s2 - m)
    den = jnp.sum(p1, axis=-1, keepdims=True) + jnp.sum(p2, axis=-1, keepdims=True)
    num = _nt(p1.astype(BF16), vt) + _dot(p2.astype(BF16), newp[:, w:2 * w].astype(BF16))
    o = num / den
    lse = m + jnp.log(den)
    for h in range(B_HPG):
        rs = slice(h * ROWS8, (h + 1) * ROWS8)
        hs = slice(h * HEAD_DIM, (h + 1) * HEAD_DIM)
        o_ref[0, :, hs] = o[rs, hs]
        lse_ref[0, :, hs] = jnp.broadcast_to(lse[rs], (ROWS8, HEAD_DIM))
    rolled = pltpu.roll(buf.reshape(2 * w, wb), wb - l, 1)
    lane = lax.broadcasted_iota(I32, (2 * w, LANES), 1)
    last = jnp.where(lane >= LANES - l, newt_ref[0], rolled[:, wb - LANES:wb])
    if wb > LANES:
        st_ref[0, :, :, :, 0:wb - LANES] = rolled[:, 0:wb - LANES].reshape(2, B_HPG, HEAD_DIM, wb - LANES)
    st_ref[0, :, :, :, wb - LANES:wb] = last.reshape(2, B_HPG, HEAD_DIM, LANES)


def _win_step_bias(tab_g, dil, window, wb, l):
    lq = np.minimum(np.arange(ROWS8), l - 1)[:, None]
    dist_b = wb + lq - np.arange(wb)[None, :]
    ok_b = (dist_b % dil == 0) & (dist_b // dil <= window // dil)
    cols = np.arange(LANES)[None, :]
    dist_n = lq - cols
    ok_n = (cols < l) & (dist_n >= 0) & (dist_n % dil == 0) & (dist_n // dil <= window // dil)
    rows = B_HPG * ROWS8
    tb = bias_lookup(tab_g, np.where(ok_b, _rel_bucket_np(dist_b), -1)).reshape(rows, wb)
    tn = bias_lookup(tab_g, np.where(ok_n, _rel_bucket_np(dist_n), -1)).reshape(rows, LANES)
    return tb, tn


def win_step(q8, kv_new, cache_t, layer, g, tab_g):
    nb = q8.shape[0]
    l = kv_new.shape[1]
    w = B_HPG * HEAD_DIM
    wb = cache_t.shape[-1]
    window, dil = WINDOWS[g], DILATIONS[g]
    assert wb == window, "the step kernel keeps a full window of rows"
    tb, tn = _win_step_bias(tab_g, dil, window, wb, l)
    rows = B_HPG * ROWS8
    new_t = jnp.pad(kv_new.transpose(0, 2, 1), ((0, 0), (0, 0), (LANES - l, 0)))
    o, lse, st = pl.pallas_call(
        functools.partial(_win_step_kernel, w=w, wb=wb, l=l),
        grid=(nb,),
        in_specs=[pl.BlockSpec((1, ROWS8, w), lambda n: (n, 0, 0)),
                  pl.BlockSpec((1, l, 2 * w), lambda n: (n, 0, 0)),
                  pl.BlockSpec((1, 2 * w, LANES), lambda n: (n, 0, 0)),
                  pl.BlockSpec((1, 1, 2, B_HPG, HEAD_DIM, wb), lambda n: (layer, n, 0, 0, 0, 0)),
                  pl.BlockSpec((rows, wb), lambda n: (0, 0)),
                  pl.BlockSpec((rows, LANES), lambda n: (0, 0))],
        out_specs=[pl.BlockSpec((1, ROWS8, w), lambda n: (n, 0, 0)),
                   pl.BlockSpec((1, ROWS8, w), lambda n: (n, 0, 0)),
                   pl.BlockSpec((1, 2, B_HPG, HEAD_DIM, wb), lambda n: (n, 0, 0, 0, 0))],
        out_shape=[jax.ShapeDtypeStruct((nb, ROWS8, w), F32),
                   jax.ShapeDtypeStruct((nb, ROWS8, w), F32),
                   jax.ShapeDtypeStruct((nb, 2, B_HPG, HEAD_DIM, wb), F32)],
        scratch_shapes=[pltpu.VMEM((rows, w), BF16), pltpu.VMEM((LANES, 2 * w), F32)],
        compiler_params=_cparams("arbitrary"),
        name=f"win_step_g{g}",
    )(q8, kv_new, new_t, cache_t, tb, tn)
    return o[:, :l], lse[:, :l], st


def _ab_merge_kernel(o0, o1, o2, l0, l1, l2, out_ref):
    a, b, c = l0[...], l1[...], l2[...]
    m = jnp.maximum(jnp.maximum(a, b), c)
    ea, eb, ec = jnp.exp(a - m), jnp.exp(b - m), jnp.exp(c - m)
    out_ref[...] = ((ea * o0[...] + eb * o1[...] + ec * o2[...]) / (ea + eb + ec)).astype(BF16)


def ab_merge(outs, lses):
    t, w = outs[0].shape
    tm = min(t, 1024)
    spec = pl.BlockSpec((tm, w), lambda i: (i, 0))
    return pl.pallas_call(
        _ab_merge_kernel,
        grid=(t // tm,),
        in_specs=[spec] * 6,
        out_specs=spec,
        out_shape=jax.ShapeDtypeStruct((t, w), BF16),
        compiler_params=_cparams("parallel"),
        name="ab_merge",
    )(*outs, *lses)


CD_COLS = dict(cq=0, ck=256, cv=512, cgate=1024, dq=1536, dk=2048, dv=2560, iq=3072, ik=3328, clr=3392, iw=3408)
CD_PAD = 3456
SMALL_BLOCK = CD_COLS["ik"] // LANES
SM_IK, SM_CLR, SM_IW = 0, CD_COLS["clr"] - CD_COLS["ik"], CD_COLS["iw"] - CD_COLS["ik"]


def cd_reorder_w(w_in):
    sizes = (256, 256, 512, 16, 512, 512, 512, 512, 256, 64, 4)
    names = ("cq", "ck", "cv", "clr", "cgate", "dq", "dk", "dv", "iq", "ik", "iw")
    starts = np.concatenate([[0], np.cumsum(sizes)[:-1]])
    out = jnp.zeros((w_in.shape[0], CD_PAD), w_in.dtype)
    for nm, st, sz in zip(names, starts, sizes):
        out = lax.dynamic_update_slice(out, w_in[:, st:st + sz], (0, CD_COLS[nm]))
    return out


def _cd_pre_kernel(d_ref, sm_ref, wa_ref, ba_ref, qg_ref, kg_ref, seg_ref, g_ref, dq_ref, kv_ref, *t_refs, w):
    d = d_ref[...]
    sm = sm_ref[...]
    clr = sm[:, SM_CLR:SM_CLR + GLA_GATE_RANK]
    wa = wa_ref[...]
    pre = ba_ref[...]
    for part in _split3(clr):
        pre = pre + _dot(part, wa[0]) + _dot(part, wa[1])
    lsig = jnp.minimum(pre, 0.0) - jnp.log(1.0 + jnp.exp(-jnp.abs(pre)))
    g_ref[...] = lsig * (1.0 / GLA_TAU)
    q_scale = HEAD_DIM ** -0.5 * (LOG2E if t_refs else 1.0)
    dq_ref[...] = (_head_norm(d[:, 0:w], qg_ref[...], seg_ref) * q_scale).astype(BF16)
    kn = _head_norm(d[:, w:2 * w], kg_ref[...], seg_ref)
    v = d[:, 2 * w:3 * w]
    kv_ref[:, 0:w] = kn
    kv_ref[:, w:2 * w] = v
    if t_refs:
        kb_ref, vt_ref, smt_ref = t_refs
        kb_ref[...] = kn.astype(BF16)
        vt_ref[0] = v.T.astype(BF16)
        smt_ref[0] = sm.T


def cd_pre(z, wa2, ba, qn, kn, seq=None):
    t = z.shape[0]
    w = DSA_HEADS * HEAD_DIM
    gw = GLA_HEADS * GLA_DK
    tm = min(t, 512)
    wa_hi, wa_lo = _split2(wa2)
    out_specs = [pl.BlockSpec((tm, gw), lambda i: (i, 0)),
                 pl.BlockSpec((tm, w), lambda i: (i, 0)),
                 pl.BlockSpec((tm, 2 * w), lambda i: (i, 0))]
    out_shape = [jax.ShapeDtypeStruct((t, gw), F32),
                 jax.ShapeDtypeStruct((t, w), BF16),
                 jax.ShapeDtypeStruct((t, 2 * w), F32)]
    if seq is not None:
        nt = seq // tm
        out_specs += [pl.BlockSpec((tm, w), lambda i: (i, 0)),
                      pl.BlockSpec((1, w, tm), lambda i: (i // nt, 0, i % nt)),
                      pl.BlockSpec((1, LANES, tm), lambda i: (i // nt, 0, i % nt))]
        out_shape += [jax.ShapeDtypeStruct((t, w), BF16),
                      jax.ShapeDtypeStruct((t // seq, w, seq), BF16),
                      jax.ShapeDtypeStruct((t // seq, LANES, seq), F32)]
    return pl.pallas_call(
        functools.partial(_cd_pre_kernel, w=w),
        grid=(t // tm,),
        in_specs=[pl.BlockSpec((tm, 3 * w), lambda i: (i, CD_COLS["dq"] // (3 * w))),
                  pl.BlockSpec((tm, LANES), lambda i: (i, SMALL_BLOCK)),
                  pl.BlockSpec((2, GLA_GATE_RANK, gw), lambda i: (0, 0, 0)),
                  pl.BlockSpec((1, gw), lambda i: (0, 0)),
                  pl.BlockSpec((1, w), lambda i: (0, 0)),
                  pl.BlockSpec((1, w), lambda i: (0, 0)),
                  pl.BlockSpec((w, w), lambda i: (0, 0))],
        out_specs=out_specs,
        out_shape=out_shape,
        compiler_params=_cparams("parallel"),
        name="cd_pre",
    )(z, z, jnp.stack([wa_hi, wa_lo]), ba.reshape(1, gw), jnp.tile(qn, DSA_HEADS).reshape(1, w),
      jnp.tile(kn, DSA_HEADS).reshape(1, w), _seg_matrix(w))


def _gla_consts(c):
    levels = []
    s = c // 2
    while s >= 1:
        levels.append(s)
        s //= 2
    i = np.arange(c)
    tri = (i[None, :] <= i[:, None]).astype(np.float32)
    mats = [tri]
    masks = []
    for s in levels:
        ref = (i // (2 * s)) * (2 * s) + s - 1
        r = (i[None, :] <= ref[:, None]).astype(np.float32)
        mats.append(tri - r)
        same = (i[:, None] // (2 * s)) == (i[None, :] // (2 * s))
        masks.append(same & ((i[:, None] % (2 * s)) >= s) & ((i[None, :] % (2 * s)) < s))
    masks.append(i[:, None] == i[None, :])
    mstack = np.concatenate(mats, axis=0)
    mask = np.stack([np.tile(m.astype(np.float32), (1, GLA_HEADS)) for m in masks])
    return jnp.asarray(mstack, BF16), jnp.asarray(mask, F32), len(levels)


def _gla_kernel(qk_ref, v_ref, gate_ref, g_ref, s0_ref, mstack_ref, mask_ref, gn_ref, eye_ref,
                o_ref, sf_ref, st_ref, *, c, nl):
    t = pl.program_id(1)
    kw = GLA_HEADS * GLA_DK
    vw = GLA_HEADS * GLA_DV
    lane_k = lax.broadcasted_iota(I32, (1, kw), 1) // GLA_DK
    lane_v = lax.broadcasted_iota(I32, (1, vw), 1) // GLA_DV

    @pl.when(t == 0)
    def _():
        rows = []
        for h in range(GLA_HEADS):
            z = jnp.zeros((GLA_DK, GLA_DV), F32)
            rows.append(jnp.concatenate([s0_ref[0, h] if hh == h else z for hh in range(GLA_HEADS)], axis=1))
        st_ref[...] = jnp.concatenate(rows, axis=0).T

    qk = qk_ref[0]
    q = qk[:, 0:kw] * GLA_DK ** -0.5
    k = qk[:, kw:2 * kw]
    v = v_ref[0]
    mstack = mstack_ref[...]
    r = None
    for part in _split3(g_ref[0]):
        d = _dot(mstack, part)
        r = d if r is None else r + d
    b = r[0:c]

    def expand_k(x):
        return jnp.concatenate([jnp.where(lane_k == h, x, 0.0) for h in range(GLA_HEADS)], axis=0).astype(BF16)

    st = st_ref[...]
    o = _nt((q * jnp.exp(b)).astype(BF16), st.astype(BF16))
    a = mask_ref[nl] * _nt(q.astype(BF16), expand_k(k))
    for lv in range(nl):
        sc = jnp.exp(-jnp.abs(r[(lv + 1) * c:(lv + 2) * c]))
        a = a + mask_ref[lv] * _nt((q * sc).astype(BF16), expand_k(k * sc))
    vexp = jnp.concatenate([jnp.where(lane_v == h, v, 0.0) for h in range(GLA_HEADS)], axis=0).astype(BF16)
    o = o + _dot(a.astype(BF16), vexp)

    blast = b[c - 1:c]
    kt = (k * jnp.exp(blast - b)).astype(BF16)
    vt = _nt(eye_ref[...], v.astype(BF16)).astype(BF16)
    upd = _dot(vt, kt)
    row_h = lax.broadcasted_iota(I32, (vw, 1), 0) // GLA_DV
    st_new = st * jnp.exp(blast) + jnp.where(row_h == lane_k, upd, 0.0)
    st_ref[...] = st_new

    gate = gate_ref[0]
    gn = gn_ref[...]
    for h in range(GLA_HEADS):
        hs = slice(h * GLA_DV, (h + 1) * GLA_DV)
        oh = o[:, hs]
        y = oh * lax.rsqrt(jnp.mean(oh * oh, axis=-1, keepdims=True) + EPS) * gn
        gh = gate[:, hs]
        o_ref[0, :, hs] = (y * (gh * jax.nn.sigmoid(gh))).astype(BF16)

    @pl.when(t == pl.num_programs(1) - 1)
    def _():
        s_t = st_new.T
        for h in range(GLA_HEADS):
            sf_ref[0, h] = s_t[h * GLA_DK:(h + 1) * GLA_DK, h * GLA_DV:(h + 1) * GLA_DV]


def gla(z3, g3, s0, gla_norm):
    nb, l, _ = z3.shape
    c = min(l, GLA_CHUNK)
    assert l % c == 0 and c & (c - 1) == 0
    kw = GLA_HEADS * GLA_DK
    vw = GLA_HEADS * GLA_DV
    mstack, mask, nl = _gla_consts(c)
    eye = jnp.eye(vw, dtype=BF16)
    o, sf = pl.pallas_call(
        functools.partial(_gla_kernel, c=c, nl=nl),
        grid=(nb, l // c),
        in_specs=[pl.BlockSpec((1, c, 2 * kw), lambda n, t: (n, t, 0)),
                  pl.BlockSpec((1, c, vw), lambda n, t: (n, t, CD_COLS["cv"] // vw)),
                  pl.BlockSpec((1, c, vw), lambda n, t: (n, t, CD_COLS["cgate"] // vw)),
                  pl.BlockSpec((1, c, kw), lambda n, t: (n, t, 0)),
                  pl.BlockSpec((1, GLA_HEADS, GLA_DK, GLA_DV), lambda n, t: (n, 0, 0, 0)),
                  pl.BlockSpec(mstack.shape, lambda n, t: (0, 0)),
                  pl.BlockSpec(mask.shape, lambda n, t: (0, 0, 0)),
                  pl.BlockSpec((1, GLA_DV), lambda n, t: (0, 0)),
                  pl.BlockSpec((vw, vw), lambda n, t: (0, 0))],
        out_specs=[pl.BlockSpec((1, c, vw), lambda n, t: (n, t, 0)),
                   pl.BlockSpec((1, GLA_HEADS, GLA_DK, GLA_DV), lambda n, t: (n, 0, 0, 0))],
        out_shape=[jax.ShapeDtypeStruct((nb, l, vw), BF16),
                   jax.ShapeDtypeStruct((nb, GLA_HEADS, GLA_DK, GLA_DV), F32)],
        scratch_shapes=[pltpu.VMEM((vw, kw), F32)],
        compiler_params=_cparams("parallel", "arbitrary"),
        name="gla",
    )(z3, z3, z3, g3, s0, mstack, mask, gla_norm.reshape(1, GLA_DV), eye)
    return o, sf


def _sort_key(x):
    bits = pltpu.bitcast(x, I32)
    return jnp.where(bits < 0, (bits ^ 0x7FFFFFFF) + 1, bits)


def _idx_lhs(iq, h):
    hi, lo = _split2(iq[:, h * IDX_DIM:(h + 1) * IDX_DIM])
    return jnp.concatenate([hi, hi, lo, jnp.zeros_like(hi)], axis=1)


def _idx_rhs(ik):
    hi, lo = _split2(ik)
    return jnp.concatenate([hi, lo, hi, jnp.zeros_like(hi)], axis=1)


def _idx_scores(lhs, rhs, wcol):
    sc = None
    for h in range(IDX_HEADS):
        d = jnp.maximum(_nt(lhs[h], rhs) * IDX_DIM ** -0.5, 0.0) * (wcol[h] * IDX_HEADS ** -0.5)
        sc = d if sc is None else sc + d
    return sc


def _dsa_select_kernel(iq_ref, smt_ref, smk_ref, tril_ref, mask_ref, kb3_ref, hi_ref, d1_ref, d0_ref, *, nc, topk):
    i = pl.program_id(1)
    ck = DSA_CK
    qb = DSA_SQ
    bpc = ck // qb
    sub = ck // 8

    @pl.when(i == 0)
    def _():
        for c in range(nc):
            kb3_ref[c] = _idx_rhs(smk_ref[0, c * ck:(c + 1) * ck, SM_IK:SM_IK + IDX_DIM])

    iq = iq_ref[0]
    smt = smt_ref[0]
    lhs = jnp.concatenate([_idx_lhs(iq, h) for h in range(IDX_HEADS)], axis=0)
    wrow = [smt[SM_IW + h:SM_IW + h + 1, :] * (IDX_HEADS ** -0.5 * IDX_DIM ** -0.5) for h in range(IDX_HEADS)]
    nch = (i + bpc) // bpc

    def chunk_planes(c, causal):
        dots = _nt(kb3_ref[c], lhs)
        sc = None
        for h in range(IDX_HEADS):
            t = jnp.maximum(dots[:, h * qb:(h + 1) * qb], 0.0) * wrow[h]
            sc = t if sc is None else sc + t
        sc = jnp.where(jnp.abs(sc) < F32_MIN_NORMAL, 0.0, sc)
        bits = pltpu.bitcast(sc, I32)
        key = bits ^ (lax.shift_right_arithmetic(bits, 31) & 0x7FFFFFFF)
        hi = pltpu.bitcast(bits & -65536, F32)
        if causal:
            drc = lax.broadcasted_iota(I32, (ck, qb), 0) - lax.broadcasted_iota(I32, (ck, qb), 1)
            hi = jnp.where(drc <= i * qb - c * ck, hi, -jnp.inf)
        hi_ref[c] = hi.astype(BF16)
        d1_ref[c] = (lax.shift_right_logical(key, 8) & 0xFF).astype(F32).astype(BF16)
        d0_ref[c] = (key & 0xFF).astype(F32).astype(BF16)

    def score_body(c, carry):
        chunk_planes(c, False)
        return carry

    lax.fori_loop(0, nch - 1, score_body, 0)
    chunk_planes(nch - 1, True)

    one, zero = jnp.ones((), BF16), jnp.zeros((), BF16)

    def count(ref, cand, strict):
        def body(c, acc):
            blk = ref[c]
            m = jnp.where(blk > cand if strict else blk >= cand, one, zero)
            for j in range(8):
                acc = acc + m[j * sub:(j + 1) * sub, :]
            return acc
        acc = lax.fori_loop(0, nch, body, jnp.zeros((sub, qb), BF16))
        return jnp.sum(acc.astype(F32), axis=0, keepdims=True)

    qpos = i * qb + lax.broadcasted_iota(I32, (1, qb), 1)
    kk = jnp.minimum(topk, qpos + 1).astype(F32)

    def hi_value(s16):
        p = jnp.where(s16 >= 0, s16, s16 ^ 0x7FFF) & 0xFFFF
        return pltpu.bitcast(lax.shift_left(p, 16), F32).astype(BF16)

    def descend(ref, nbits, start, as_cand, want, c_start):
        def step(t, carry):
            thr, cge = carry
            cand = thr + lax.shift_left(jnp.int32(1), nbits - 1 - t)
            cnt = count(ref, as_cand(cand), False)
            ok = cnt >= want
            return jnp.where(ok, cand, thr), jnp.where(ok, cnt, cge)
        return lax.fori_loop(0, nbits, step, (jnp.full((1, qb), start, I32), c_start))

    digit = lambda t: t.astype(F32).astype(BF16)

    h16, cge1 = descend(hi_ref, 16, -(2 ** 15), hi_value, kk, jnp.zeros((1, qb), F32))
    hv = hi_value(h16)
    cgt1 = count(hi_ref, hv, True)
    need1 = kk - cgt1

    def restrict(dst_ref, cls_ref, cls_val):
        def body(c, carry):
            dst_ref[c] = jnp.where(cls_ref[c] == cls_val, dst_ref[c], -one)
            return carry
        lax.fori_loop(0, nch, body, 0)

    restrict(d1_ref, hi_ref, hv)
    b1, cge2 = descend(d1_ref, 8, 0, digit, need1, cge1 - cgt1)
    b1v = digit(b1)
    cgt2 = count(d1_ref, b1v, True)
    need2 = need1 - cgt2
    restrict(d0_ref, d1_ref, b1v)
    b0, cge = descend(d0_ref, 8, 0, digit, need2, cge2 - cgt2)
    b0v = digit(b0)
    tied = jnp.max(jnp.where(cge > need2, 1, 0))

    @pl.when(tied > 0)
    def _():
        need3 = need2 - count(d0_ref, b0v, True)
        b0f = b0.astype(F32)

        def tie_body(c, off):
            blk = d0_ref[c].astype(F32)
            eq = blk == b0f
            pref = _dot(tril_ref[...], jnp.where(eq, 1.0, 0.0).astype(BF16))
            drop = jnp.logical_and(eq, pref + off > need3)
            d0_ref[c] = jnp.where(drop, -1.0, blk).astype(BF16)
            return off + pref[ck - 1:ck, :]

        lax.fori_loop(0, nch, tie_body, jnp.zeros((1, qb), F32))

    neg = jnp.full((), NEG, BF16)

    def live_body(c, carry):
        keep3 = jnp.where(d0_ref[c] >= b0v, zero, neg)
        keep2 = jnp.where(d1_ref[c] > b1v, zero, keep3)
        mask_ref[0, 0, c] = jnp.where(hi_ref[c] > hv, zero, keep2)
        return carry

    def dead_body(c, carry):
        mask_ref[0, 0, c] = jnp.full((ck, qb), NEG, BF16)
        return carry

    lax.fori_loop(0, nch, live_body, 0)
    lax.fori_loop(nch, nc, dead_body, 0)


def dsa_select(z3, smt):
    b, s, _ = z3.shape
    nq, nc = s // DSA_SQ, s // DSA_CK
    topk = min(DSA_TOPK_MAX, s // 4)
    tril = jnp.asarray(np.tril(np.ones((DSA_CK, DSA_CK), np.float32)), BF16)
    return pl.pallas_call(
        functools.partial(_dsa_select_kernel, nc=nc, topk=topk),
        grid=(b, nq),
        in_specs=[pl.BlockSpec((1, DSA_SQ, IDX_HEADS * IDX_DIM), lambda n, i: (n, i, CD_COLS["iq"] // (IDX_HEADS * IDX_DIM))),
                  pl.BlockSpec((1, LANES, DSA_SQ), lambda n, i: (n, 0, i)),
                  pl.BlockSpec((1, s, LANES), lambda n, i: (n, 0, SMALL_BLOCK)),
                  pl.BlockSpec((DSA_CK, DSA_CK), lambda n, i: (0, 0))],
        out_specs=pl.BlockSpec((1, 1, nc, DSA_CK, DSA_SQ), lambda n, i: (n, i, 0, 0, 0)),
        out_shape=jax.ShapeDtypeStruct((b, nq, nc, DSA_CK, DSA_SQ), BF16),
        scratch_shapes=[pltpu.VMEM((nc, DSA_CK, 4 * IDX_DIM), BF16)] + [pltpu.VMEM((nc, DSA_CK, DSA_SQ), BF16)] * 3,
        compiler_params=_cparams("parallel", "arbitrary"),
        name="dsa_select",
    )(z3, smt, z3, tril)


def _dsa_bias_tiles(tab_d):
    o = 0
    while _rel_bucket_np(max(o * DSA_QB - (DSA_QB - 1), 0)) < REL_BUCKETS - 1:
        o += 1
    offs = np.arange(o + 1)[:, None, None] * DSA_QB
    d = offs + np.arange(DSA_QB)[None, None, :] - np.arange(DSA_QB)[None, :, None]
    tiles = bias_lookup(tab_d, _rel_bucket_np(d).reshape((o + 1) * DSA_QB, DSA_QB))
    return tiles.reshape(DSA_HEADS, o + 1, DSA_QB, DSA_QB)


def _dsa_attn_kernel(qi_ref, kc_ref, q_ref, k_ref, vt_ref, mask_ref, bt_ref, o_ref, m_ref, l_ref, acc_ref, s_ref,
                     *, n_off):
    s_id = pl.program_id(1)
    i = qi_ref[s_id]
    c = kc_ref[s_id]
    qb, ck = DSA_QB, DSA_CK
    bpc = ck // qb

    @pl.when(c == 0)
    def _():
        m_ref[...] = jnp.full(m_ref.shape, NEG, F32)
        l_ref[...] = jnp.zeros(l_ref.shape, F32)
        acc_ref[...] = jnp.zeros(acc_ref.shape, F32)

    madd = mask_ref[0, 0, 0].astype(F32)
    q = q_ref[0]
    k = k_ref[0]
    vt = vt_ref[0]
    offs = [jnp.clip(i - (bpc * c + t), 0, n_off - 1) for t in range(bpc)]
    m_all = m_ref[...]
    l_all = l_ref[...]
    m_rows, l_rows = [], []
    for h in range(DSA_HEADS):
        hs = slice(h * HEAD_DIM, (h + 1) * HEAD_DIM)
        bias = jnp.concatenate([bt_ref[h, offs[t]] for t in range(bpc)], axis=0)
        s = _nt(k[:, hs], q[:, hs]) + bias + madd
        s_ref[h] = s
        m_rows.append(jnp.maximum(m_all[h:h + 1, :], jnp.max(s, axis=0, keepdims=True)))
    ones = jnp.ones((16, ck), BF16)
    for h in range(DSA_HEADS):
        hs = slice(h * HEAD_DIM, (h + 1) * HEAD_DIM)
        alpha = jnp.exp2(m_all[h:h + 1, :] - m_rows[h])
        p = jnp.exp2(s_ref[h] - m_rows[h]).astype(BF16)
        pv = _dot(jnp.concatenate([vt[hs, :], ones], axis=0), p)
        l_rows.append(alpha * l_all[h:h + 1, :] + pv[HEAD_DIM:HEAD_DIM + 1, :])
        acc_ref[hs, :] = alpha * acc_ref[hs, :] + pv[0:HEAD_DIM, :]
    m_ref[...] = jnp.concatenate(m_rows, axis=0)
    l_ref[...] = jnp.concatenate(l_rows, axis=0)

    @pl.when(c == i // bpc)
    def _():
        inv = 1.0 / l_ref[...]
        ot = jnp.concatenate([acc_ref[h * HEAD_DIM:(h + 1) * HEAD_DIM, :] * inv[h:h + 1, :]
                              for h in range(DSA_HEADS)], axis=0)
        o_ref[0] = ot.T.astype(BF16)


def dsa_attend(dq, kb, vt, mask, tab_d):
    b, s, w = dq.shape
    nq = s // DSA_QB
    bpc = DSA_CK // DSA_QB
    qps = DSA_SQ // DSA_QB
    bt = _dsa_bias_tiles(tab_d) * LOG2E
    n_off = bt.shape[1]
    qi = np.concatenate([np.full(i // bpc + 1, i) for i in range(nq)]).astype(np.int32)
    kc = np.concatenate([np.arange(i // bpc + 1) for i in range(nq)]).astype(np.int32)
    grid_spec = pltpu.PrefetchScalarGridSpec(
        num_scalar_prefetch=2,
        grid=(b, len(qi)),
        in_specs=[pl.BlockSpec((1, DSA_QB, w), lambda n, t, qi_, kc_: (n, qi_[t], 0)),
                  pl.BlockSpec((1, DSA_CK, w), lambda n, t, qi_, kc_: (n, kc_[t], 0)),
                  pl.BlockSpec((1, w, DSA_CK), lambda n, t, qi_, kc_: (n, 0, kc_[t])),
                  pl.BlockSpec((1, 1, 1, DSA_CK, DSA_QB),
                               lambda n, t, qi_, kc_: (n, qi_[t] // qps, kc_[t], 0, qi_[t] % qps)),
                  pl.BlockSpec(bt.shape, lambda n, t, qi_, kc_: (0, 0, 0, 0))],
        out_specs=pl.BlockSpec((1, DSA_QB, w), lambda n, t, qi_, kc_: (n, qi_[t], 0)),
        scratch_shapes=[pltpu.VMEM((DSA_HEADS, DSA_QB), F32), pltpu.VMEM((DSA_HEADS, DSA_QB), F32),
                        pltpu.VMEM((w, DSA_QB), F32), pltpu.VMEM((DSA_HEADS, DSA_CK, DSA_QB), F32)],
    )
    return pl.pallas_call(
        functools.partial(_dsa_attn_kernel, n_off=n_off),
        grid_spec=grid_spec,
        out_shape=jax.ShapeDtypeStruct((b, s, w), BF16),
        compiler_params=_cparams("parallel", "arbitrary"),
        name="dsa_attend",
    )(jnp.asarray(qi), jnp.asarray(kc), dq, kb, vt, mask, bt)


SCORE_PAGES = 16
ATTN_PAGES = 16


def _dsa_step_scores_kernel(pt_ref, iq_ref, sm_ref, *rest, npg):
    k_refs, o_ref = rest[:npg], rest[npg]
    iq = iq_ref[0]
    sm = sm_ref[0]
    lhs = jnp.concatenate([_idx_lhs(iq, h) for h in range(IDX_HEADS)], axis=0)
    wcol = [sm[:, SM_IW + h:SM_IW + h + 1] * IDX_HEADS ** -0.5 for h in range(IDX_HEADS)]
    for j in range(npg):
        hi, lo = _split2(k_refs[j][0, 0])
        d = _dot(lhs, jnp.concatenate([hi, lo, hi, jnp.zeros_like(hi)], axis=0))
        sc = None
        for h in range(IDX_HEADS):
            t = jnp.maximum(d[h * ROWS8:(h + 1) * ROWS8] * IDX_DIM ** -0.5, 0.0) * wcol[h]
            sc = t if sc is None else sc + t
        o_ref[0, :, j * PAGE_SIZE:(j + 1) * PAGE_SIZE] = sc


def dsa_step_scores(z8, kidx_t, layer, page_table):
    nb = z8.shape[0]
    n_pages = page_table.shape[1]
    npg = math.gcd(SCORE_PAGES, n_pages)
    iqw = IDX_HEADS * IDX_DIM

    def page_spec(j):
        return pl.BlockSpec((1, 1, IDX_DIM, PAGE_SIZE),
                            lambda n, p, pt: (layer, pt[n * n_pages + p * npg + j], 0, 0))

    grid_spec = pltpu.PrefetchScalarGridSpec(
        num_scalar_prefetch=1,
        grid=(nb, n_pages // npg),
        in_specs=[pl.BlockSpec((1, ROWS8, iqw), lambda n, p, pt: (n, 0, CD_COLS["iq"] // iqw)),
                  pl.BlockSpec((1, ROWS8, LANES), lambda n, p, pt: (n, 0, SMALL_BLOCK))]
                 + [page_spec(j) for j in range(npg)],
        out_specs=pl.BlockSpec((1, ROWS8, npg * PAGE_SIZE), lambda n, p, pt: (n, 0, p)),
    )
    return pl.pallas_call(
        functools.partial(_dsa_step_scores_kernel, npg=npg),
        grid_spec=grid_spec,
        out_shape=jax.ShapeDtypeStruct((nb, ROWS8, n_pages * PAGE_SIZE), F32),
        compiler_params=_cparams("parallel", "arbitrary"),
        name="dsa_step_scores",
    )(page_table.reshape(-1), z8, z8, *([kidx_t] * npg))


def _dsa_step_select_kernel(sc_ref, iq_ref, sm_ref, triu_ref, mp_ref, mn_ref, key_ref, *, past, l_new, topk):
    ck = DSA_CK
    nck = past // ck
    iq = iq_ref[0]
    sm = sm_ref[0]
    lhs = [_idx_lhs(iq, h) for h in range(IDX_HEADS)]
    wcol = [sm[:, SM_IW + h:SM_IW + h + 1] for h in range(IDX_HEADS)]
    rhs_new = _idx_rhs(jnp.concatenate([sm[:, SM_IK:SM_IK + IDX_DIM], jnp.zeros((LANES - ROWS8, IDX_DIM), F32)], axis=0))
    sc_new = _idx_scores(lhs, rhs_new, wcol)
    row = lax.broadcasted_iota(I32, (ROWS8, LANES), 0)
    col = lax.broadcasted_iota(I32, (ROWS8, LANES), 1)
    key_new = jnp.where(jnp.logical_and(col <= row, col < l_new), _sort_key(sc_new), INT_MIN)
    key_ref[...] = _sort_key(sc_ref[0])
    kk = jnp.minimum(topk, past + 1 + lax.broadcasted_iota(I32, (ROWS8, 1), 0))

    def count(pred):
        return (jnp.sum(jnp.where(pred(key_ref[...]), 1, 0), axis=1, keepdims=True)
                + jnp.sum(jnp.where(pred(key_new), 1, 0), axis=1, keepdims=True))

    def bit_body(t, carry):
        thr, cge = carry
        cand = thr + lax.shift_left(jnp.int32(1), 31 - t)
        cnt = count(lambda x: x >= cand)
        ok = cnt >= kk
        return jnp.where(ok, cand, thr), jnp.where(ok, cnt, cge)

    thr, cge = lax.fori_loop(0, 32, bit_body, (jnp.full((ROWS8, 1), INT_MIN, I32), jnp.zeros((ROWS8, 1), I32)))
    cgt = count(lambda x: x > thr)
    need = kk - cgt
    needf = need.astype(F32)
    tied = jnp.max(jnp.where(need < cge - cgt, 1, 0))
    mn_ref[0] = jnp.where(key_new >= thr, 0.0, NEG)

    @pl.when(tied == 0)
    def _():
        mp_ref[0] = jnp.where(key_ref[...] >= thr, 0.0, NEG)

    @pl.when(tied > 0)
    def _():
        off = jnp.zeros((ROWS8, 1), F32)
        triu = triu_ref[...]
        for c in range(nck):
            blk = key_ref[:, c * ck:(c + 1) * ck]
            eq = blk == thr
            pref = _dot(jnp.where(eq, 1.0, 0.0).astype(BF16), triu)
            keep = jnp.logical_or(blk > thr, jnp.logical_and(eq, pref + off <= needf))
            mp_ref[0, :, c * ck:(c + 1) * ck] = jnp.where(keep, 0.0, NEG)
            off = off + pref[:, ck - 1:ck]
        eq = key_new == thr
        pref = _dot(jnp.where(eq, 1.0, 0.0).astype(BF16), triu[0:LANES, 0:LANES])
        keep = jnp.logical_or(key_new > thr, jnp.logical_and(eq, pref + off <= needf))
        mn_ref[0] = jnp.where(keep, 0.0, NEG)


def dsa_step_select(scores, z8, l_new):
    nb, _, past = scores.shape
    topk = min(DSA_TOPK_MAX, (past + l_new) // 4)
    iqw = IDX_HEADS * IDX_DIM
    assert past % DSA_CK == 0
    triu = jnp.asarray(np.triu(np.ones((DSA_CK, DSA_CK), np.float32)), BF16)
    return pl.pallas_call(
        functools.partial(_dsa_step_select_kernel, past=past, l_new=l_new, topk=topk),
        grid=(nb,),
        in_specs=[pl.BlockSpec((1, ROWS8, past), lambda n: (n, 0, 0)),
                  pl.BlockSpec((1, ROWS8, iqw), lambda n: (n, 0, CD_COLS["iq"] // iqw)),
                  pl.BlockSpec((1, ROWS8, LANES), lambda n: (n, 0, SMALL_BLOCK)),
                  pl.BlockSpec((DSA_CK, DSA_CK), lambda n: (0, 0))],
        out_specs=[pl.BlockSpec((1, ROWS8, past), lambda n: (n, 0, 0)),
                   pl.BlockSpec((1, ROWS8, LANES), lambda n: (n, 0, 0))],
        out_shape=[jax.ShapeDtypeStruct((nb, ROWS8, past), F32),
                   jax.ShapeDtypeStruct((nb, ROWS8, LANES), F32)],
        scratch_shapes=[pltpu.VMEM((ROWS8, past), I32)],
        compiler_params=_cparams("parallel"),
        name="dsa_step_select",
    )(scores, z8, z8, triu)


def _dsa_step_attn_kernel(pt_ref, q_ref, kvn_ref, mp_ref, mn_ref, bp_ref, bn_ref, *rest, w, npg):
    kv_refs, o_ref = rest[:npg], rest[npg]
    qbd_ref, newpage_ref, m_ref, l_ref, acc_ref = rest[npg + 1:]
    n = pl.program_id(0)
    p = pl.program_id(1)
    rows = DSA_HEADS * ROWS8
    lane_h = lax.broadcasted_iota(I32, (ROWS8, w), 1) // HEAD_DIM

    @pl.when(jnp.logical_and(n == 0, p == 0))
    def _():
        newpage_ref[...] = jnp.zeros(newpage_ref.shape, F32)

    @pl.when(p == 0)
    def _():
        q = q_ref[0]
        for h in range(DSA_HEADS):
            qbd_ref[h * ROWS8:(h + 1) * ROWS8, :] = jnp.where(lane_h == h, q, jnp.zeros_like(q))
        m_ref[...] = jnp.full(m_ref.shape, NEG, F32)
        l_ref[...] = jnp.zeros(l_ref.shape, F32)
        acc_ref[...] = jnp.zeros(acc_ref.shape, F32)

    def accumulate(scores, madd8, bias, pv):
        s = scores + bias + jnp.concatenate([madd8] * DSA_HEADS, axis=0)
        m_old = m_ref[:, 0:1]
        m_new = jnp.maximum(m_old, jnp.max(s, axis=-1, keepdims=True))
        alpha = jnp.exp(m_old - m_new)
        pr = jnp.exp(s - m_new)
        l_ref[...] = jnp.broadcast_to(alpha * l_ref[:, 0:1] + jnp.sum(pr, axis=-1, keepdims=True), (rows, LANES))
        m_ref[...] = jnp.broadcast_to(m_new, (rows, LANES))
        acc_ref[...] = alpha * acc_ref[...] + pv(pr.astype(BF16))

    qbd = qbd_ref[...]
    scores = jnp.concatenate([_dot(qbd, kv_refs[j][0, 0, 0].reshape(w, PAGE_SIZE).astype(BF16))
                              for j in range(npg)], axis=1)

    def pv_pages(pr):
        out = None
        for j in range(npg):
            t = _nt(pr[:, j * PAGE_SIZE:(j + 1) * PAGE_SIZE], kv_refs[j][0, 0, 1].reshape(w, PAGE_SIZE).astype(BF16))
            out = t if out is None else out + t
        return out

    accumulate(scores, mp_ref[0], bp_ref[...], pv_pages)

    @pl.when(p == pl.num_programs(1) - 1)
    def _():
        newpage_ref[0:ROWS8, :] = kvn_ref[0]
        newp = newpage_ref[...]
        accumulate(_nt(qbd, newp[:, 0:w].astype(BF16)), mn_ref[0], bn_ref[...],
                   lambda pr: _dot(pr, newp[:, w:2 * w].astype(BF16)))
        for h in range(DSA_HEADS):
            rs = slice(h * ROWS8, (h + 1) * ROWS8)
            hs = slice(h * HEAD_DIM, (h + 1) * HEAD_DIM)
            o_ref[0, :, hs] = (acc_ref[rs, hs] / l_ref[rs, 0:1]).astype(BF16)


def _dsa_step_bias(tab_d, past, l_new):
    lq = np.minimum(np.arange(ROWS8), l_new - 1)[:, None]
    d_past = past + lq - np.arange(past)[None, :]
    d_new = lq - np.arange(LANES)[None, :]

    def table(d):
        return bias_lookup(tab_d, _rel_bucket_np(d)).reshape(DSA_HEADS * ROWS8, d.shape[1])

    return table(d_past), table(d_new)


def dsa_step_attend(dq8, kv_new8, kv_t, layer, page_table, mask_past, mask_new, tab_d, l_new):
    nb, _, w = dq8.shape
    n_pages = page_table.shape[1]
    npg = math.gcd(ATTN_PAGES, n_pages)
    past = n_pages * PAGE_SIZE
    rows = DSA_HEADS * ROWS8
    bp, bn = _dsa_step_bias(tab_d, past, l_new)

    def page_spec(j):
        return pl.BlockSpec((1, 1, 2, DSA_HEADS, HEAD_DIM, PAGE_SIZE),
                            lambda n, p, pt: (layer, pt[n * n_pages + p * npg + j], 0, 0, 0, 0))

    grid_spec = pltpu.PrefetchScalarGridSpec(
        num_scalar_prefetch=1,
        grid=(nb, n_pages // npg),
        in_specs=[pl.BlockSpec((1, ROWS8, w), lambda n, p, pt: (n, 0, 0)),
                  pl.BlockSpec((1, ROWS8, 2 * w), lambda n, p, pt: (n, 0, 0)),
                  pl.BlockSpec((1, ROWS8, npg * PAGE_SIZE), lambda n, p, pt: (n, 0, p)),
                  pl.BlockSpec((1, ROWS8, LANES), lambda n, p, pt: (n, 0, 0)),
                  pl.BlockSpec((rows, npg * PAGE_SIZE), lambda n, p, pt: (0, p)),
                  pl.BlockSpec((rows, LANES), lambda n, p, pt: (0, 0))]
                 + [page_spec(j) for j in range(npg)],
        out_specs=pl.BlockSpec((1, ROWS8, w), lambda n, p, pt: (n, 0, 0)),
        scratch_shapes=[pltpu.VMEM((rows, w), BF16), pltpu.VMEM((PAGE_SIZE, 2 * w), F32),
                        pltpu.VMEM((rows, LANES), F32), pltpu.VMEM((rows, LANES), F32), pltpu.VMEM((rows, w), F32)],
    )
    return pl.pallas_call(
        functools.partial(_dsa_step_attn_kernel, w=w, npg=npg),
        grid_spec=grid_spec,
        out_shape=jax.ShapeDtypeStruct((nb, ROWS8, w), BF16),
        compiler_params=_cparams("arbitrary", "arbitrary"),
        name="dsa_step_attend",
    )(page_table.reshape(-1), dq8, kv_new8, mask_past, mask_new, bp, bn, *([kv_t] * npg))


def _pad_rows(x3, rows):
    return jnp.pad(x3, ((0, 0), (0, rows - x3.shape[1]), (0, 0)))


def _trunk(x, is_step, conv_state, win_states, gla_state, dsa_kv, dsa_kidx, page_table, wts):
    (norm_mix, norm_ffn, w_in_ab, conv_w, conv_b, conv_ln_g, conv_ln_b, qn_ab, kn_ab, w_out_ab,
     w_in_cd, gla_wa2, gla_ba, gla_norm, qn_cd, kn_cd, w_out_cd, rel_bias, w_g, w_u, w_d) = wts
    nb, l, d = x.shape
    t = nb * l
    depth = norm_mix.shape[0]
    c = conv_w.shape[2]
    wq = B_HPG * HEAD_DIM
    x2 = x.reshape(t, d)
    conv_new, gla_new, kv_new, kidx_new = [], [], [], []
    win_new = [[] for _ in WINDOWS]
    for layer in range(depth):
        i = layer // 2
        if layer % 2 == 0:
            z = norm_matmul(x2, norm_mix[layer], w_in_ab[i])
            hist = conv_state[i] if is_step else jnp.zeros((nb, CONV_WIDTH - 1, c), F32)
            a_out, c_st = conv_module(z.reshape(nb, l, -1), hist, conv_w[i], conv_b[i], conv_ln_g[i], conv_ln_b[i])
            conv_new.append(c_st)
            if is_step:
                qs, kvs = ab_qkv(z, qn_ab[i], kn_ab[i], 2 * c, t, (1,) * len(WINDOWS))
            else:
                qs, kvs = ab_qkv(z, qn_ab[i], kn_ab[i], 2 * c, l, DILATIONS)
            outs, lses = [], []
            for g, window in enumerate(WINDOWS):
                tab_g = rel_bias[:, g * B_HPG:(g + 1) * B_HPG]
                if is_step:
                    o, lse, st = win_step(_pad_rows(qs[g].reshape(nb, l, wq), ROWS8),
                                          kvs[g].reshape(nb, l, 2 * wq), win_states[g], i, g, tab_g)
                else:
                    o, lse = win_prompt(qs[g], kvs[g], g, tab_g)
                    keep = min(window, l)
                    st = kvs[g][:, -(keep // DILATIONS[g]):].reshape(nb, keep, 2, B_HPG, HEAD_DIM)
                outs.append(o.reshape(t, wq))
                lses.append(lse.reshape(t, wq))
                win_new[g].append(st)
            m1, m2 = a_out.reshape(t, c), ab_merge(outs, lses)
            wo = w_out_ab[i]
        else:
            z = norm_matmul(x2, norm_mix[layer], w_in_cd[i])
            pre = cd_pre(z, gla_wa2[i], gla_ba[i], qn_cd[i], kn_cd[i], None if is_step else l)
            gdec, dq, kv = pre[:3]
            z3 = z.reshape(nb, l, -1)
            wd = DSA_HEADS * HEAD_DIM
            ik = z3[:, :, CD_COLS["ik"]:CD_COLS["ik"] + IDX_DIM]
            tab_d = rel_bias[:, B_HEADS:]
            if is_step:
                lp = GLA_STEP_ROWS
                o_c, s_c = gla(_pad_rows(z3, lp), _pad_rows(gdec.reshape(nb, l, -1), lp), gla_state[i], gla_norm[i])
                o_c = o_c[:, :l]
                z8 = _pad_rows(z3, ROWS8)
                scores = dsa_step_scores(z8, dsa_kidx, i, page_table)
                mask_p, mask_n = dsa_step_select(scores, z8, l)
                o_d = dsa_step_attend(_pad_rows(dq.reshape(nb, l, wd), ROWS8), _pad_rows(kv.reshape(nb, l, 2 * wd), ROWS8),
                                      dsa_kv, i, page_table, mask_p, mask_n, tab_d, l)[:, :l]
                kv_st = kv.reshape(nb, l, 2, DSA_HEADS, HEAD_DIM)
                ki_st = ik
            else:
                s0 = jnp.zeros((nb, GLA_HEADS, GLA_DK, GLA_DV), F32)
                o_c, s_c = gla(z3, gdec.reshape(nb, l, -1), s0, gla_norm[i])
                kb, vt, smt = pre[3:]
                mask = dsa_select(z3, smt)
                o_d = dsa_attend(dq.reshape(nb, l, wd), kb.reshape(nb, l, wd), vt, mask, tab_d)
                n_pg = l // PAGE_SIZE
                kv_st = kv.reshape(nb, n_pg, PAGE_SIZE, 2, DSA_HEADS, HEAD_DIM)
                ki_st = ik.reshape(nb, n_pg, PAGE_SIZE, IDX_DIM)
            gla_new.append(s_c)
            kv_new.append(kv_st)
            kidx_new.append(ki_st)
            m1, m2 = o_c.reshape(t, -1), o_d.reshape(t, -1)
            wo = w_out_cd[i]
        d1 = m1.shape[1]
        x2 = mix_ffn(x2, m1, m2, wo[:d1], wo[d1:], norm_ffn[layer], w_g[layer], w_u[layer], w_d[layer])
    wins = [jnp.stack(ws) for ws in win_new]
    if is_step:
        wins = [ws.transpose(0, 1, 5, 2, 3, 4) for ws in wins]
    states = (jnp.stack(conv_new), wins[0], wins[1], wins[2],
              jnp.stack(gla_new), jnp.stack(kv_new), jnp.stack(kidx_new))
    return x2.reshape(nb, l, d), states


def kernel(x_prompt, x_sample, state_conv, cache_win128, cache_win512, cache_win2048, state_gla, cache_dsa_kv, cache_dsa_kidx, page_table, norm_mix, norm_ffn, w_in_ab, conv_w, conv_b, conv_ln_g, conv_ln_b, qn_ab, kn_ab, w_out_ab, w_in_cd, gla_wa2, gla_ba, gla_norm, qn_cd, kn_cd, w_out_cd, rel_bias, w_ffn_gate, w_ffn_up, w_ffn_down):
    bf = lambda a: a.astype(BF16)
    w_in_cd_r = jnp.stack([cd_reorder_w(w_in_cd[i]) for i in range(w_in_cd.shape[0])])
    wts = (norm_mix, norm_ffn, bf(w_in_ab), conv_w, conv_b, conv_ln_g, conv_ln_b, qn_ab, kn_ab, bf(w_out_ab),
           bf(w_in_cd_r), gla_wa2, gla_ba, gla_norm, qn_cd, kn_cd, bf(w_out_cd), rel_bias,
           bf(w_ffn_gate), bf(w_ffn_up), bf(w_ffn_down))
    y_p, sp = _trunk(x_prompt, False, None, None, None, None, None, None, wts)
    wins_t = tuple(cw.transpose(0, 1, 3, 4, 5, 2) for cw in (cache_win128, cache_win512, cache_win2048))
    y_s, ss = _trunk(x_sample, True, state_conv, wins_t, state_gla,
                     cache_dsa_kv.transpose(0, 1, 3, 4, 5, 2), cache_dsa_kidx.transpose(0, 1, 3, 2), page_table, wts)
    conv_p, win128_p, win512_p, win2048_p, gla_p, dsa_kv_p, dsa_kidx_p = sp
    conv_s, win128_s, win512_s, win2048_s, gla_s, dsa_kv_s, dsa_kidx_s = ss
    return (y_p, y_s, conv_p, conv_s, win128_p, win128_s, win512_p, win512_s, win2048_p, win2048_s,
            gla_p, gla_s, dsa_kv_p, dsa_kv_s, dsa_kidx_p, dsa_kidx_s)
```

```python
import functools
import math

import numpy as np
import jax
import jax.numpy as jnp
from jax import lax
from jax.experimental import pallas as pl
from jax.experimental.pallas import tpu as pltpu

F32 = jnp.float32
BF16 = jnp.bfloat16
I32 = jnp.int32

EPS = 1e-6
NEG = -1e30
LOG2E = math.log2(math.e)
F32_MIN_NORMAL = 2.0 ** -126
INT_MIN = -(2 ** 31)

V7X_VMEM_BYTES = 64 * 1024 * 1024
VMEM_LIMIT = V7X_VMEM_BYTES - 12 * 1024 * 1024
LANES = 128

HEAD_DIM = 64
CONV_WIDTH = 31
WINDOWS = (128, 512, 2048)
DILATIONS = (1, 4, 16)
B_HPG = 4
B_HEADS = B_HPG * len(WINDOWS)
SW_BLOCK = 128
GLA_HEADS = 4
GLA_DK = 64
GLA_DV = 128
GLA_GATE_RANK = 16
GLA_TAU = 16.0
DSA_HEADS = 8
IDX_HEADS = 4
IDX_DIM = 64
DSA_TOPK_MAX = 256
PAGE_SIZE = 128
REL_BUCKETS = 32
REL_MAX_DIST = 2048

GLA_CHUNK = 128
GLA_STEP_ROWS = 64
DSA_QB = 256
DSA_BT = 128
DSA_SQ = 256
DSA_CK = 512
HIST_PAD = 32


def _cparams(*sem):
    return pltpu.CompilerParams(dimension_semantics=sem, vmem_limit_bytes=VMEM_LIMIT)


def _nt(a, b):
    return lax.dot_general(a, b, (((1,), (1,)), ((), ())), preferred_element_type=F32)


def _dot(a, b):
    return jnp.dot(a, b, preferred_element_type=F32)


def _split2(x):
    hi = x.astype(BF16)
    lo = (x - hi.astype(F32)).astype(BF16)
    return hi, lo


def _split3(x):
    hi = x.astype(BF16)
    r = x - hi.astype(F32)
    mid = r.astype(BF16)
    lo = (r - mid.astype(F32)).astype(BF16)
    return hi, mid, lo


def _rel_bucket_np(dist):
    n = np.maximum(np.asarray(dist, np.int64), 0)
    max_exact = REL_BUCKETS // 2
    nf = np.maximum(n, max_exact).astype(np.float32)
    large = max_exact + (np.log(nf / np.float32(max_exact)) / np.float32(math.log(REL_MAX_DIST / max_exact))
                         * np.float32(REL_BUCKETS - max_exact)).astype(np.int32)
    large = np.minimum(large, REL_BUCKETS - 1)
    return np.where(n < max_exact, n, large).astype(np.int32)


def _bias_lookup_kernel(tab_ref, idx_ref, o_ref, *, nh):
    idx = idx_ref[...]
    for h in range(nh):
        acc = jnp.full(idx.shape, NEG, F32)
        for b in range(REL_BUCKETS):
            acc = jnp.where(idx == b, tab_ref[b, h], acc)
        o_ref[h] = acc


def bias_lookup(tab, idx_np):
    r, c = idx_np.shape
    nh = tab.shape[1]
    tr = 8 if (c >= 2048 and r % 8 == 0) else (128 if r % 128 == 0 else r)
    return pl.pallas_call(
        functools.partial(_bias_lookup_kernel, nh=nh),
        grid=(r // tr,),
        in_specs=[pl.BlockSpec(memory_space=pltpu.SMEM),
                  pl.BlockSpec((tr, c), lambda i: (i, 0))],
        out_specs=pl.BlockSpec((nh, tr, c), lambda i: (0, i, 0)),
        out_shape=jax.ShapeDtypeStruct((nh, r, c), F32),
        compiler_params=_cparams("parallel"),
        name="bias_lookup",
    )(tab, jnp.asarray(idx_np.astype(np.int32)))


def _norm_matmul_kernel(x_ref, g_ref, w_ref, o_ref):
    x = x_ref[...]
    y = x * lax.rsqrt(jnp.mean(x * x, axis=-1, keepdims=True) + EPS) * g_ref[...]
    o_ref[...] = _dot(y.astype(BF16), w_ref[...])


def norm_matmul(x, g, w):
    t, d = x.shape
    n = w.shape[1]
    tm = min(t, 256)
    return pl.pallas_call(
        _norm_matmul_kernel,
        grid=(t // tm,),
        in_specs=[pl.BlockSpec((tm, d), lambda i: (i, 0)),
                  pl.BlockSpec((1, d), lambda i: (0, 0)),
                  pl.BlockSpec((d, n), lambda i: (0, 0))],
        out_specs=pl.BlockSpec((tm, n), lambda i: (i, 0)),
        out_shape=jax.ShapeDtypeStruct((t, n), F32),
        compiler_params=_cparams("parallel"),
        name="norm_matmul",
    )(x, g.reshape(1, d), w)


def _mix_ffn_kernel(x_ref, m1_ref, m2_ref, wo1_ref, wo2_ref, g_ref, wg_ref, wu_ref, wd_ref, o_ref,
                    x1_ref, hf_ref, acc_ref):
    j = pl.program_id(1)

    @pl.when(j == 0)
    def _():
        x1 = x_ref[...] + _dot(m1_ref[...], wo1_ref[...]) + _dot(m2_ref[...], wo2_ref[...])
        x1_ref[...] = x1
        hf = x1 * lax.rsqrt(jnp.mean(x1 * x1, axis=-1, keepdims=True) + EPS) * g_ref[...]
        hf_ref[...] = hf.astype(BF16)
        acc_ref[...] = jnp.zeros_like(acc_ref)

    hf = hf_ref[...]
    a = _dot(hf, wg_ref[...])
    u = _dot(hf, wu_ref[...])
    act = (a * jax.nn.sigmoid(a) * u).astype(BF16)
    acc_ref[...] += _dot(act, wd_ref[...])

    @pl.when(j == pl.num_programs(1) - 1)
    def _():
        o_ref[...] = x1_ref[...] + acc_ref[...]


def mix_ffn(x, m1, m2, wo1, wo2, g, wg, wu, wd):
    t, d = x.shape
    hid = wg.shape[1]
    tm = min(t, 512)
    th = hid // 2 if hid % 256 == 0 else hid
    d1, d2 = m1.shape[1], m2.shape[1]
    return pl.pallas_call(
        _mix_ffn_kernel,
        grid=(t // tm, hid // th),
        in_specs=[pl.BlockSpec((tm, d), lambda i, j: (i, 0)),
                  pl.BlockSpec((tm, d1), lambda i, j: (i, 0)),
                  pl.BlockSpec((tm, d2), lambda i, j: (i, 0)),
                  pl.BlockSpec((d1, d), lambda i, j: (0, 0)),
                  pl.BlockSpec((d2, d), lambda i, j: (0, 0)),
                  pl.BlockSpec((1, d), lambda i, j: (0, 0)),
                  pl.BlockSpec((d, th), lambda i, j: (0, j)),
                  pl.BlockSpec((d, th), lambda i, j: (0, j)),
                  pl.BlockSpec((th, d), lambda i, j: (j, 0))],
        out_specs=pl.BlockSpec((tm, d), lambda i, j: (i, 0)),
        out_shape=jax.ShapeDtypeStruct((t, d), F32),
        scratch_shapes=[pltpu.VMEM((tm, d), F32), pltpu.VMEM((tm, d), BF16), pltpu.VMEM((tm, d), F32)],
        compiler_params=_cparams("parallel", "arbitrary"),
        name="mix_ffn",
    )(x, m1, m2, wo1, wo2, g.reshape(1, d), wg, wu, wd)


def _conv_kernel(z_ref, hist_ref, w_ref, b_ref, lg_ref, lb_ref, o_ref, tail_ref, uh_ref, *, ts, c):
    t = pl.program_id(1)

    @pl.when(t == 0)
    def _():
        uh_ref[0:HIST_PAD, :] = hist_ref[0]

    z = z_ref[0]
    u = z[:, 0:c] * jax.nn.sigmoid(z[:, c:2 * c])
    uh_ref[HIST_PAD:HIST_PAD + ts, :] = u
    acc = jnp.zeros((ts, c), F32) + b_ref[...]
    off = HIST_PAD - (CONV_WIDTH - 1)
    for j in range(CONV_WIDTH):
        acc = acc + w_ref[j:j + 1, :] * uh_ref[off + j:off + j + ts, :]
    mu = jnp.mean(acc, axis=-1, keepdims=True)
    var = jnp.mean(jnp.square(acc - mu), axis=-1, keepdims=True)
    yn = (acc - mu) * lax.rsqrt(var + EPS) * lg_ref[...] + lb_ref[...]
    o_ref[0] = (yn * jax.nn.sigmoid(yn)).astype(BF16)
    tail = uh_ref[ts:ts + HIST_PAD, :]
    uh_ref[0:HIST_PAD, :] = tail
    tail_ref[0] = tail


def conv_module(z3, hist, conv_w, conv_b, ln_g, ln_b):
    nb, l, _ = z3.shape
    c = conv_w.shape[1]
    ts = min(l, 512)
    hist_p = jnp.pad(hist, ((0, 0), (HIST_PAD - (CONV_WIDTH - 1), 0), (0, 0)))
    w_p = jnp.pad(conv_w, ((0, HIST_PAD - CONV_WIDTH), (0, 0)))
    a_out, tail = pl.pallas_call(
        functools.partial(_conv_kernel, ts=ts, c=c),
        grid=(nb, l // ts),
        in_specs=[pl.BlockSpec((1, ts, 2 * c), lambda n, t: (n, t, 0)),
                  pl.BlockSpec((1, HIST_PAD, c), lambda n, t: (n, 0, 0)),
                  pl.BlockSpec((HIST_PAD, c), lambda n, t: (0, 0)),
                  pl.BlockSpec((1, c), lambda n, t: (0, 0)),
                  pl.BlockSpec((1, c), lambda n, t: (0, 0)),
                  pl.BlockSpec((1, c), lambda n, t: (0, 0))],
        out_specs=[pl.BlockSpec((1, ts, c), lambda n, t: (n, t, 0)),
                   pl.BlockSpec((1, HIST_PAD, c), lambda n, t: (n, 0, 0))],
        out_shape=[jax.ShapeDtypeStruct((nb, l, c), BF16),
                   jax.ShapeDtypeStruct((nb, HIST_PAD, c), F32)],
        scratch_shapes=[pltpu.VMEM((HIST_PAD + ts, c), F32)],
        compiler_params=_cparams("parallel", "arbitrary"),
        name="conv_module",
    )(z3, hist_p, w_p, conv_b.reshape(1, c), ln_g.reshape(1, c), ln_b.reshape(1, c))
    return a_out, tail[:, HIST_PAD - (CONV_WIDTH - 1):]


def _seg_mean_sq(x, seg_ref):
    hi, lo = _split2(x * x)
    seg = seg_ref[...]
    return (_dot(hi, seg) + _dot(lo, seg)) * (1.0 / HEAD_DIM)


def _head_norm(x, g, seg_ref):
    return x * lax.rsqrt(_seg_mean_sq(x, seg_ref) + EPS) * g


def _seg_matrix(width):
    idx = np.arange(width) // HEAD_DIM
    return jnp.asarray((idx[:, None] == idx[None, :]).astype(np.float32), BF16)


def _ab_qkv_kernel(*refs, w, tm, dils):
    ng = len(dils)
    qkv_refs = refs[:3 * ng]
    qg_ref, kg_ref, seg_ref = refs[3 * ng:3 * ng + 3]
    qo_refs = refs[3 * ng + 3:4 * ng + 3]
    kvo_refs = refs[4 * ng + 3:5 * ng + 3]
    qs_ref, kvs_ref = refs[5 * ng + 3:]
    for g, dil in enumerate(dils):
        qn = _head_norm(qkv_refs[g][...], qg_ref[...], seg_ref) * HEAD_DIM ** -0.5
        kn = _head_norm(qkv_refs[ng + g][...], kg_ref[...], seg_ref)
        v = qkv_refs[2 * ng + g][...]
        if dil == 1:
            qo_refs[g][0] = qn.astype(BF16)
            kvo_refs[g][0, :, 0:w] = kn
            kvo_refs[g][0, :, w:2 * w] = v
        else:
            nq, nkv = w // LANES, 2 * w // LANES
            kv = jnp.concatenate([kn, v], axis=1)
            for j in range(nq):
                qs_ref[j] = qn[:, j * LANES:(j + 1) * LANES]
            for j in range(nkv):
                kvs_ref[j] = kv[:, j * LANES:(j + 1) * LANES]
            for r in range(dil):
                rows = pl.ds(r, tm // dil, stride=dil)
                for j in range(nq):
                    qo_refs[g][0, :, r * w + j * LANES:r * w + (j + 1) * LANES] = qs_ref[j, rows, :].astype(BF16)
                for j in range(nkv):
                    kvo_refs[g][0, :, r * 2 * w + j * LANES:r * 2 * w + (j + 1) * LANES] = kvs_ref[j, rows, :]


def ab_qkv(z, qn, kn, col0, seq, dils):
    t = z.shape[0]
    w = B_HPG * HEAD_DIM
    ng = len(dils)
    tm = min(seq, 512)
    nt = seq // tm
    cb = col0 // w
    assert all(tm % (8 * d) == 0 or d == 1 for d in dils)
    col_spec = lambda j: pl.BlockSpec((tm, w), lambda i: (i, cb + j))
    out_specs = ([pl.BlockSpec((1, tm // d, d * w), lambda i: (i // nt, i % nt, 0)) for d in dils]
                 + [pl.BlockSpec((1, tm // d, d * 2 * w), lambda i: (i // nt, i % nt, 0)) for d in dils])
    out_shape = ([jax.ShapeDtypeStruct((t // seq, seq // d, d * w), BF16) for d in dils]
                 + [jax.ShapeDtypeStruct((t // seq, seq // d, d * 2 * w), F32) for d in dils])
    outs = pl.pallas_call(
        functools.partial(_ab_qkv_kernel, w=w, tm=tm, dils=tuple(dils)),
        grid=(t // tm,),
        in_specs=[col_spec(j) for j in range(3 * ng)]
                 + [pl.BlockSpec((1, w), lambda i: (0, 0)),
                    pl.BlockSpec((1, w), lambda i: (0, 0)),
                    pl.BlockSpec((w, w), lambda i: (0, 0))],
        out_specs=out_specs,
        out_shape=out_shape,
        scratch_shapes=[pltpu.VMEM((w // LANES, tm, LANES), F32), pltpu.VMEM((2 * w // LANES, tm, LANES), F32)],
        compiler_params=_cparams("parallel"),
        name="ab_qkv",
    )(*([z] * (3 * ng)), jnp.tile(qn, B_HPG).reshape(1, w), jnp.tile(kn, B_HPG).reshape(1, w), _seg_matrix(w))
    return outs[:ng], outs[ng:]


WIN_BLOCKS = 4


def _win_prompt_kernel(q_ref, kvp_ref, kvc_ref, bias_ref, o_ref, lse_ref, *, w, nblk):
    step = pl.program_id(2)
    col = lax.broadcasted_iota(I32, (SW_BLOCK, 2 * SW_BLOCK), 1)
    first = jnp.logical_and(step == 0, col < SW_BLOCK)
    for j in range(nblk):
        rows = slice(j * SW_BLOCK, (j + 1) * SW_BLOCK)
        q = q_ref[0, rows, :]
        kvc = kvc_ref[0, rows, :]
        kvp = kvp_ref[0] if j == 0 else kvc_ref[0, (j - 1) * SW_BLOCK:j * SW_BLOCK, :]
        for h in range(B_HPG):
            hs = slice(h * HEAD_DIM, (h + 1) * HEAD_DIM)
            vs = slice(w + h * HEAD_DIM, w + (h + 1) * HEAD_DIM)
            k2 = jnp.concatenate([kvp[:, hs], kvc[:, hs]], axis=0).astype(BF16)
            v2 = jnp.concatenate([kvp[:, vs], kvc[:, vs]], axis=0).astype(BF16)
            s = _nt(q[:, hs], k2) + bias_ref[h]
            if j == 0:
                s = jnp.where(first, NEG, s)
            m = jnp.max(s, axis=-1, keepdims=True)
            p = jnp.exp(s - m)
            l = jnp.sum(p, axis=-1, keepdims=True)
            o_ref[0, rows, hs] = _dot(p.astype(BF16), v2) / l
            lse_ref[0, rows, hs] = jnp.broadcast_to(m + jnp.log(l), (SW_BLOCK, HEAD_DIM))


def _win_prompt_bias(tab_g, dil, reach):
    ql = np.arange(SW_BLOCK)[:, None]
    kl = np.arange(2 * SW_BLOCK)[None, :] - SW_BLOCK
    rel = ql - kl
    ok = (rel >= 0) & (rel <= reach)
    return bias_lookup(tab_g, np.where(ok, _rel_bucket_np(rel * dil), -1))


def win_prompt(qv, kvv, g, tab_g):
    w = B_HPG * HEAD_DIM
    dil = DILATIONS[g]
    b, n, _ = qv.shape
    s = n * dil
    nb = n // SW_BLOCK
    assert nb * SW_BLOCK * dil == s
    bias = _win_prompt_bias(tab_g, dil, WINDOWS[g] // dil)
    nblk = math.gcd(WIN_BLOCKS, nb)
    rows = nblk * SW_BLOCK
    o, lse = pl.pallas_call(
        functools.partial(_win_prompt_kernel, w=w, nblk=nblk),
        grid=(b, dil, nb // nblk),
        in_specs=[pl.BlockSpec((1, rows, w), lambda n_, r, k: (n_, k, r)),
                  pl.BlockSpec((1, SW_BLOCK, 2 * w), lambda n_, r, k: (n_, jnp.maximum(k * nblk - 1, 0), r)),
                  pl.BlockSpec((1, rows, 2 * w), lambda n_, r, k: (n_, k, r)),
                  pl.BlockSpec((B_HPG, SW_BLOCK, 2 * SW_BLOCK), lambda n_, r, k: (0, 0, 0))],
        out_specs=[pl.BlockSpec((1, rows, w), lambda n_, r, k: (n_, k, r)),
                   pl.BlockSpec((1, rows, w), lambda n_, r, k: (n_, k, r))],
        out_shape=[jax.ShapeDtypeStruct((b, n, dil * w), F32),
                   jax.ShapeDtypeStruct((b, n, dil * w), F32)],
        compiler_params=_cparams("parallel", "parallel", "arbitrary"),
        name=f"win_prompt_g{g}",
    )(qv, kvv, kvv, bias)
    return o.reshape(b, s, w), lse.reshape(b, s, w)


ROWS8 = 8


def _win_step_kernel(q_ref, kvn_ref, newt_ref, buf_ref, tb_ref, tn_ref, o_ref, lse_ref, st_ref,
                     qbd_ref, newpage_ref, *, w, wb, l):
    n = pl.program_id(0)

    @pl.when(n == 0)
    def _():
        newpage_ref[...] = jnp.zeros(newpage_ref.shape, F32)

    q = q_ref[0]
    lane_h = lax.broadcasted_iota(I32, (ROWS8, w), 1) // HEAD_DIM
    for h in range(B_HPG):
        qbd_ref[h * ROWS8:(h + 1) * ROWS8, :] = jnp.where(lane_h == h, q, jnp.zeros_like(q))
    newpage_ref[0:l, :] = kvn_ref[0]
    qbd = qbd_ref[...]
    newp = newpage_ref[...]
    buf = buf_ref[0, 0]
    kt = buf[0].reshape(w, wb).astype(BF16)
    vt = buf[1].reshape(w, wb).astype(BF16)
    s1 = _dot(qbd, kt) + tb_ref[...]
    s2 = _nt(qbd, newp[:, 0:w].astype(BF16)) + tn_ref[...]
    m = jnp.maximum(jnp.max(s1, axis=-1, keepdims=True), jnp.max(s2, axis=-1, keepdims=True))
    p1 = jnp.exp(s1 - m)
    p2 = jnp.exp(s2 - m)
    den = jnp.sum(p1, axis=-1, keepdims=True) + jnp.sum(p2, axis=-1, keepdims=True)
    num = _nt(p1.astype(BF16), vt) + _dot(p2.astype(BF16), newp[:, w:2 * w].astype(BF16))
    o = num / den
    lse = m + jnp.log(den)
    for h in range(B_HPG):
        rs = slice(h * ROWS8, (h + 1) * ROWS8)
        hs = slice(h * HEAD_DIM, (h + 1) * HEAD_DIM)
        o_ref[0, :, hs] = o[rs, hs]
        lse_ref[0, :, hs] = jnp.broadcast_to(lse[rs], (ROWS8, HEAD_DIM))
    rolled = pltpu.roll(buf.reshape(2 * w, wb), wb - l, 1)
    lane = lax.broadcasted_iota(I32, (2 * w, LANES), 1)
    last = jnp.where(lane >= LANES - l, newt_ref[0], rolled[:, wb - LANES:wb])
    if wb > LANES:
        st_ref[0, :, :, :, 0:wb - LANES] = rolled[:, 0:wb - LANES].reshape(2, B_HPG, HEAD_DIM, wb - LANES)
    st_ref[0, :, :, :, wb - LANES:wb] = last.reshape(2, B_HPG, HEAD_DIM, LANES)


def _win_step_bias(tab_g, dil, window, wb, l):
    lq = np.minimum(np.arange(ROWS8), l - 1)[:, None]
    dist_b = wb + lq - np.arange(wb)[None, :]
    ok_b = (dist_b % dil == 0) & (dist_b // dil <= window // dil)
    cols = np.arange(LANES)[None, :]
    dist_n = lq - cols
    ok_n = (cols < l) & (dist_n >= 0) & (dist_n % dil == 0) & (dist_n // dil <= window // dil)
    rows = B_HPG * ROWS8
    tb = bias_lookup(tab_g, np.where(ok_b, _rel_bucket_np(dist_b), -1)).reshape(rows, wb)
    tn = bias_lookup(tab_g, np.where(ok_n, _rel_bucket_np(dist_n), -1)).reshape(rows, LANES)
    return tb, tn


def win_step(q8, kv_new, cache_t, layer, g, tab_g):
    nb = q8.shape[0]
    l = kv_new.shape[1]
    w = B_HPG * HEAD_DIM
    wb = cache_t.shape[-1]
    window, dil = WINDOWS[g], DILATIONS[g]
    assert wb == window, "the step kernel keeps a full window of rows"
    tb, tn = _win_step_bias(tab_g, dil, window, wb, l)
    rows = B_HPG * ROWS8
    new_t = jnp.pad(kv_new.transpose(0, 2, 1), ((0, 0), (0, 0), (LANES - l, 0)))
    o, lse, st = pl.pallas_call(
        functools.partial(_win_step_kernel, w=w, wb=wb, l=l),
        grid=(nb,),
        in_specs=[pl.BlockSpec((1, ROWS8, w), lambda n: (n, 0, 0)),
                  pl.BlockSpec((1, l, 2 * w), lambda n: (n, 0, 0)),
                  pl.BlockSpec((1, 2 * w, LANES), lambda n: (n, 0, 0)),
                  pl.BlockSpec((1, 1, 2, B_HPG, HEAD_DIM, wb), lambda n: (layer, n, 0, 0, 0, 0)),
                  pl.BlockSpec((rows, wb), lambda n: (0, 0)),
                  pl.BlockSpec((rows, LANES), lambda n: (0, 0))],
        out_specs=[pl.BlockSpec((1, ROWS8, w), lambda n: (n, 0, 0)),
                   pl.BlockSpec((1, ROWS8, w), lambda n: (n, 0, 0)),
                   pl.BlockSpec((1, 2, B_HPG, HEAD_DIM, wb), lambda n: (n, 0, 0, 0, 0))],
        out_shape=[jax.ShapeDtypeStruct((nb, ROWS8, w), F32),
                   jax.ShapeDtypeStruct((nb, ROWS8, w), F32),
                   jax.ShapeDtypeStruct((nb, 2, B_HPG, HEAD_DIM, wb), F32)],
        scratch_shapes=[pltpu.VMEM((rows, w), BF16), pltpu.VMEM((LANES, 2 * w), F32)],
        compiler_params=_cparams("arbitrary"),
        name=f"win_step_g{g}",
    )(q8, kv_new, new_t, cache_t, tb, tn)
    return o[:, :l], lse[:, :l], st


def _ab_merge_kernel(o0, o1, o2, l0, l1, l2, out_ref):
    a, b, c = l0[...], l1[...], l2[...]
    m = jnp.maximum(jnp.maximum(a, b), c)
    ea, eb, ec = jnp.exp(a - m), jnp.exp(b - m), jnp.exp(c - m)
    out_ref[...] = ((ea * o0[...] + eb * o1[...] + ec * o2[...]) / (ea + eb + ec)).astype(BF16)


def ab_merge(outs, lses):
    t, w = outs[0].shape
    tm = min(t, 1024)
    spec = pl.BlockSpec((tm, w), lambda i: (i, 0))
    return pl.pallas_call(
        _ab_merge_kernel,
        grid=(t // tm,),
        in_specs=[spec] * 6,
        out_specs=spec,
        out_shape=jax.ShapeDtypeStruct((t, w), BF16),
        compiler_params=_cparams("parallel"),
        name="ab_merge",
    )(*outs, *lses)


CD_COLS = dict(cq=0, ck=256, cv=512, cgate=1024, dq=1536, dk=2048, dv=2560, iq=3072, ik=3328, clr=3392, iw=3408)
CD_PAD = 3456
SMALL_BLOCK = CD_COLS["ik"] // LANES
SM_IK, SM_CLR, SM_IW = 0, CD_COLS["clr"] - CD_COLS["ik"], CD_COLS["iw"] - CD_COLS["ik"]


def cd_reorder_w(w_in):
    sizes = (256, 256, 512, 16, 512, 512, 512, 512, 256, 64, 4)
    names = ("cq", "ck", "cv", "clr", "cgate", "dq", "dk", "dv", "iq", "ik", "iw")
    starts = np.concatenate([[0], np.cumsum(sizes)[:-1]])
    out = jnp.zeros((w_in.shape[0], CD_PAD), w_in.dtype)
    for nm, st, sz in zip(names, starts, sizes):
        out = lax.dynamic_update_slice(out, w_in[:, st:st + sz], (0, CD_COLS[nm]))
    return out


def _cd_pre_kernel(d_ref, sm_ref, wa_ref, ba_ref, qg_ref, kg_ref, seg_ref, g_ref, dq_ref, kv_ref, *t_refs, w):
    d = d_ref[...]
    sm = sm_ref[...]
    clr = sm[:, SM_CLR:SM_CLR + GLA_GATE_RANK]
    wa = wa_ref[...]
    pre = ba_ref[...]
    for part in _split3(clr):
        pre = pre + _dot(part, wa[0]) + _dot(part, wa[1])
    lsig = jnp.minimum(pre, 0.0) - jnp.log(1.0 + jnp.exp(-jnp.abs(pre)))
    g_ref[...] = lsig * (1.0 / GLA_TAU)
    q_scale = HEAD_DIM ** -0.5 * (LOG2E if t_refs else 1.0)
    dq_ref[...] = (_head_norm(d[:, 0:w], qg_ref[...], seg_ref) * q_scale).astype(BF16)
    kn = _head_norm(d[:, w:2 * w], kg_ref[...], seg_ref)
    v = d[:, 2 * w:3 * w]
    kv_ref[:, 0:w] = kn
    kv_ref[:, w:2 * w] = v
    if t_refs:
        kb_ref, vt_ref, smt_ref = t_refs
        kb_ref[...] = kn.astype(BF16)
        vt_ref[0] = v.T.astype(BF16)
        smt_ref[0] = sm.T


def cd_pre(z, wa2, ba, qn, kn, seq=None):
    t = z.shape[0]
    w = DSA_HEADS * HEAD_DIM
    gw = GLA_HEADS * GLA_DK
    tm = min(t, 512)
    wa_hi, wa_lo = _split2(wa2)
    out_specs = [pl.BlockSpec((tm, gw), lambda i: (i, 0)),
                 pl.BlockSpec((tm, w), lambda i: (i, 0)),
                 pl.BlockSpec((tm, 2 * w), lambda i: (i, 0))]
    out_shape = [jax.ShapeDtypeStruct((t, gw), F32),
                 jax.ShapeDtypeStruct((t, w), BF16),
                 jax.ShapeDtypeStruct((t, 2 * w), F32)]
    if seq is not None:
        nt = seq // tm
        out_specs += [pl.BlockSpec((tm, w), lambda i: (i, 0)),
                      pl.BlockSpec((1, w, tm), lambda i: (i // nt, 0, i % nt)),
                      pl.BlockSpec((1, LANES, tm), lambda i: (i // nt, 0, i % nt))]
        out_shape += [jax.ShapeDtypeStruct((t, w), BF16),
                      jax.ShapeDtypeStruct((t // seq, w, seq), BF16),
                      jax.ShapeDtypeStruct((t // seq, LANES, seq), F32)]
    return pl.pallas_call(
        functools.partial(_cd_pre_kernel, w=w),
        grid=(t // tm,),
        in_specs=[pl.BlockSpec((tm, 3 * w), lambda i: (i, CD_COLS["dq"] // (3 * w))),
                  pl.BlockSpec((tm, LANES), lambda i: (i, SMALL_BLOCK)),
                  pl.BlockSpec((2, GLA_GATE_RANK, gw), lambda i: (0, 0, 0)),
                  pl.BlockSpec((1, gw), lambda i: (0, 0)),
                  pl.BlockSpec((1, w), lambda i: (0, 0)),
                  pl.BlockSpec((1, w), lambda i: (0, 0)),
                  pl.BlockSpec((w, w), lambda i: (0, 0))],
        out_specs=out_specs,
        out_shape=out_shape,
        compiler_params=_cparams("parallel"),
        name="cd_pre",
    )(z, z, jnp.stack([wa_hi, wa_lo]), ba.reshape(1, gw), jnp.tile(qn, DSA_HEADS).reshape(1, w),
      jnp.tile(kn, DSA_HEADS).reshape(1, w), _seg_matrix(w))


def _gla_consts(c):
    levels = []
    s = c // 2
    while s >= 1:
        levels.append(s)
        s //= 2
    i = np.arange(c)
    tri = (i[None, :] <= i[:, None]).astype(np.float32)
    mats = [tri]
    masks = []
    for s in levels:
        ref = (i // (2 * s)) * (2 * s) + s - 1
        r = (i[None, :] <= ref[:, None]).astype(np.float32)
        mats.append(tri - r)
        same = (i[:, None] // (2 * s)) == (i[None, :] // (2 * s))
        masks.append(same & ((i[:, None] % (2 * s)) >= s) & ((i[None, :] % (2 * s)) < s))
    masks.append(i[:, None] == i[None, :])
    mstack = np.concatenate(mats, axis=0)
    mask = np.stack([np.tile(m.astype(np.float32), (1, GLA_HEADS)) for m in masks])
    return jnp.asarray(mstack, BF16), jnp.asarray(mask, F32), len(levels)


def _gla_kernel(qk_ref, v_ref, gate_ref, g_ref, s0_ref, mstack_ref, mask_ref, gn_ref, eye_ref,
                o_ref, sf_ref, st_ref, *, c, nl):
    t = pl.program_id(1)
    kw = GLA_HEADS * GLA_DK
    vw = GLA_HEADS * GLA_DV
    lane_k = lax.broadcasted_iota(I32, (1, kw), 1) // GLA_DK
    lane_v = lax.broadcasted_iota(I32, (1, vw), 1) // GLA_DV

    @pl.when(t == 0)
    def _():
        rows = []
        for h in range(GLA_HEADS):
            z = jnp.zeros((GLA_DK, GLA_DV), F32)
            rows.append(jnp.concatenate([s0_ref[0, h] if hh == h else z for hh in range(GLA_HEADS)], axis=1))
        st_ref[...] = jnp.concatenate(rows, axis=0).T

    qk = qk_ref[0]
    q = qk[:, 0:kw] * GLA_DK ** -0.5
    k = qk[:, kw:2 * kw]
    v = v_ref[0]
    mstack = mstack_ref[...]
    r = None
    for part in _split3(g_ref[0]):
        d = _dot(mstack, part)
        r = d if r is None else r + d
    b = r[0:c]

    def expand_k(x):
        return jnp.concatenate([jnp.where(lane_k == h, x, 0.0) for h in range(GLA_HEADS)], axis=0).astype(BF16)

    st = st_ref[...]
    o = _nt((q * jnp.exp(b)).astype(BF16), st.astype(BF16))
    a = mask_ref[nl] * _nt(q.astype(BF16), expand_k(k))
    for lv in range(nl):
        sc = jnp.exp(-jnp.abs(r[(lv + 1) * c:(lv + 2) * c]))
        a = a + mask_ref[lv] * _nt((q * sc).astype(BF16), expand_k(k * sc))
    vexp = jnp.concatenate([jnp.where(lane_v == h, v, 0.0) for h in range(GLA_HEADS)], axis=0).astype(BF16)
    o = o + _dot(a.astype(BF16), vexp)

    blast = b[c - 1:c]
    kt = (k * jnp.exp(blast - b)).astype(BF16)
    vt = _nt(eye_ref[...], v.astype(BF16)).astype(BF16)
    upd = _dot(vt, kt)
    row_h = lax.broadcasted_iota(I32, (vw, 1), 0) // GLA_DV
    st_new = st * jnp.exp(blast) + jnp.where(row_h == lane_k, upd, 0.0)
    st_ref[...] = st_new

    gate = gate_ref[0]
    gn = gn_ref[...]
    for h in range(GLA_HEADS):
        hs = slice(h * GLA_DV, (h + 1) * GLA_DV)
        oh = o[:, hs]
        y = oh * lax.rsqrt(jnp.mean(oh * oh, axis=-1, keepdims=True) + EPS) * gn
        gh = gate[:, hs]
        o_ref[0, :, hs] = (y * (gh * jax.nn.sigmoid(gh))).astype(BF16)

    @pl.when(t == pl.num_programs(1) - 1)
    def _():
        s_t = st_new.T
        for h in range(GLA_HEADS):
            sf_ref[0, h] = s_t[h * GLA_DK:(h + 1) * GLA_DK, h * GLA_DV:(h + 1) * GLA_DV]


def gla(z3, g3, s0, gla_norm):
    nb, l, _ = z3.shape
    c = min(l, GLA_CHUNK)
    assert l % c == 0 and c & (c - 1) == 0
    kw = GLA_HEADS * GLA_DK
    vw = GLA_HEADS * GLA_DV
    mstack, mask, nl = _gla_consts(c)
    eye = jnp.eye(vw, dtype=BF16)
    o, sf = pl.pallas_call(
        functools.partial(_gla_kernel, c=c, nl=nl),
        grid=(nb, l // c),
        in_specs=[pl.BlockSpec((1, c, 2 * kw), lambda n, t: (n, t, 0)),
                  pl.BlockSpec((1, c, vw), lambda n, t: (n, t, CD_COLS["cv"] // vw)),
                  pl.BlockSpec((1, c, vw), lambda n, t: (n, t, CD_COLS["cgate"] // vw)),
                  pl.BlockSpec((1, c, kw), lambda n, t: (n, t, 0)),
                  pl.BlockSpec((1, GLA_HEADS, GLA_DK, GLA_DV), lambda n, t: (n, 0, 0, 0)),
                  pl.BlockSpec(mstack.shape, lambda n, t: (0, 0)),
                  pl.BlockSpec(mask.shape, lambda n, t: (0, 0, 0)),
                  pl.BlockSpec((1, GLA_DV), lambda n, t: (0, 0)),
                  pl.BlockSpec((vw, vw), lambda n, t: (0, 0))],
        out_specs=[pl.BlockSpec((1, c, vw), lambda n, t: (n, t, 0)),
                   pl.BlockSpec((1, GLA_HEADS, GLA_DK, GLA_DV), lambda n, t: (n, 0, 0, 0))],
        out_shape=[jax.ShapeDtypeStruct((nb, l, vw), BF16),
                   jax.ShapeDtypeStruct((nb, GLA_HEADS, GLA_DK, GLA_DV), F32)],
        scratch_shapes=[pltpu.VMEM((vw, kw), F32)],
        compiler_params=_cparams("parallel", "arbitrary"),
        name="gla",
    )(z3, z3, z3, g3, s0, mstack, mask, gla_norm.reshape(1, GLA_DV), eye)
    return o, sf


def _sort_key(x):
    bits = pltpu.bitcast(x, I32)
    return jnp.where(bits < 0, (bits ^ 0x7FFFFFFF) + 1, bits)


def _idx_lhs(iq, h):
    hi, lo = _split2(iq[:, h * IDX_DIM:(h + 1) * IDX_DIM])
    return jnp.concatenate([hi, hi, lo, jnp.zeros_like(hi)], axis=1)


def _idx_rhs(ik):
    hi, lo = _split2(ik)
    return jnp.concatenate([hi, lo, hi, jnp.zeros_like(hi)], axis=1)


def _idx_scores(lhs, rhs, wcol):
    sc = None
    for h in range(IDX_HEADS):
        d = jnp.maximum(_nt(lhs[h], rhs) * IDX_DIM ** -0.5, 0.0) * (wcol[h] * IDX_HEADS ** -0.5)
        sc = d if sc is None else sc + d
    return sc


def _dsa_select_kernel(iq_ref, smt_ref, smk_ref, tril_ref, mask_ref, kb3_ref, hi_ref, d1_ref, d0_ref, *, nc, topk):
    i = pl.program_id(1)
    ck = DSA_CK
    qb = DSA_SQ
    bpc = ck // qb
    sub = ck // 8

    @pl.when(i == 0)
    def _():
        for c in range(nc):
            kb3_ref[c] = _idx_rhs(smk_ref[0, c * ck:(c + 1) * ck, SM_IK:SM_IK + IDX_DIM])

    iq = iq_ref[0]
    smt = smt_ref[0]
    lhs = jnp.concatenate([_idx_lhs(iq, h) for h in range(IDX_HEADS)], axis=0)
    wrow = [smt[SM_IW + h:SM_IW + h + 1, :] * (IDX_HEADS ** -0.5 * IDX_DIM ** -0.5) for h in range(IDX_HEADS)]
    nch = (i + bpc) // bpc

    def chunk_planes(c, causal):
        dots = _nt(kb3_ref[c], lhs)
        sc = None
        for h in range(IDX_HEADS):
            t = jnp.maximum(dots[:, h * qb:(h + 1) * qb], 0.0) * wrow[h]
            sc = t if sc is None else sc + t
        sc = jnp.where(jnp.abs(sc) < F32_MIN_NORMAL, 0.0, sc)
        bits = pltpu.bitcast(sc, I32)
        key = bits ^ (lax.shift_right_arithmetic(bits, 31) & 0x7FFFFFFF)
        hi = pltpu.bitcast(bits & -65536, F32)
        if causal:
            drc = lax.broadcasted_iota(I32, (ck, qb), 0) - lax.broadcasted_iota(I32, (ck, qb), 1)
            hi = jnp.where(drc <= i * qb - c * ck, hi, -jnp.inf)
        hi_ref[c] = hi.astype(BF16)
        d1_ref[c] = (lax.shift_right_logical(key, 8) & 0xFF).astype(F32).astype(BF16)
        d0_ref[c] = (key & 0xFF).astype(F32).astype(BF16)

    def score_body(c, carry):
        chunk_planes(c, False)
        return carry

    lax.fori_loop(0, nch - 1, score_body, 0)
    chunk_planes(nch - 1, True)

    one, zero = jnp.ones((), BF16), jnp.zeros((), BF16)

    def count(ref, cand, strict):
        def body(c, acc):
            blk = ref[c]
            m = jnp.where(blk > cand if strict else blk >= cand, one, zero)
            for j in range(8):
                acc = acc + m[j * sub:(j + 1) * sub, :]
            return acc
        acc = lax.fori_loop(0, nch, body, jnp.zeros((sub, qb), BF16))
        return jnp.sum(acc.astype(F32), axis=0, keepdims=True)

    qpos = i * qb + lax.broadcasted_iota(I32, (1, qb), 1)
    kk = jnp.minimum(topk, qpos + 1).astype(F32)

    def hi_value(s16):
        p = jnp.where(s16 >= 0, s16, s16 ^ 0x7FFF) & 0xFFFF
        return pltpu.bitcast(lax.shift_left(p, 16), F32).astype(BF16)

    def descend(ref, nbits, start, as_cand, want, c_start):
        def step(t, carry):
            thr, cge = carry
            cand = thr + lax.shift_left(jnp.int32(1), nbits - 1 - t)
            cnt = count(ref, as_cand(cand), False)
            ok = cnt >= want
            return jnp.where(ok, cand, thr), jnp.where(ok, cnt, cge)
        return lax.fori_loop(0, nbits, step, (jnp.full((1, qb), start, I32), c_start))

    digit = lambda t: t.astype(F32).astype(BF16)

    h16, cge1 = descend(hi_ref, 16, -(2 ** 15), hi_value, kk, jnp.zeros((1, qb), F32))
    hv = hi_value(h16)
    cgt1 = count(hi_ref, hv, True)
    need1 = kk - cgt1

    def restrict(dst_ref, cls_ref, cls_val):
        def body(c, carry):
            dst_ref[c] = jnp.where(cls_ref[c] == cls_val, dst_ref[c], -one)
            return carry
        lax.fori_loop(0, nch, body, 0)

    restrict(d1_ref, hi_ref, hv)
    b1, cge2 = descend(d1_ref, 8, 0, digit, need1, cge1 - cgt1)
    b1v = digit(b1)
    cgt2 = count(d1_ref, b1v, True)
    need2 = need1 - cgt2
    restrict(d0_ref, d1_ref, b1v)
    b0, cge = descend(d0_ref, 8, 0, digit, need2, cge2 - cgt2)
    b0v = digit(b0)
    tied = jnp.max(jnp.where(cge > need2, 1, 0))

    @pl.when(tied > 0)
    def _():
        need3 = need2 - count(d0_ref, b0v, True)
        b0f = b0.astype(F32)

        def tie_body(c, off):
            blk = d0_ref[c].astype(F32)
            eq = blk == b0f
            pref = _dot(tril_ref[...], jnp.where(eq, 1.0, 0.0).astype(BF16))
            drop = jnp.logical_and(eq, pref + off > need3)
            d0_ref[c] = jnp.where(drop, -1.0, blk).astype(BF16)
            return off + pref[ck - 1:ck, :]

        lax.fori_loop(0, nch, tie_body, jnp.zeros((1, qb), F32))

    neg = jnp.full((), NEG, BF16)

    def live_body(c, carry):
        keep3 = jnp.where(d0_ref[c] >= b0v, zero, neg)
        keep2 = jnp.where(d1_ref[c] > b1v, zero, keep3)
        mask_ref[0, 0, c] = jnp.where(hi_ref[c] > hv, zero, keep2)
        return carry

    def dead_body(c, carry):
        mask_ref[0, 0, c] = jnp.full((ck, qb), NEG, BF16)
        return carry

    lax.fori_loop(0, nch, live_body, 0)
    lax.fori_loop(nch, nc, dead_body, 0)


def dsa_select(z3, smt):
    b, s, _ = z3.shape
    nq, nc = s // DSA_SQ, s // DSA_CK
    topk = min(DSA_TOPK_MAX, s // 4)
    tril = jnp.asarray(np.tril(np.ones((DSA_CK, DSA_CK), np.float32)), BF16)
    return pl.pallas_call(
        functools.partial(_dsa_select_kernel, nc=nc, topk=topk),
        grid=(b, nq),
        in_specs=[pl.BlockSpec((1, DSA_SQ, IDX_HEADS * IDX_DIM), lambda n, i: (n, i, CD_COLS["iq"] // (IDX_HEADS * IDX_DIM))),
                  pl.BlockSpec((1, LANES, DSA_SQ), lambda n, i: (n, 0, i)),
                  pl.BlockSpec((1, s, LANES), lambda n, i: (n, 0, SMALL_BLOCK)),
                  pl.BlockSpec((DSA_CK, DSA_CK), lambda n, i: (0, 0))],
        out_specs=pl.BlockSpec((1, 1, nc, DSA_CK, DSA_SQ), lambda n, i: (n, i, 0, 0, 0)),
        out_shape=jax.ShapeDtypeStruct((b, nq, nc, DSA_CK, DSA_SQ), BF16),
        scratch_shapes=[pltpu.VMEM((nc, DSA_CK, 4 * IDX_DIM), BF16)] + [pltpu.VMEM((nc, DSA_CK, DSA_SQ), BF16)] * 3,
        compiler_params=_cparams("parallel", "arbitrary"),
        name="dsa_select",
    )(z3, smt, z3, tril)


def _dsa_bias_tiles(tab_d):
    o = 0
    while _rel_bucket_np(max(o * DSA_BT - (DSA_BT - 1), 0)) < REL_BUCKETS - 1:
        o += 1
    offs = np.arange(o + 1)[:, None, None] * DSA_BT
    d = offs + np.arange(DSA_BT)[None, None, :] - np.arange(DSA_BT)[None, :, None]
    tiles = bias_lookup(tab_d, _rel_bucket_np(d).reshape((o + 1) * DSA_BT, DSA_BT))
    return tiles.reshape(DSA_HEADS, o + 1, DSA_BT, DSA_BT)


def _dsa_attn_kernel(qi_ref, kc_ref, q_ref, k_ref, vt_ref, mask_ref, bt_ref, o_ref, m_ref, l_ref, acc_ref, s_ref,
                     *, n_off):
    s_id = pl.program_id(1)
    i = qi_ref[s_id]
    c = kc_ref[s_id]
    qb, ck = DSA_QB, DSA_CK
    tk, tq = ck // DSA_BT, qb // DSA_BT

    @pl.when(c == 0)
    def _():
        m_ref[...] = jnp.full(m_ref.shape, NEG, F32)
        l_ref[...] = jnp.zeros(l_ref.shape, F32)
        acc_ref[...] = jnp.zeros(acc_ref.shape, F32)

    madd = mask_ref[0, 0, 0].astype(F32)
    q = q_ref[0]
    k = k_ref[0]
    vt = vt_ref[0]
    offs = [[jnp.clip((i * tq + u) - (c * tk + t), 0, n_off - 1) for u in range(tq)] for t in range(tk)]
    m_all = m_ref[...]
    l_all = l_ref[...]
    m_rows, l_rows = [], []
    for h in range(DSA_HEADS):
        hs = slice(h * HEAD_DIM, (h + 1) * HEAD_DIM)
        bias = jnp.concatenate([jnp.concatenate([bt_ref[h, offs[t][u]] for u in range(tq)], axis=1)
                                for t in range(tk)], axis=0)
        s = _nt(k[:, hs], q[:, hs]) + bias + madd
        s_ref[h] = s
        m_rows.append(jnp.maximum(m_all[h:h + 1, :], jnp.max(s, axis=0, keepdims=True)))
    ones = jnp.ones((16, ck), BF16)
    for h in range(DSA_HEADS):
        hs = slice(h * HEAD_DIM, (h + 1) * HEAD_DIM)
        alpha = jnp.exp2(m_all[h:h + 1, :] - m_rows[h])
        p = jnp.exp2(s_ref[h] - m_rows[h]).astype(BF16)
        pv = _dot(jnp.concatenate([vt[hs, :], ones], axis=0), p)
        l_rows.append(alpha * l_all[h:h + 1, :] + pv[HEAD_DIM:HEAD_DIM + 1, :])
        acc_ref[hs, :] = alpha * acc_ref[hs, :] + pv[0:HEAD_DIM, :]
    m_ref[...] = jnp.concatenate(m_rows, axis=0)
    l_ref[...] = jnp.concatenate(l_rows, axis=0)

    @pl.when(c == ((i + 1) * qb - 1) // ck)
    def _():
        inv = 1.0 / l_ref[...]
        ot = jnp.concatenate([acc_ref[h * HEAD_DIM:(h + 1) * HEAD_DIM, :] * inv[h:h + 1, :]
                              for h in range(DSA_HEADS)], axis=0)
        o_ref[0] = ot.T.astype(BF16)


def dsa_attend(dq, kb, vt, mask, tab_d):
    b, s, w = dq.shape
    nq = s // DSA_QB
    last_c = lambda i: ((i + 1) * DSA_QB - 1) // DSA_CK
    qps = DSA_SQ // DSA_QB
    bt = _dsa_bias_tiles(tab_d) * LOG2E
    n_off = bt.shape[1]
    qi = np.concatenate([np.full(last_c(i) + 1, i) for i in range(nq)]).astype(np.int32)
    kc = np.concatenate([np.arange(last_c(i) + 1) for i in range(nq)]).astype(np.int32)
    grid_spec = pltpu.PrefetchScalarGridSpec(
        num_scalar_prefetch=2,
        grid=(b, len(qi)),
        in_specs=[pl.BlockSpec((1, DSA_QB, w), lambda n, t, qi_, kc_: (n, qi_[t], 0)),
                  pl.BlockSpec((1, DSA_CK, w), lambda n, t, qi_, kc_: (n, kc_[t], 0)),
                  pl.BlockSpec((1, w, DSA_CK), lambda n, t, qi_, kc_: (n, 0, kc_[t])),
                  pl.BlockSpec((1, 1, 1, DSA_CK, DSA_QB),
                               lambda n, t, qi_, kc_: (n, qi_[t] // qps, kc_[t], 0, qi_[t] % qps)),
                  pl.BlockSpec(bt.shape, lambda n, t, qi_, kc_: (0, 0, 0, 0))],
        out_specs=pl.BlockSpec((1, DSA_QB, w), lambda n, t, qi_, kc_: (n, qi_[t], 0)),
        scratch_shapes=[pltpu.VMEM((DSA_HEADS, DSA_QB), F32), pltpu.VMEM((DSA_HEADS, DSA_QB), F32),
                        pltpu.VMEM((w, DSA_QB), F32), pltpu.VMEM((DSA_HEADS, DSA_CK, DSA_QB), F32)],
    )
    return pl.pallas_call(
        functools.partial(_dsa_attn_kernel, n_off=n_off),
        grid_spec=grid_spec,
        out_shape=jax.ShapeDtypeStruct((b, s, w), BF16),
        compiler_params=_cparams("parallel", "arbitrary"),
        name="dsa_attend",
    )(jnp.asarray(qi), jnp.asarray(kc), dq, kb, vt, mask, bt)


SCORE_PAGES = 16
ATTN_PAGES = 16


def _dsa_step_scores_kernel(pt_ref, iq_ref, sm_ref, *rest, npg):
    k_refs, o_ref = rest[:npg], rest[npg]
    iq = iq_ref[0]
    sm = sm_ref[0]
    lhs = jnp.concatenate([_idx_lhs(iq, h) for h in range(IDX_HEADS)], axis=0)
    wcol = [sm[:, SM_IW + h:SM_IW + h + 1] * IDX_HEADS ** -0.5 for h in range(IDX_HEADS)]
    for j in range(npg):
        hi, lo = _split2(k_refs[j][0, 0])
        d = _dot(lhs, jnp.concatenate([hi, lo, hi, jnp.zeros_like(hi)], axis=0))
        sc = None
        for h in range(IDX_HEADS):
            t = jnp.maximum(d[h * ROWS8:(h + 1) * ROWS8] * IDX_DIM ** -0.5, 0.0) * wcol[h]
            sc = t if sc is None else sc + t
        o_ref[0, :, j * PAGE_SIZE:(j + 1) * PAGE_SIZE] = sc


def dsa_step_scores(z8, kidx_t, layer, page_table):
    nb = z8.shape[0]
    n_pages = page_table.shape[1]
    npg = math.gcd(SCORE_PAGES, n_pages)
    iqw = IDX_HEADS * IDX_DIM

    def page_spec(j):
        return pl.BlockSpec((1, 1, IDX_DIM, PAGE_SIZE),
                            lambda n, p, pt: (layer, pt[n * n_pages + p * npg + j], 0, 0))

    grid_spec = pltpu.PrefetchScalarGridSpec(
        num_scalar_prefetch=1,
        grid=(nb, n_pages // npg),
        in_specs=[pl.BlockSpec((1, ROWS8, iqw), lambda n, p, pt: (n, 0, CD_COLS["iq"] // iqw)),
                  pl.BlockSpec((1, ROWS8, LANES), lambda n, p, pt: (n, 0, SMALL_BLOCK))]
                 + [page_spec(j) for j in range(npg)],
        out_specs=pl.BlockSpec((1, ROWS8, npg * PAGE_SIZE), lambda n, p, pt: (n, 0, p)),
    )
    return pl.pallas_call(
        functools.partial(_dsa_step_scores_kernel, npg=npg),
        grid_spec=grid_spec,
        out_shape=jax.ShapeDtypeStruct((nb, ROWS8, n_pages * PAGE_SIZE), F32),
        compiler_params=_cparams("parallel", "arbitrary"),
        name="dsa_step_scores",
    )(page_table.reshape(-1), z8, z8, *([kidx_t] * npg))


STEP_SELECT_SEQS = 4


def _dsa_step_select_kernel(sc_ref, iq_ref, sm_ref, triu_ref, mp_ref, mn_ref, key_ref, *, past, l_new, topk, g):
    ck = DSA_CK
    nck = past // ck
    rows = g * ROWS8
    new_keys = []
    for j in range(g):
        iq = iq_ref[j]
        sm = sm_ref[j]
        lhs = [_idx_lhs(iq, h) for h in range(IDX_HEADS)]
        wcol = [sm[:, SM_IW + h:SM_IW + h + 1] for h in range(IDX_HEADS)]
        rhs_new = _idx_rhs(jnp.concatenate([sm[:, SM_IK:SM_IK + IDX_DIM],
                                            jnp.zeros((LANES - ROWS8, IDX_DIM), F32)], axis=0))
        new_keys.append(_sort_key(_idx_scores(lhs, rhs_new, wcol)))
    rloc = lax.broadcasted_iota(I32, (rows, LANES), 0) % ROWS8
    col = lax.broadcasted_iota(I32, (rows, LANES), 1)
    key_new = jnp.where(jnp.logical_and(col <= rloc, col < l_new), jnp.concatenate(new_keys, axis=0), INT_MIN)
    key_ref[...] = _sort_key(sc_ref[...].reshape(rows, past))
    kk = jnp.minimum(topk, past + 1 + lax.broadcasted_iota(I32, (rows, 1), 0) % ROWS8)

    def count(pred):
        return (jnp.sum(jnp.where(pred(key_ref[...]), 1, 0), axis=1, keepdims=True)
                + jnp.sum(jnp.where(pred(key_new), 1, 0), axis=1, keepdims=True))

    def bit_body(t, carry):
        thr, cge = carry
        cand = thr + lax.shift_left(jnp.int32(1), 31 - t)
        cnt = count(lambda x: x >= cand)
        ok = cnt >= kk
        return jnp.where(ok, cand, thr), jnp.where(ok, cnt, cge)

    thr, cge = lax.fori_loop(0, 32, bit_body, (jnp.full((rows, 1), INT_MIN, I32), jnp.zeros((rows, 1), I32)))
    cgt = count(lambda x: x > thr)
    need = kk - cgt
    needf = need.astype(F32)
    tied = jnp.max(jnp.where(need < cge - cgt, 1, 0))
    mn_ref[...] = jnp.where(key_new >= thr, 0.0, NEG).reshape(g, ROWS8, LANES)

    @pl.when(tied == 0)
    def _():
        mp_ref[...] = jnp.where(key_ref[...] >= thr, 0.0, NEG).reshape(g, ROWS8, past)

    @pl.when(tied > 0)
    def _():
        off = jnp.zeros((rows, 1), F32)
        triu = triu_ref[...]
        for c in range(nck):
            blk = key_ref[:, c * ck:(c + 1) * ck]
            eq = blk == thr
            pref = _dot(jnp.where(eq, 1.0, 0.0).astype(BF16), triu)
            keep = jnp.logical_or(blk > thr, jnp.logical_and(eq, pref + off <= needf))
            mp_ref[:, :, c * ck:(c + 1) * ck] = jnp.where(keep, 0.0, NEG).reshape(g, ROWS8, ck)
            off = off + pref[:, ck - 1:ck]
        eq = key_new == thr
        pref = _dot(jnp.where(eq, 1.0, 0.0).astype(BF16), triu[0:LANES, 0:LANES])
        keep = jnp.logical_or(key_new > thr, jnp.logical_and(eq, pref + off <= needf))
        mn_ref[...] = jnp.where(keep, 0.0, NEG).reshape(g, ROWS8, LANES)


def dsa_step_select(scores, z8, l_new):
    nb, _, past = scores.shape
    topk = min(DSA_TOPK_MAX, (past + l_new) // 4)
    iqw = IDX_HEADS * IDX_DIM
    g = math.gcd(STEP_SELECT_SEQS, nb)
    assert past % DSA_CK == 0
    triu = jnp.asarray(np.triu(np.ones((DSA_CK, DSA_CK), np.float32)), BF16)
    return pl.pallas_call(
        functools.partial(_dsa_step_select_kernel, past=past, l_new=l_new, topk=topk, g=g),
        grid=(nb // g,),
        in_specs=[pl.BlockSpec((g, ROWS8, past), lambda n: (n, 0, 0)),
                  pl.BlockSpec((g, ROWS8, iqw), lambda n: (n, 0, CD_COLS["iq"] // iqw)),
                  pl.BlockSpec((g, ROWS8, LANES), lambda n: (n, 0, SMALL_BLOCK)),
                  pl.BlockSpec((DSA_CK, DSA_CK), lambda n: (0, 0))],
        out_specs=[pl.BlockSpec((g, ROWS8, past), lambda n: (n, 0, 0)),
                   pl.BlockSpec((g, ROWS8, LANES), lambda n: (n, 0, 0))],
        out_shape=[jax.ShapeDtypeStruct((nb, ROWS8, past), F32),
                   jax.ShapeDtypeStruct((nb, ROWS8, LANES), F32)],
        scratch_shapes=[pltpu.VMEM((g * ROWS8, past), I32)],
        compiler_params=_cparams("parallel"),
        name="dsa_step_select",
    )(scores, z8, z8, triu)


def _dsa_step_attn_kernel(pt_ref, q_ref, kvn_ref, mp_ref, mn_ref, bp_ref, bn_ref, *rest, w, npg):
    kv_refs, o_ref = rest[:npg], rest[npg]
    qbd_ref, newpage_ref, m_ref, l_ref, acc_ref = rest[npg + 1:]
    n = pl.program_id(0)
    p = pl.program_id(1)
    rows = DSA_HEADS * ROWS8
    lane_h = lax.broadcasted_iota(I32, (ROWS8, w), 1) // HEAD_DIM

    @pl.when(jnp.logical_and(n == 0, p == 0))
    def _():
        newpage_ref[...] = jnp.zeros(newpage_ref.shape, F32)

    @pl.when(p == 0)
    def _():
        q = q_ref[0]
        for h in range(DSA_HEADS):
            qbd_ref[h * ROWS8:(h + 1) * ROWS8, :] = jnp.where(lane_h == h, q, jnp.zeros_like(q))
        m_ref[...] = jnp.full(m_ref.shape, NEG, F32)
        l_ref[...] = jnp.zeros(l_ref.shape, F32)
        acc_ref[...] = jnp.zeros(acc_ref.shape, F32)

    def accumulate(scores, madd8, bias, pv):
        s = scores + bias + jnp.concatenate([madd8] * DSA_HEADS, axis=0)
        m_old = m_ref[:, 0:1]
        m_new = jnp.maximum(m_old, jnp.max(s, axis=-1, keepdims=True))
        alpha = jnp.exp(m_old - m_new)
        pr = jnp.exp(s - m_new)
        l_ref[...] = jnp.broadcast_to(alpha * l_ref[:, 0:1] + jnp.sum(pr, axis=-1, keepdims=True), (rows, LANES))
        m_ref[...] = jnp.broadcast_to(m_new, (rows, LANES))
        acc_ref[...] = alpha * acc_ref[...] + pv(pr.astype(BF16))

    qbd = qbd_ref[...]
    scores = jnp.concatenate([_dot(qbd, kv_refs[j][0, 0, 0].reshape(w, PAGE_SIZE).astype(BF16))
                              for j in range(npg)], axis=1)

    def pv_pages(pr):
        out = None
        for j in range(npg):
            t = _nt(pr[:, j * PAGE_SIZE:(j + 1) * PAGE_SIZE], kv_refs[j][0, 0, 1].reshape(w, PAGE_SIZE).astype(BF16))
            out = t if out is None else out + t
        return out

    accumulate(scores, mp_ref[0], bp_ref[...], pv_pages)

    @pl.when(p == pl.num_programs(1) - 1)
    def _():
        newpage_ref[0:ROWS8, :] = kvn_ref[0]
        newp = newpage_ref[...]
        accumulate(_nt(qbd, newp[:, 0:w].astype(BF16)), mn_ref[0], bn_ref[...],
                   lambda pr: _dot(pr, newp[:, w:2 * w].astype(BF16)))
        for h in range(DSA_HEADS):
            rs = slice(h * ROWS8, (h + 1) * ROWS8)
            hs = slice(h * HEAD_DIM, (h + 1) * HEAD_DIM)
            o_ref[0, :, hs] = (acc_ref[rs, hs] / l_ref[rs, 0:1]).astype(BF16)


def _dsa_step_bias(tab_d, past, l_new):
    lq = np.minimum(np.arange(ROWS8), l_new - 1)[:, None]
    d_past = past + lq - np.arange(past)[None, :]
    d_new = lq - np.arange(LANES)[None, :]

    def table(d):
        return bias_lookup(tab_d, _rel_bucket_np(d)).reshape(DSA_HEADS * ROWS8, d.shape[1])

    return table(d_past), table(d_new)


def dsa_step_attend(dq8, kv_new8, kv_t, layer, page_table, mask_past, mask_new, tab_d, l_new):
    nb, _, w = dq8.shape
    n_pages = page_table.shape[1]
    npg = math.gcd(ATTN_PAGES, n_pages)
    past = n_pages * PAGE_SIZE
    rows = DSA_HEADS * ROWS8
    bp, bn = _dsa_step_bias(tab_d, past, l_new)

    def page_spec(j):
        return pl.BlockSpec((1, 1, 2, DSA_HEADS, HEAD_DIM, PAGE_SIZE),
                            lambda n, p, pt: (layer, pt[n * n_pages + p * npg + j], 0, 0, 0, 0))

    grid_spec = pltpu.PrefetchScalarGridSpec(
        num_scalar_prefetch=1,
        grid=(nb, n_pages // npg),
        in_specs=[pl.BlockSpec((1, ROWS8, w), lambda n, p, pt: (n, 0, 0)),
                  pl.BlockSpec((1, ROWS8, 2 * w), lambda n, p, pt: (n, 0, 0)),
                  pl.BlockSpec((1, ROWS8, npg * PAGE_SIZE), lambda n, p, pt: (n, 0, p)),
                  pl.BlockSpec((1, ROWS8, LANES), lambda n, p, pt: (n, 0, 0)),
                  pl.BlockSpec((rows, npg * PAGE_SIZE), lambda n, p, pt: (0, p)),
                  pl.BlockSpec((rows, LANES), lambda n, p, pt: (0, 0))]
                 + [page_spec(j) for j in range(npg)],
        out_specs=pl.BlockSpec((1, ROWS8, w), lambda n, p, pt: (n, 0, 0)),
        scratch_shapes=[pltpu.VMEM((rows, w), BF16), pltpu.VMEM((PAGE_SIZE, 2 * w), F32),
                        pltpu.VMEM((rows, LANES), F32), pltpu.VMEM((rows, LANES), F32), pltpu.VMEM((rows, w), F32)],
    )
    return pl.pallas_call(
        functools.partial(_dsa_step_attn_kernel, w=w, npg=npg),
        grid_spec=grid_spec,
        out_shape=jax.ShapeDtypeStruct((nb, ROWS8, w), BF16),
        compiler_params=_cparams("arbitrary", "arbitrary"),
        name="dsa_step_attend",
    )(page_table.reshape(-1), dq8, kv_new8, mask_past, mask_new, bp, bn, *([kv_t] * npg))


def _pad_rows(x3, rows):
    return jnp.pad(x3, ((0, 0), (0, rows - x3.shape[1]), (0, 0)))


def _trunk(x, is_step, conv_state, win_states, gla_state, dsa_kv, dsa_kidx, page_table, wts):
    (norm_mix, norm_ffn, w_in_ab, conv_w, conv_b, conv_ln_g, conv_ln_b, qn_ab, kn_ab, w_out_ab,
     w_in_cd, gla_wa2, gla_ba, gla_norm, qn_cd, kn_cd, w_out_cd, rel_bias, w_g, w_u, w_d) = wts
    nb, l, d = x.shape
    t = nb * l
    depth = norm_mix.shape[0]
    c = conv_w.shape[2]
    wq = B_HPG * HEAD_DIM
    x2 = x.reshape(t, d)
    conv_new, gla_new, kv_new, kidx_new = [], [], [], []
    win_new = [[] for _ in WINDOWS]
    for layer in range(depth):
        i = layer // 2
        if layer % 2 == 0:
            z = norm_matmul(x2, norm_mix[layer], w_in_ab[i])
            hist = conv_state[i] if is_step else jnp.zeros((nb, CONV_WIDTH - 1, c), F32)
            a_out, c_st = conv_module(z.reshape(nb, l, -1), hist, conv_w[i], conv_b[i], conv_ln_g[i], conv_ln_b[i])
            conv_new.append(c_st)
            if is_step:
                qs, kvs = ab_qkv(z, qn_ab[i], kn_ab[i], 2 * c, t, (1,) * len(WINDOWS))
            else:
                qs, kvs = ab_qkv(z, qn_ab[i], kn_ab[i], 2 * c, l, DILATIONS)
            outs, lses = [], []
            for g, window in enumerate(WINDOWS):
                tab_g = rel_bias[:, g * B_HPG:(g + 1) * B_HPG]
                if is_step:
                    o, lse, st = win_step(_pad_rows(qs[g].reshape(nb, l, wq), ROWS8),
                                          kvs[g].reshape(nb, l, 2 * wq), win_states[g], i, g, tab_g)
                else:
                    o, lse = win_prompt(qs[g], kvs[g], g, tab_g)
                    keep = min(window, l)
                    st = kvs[g][:, -(keep // DILATIONS[g]):].reshape(nb, keep, 2, B_HPG, HEAD_DIM)
                outs.append(o.reshape(t, wq))
                lses.append(lse.reshape(t, wq))
                win_new[g].append(st)
            m1, m2 = a_out.reshape(t, c), ab_merge(outs, lses)
            wo = w_out_ab[i]
        else:
            z = norm_matmul(x2, norm_mix[layer], w_in_cd[i])
            pre = cd_pre(z, gla_wa2[i], gla_ba[i], qn_cd[i], kn_cd[i], None if is_step else l)
            gdec, dq, kv = pre[:3]
            z3 = z.reshape(nb, l, -1)
            wd = DSA_HEADS * HEAD_DIM
            ik = z3[:, :, CD_COLS["ik"]:CD_COLS["ik"] + IDX_DIM]
            tab_d = rel_bias[:, B_HEADS:]
            if is_step:
                lp = GLA_STEP_ROWS
                o_c, s_c = gla(_pad_rows(z3, lp), _pad_rows(gdec.reshape(nb, l, -1), lp), gla_state[i], gla_norm[i])
                o_c = o_c[:, :l]
                z8 = _pad_rows(z3, ROWS8)
                scores = dsa_step_scores(z8, dsa_kidx, i, page_table)
                mask_p, mask_n = dsa_step_select(scores, z8, l)
                o_d = dsa_step_attend(_pad_rows(dq.reshape(nb, l, wd), ROWS8), _pad_rows(kv.reshape(nb, l, 2 * wd), ROWS8),
                                      dsa_kv, i, page_table, mask_p, mask_n, tab_d, l)[:, :l]
                kv_st = kv.reshape(nb, l, 2, DSA_HEADS, HEAD_DIM)
                ki_st = ik
            else:
                s0 = jnp.zeros((nb, GLA_HEADS, GLA_DK, GLA_DV), F32)
                o_c, s_c = gla(z3, gdec.reshape(nb, l, -1), s0, gla_norm[i])
                kb, vt, smt = pre[3:]
                mask = dsa_select(z3, smt)
                o_d = dsa_attend(dq.reshape(nb, l, wd), kb.reshape(nb, l, wd), vt, mask, tab_d)
                n_pg = l // PAGE_SIZE
                kv_st = kv.reshape(nb, n_pg, PAGE_SIZE, 2, DSA_HEADS, HEAD_DIM)
                ki_st = ik.reshape(nb, n_pg, PAGE_SIZE, IDX_DIM)
            gla_new.append(s_c)
            kv_new.append(kv_st)
            kidx_new.append(ki_st)
            m1, m2 = o_c.reshape(t, -1), o_d.reshape(t, -1)
            wo = w_out_cd[i]
        d1 = m1.shape[1]
        x2 = mix_ffn(x2, m1, m2, wo[:d1], wo[d1:], norm_ffn[layer], w_g[layer], w_u[layer], w_d[layer])
    wins = [jnp.stack(ws) for ws in win_new]
    if is_step:
        wins = [ws.transpose(0, 1, 5, 2, 3, 4) for ws in wins]
    states = (jnp.stack(conv_new), wins[0], wins[1], wins[2],
              jnp.stack(gla_new), jnp.stack(kv_new), jnp.stack(kidx_new))
    return x2.reshape(nb, l, d), states


def kernel(x_prompt, x_sample, state_conv, cache_win128, cache_win512, cache_win2048, state_gla, cache_dsa_kv, cache_dsa_kidx, page_table, norm_mix, norm_ffn, w_in_ab, conv_w, conv_b, conv_ln_g, conv_ln_b, qn_ab, kn_ab, w_out_ab, w_in_cd, gla_wa2, gla_ba, gla_norm, qn_cd, kn_cd, w_out_cd, rel_bias, w_ffn_gate, w_ffn_up, w_ffn_down):
    bf = lambda a: a.astype(BF16)
    w_in_cd_r = jnp.stack([cd_reorder_w(w_in_cd[i]) for i in range(w_in_cd.shape[0])])
    wts = (norm_mix, norm_ffn, bf(w_in_ab), conv_w, conv_b, conv_ln_g, conv_ln_b, qn_ab, kn_ab, bf(w_out_ab),
           bf(w_in_cd_r), gla_wa2, gla_ba, gla_norm, qn_cd, kn_cd, bf(w_out_cd), rel_bias,
           bf(w_ffn_gate), bf(w_ffn_up), bf(w_ffn_down))
    y_p, sp = _trunk(x_prompt, False, None, None, None, None, None, None, wts)
    wins_t = tuple(cw.transpose(0, 1, 3, 4, 5, 2) for cw in (cache_win128, cache_win512, cache_win2048))
    y_s, ss = _trunk(x_sample, True, state_conv, wins_t, state_gla,
                     cache_dsa_kv.transpose(0, 1, 3, 4, 5, 2), cache_dsa_kidx.transpose(0, 1, 3, 2), page_table, wts)
    conv_p, win128_p, win512_p, win2048_p, gla_p, dsa_kv_p, dsa_kidx_p = sp
    conv_s, win128_s, win512_s, win2048_s, gla_s, dsa_kv_s, dsa_kidx_s = ss
    return (y_p, y_s, conv_p, conv_s, win128_p, win128_s, win512_p, win512_s, win2048_p, win2048_s,
            gla_p, gla_s, dsa_kv_p, dsa_kv_s, dsa_kidx_p, dsa_kidx_s)
```

```python
import functools
import math

import numpy as np
import jax
import jax.numpy as jnp
from jax import lax
from jax.experimental import pallas as pl
from jax.experimental.pallas import tpu as pltpu

F32 = jnp.float32
BF16 = jnp.bfloat16
I32 = jnp.int32

EPS = 1e-6
NEG = -1e30
LOG2E = math.log2(math.e)
F32_MIN_NORMAL = 2.0 ** -126
INT_MIN = -(2 ** 31)

V7X_VMEM_BYTES = 64 * 1024 * 1024
VMEM_LIMIT = V7X_VMEM_BYTES - 12 * 1024 * 1024
LANES = 128

HEAD_DIM = 64
CONV_WIDTH = 31
WINDOWS = (128, 512, 2048)
DILATIONS = (1, 4, 16)
B_HPG = 4
B_HEADS = B_HPG * len(WINDOWS)
SW_BLOCK = 128
GLA_HEADS = 4
GLA_DK = 64
GLA_DV = 128
GLA_GATE_RANK = 16
GLA_TAU = 16.0
DSA_HEADS = 8
IDX_HEADS = 4
IDX_DIM = 64
DSA_TOPK_MAX = 256
PAGE_SIZE = 128
REL_BUCKETS = 32
REL_MAX_DIST = 2048

GLA_CHUNK = 128
GLA_STEP_ROWS = 64
DSA_QB = 256
DSA_BT = 128
DSA_SQ = 256
DSA_CK = 512
HIST_PAD = 32


def _cparams(*sem):
    return pltpu.CompilerParams(dimension_semantics=sem, vmem_limit_bytes=VMEM_LIMIT)


def _nt(a, b):
    return lax.dot_general(a, b, (((1,), (1,)), ((), ())), preferred_element_type=F32)


def _dot(a, b):
    return jnp.dot(a, b, preferred_element_type=F32)


def _split2(x):
    hi = x.astype(BF16)
    lo = (x - hi.astype(F32)).astype(BF16)
    return hi, lo


def _split3(x):
    hi = x.astype(BF16)
    r = x - hi.astype(F32)
    mid = r.astype(BF16)
    lo = (r - mid.astype(F32)).astype(BF16)
    return hi, mid, lo


def _rel_bucket_np(dist):
    n = np.maximum(np.asarray(dist, np.int64), 0)
    max_exact = REL_BUCKETS // 2
    nf = np.maximum(n, max_exact).astype(np.float32)
    large = max_exact + (np.log(nf / np.float32(max_exact)) / np.float32(math.log(REL_MAX_DIST / max_exact))
                         * np.float32(REL_BUCKETS - max_exact)).astype(np.int32)
    large = np.minimum(large, REL_BUCKETS - 1)
    return np.where(n < max_exact, n, large).astype(np.int32)


def _bias_lookup_kernel(tab_ref, idx_ref, o_ref, *, nh):
    idx = idx_ref[...]
    for h in range(nh):
        acc = jnp.full(idx.shape, NEG, F32)
        for b in range(REL_BUCKETS):
            acc = jnp.where(idx == b, tab_ref[b, h], acc)
        o_ref[h] = acc


def bias_lookup(tab, idx_np):
    r, c = idx_np.shape
    nh = tab.shape[1]
    tr = 8 if (c >= 2048 and r % 8 == 0) else (128 if r % 128 == 0 else r)
    return pl.pallas_call(
        functools.partial(_bias_lookup_kernel, nh=nh),
        grid=(r // tr,),
        in_specs=[pl.BlockSpec(memory_space=pltpu.SMEM),
                  pl.BlockSpec((tr, c), lambda i: (i, 0))],
        out_specs=pl.BlockSpec((nh, tr, c), lambda i: (0, i, 0)),
        out_shape=jax.ShapeDtypeStruct((nh, r, c), F32),
        compiler_params=_cparams("parallel"),
        name="bias_lookup",
    )(tab, jnp.asarray(idx_np.astype(np.int32)))


def _norm_matmul_kernel(x_ref, g_ref, w_ref, o_ref):
    x = x_ref[...]
    y = x * lax.rsqrt(jnp.mean(x * x, axis=-1, keepdims=True) + EPS) * g_ref[...]
    o_ref[...] = _dot(y.astype(BF16), w_ref[...])


def norm_matmul(x, g, w):
    t, d = x.shape
    n = w.shape[1]
    tm = min(t, 512)
    return pl.pallas_call(
        _norm_matmul_kernel,
        grid=(t // tm,),
        in_specs=[pl.BlockSpec((tm, d), lambda i: (i, 0)),
                  pl.BlockSpec((1, d), lambda i: (0, 0)),
                  pl.BlockSpec((d, n), lambda i: (0, 0))],
        out_specs=pl.BlockSpec((tm, n), lambda i: (i, 0)),
        out_shape=jax.ShapeDtypeStruct((t, n), F32),
        compiler_params=_cparams("parallel"),
        name="norm_matmul",
    )(x, g.reshape(1, d), w)


def _mix_ffn_kernel(x_ref, m1_ref, m2_ref, wo1_ref, wo2_ref, g_ref, wg_ref, wu_ref, wd_ref, o_ref,
                    x1_ref, hf_ref, acc_ref):
    j = pl.program_id(1)

    @pl.when(j == 0)
    def _():
        x1 = x_ref[...] + _dot(m1_ref[...], wo1_ref[...]) + _dot(m2_ref[...], wo2_ref[...])
        x1_ref[...] = x1
        hf = x1 * lax.rsqrt(jnp.mean(x1 * x1, axis=-1, keepdims=True) + EPS) * g_ref[...]
        hf_ref[...] = hf.astype(BF16)
        acc_ref[...] = jnp.zeros_like(acc_ref)

    hf = hf_ref[...]
    a = _dot(hf, wg_ref[...])
    u = _dot(hf, wu_ref[...])
    act = (a * jax.nn.sigmoid(a) * u).astype(BF16)
    acc_ref[...] += _dot(act, wd_ref[...])

    @pl.when(j == pl.num_programs(1) - 1)
    def _():
        o_ref[...] = x1_ref[...] + acc_ref[...]


def mix_ffn(x, m1, m2, wo1, wo2, g, wg, wu, wd):
    t, d = x.shape
    hid = wg.shape[1]
    tm = min(t, 512)
    th = hid // 2 if hid % 256 == 0 else hid
    d1, d2 = m1.shape[1], m2.shape[1]
    return pl.pallas_call(
        _mix_ffn_kernel,
        grid=(t // tm, hid // th),
        in_specs=[pl.BlockSpec((tm, d), lambda i, j: (i, 0)),
                  pl.BlockSpec((tm, d1), lambda i, j: (i, 0)),
                  pl.BlockSpec((tm, d2), lambda i, j: (i, 0)),
                  pl.BlockSpec((d1, d), lambda i, j: (0, 0)),
                  pl.BlockSpec((d2, d), lambda i, j: (0, 0)),
                  pl.BlockSpec((1, d), lambda i, j: (0, 0)),
                  pl.BlockSpec((d, th), lambda i, j: (0, j)),
                  pl.BlockSpec((d, th), lambda i, j: (0, j)),
                  pl.BlockSpec((th, d), lambda i, j: (j, 0))],
        out_specs=pl.BlockSpec((tm, d), lambda i, j: (i, 0)),
        out_shape=jax.ShapeDtypeStruct((t, d), F32),
        scratch_shapes=[pltpu.VMEM((tm, d), F32), pltpu.VMEM((tm, d), BF16), pltpu.VMEM((tm, d), F32)],
        compiler_params=_cparams("parallel", "arbitrary"),
        name="mix_ffn",
    )(x, m1, m2, wo1, wo2, g.reshape(1, d), wg, wu, wd)


def _conv_kernel(z_ref, hist_ref, w_ref, b_ref, lg_ref, lb_ref, o_ref, tail_ref, uh_ref, us_ref, *, ts, c):
    t = pl.program_id(1)
    sl = 8

    @pl.when(t == 0)
    def _():
        uh_ref[0:HIST_PAD, :] = hist_ref[0]

    z = z_ref[0]
    u = z[:, 0:c] * jax.nn.sigmoid(z[:, c:2 * c])
    uh_ref[HIST_PAD:HIST_PAD + ts, :] = u
    span = HIST_PAD + ts - sl
    for s in range(1, sl):
        us_ref[s - 1, 0:span, :] = uh_ref[s:s + span, :]
    acc = jnp.zeros((ts, c), F32) + b_ref[...]
    off = HIST_PAD - (CONV_WIDTH - 1)
    for j in range(CONV_WIDTH):
        a, s = divmod(off + j, sl)
        rows = uh_ref[a * sl:a * sl + ts, :] if s == 0 else us_ref[s - 1, a * sl:a * sl + ts, :]
        acc = acc + w_ref[j:j + 1, :] * rows
    mu = jnp.mean(acc, axis=-1, keepdims=True)
    var = jnp.mean(jnp.square(acc - mu), axis=-1, keepdims=True)
    yn = (acc - mu) * lax.rsqrt(var + EPS) * lg_ref[...] + lb_ref[...]
    o_ref[0] = (yn * jax.nn.sigmoid(yn)).astype(BF16)
    tail = uh_ref[ts:ts + HIST_PAD, :]
    uh_ref[0:HIST_PAD, :] = tail
    tail_ref[0] = tail


def conv_module(z3, hist, conv_w, conv_b, ln_g, ln_b):
    nb, l, _ = z3.shape
    c = conv_w.shape[1]
    ts = min(l, 512)
    hist_p = jnp.pad(hist, ((0, 0), (HIST_PAD - (CONV_WIDTH - 1), 0), (0, 0)))
    w_p = jnp.pad(conv_w, ((0, HIST_PAD - CONV_WIDTH), (0, 0)))
    a_out, tail = pl.pallas_call(
        functools.partial(_conv_kernel, ts=ts, c=c),
        grid=(nb, l // ts),
        in_specs=[pl.BlockSpec((1, ts, 2 * c), lambda n, t: (n, t, 0)),
                  pl.BlockSpec((1, HIST_PAD, c), lambda n, t: (n, 0, 0)),
                  pl.BlockSpec((HIST_PAD, c), lambda n, t: (0, 0)),
                  pl.BlockSpec((1, c), lambda n, t: (0, 0)),
                  pl.BlockSpec((1, c), lambda n, t: (0, 0)),
                  pl.BlockSpec((1, c), lambda n, t: (0, 0))],
        out_specs=[pl.BlockSpec((1, ts, c), lambda n, t: (n, t, 0)),
                   pl.BlockSpec((1, HIST_PAD, c), lambda n, t: (n, 0, 0))],
        out_shape=[jax.ShapeDtypeStruct((nb, l, c), BF16),
                   jax.ShapeDtypeStruct((nb, HIST_PAD, c), F32)],
        scratch_shapes=[pltpu.VMEM((HIST_PAD + ts, c), F32), pltpu.VMEM((7, HIST_PAD + ts, c), F32)],
        compiler_params=_cparams("parallel", "arbitrary"),
        name="conv_module",
    )(z3, hist_p, w_p, conv_b.reshape(1, c), ln_g.reshape(1, c), ln_b.reshape(1, c))
    return a_out, tail[:, HIST_PAD - (CONV_WIDTH - 1):]


def _seg_mean_sq(x, seg_ref):
    hi, lo = _split2(x * x)
    seg = seg_ref[...]
    return (_dot(hi, seg) + _dot(lo, seg)) * (1.0 / HEAD_DIM)


def _head_norm(x, g, seg_ref):
    return x * lax.rsqrt(_seg_mean_sq(x, seg_ref) + EPS) * g


def _seg_matrix(width):
    idx = np.arange(width) // HEAD_DIM
    return jnp.asarray((idx[:, None] == idx[None, :]).astype(np.float32), BF16)


def _ab_qkv_kernel(*refs, w, tm, dils):
    ng = len(dils)
    qkv_refs = refs[:3 * ng]
    qg_ref, kg_ref, seg_ref = refs[3 * ng:3 * ng + 3]
    qo_refs = refs[3 * ng + 3:4 * ng + 3]
    kvo_refs = refs[4 * ng + 3:5 * ng + 3]
    qs_ref, kvs_ref = refs[5 * ng + 3:]
    for g, dil in enumerate(dils):
        qn = _head_norm(qkv_refs[g][...], qg_ref[...], seg_ref) * HEAD_DIM ** -0.5
        kn = _head_norm(qkv_refs[ng + g][...], kg_ref[...], seg_ref)
        v = qkv_refs[2 * ng + g][...]
        if dil == 1:
            qo_refs[g][0] = qn.astype(BF16)
            kvo_refs[g][0, :, 0:w] = kn
            kvo_refs[g][0, :, w:2 * w] = v
        else:
            nq, nkv = w // LANES, 2 * w // LANES
            kv = jnp.concatenate([kn, v], axis=1)
            for j in range(nq):
                qs_ref[j] = qn[:, j * LANES:(j + 1) * LANES]
            for j in range(nkv):
                kvs_ref[j] = kv[:, j * LANES:(j + 1) * LANES]
            for r in range(dil):
                rows = pl.ds(r, tm // dil, stride=dil)
                for j in range(nq):
                    qo_refs[g][0, :, r * w + j * LANES:r * w + (j + 1) * LANES] = qs_ref[j, rows, :].astype(BF16)
                for j in range(nkv):
                    kvo_refs[g][0, :, r * 2 * w + j * LANES:r * 2 * w + (j + 1) * LANES] = kvs_ref[j, rows, :]


def ab_qkv(z, qn, kn, col0, seq, dils):
    t = z.shape[0]
    w = B_HPG * HEAD_DIM
    ng = len(dils)
    tm = min(seq, 512)
    nt = seq // tm
    cb = col0 // w
    assert all(tm % (8 * d) == 0 or d == 1 for d in dils)
    col_spec = lambda j: pl.BlockSpec((tm, w), lambda i: (i, cb + j))
    out_specs = ([pl.BlockSpec((1, tm // d, d * w), lambda i: (i // nt, i % nt, 0)) for d in dils]
                 + [pl.BlockSpec((1, tm // d, d * 2 * w), lambda i: (i // nt, i % nt, 0)) for d in dils])
    out_shape = ([jax.ShapeDtypeStruct((t // seq, seq // d, d * w), BF16) for d in dils]
                 + [jax.ShapeDtypeStruct((t // seq, seq // d, d * 2 * w), F32) for d in dils])
    outs = pl.pallas_call(
        functools.partial(_ab_qkv_kernel, w=w, tm=tm, dils=tuple(dils)),
        grid=(t // tm,),
        in_specs=[col_spec(j) for j in range(3 * ng)]
                 + [pl.BlockSpec((1, w), lambda i: (0, 0)),
                    pl.BlockSpec((1, w), lambda i: (0, 0)),
                    pl.BlockSpec((w, w), lambda i: (0, 0))],
        out_specs=out_specs,
        out_shape=out_shape,
        scratch_shapes=[pltpu.VMEM((w // LANES, tm, LANES), F32), pltpu.VMEM((2 * w // LANES, tm, LANES), F32)],
        compiler_params=_cparams("parallel"),
        name="ab_qkv",
    )(*([z] * (3 * ng)), jnp.tile(qn, B_HPG).reshape(1, w), jnp.tile(kn, B_HPG).reshape(1, w), _seg_matrix(w))
    return outs[:ng], outs[ng:]


WIN_BLOCKS = 4


def _win_prompt_kernel(q_ref, kvp_ref, kvc_ref, bias_ref, o_ref, lse_ref, *, w, nblk):
    step = pl.program_id(2)
    col = lax.broadcasted_iota(I32, (SW_BLOCK, 2 * SW_BLOCK), 1)
    first = jnp.logical_and(step == 0, col < SW_BLOCK)
    for j in range(nblk):
        rows = slice(j * SW_BLOCK, (j + 1) * SW_BLOCK)
        q = q_ref[0, rows, :]
        kvc = kvc_ref[0, rows, :]
        kvp = kvp_ref[0] if j == 0 else kvc_ref[0, (j - 1) * SW_BLOCK:j * SW_BLOCK, :]
        for h in range(B_HPG):
            hs = slice(h * HEAD_DIM, (h + 1) * HEAD_DIM)
            vs = slice(w + h * HEAD_DIM, w + (h + 1) * HEAD_DIM)
            k2 = jnp.concatenate([kvp[:, hs], kvc[:, hs]], axis=0).astype(BF16)
            v2 = jnp.concatenate([kvp[:, vs], kvc[:, vs]], axis=0).astype(BF16)
            s = _nt(q[:, hs], k2) + bias_ref[h]
            if j == 0:
                s = jnp.where(first, NEG, s)
            m = jnp.max(s, axis=-1, keepdims=True)
            p = jnp.exp(s - m)
            l = jnp.sum(p, axis=-1, keepdims=True)
            o_ref[0, rows, hs] = _dot(p.astype(BF16), v2) / l
            lse_ref[0, rows, hs] = jnp.broadcast_to(m + jnp.log(l), (SW_BLOCK, HEAD_DIM))


def _win_prompt_bias(tab_g, dil, reach):
    ql = np.arange(SW_BLOCK)[:, None]
    kl = np.arange(2 * SW_BLOCK)[None, :] - SW_BLOCK
    rel = ql - kl
    ok = (rel >= 0) & (rel <= reach)
    return bias_lookup(tab_g, np.where(ok, _rel_bucket_np(rel * dil), -1))


def win_prompt(qv, kvv, g, tab_g):
    w = B_HPG * HEAD_DIM
    dil = DILATIONS[g]
    b, n, _ = qv.shape
    s = n * dil
    nb = n // SW_BLOCK
    assert nb * SW_BLOCK * dil == s
    bias = _win_prompt_bias(tab_g, dil, WINDOWS[g] // dil)
    nblk = math.gcd(WIN_BLOCKS, nb)
    rows = nblk * SW_BLOCK
    o, lse = pl.pallas_call(
        functools.partial(_win_prompt_kernel, w=w, nblk=nblk),
        grid=(b, dil, nb // nblk),
        in_specs=[pl.BlockSpec((1, rows, w), lambda n_, r, k: (n_, k, r)),
                  pl.BlockSpec((1, SW_BLOCK, 2 * w), lambda n_, r, k: (n_, jnp.maximum(k * nblk - 1, 0), r)),
                  pl.BlockSpec((1, rows, 2 * w), lambda n_, r, k: (n_, k, r)),
                  pl.BlockSpec((B_HPG, SW_BLOCK, 2 * SW_BLOCK), lambda n_, r, k: (0, 0, 0))],
        out_specs=[pl.BlockSpec((1, rows, w), lambda n_, r, k: (n_, k, r)),
                   pl.BlockSpec((1, rows, w), lambda n_, r, k: (n_, k, r))],
        out_shape=[jax.ShapeDtypeStruct((b, n, dil * w), F32),
                   jax.ShapeDtypeStruct((b, n, dil * w), F32)],
        compiler_params=_cparams("parallel", "parallel", "arbitrary"),
        name=f"win_prompt_g{g}",
    )(qv, kvv, kvv, bias)
    return o, lse


ROWS8 = 8


def _win_step_kernel(q_ref, kvn_ref, newt_ref, buf_ref, tb_ref, tn_ref, o_ref, lse_ref, st_ref,
                     qbd_ref, newpage_ref, *, w, wb, l):
    n = pl.program_id(0)

    @pl.when(n == 0)
    def _():
        newpage_ref[...] = jnp.zeros(newpage_ref.shape, F32)

    q = q_ref[0]
    lane_h = lax.broadcasted_iota(I32, (ROWS8, w), 1) // HEAD_DIM
    for h in range(B_HPG):
        qbd_ref[h * ROWS8:(h + 1) * ROWS8, :] = jnp.where(lane_h == h, q, jnp.zeros_like(q))
    newpage_ref[0:l, :] = kvn_ref[0]
    qbd = qbd_ref[...]
    newp = newpage_ref[...]
    buf = buf_ref[0, 0]
    kt = buf[0].reshape(w, wb).astype(BF16)
    vt = buf[1].reshape(w, wb).astype(BF16)
    s1 = _dot(qbd, kt) + tb_ref[...]
    s2 = _nt(qbd, newp[:, 0:w].astype(BF16)) + tn_ref[...]
    m = jnp.maximum(jnp.max(s1, axis=-1, keepdims=True), jnp.max(s2, axis=-1, keepdims=True))
    p1 = jnp.exp(s1 - m)
    p2 = jnp.exp(s2 - m)
    den = jnp.sum(p1, axis=-1, keepdims=True) + jnp.sum(p2, axis=-1, keepdims=True)
    num = _nt(p1.astype(BF16), vt) + _dot(p2.astype(BF16), newp[:, w:2 * w].astype(BF16))
    o = num / den
    lse = m + jnp.log(den)
    for h in range(B_HPG):
        rs = slice(h * ROWS8, (h + 1) * ROWS8)
        hs = slice(h * HEAD_DIM, (h + 1) * HEAD_DIM)
        o_ref[0, :, hs] = o[rs, hs]
        lse_ref[0, :, hs] = jnp.broadcast_to(lse[rs], (ROWS8, HEAD_DIM))
    rolled = pltpu.roll(buf.reshape(2 * w, wb), wb - l, 1)
    lane = lax.broadcasted_iota(I32, (2 * w, LANES), 1)
    last = jnp.where(lane >= LANES - l, newt_ref[0], rolled[:, wb - LANES:wb])
    if wb > LANES:
        st_ref[0, :, :, :, 0:wb - LANES] = rolled[:, 0:wb - LANES].reshape(2, B_HPG, HEAD_DIM, wb - LANES)
    st_ref[0, :, :, :, wb - LANES:wb] = last.reshape(2, B_HPG, HEAD_DIM, LANES)


def _win_step_bias(tab_g, dil, window, wb, l):
    lq = np.minimum(np.arange(ROWS8), l - 1)[:, None]
    dist_b = wb + lq - np.arange(wb)[None, :]
    ok_b = (dist_b % dil == 0) & (dist_b // dil <= window // dil)
    cols = np.arange(LANES)[None, :]
    dist_n = lq - cols
    ok_n = (cols < l) & (dist_n >= 0) & (dist_n % dil == 0) & (dist_n // dil <= window // dil)
    rows = B_HPG * ROWS8
    tb = bias_lookup(tab_g, np.where(ok_b, _rel_bucket_np(dist_b), -1)).reshape(rows, wb)
    tn = bias_lookup(tab_g, np.where(ok_n, _rel_bucket_np(dist_n), -1)).reshape(rows, LANES)
    return tb, tn


def win_step(q8, kv_new, cache_t, layer, g, tab_g):
    nb = q8.shape[0]
    l = kv_new.shape[1]
    w = B_HPG * HEAD_DIM
    wb = cache_t.shape[-1]
    window, dil = WINDOWS[g], DILATIONS[g]
    assert wb == window, "the step kernel keeps a full window of rows"
    tb, tn = _win_step_bias(tab_g, dil, window, wb, l)
    rows = B_HPG * ROWS8
    new_t = jnp.pad(kv_new.transpose(0, 2, 1), ((0, 0), (0, 0), (LANES - l, 0)))
    o, lse, st = pl.pallas_call(
        functools.partial(_win_step_kernel, w=w, wb=wb, l=l),
        grid=(nb,),
        in_specs=[pl.BlockSpec((1, ROWS8, w), lambda n: (n, 0, 0)),
                  pl.BlockSpec((1, l, 2 * w), lambda n: (n, 0, 0)),
                  pl.BlockSpec((1, 2 * w, LANES), lambda n: (n, 0, 0)),
                  pl.BlockSpec((1, 1, 2, B_HPG, HEAD_DIM, wb), lambda n: (layer, n, 0, 0, 0, 0)),
                  pl.BlockSpec((rows, wb), lambda n: (0, 0)),
                  pl.BlockSpec((rows, LANES), lambda n: (0, 0))],
        out_specs=[pl.BlockSpec((1, ROWS8, w), lambda n: (n, 0, 0)),
                   pl.BlockSpec((1, ROWS8, w), lambda n: (n, 0, 0)),
                   pl.BlockSpec((1, 2, B_HPG, HEAD_DIM, wb), lambda n: (n, 0, 0, 0, 0))],
        out_shape=[jax.ShapeDtypeStruct((nb, ROWS8, w), F32),
                   jax.ShapeDtypeStruct((nb, ROWS8, w), F32),
                   jax.ShapeDtypeStruct((nb, 2, B_HPG, HEAD_DIM, wb), F32)],
        scratch_shapes=[pltpu.VMEM((rows, w), BF16), pltpu.VMEM((LANES, 2 * w), F32)],
        compiler_params=_cparams("arbitrary"),
        name=f"win_step_g{g}",
    )(q8, kv_new, new_t, cache_t, tb, tn)
    return o[:, :l], lse[:, :l], st


def _ab_merge_kernel(*refs, w, tm, dils):
    ng = len(dils)
    out_ref, stage_ref = refs[2 * ng], refs[2 * ng + 1]
    nl = w // LANES

    def token_rows(ref, dil):
        if dil == 1:
            return ref[0]
        for r in range(dil):
            for j in range(nl):
                stage_ref[j, pl.ds(r, tm // dil, stride=dil), :] = ref[0, :, r * w + j * LANES:r * w + (j + 1) * LANES]
        return jnp.concatenate([stage_ref[j] for j in range(nl)], axis=1)

    os_ = [token_rows(refs[g], dils[g]) for g in range(ng)]
    ls_ = [token_rows(refs[ng + g], dils[g]) for g in range(ng)]
    m = functools.reduce(jnp.maximum, ls_)
    es = [jnp.exp(l - m) for l in ls_]
    num = functools.reduce(lambda x, y: x + y, [e * o for e, o in zip(es, os_)])
    den = functools.reduce(lambda x, y: x + y, es)
    out_ref[...] = (num / den).astype(BF16)


def ab_merge(outs, lses, seq, dils):
    w = B_HPG * HEAD_DIM
    t = outs[0].shape[0] * seq
    tm = min(seq, 512)
    nt = seq // tm
    specs = [pl.BlockSpec((1, tm // d, d * w), lambda i: (i // nt, i % nt, 0)) for d in dils]
    return pl.pallas_call(
        functools.partial(_ab_merge_kernel, w=w, tm=tm, dils=tuple(dils)),
        grid=(t // tm,),
        in_specs=specs + specs,
        out_specs=pl.BlockSpec((tm, w), lambda i: (i, 0)),
        out_shape=jax.ShapeDtypeStruct((t, w), BF16),
        scratch_shapes=[pltpu.VMEM((w // LANES, tm, LANES), F32)],
        compiler_params=_cparams("parallel"),
        name="ab_merge",
    )(*outs, *lses)


CD_COLS = dict(cq=0, ck=256, cv=512, cgate=1024, dq=1536, dk=2048, dv=2560, iq=3072, ik=3328, clr=3392, iw=3408)
CD_PAD = 3456
SMALL_BLOCK = CD_COLS["ik"] // LANES
SM_IK, SM_CLR, SM_IW = 0, CD_COLS["clr"] - CD_COLS["ik"], CD_COLS["iw"] - CD_COLS["ik"]


def cd_reorder_w(w_in):
    sizes = (256, 256, 512, 16, 512, 512, 512, 512, 256, 64, 4)
    names = ("cq", "ck", "cv", "clr", "cgate", "dq", "dk", "dv", "iq", "ik", "iw")
    starts = np.concatenate([[0], np.cumsum(sizes)[:-1]])
    out = jnp.zeros((w_in.shape[0], CD_PAD), w_in.dtype)
    for nm, st, sz in zip(names, starts, sizes):
        out = lax.dynamic_update_slice(out, w_in[:, st:st + sz], (0, CD_COLS[nm]))
    return out


def _cd_pre_kernel(d_ref, sm_ref, wa_ref, ba_ref, qg_ref, kg_ref, seg_ref, g_ref, dq_ref, kv_ref, *t_refs, w):
    d = d_ref[...]
    sm = sm_ref[...]
    clr = sm[:, SM_CLR:SM_CLR + GLA_GATE_RANK]
    wa = wa_ref[...]
    pre = ba_ref[...]
    for part in _split3(clr):
        pre = pre + _dot(part, wa[0]) + _dot(part, wa[1])
    lsig = jnp.minimum(pre, 0.0) - jnp.log(1.0 + jnp.exp(-jnp.abs(pre)))
    g_ref[...] = lsig * (1.0 / GLA_TAU)
    q_scale = HEAD_DIM ** -0.5 * (LOG2E if t_refs else 1.0)
    dq_ref[...] = (_head_norm(d[:, 0:w], qg_ref[...], seg_ref) * q_scale).astype(BF16)
    kn = _head_norm(d[:, w:2 * w], kg_ref[...], seg_ref)
    v = d[:, 2 * w:3 * w]
    kv_ref[:, 0:w] = kn
    kv_ref[:, w:2 * w] = v
    if t_refs:
        kb_ref, vt_ref, smt_ref = t_refs
        kb_ref[...] = kn.astype(BF16)
        vt_ref[0] = v.T.astype(BF16)
        smt_ref[0] = sm.T


def cd_pre(z, wa2, ba, qn, kn, seq=None):
    t = z.shape[0]
    w = DSA_HEADS * HEAD_DIM
    gw = GLA_HEADS * GLA_DK
    tm = min(t, 512)
    wa_hi, wa_lo = _split2(wa2)
    out_specs = [pl.BlockSpec((tm, gw), lambda i: (i, 0)),
                 pl.BlockSpec((tm, w), lambda i: (i, 0)),
                 pl.BlockSpec((tm, 2 * w), lambda i: (i, 0))]
    out_shape = [jax.ShapeDtypeStruct((t, gw), F32),
                 jax.ShapeDtypeStruct((t, w), BF16),
                 jax.ShapeDtypeStruct((t, 2 * w), F32)]
    if seq is not None:
        nt = seq // tm
        out_specs += [pl.BlockSpec((tm, w), lambda i: (i, 0)),
                      pl.BlockSpec((1, w, tm), lambda i: (i // nt, 0, i % nt)),
                      pl.BlockSpec((1, LANES, tm), lambda i: (i // nt, 0, i % nt))]
        out_shape += [jax.ShapeDtypeStruct((t, w), BF16),
                      jax.ShapeDtypeStruct((t // seq, w, seq), BF16),
                      jax.ShapeDtypeStruct((t // seq, LANES, seq), F32)]
    return pl.pallas_call(
        functools.partial(_cd_pre_kernel, w=w),
        grid=(t // tm,),
        in_specs=[pl.BlockSpec((tm, 3 * w), lambda i: (i, CD_COLS["dq"] // (3 * w))),
                  pl.BlockSpec((tm, LANES), lambda i: (i, SMALL_BLOCK)),
                  pl.BlockSpec((2, GLA_GATE_RANK, gw), lambda i: (0, 0, 0)),
                  pl.BlockSpec((1, gw), lambda i: (0, 0)),
                  pl.BlockSpec((1, w), lambda i: (0, 0)),
                  pl.BlockSpec((1, w), lambda i: (0, 0)),
                  pl.BlockSpec((w, w), lambda i: (0, 0))],
        out_specs=out_specs,
        out_shape=out_shape,
        compiler_params=_cparams("parallel"),
        name="cd_pre",
    )(z, z, jnp.stack([wa_hi, wa_lo]), ba.reshape(1, gw), jnp.tile(qn, DSA_HEADS).reshape(1, w),
      jnp.tile(kn, DSA_HEADS).reshape(1, w), _seg_matrix(w))


def _gla_consts(c):
    levels = []
    s = c // 2
    while s >= 1:
        levels.append(s)
        s //= 2
    i = np.arange(c)
    tri = (i[None, :] <= i[:, None]).astype(np.float32)
    mats = [tri]
    masks = []
    for s in levels:
        ref = (i // (2 * s)) * (2 * s) + s - 1
        r = (i[None, :] <= ref[:, None]).astype(np.float32)
        mats.append(tri - r)
        same = (i[:, None] // (2 * s)) == (i[None, :] // (2 * s))
        masks.append(same & ((i[:, None] % (2 * s)) >= s) & ((i[None, :] % (2 * s)) < s))
    masks.append(i[:, None] == i[None, :])
    mstack = np.concatenate(mats, axis=0)
    mask = np.stack([np.tile(m.astype(np.float32), (1, GLA_HEADS)) for m in masks])
    return jnp.asarray(mstack, BF16), jnp.asarray(mask, F32), len(levels)


def _gla_kernel(qk_ref, v_ref, gate_ref, g_ref, s0_ref, mstack_ref, mask_ref, gn_ref, eye_ref,
                o_ref, sf_ref, st_ref, *, c, nl):
    t = pl.program_id(1)
    kw = GLA_HEADS * GLA_DK
    vw = GLA_HEADS * GLA_DV
    lane_k = lax.broadcasted_iota(I32, (1, kw), 1) // GLA_DK
    lane_v = lax.broadcasted_iota(I32, (1, vw), 1) // GLA_DV

    @pl.when(t == 0)
    def _():
        rows = []
        for h in range(GLA_HEADS):
            z = jnp.zeros((GLA_DK, GLA_DV), F32)
            rows.append(jnp.concatenate([s0_ref[0, h] if hh == h else z for hh in range(GLA_HEADS)], axis=1))
        st_ref[...] = jnp.concatenate(rows, axis=0).T

    qk = qk_ref[0]
    q = qk[:, 0:kw] * GLA_DK ** -0.5
    k = qk[:, kw:2 * kw]
    v = v_ref[0]
    mstack = mstack_ref[...]
    r = None
    for part in _split3(g_ref[0]):
        d = _dot(mstack, part)
        r = d if r is None else r + d
    b = r[0:c]

    def expand_k(x):
        return jnp.concatenate([jnp.where(lane_k == h, x, 0.0) for h in range(GLA_HEADS)], axis=0).astype(BF16)

    st = st_ref[...]
    o = _nt((q * jnp.exp(b)).astype(BF16), st.astype(BF16))
    a = mask_ref[nl] * _nt(q.astype(BF16), expand_k(k))
    for lv in range(nl):
        sc = jnp.exp(-jnp.abs(r[(lv + 1) * c:(lv + 2) * c]))
        a = a + mask_ref[lv] * _nt((q * sc).astype(BF16), expand_k(k * sc))
    vexp = jnp.concatenate([jnp.where(lane_v == h, v, 0.0) for h in range(GLA_HEADS)], axis=0).astype(BF16)
    o = o + _dot(a.astype(BF16), vexp)

    blast = b[c - 1:c]
    kt = (k * jnp.exp(blast - b)).astype(BF16)
    vt = _nt(eye_ref[...], v.astype(BF16)).astype(BF16)
    upd = _dot(vt, kt)
    row_h = lax.broadcasted_iota(I32, (vw, 1), 0) // GLA_DV
    st_new = st * jnp.exp(blast) + jnp.where(row_h == lane_k, upd, 0.0)
    st_ref[...] = st_new

    gate = gate_ref[0]
    gn = gn_ref[...]
    for h in range(GLA_HEADS):
        hs = slice(h * GLA_DV, (h + 1) * GLA_DV)
        oh = o[:, hs]
        y = oh * lax.rsqrt(jnp.mean(oh * oh, axis=-1, keepdims=True) + EPS) * gn
        gh = gate[:, hs]
        o_ref[0, :, hs] = (y * (gh * jax.nn.sigmoid(gh))).astype(BF16)

    @pl.when(t == pl.num_programs(1) - 1)
    def _():
        s_t = st_new.T
        for h in range(GLA_HEADS):
            sf_ref[0, h] = s_t[h * GLA_DK:(h + 1) * GLA_DK, h * GLA_DV:(h + 1) * GLA_DV]


def gla(z3, g3, s0, gla_norm):
    nb, l, _ = z3.shape
    c = min(l, GLA_CHUNK)
    assert l % c == 0 and c & (c - 1) == 0
    kw = GLA_HEADS * GLA_DK
    vw = GLA_HEADS * GLA_DV
    mstack, mask, nl = _gla_consts(c)
    eye = jnp.eye(vw, dtype=BF16)
    o, sf = pl.pallas_call(
        functools.partial(_gla_kernel, c=c, nl=nl),
        grid=(nb, l // c),
        in_specs=[pl.BlockSpec((1, c, 2 * kw), lambda n, t: (n, t, 0)),
                  pl.BlockSpec((1, c, vw), lambda n, t: (n, t, CD_COLS["cv"] // vw)),
                  pl.BlockSpec((1, c, vw), lambda n, t: (n, t, CD_COLS["cgate"] // vw)),
                  pl.BlockSpec((1, c, kw), lambda n, t: (n, t, 0)),
                  pl.BlockSpec((1, GLA_HEADS, GLA_DK, GLA_DV), lambda n, t: (n, 0, 0, 0)),
                  pl.BlockSpec(mstack.shape, lambda n, t: (0, 0)),
                  pl.BlockSpec(mask.shape, lambda n, t: (0, 0, 0)),
                  pl.BlockSpec((1, GLA_DV), lambda n, t: (0, 0)),
                  pl.BlockSpec((vw, vw), lambda n, t: (0, 0))],
        out_specs=[pl.BlockSpec((1, c, vw), lambda n, t: (n, t, 0)),
                   pl.BlockSpec((1, GLA_HEADS, GLA_DK, GLA_DV), lambda n, t: (n, 0, 0, 0))],
        out_shape=[jax.ShapeDtypeStruct((nb, l, vw), BF16),
                   jax.ShapeDtypeStruct((nb, GLA_HEADS, GLA_DK, GLA_DV), F32)],
        scratch_shapes=[pltpu.VMEM((vw, kw), F32)],
        compiler_params=_cparams("parallel", "arbitrary"),
        name="gla",
    )(z3, z3, z3, g3, s0, mstack, mask, gla_norm.reshape(1, GLA_DV), eye)
    return o, sf


def _sort_key(x):
    bits = pltpu.bitcast(x, I32)
    return jnp.where(bits < 0, (bits ^ 0x7FFFFFFF) + 1, bits)


def _idx_lhs(iq, h):
    hi, lo = _split2(iq[:, h * IDX_DIM:(h + 1) * IDX_DIM])
    return jnp.concatenate([hi, hi, lo, jnp.zeros_like(hi)], axis=1)


def _idx_rhs(ik):
    hi, lo = _split2(ik)
    return jnp.concatenate([hi, lo, hi, jnp.zeros_like(hi)], axis=1)


def _idx_scores(lhs, rhs, wcol):
    sc = None
    for h in range(IDX_HEADS):
        d = jnp.maximum(_nt(lhs[h], rhs) * IDX_DIM ** -0.5, 0.0) * (wcol[h] * IDX_HEADS ** -0.5)
        sc = d if sc is None else sc + d
    return sc


def _dsa_select_kernel(iq_ref, smt_ref, smk_ref, tril_ref, mask_ref, kb3_ref, hi_ref, d1_ref, d0_ref, *, nc, topk):
    i = pl.program_id(1)
    ck = DSA_CK
    qb = DSA_SQ
    bpc = ck // qb
    sub = ck // 8

    @pl.when(i == 0)
    def _():
        for c in range(nc):
            kb3_ref[c] = _idx_rhs(smk_ref[0, c * ck:(c + 1) * ck, SM_IK:SM_IK + IDX_DIM])

    iq = iq_ref[0]
    smt = smt_ref[0]
    lhs = jnp.concatenate([_idx_lhs(iq, h) for h in range(IDX_HEADS)], axis=0)
    wrow = [smt[SM_IW + h:SM_IW + h + 1, :] * (IDX_HEADS ** -0.5 * IDX_DIM ** -0.5) for h in range(IDX_HEADS)]
    nch = (i + bpc) // bpc

    def chunk_planes(c, causal):
        dots = _nt(kb3_ref[c], lhs)
        sc = None
        for h in range(IDX_HEADS):
            t = jnp.maximum(dots[:, h * qb:(h + 1) * qb], 0.0) * wrow[h]
            sc = t if sc is None else sc + t
        sc = jnp.where(jnp.abs(sc) < F32_MIN_NORMAL, 0.0, sc)
        bits = pltpu.bitcast(sc, I32)
        key = bits ^ (lax.shift_right_arithmetic(bits, 31) & 0x7FFFFFFF)
        hi = pltpu.bitcast(bits & -65536, F32)
        if causal:
            drc = lax.broadcasted_iota(I32, (ck, qb), 0) - lax.broadcasted_iota(I32, (ck, qb), 1)
            hi = jnp.where(drc <= i * qb - c * ck, hi, -jnp.inf)
        hi_ref[c] = hi.astype(BF16)
        d1_ref[c] = (lax.shift_right_logical(key, 8) & 0xFF).astype(F32).astype(BF16)
        d0_ref[c] = (key & 0xFF).astype(F32).astype(BF16)

    def score_body(c, carry):
        chunk_planes(c, False)
        return carry

    lax.fori_loop(0, nch - 1, score_body, 0)
    chunk_planes(nch - 1, True)

    one, zero = jnp.ones((), BF16), jnp.zeros((), BF16)

    def count(ref, cand, strict):
        def body(c, acc):
            blk = ref[c]
            m = jnp.where(blk > cand if strict else blk >= cand, one, zero)
            for j in range(8):
                acc = acc + m[j * sub:(j + 1) * sub, :]
            return acc
        acc = lax.fori_loop(0, nch, body, jnp.zeros((sub, qb), BF16))
        return jnp.sum(acc.astype(F32), axis=0, keepdims=True)

    qpos = i * qb + lax.broadcasted_iota(I32, (1, qb), 1)
    kk = jnp.minimum(topk, qpos + 1).astype(F32)

    def hi_value(s16):
        p = jnp.where(s16 >= 0, s16, s16 ^ 0x7FFF) & 0xFFFF
        return pltpu.bitcast(lax.shift_left(p, 16), F32).astype(BF16)

    def descend(ref, nbits, start, as_cand, want, c_start):
        def step(t, carry):
            thr, cge = carry
            cand = thr + lax.shift_left(jnp.int32(1), nbits - 1 - t)
            cnt = count(ref, as_cand(cand), False)
            ok = cnt >= want
            return jnp.where(ok, cand, thr), jnp.where(ok, cnt, cge)
        return lax.fori_loop(0, nbits, step, (jnp.full((1, qb), start, I32), c_start))

    digit = lambda t: t.astype(F32).astype(BF16)

    h16, cge1 = descend(hi_ref, 16, -(2 ** 15), hi_value, kk, jnp.zeros((1, qb), F32))
    hv = hi_value(h16)
    cgt1 = count(hi_ref, hv, True)
    need1 = kk - cgt1

    def restrict(dst_ref, cls_ref, cls_val):
        def body(c, carry):
            dst_ref[c] = jnp.where(cls_ref[c] == cls_val, dst_ref[c], -one)
            return carry
        lax.fori_loop(0, nch, body, 0)

    restrict(d1_ref, hi_ref, hv)
    b1, cge2 = descend(d1_ref, 8, 0, digit, need1, cge1 - cgt1)
    b1v = digit(b1)
    cgt2 = count(d1_ref, b1v, True)
    need2 = need1 - cgt2
    restrict(d0_ref, d1_ref, b1v)
    b0, cge = descend(d0_ref, 8, 0, digit, need2, cge2 - cgt2)
    b0v = digit(b0)
    tied = jnp.max(jnp.where(cge > need2, 1, 0))

    @pl.when(tied > 0)
    def _():
        need3 = need2 - count(d0_ref, b0v, True)
        b0f = b0.astype(F32)

        def tie_body(c, off):
            blk = d0_ref[c].astype(F32)
            eq = blk == b0f
            pref = _dot(tril_ref[...], jnp.where(eq, 1.0, 0.0).astype(BF16))
            drop = jnp.logical_and(eq, pref + off > need3)
            d0_ref[c] = jnp.where(drop, -1.0, blk).astype(BF16)
            return off + pref[ck - 1:ck, :]

        lax.fori_loop(0, nch, tie_body, jnp.zeros((1, qb), F32))

    neg = jnp.full((), NEG, BF16)

    def live_body(c, carry):
        keep3 = jnp.where(d0_ref[c] >= b0v, zero, neg)
        keep2 = jnp.where(d1_ref[c] > b1v, zero, keep3)
        mask_ref[0, 0, c] = jnp.where(hi_ref[c] > hv, zero, keep2)
        return carry

    def dead_body(c, carry):
        mask_ref[0, 0, c] = jnp.full((ck, qb), NEG, BF16)
        return carry

    lax.fori_loop(0, nch, live_body, 0)
    lax.fori_loop(nch, nc, dead_body, 0)


def dsa_select(z3, smt):
    b, s, _ = z3.shape
    nq, nc = s // DSA_SQ, s // DSA_CK
    topk = min(DSA_TOPK_MAX, s // 4)
    tril = jnp.asarray(np.tril(np.ones((DSA_CK, DSA_CK), np.float32)), BF16)
    return pl.pallas_call(
        functools.partial(_dsa_select_kernel, nc=nc, topk=topk),
        grid=(b, nq),
        in_specs=[pl.BlockSpec((1, DSA_SQ, IDX_HEADS * IDX_DIM), lambda n, i: (n, i, CD_COLS["iq"] // (IDX_HEADS * IDX_DIM))),
                  pl.BlockSpec((1, LANES, DSA_SQ), lambda n, i: (n, 0, i)),
                  pl.BlockSpec((1, s, LANES), lambda n, i: (n, 0, SMALL_BLOCK)),
                  pl.BlockSpec((DSA_CK, DSA_CK), lambda n, i: (0, 0))],
        out_specs=pl.BlockSpec((1, 1, nc, DSA_CK, DSA_SQ), lambda n, i: (n, i, 0, 0, 0)),
        out_shape=jax.ShapeDtypeStruct((b, nq, nc, DSA_CK, DSA_SQ), BF16),
        scratch_shapes=[pltpu.VMEM((nc, DSA_CK, 4 * IDX_DIM), BF16)] + [pltpu.VMEM((nc, DSA_CK, DSA_SQ), BF16)] * 3,
        compiler_params=_cparams("parallel", "arbitrary"),
        name="dsa_select",
    )(z3, smt, z3, tril)


def _dsa_bias_tiles(tab_d):
    o = 0
    while _rel_bucket_np(max(o * DSA_BT - (DSA_BT - 1), 0)) < REL_BUCKETS - 1:
        o += 1
    offs = np.arange(o + 1)[:, None, None] * DSA_BT
    d = offs + np.arange(DSA_BT)[None, None, :] - np.arange(DSA_BT)[None, :, None]
    tiles = bias_lookup(tab_d, _rel_bucket_np(d).reshape((o + 1) * DSA_BT, DSA_BT))
    return tiles.reshape(DSA_HEADS, o + 1, DSA_BT, DSA_BT)


def _dsa_attn_kernel(qi_ref, kc_ref, q_ref, k_ref, vt_ref, mask_ref, bt_ref, o_ref, m_ref, l_ref, acc_ref, s_ref,
                     *, n_off):
    s_id = pl.program_id(1)
    i = qi_ref[s_id]
    c = kc_ref[s_id]
    qb, ck = DSA_QB, DSA_CK
    tk, tq = ck // DSA_BT, qb // DSA_BT

    @pl.when(c == 0)
    def _():
        m_ref[...] = jnp.full(m_ref.shape, NEG, F32)
        l_ref[...] = jnp.zeros(l_ref.shape, F32)
        acc_ref[...] = jnp.zeros(acc_ref.shape, F32)

    madd = mask_ref[0, 0, 0].astype(F32)
    q = q_ref[0]
    k = k_ref[0]
    vt = vt_ref[0]
    offs = [[jnp.clip((i * tq + u) - (c * tk + t), 0, n_off - 1) for u in range(tq)] for t in range(tk)]
    m_all = m_ref[...]
    l_all = l_ref[...]
    m_rows, l_rows = [], []
    for h in range(DSA_HEADS):
        hs = slice(h * HEAD_DIM, (h + 1) * HEAD_DIM)
        bias = jnp.concatenate([jnp.concatenate([bt_ref[h, offs[t][u]] for u in range(tq)], axis=1)
                                for t in range(tk)], axis=0)
        s = _nt(k[:, hs], q[:, hs]) + bias + madd
        s_ref[h] = s
        m_rows.append(jnp.maximum(m_all[h:h + 1, :], jnp.max(s, axis=0, keepdims=True)))
    ones = jnp.ones((16, ck), BF16)
    for h in range(DSA_HEADS):
        hs = slice(h * HEAD_DIM, (h + 1) * HEAD_DIM)
        alpha = jnp.exp2(m_all[h:h + 1, :] - m_rows[h])
        p = jnp.exp2(s_ref[h] - m_rows[h]).astype(BF16)
        pv = _dot(jnp.concatenate([vt[hs, :], ones], axis=0), p)
        l_rows.append(alpha * l_all[h:h + 1, :] + pv[HEAD_DIM:HEAD_DIM + 1, :])
        acc_ref[hs, :] = alpha * acc_ref[hs, :] + pv[0:HEAD_DIM, :]
    m_ref[...] = jnp.concatenate(m_rows, axis=0)
    l_ref[...] = jnp.concatenate(l_rows, axis=0)

    @pl.when(c == ((i + 1) * qb - 1) // ck)
    def _():
        inv = 1.0 / l_ref[...]
        ot = jnp.concatenate([acc_ref[h * HEAD_DIM:(h + 1) * HEAD_DIM, :] * inv[h:h + 1, :]
                              for h in range(DSA_HEADS)], axis=0)
        o_ref[0] = ot.T.astype(BF16)


def dsa_attend(dq, kb, vt, mask, tab_d):
    b, s, w = dq.shape
    nq = s // DSA_QB
    last_c = lambda i: ((i + 1) * DSA_QB - 1) // DSA_CK
    qps = DSA_SQ // DSA_QB
    bt = _dsa_bias_tiles(tab_d) * LOG2E
    n_off = bt.shape[1]
    qi = np.concatenate([np.full(last_c(i) + 1, i) for i in range(nq)]).astype(np.int32)
    kc = np.concatenate([np.arange(last_c(i) + 1) for i in range(nq)]).astype(np.int32)
    grid_spec = pltpu.PrefetchScalarGridSpec(
        num_scalar_prefetch=2,
        grid=(b, len(qi)),
        in_specs=[pl.BlockSpec((1, DSA_QB, w), lambda n, t, qi_, kc_: (n, qi_[t], 0)),
                  pl.BlockSpec((1, DSA_CK, w), lambda n, t, qi_, kc_: (n, kc_[t], 0)),
                  pl.BlockSpec((1, w, DSA_CK), lambda n, t, qi_, kc_: (n, 0, kc_[t])),
                  pl.BlockSpec((1, 1, 1, DSA_CK, DSA_QB),
                               lambda n, t, qi_, kc_: (n, qi_[t] // qps, kc_[t], 0, qi_[t] % qps)),
                  pl.BlockSpec(bt.shape, lambda n, t, qi_, kc_: (0, 0, 0, 0))],
        out_specs=pl.BlockSpec((1, DSA_QB, w), lambda n, t, qi_, kc_: (n, qi_[t], 0)),
        scratch_shapes=[pltpu.VMEM((DSA_HEADS, DSA_QB), F32), pltpu.VMEM((DSA_HEADS, DSA_QB), F32),
                        pltpu.VMEM((w, DSA_QB), F32), pltpu.VMEM((DSA_HEADS, DSA_CK, DSA_QB), F32)],
    )
    return pl.pallas_call(
        functools.partial(_dsa_attn_kernel, n_off=n_off),
        grid_spec=grid_spec,
        out_shape=jax.ShapeDtypeStruct((b, s, w), BF16),
        compiler_params=_cparams("parallel", "arbitrary"),
        name="dsa_attend",
    )(jnp.asarray(qi), jnp.asarray(kc), dq, kb, vt, mask, bt)


SCORE_PAGES = 32
ATTN_PAGES = 16


def _dsa_step_scores_kernel(pt_ref, iq_ref, sm_ref, *rest, npg):
    k_refs, o_ref = rest[:npg], rest[npg]
    iq = iq_ref[0]
    sm = sm_ref[0]
    lhs = jnp.concatenate([_idx_lhs(iq, h) for h in range(IDX_HEADS)], axis=0)
    wcol = [sm[:, SM_IW + h:SM_IW + h + 1] * IDX_HEADS ** -0.5 for h in range(IDX_HEADS)]
    for j in range(npg):
        hi, lo = _split2(k_refs[j][0, 0])
        d = _dot(lhs, jnp.concatenate([hi, lo, hi, jnp.zeros_like(hi)], axis=0))
        sc = None
        for h in range(IDX_HEADS):
            t = jnp.maximum(d[h * ROWS8:(h + 1) * ROWS8] * IDX_DIM ** -0.5, 0.0) * wcol[h]
            sc = t if sc is None else sc + t
        o_ref[0, :, j * PAGE_SIZE:(j + 1) * PAGE_SIZE] = sc


def dsa_step_scores(z8, kidx_t, layer, page_table):
    nb = z8.shape[0]
    n_pages = page_table.shape[1]
    npg = math.gcd(SCORE_PAGES, n_pages)
    iqw = IDX_HEADS * IDX_DIM

    def page_spec(j):
        return pl.BlockSpec((1, 1, IDX_DIM, PAGE_SIZE),
                            lambda n, p, pt: (layer, pt[n * n_pages + p * npg + j], 0, 0))

    grid_spec = pltpu.PrefetchScalarGridSpec(
        num_scalar_prefetch=1,
        grid=(nb, n_pages // npg),
        in_specs=[pl.BlockSpec((1, ROWS8, iqw), lambda n, p, pt: (n, 0, CD_COLS["iq"] // iqw)),
                  pl.BlockSpec((1, ROWS8, LANES), lambda n, p, pt: (n, 0, SMALL_BLOCK))]
                 + [page_spec(j) for j in range(npg)],
        out_specs=pl.BlockSpec((1, ROWS8, npg * PAGE_SIZE), lambda n, p, pt: (n, 0, p)),
    )
    return pl.pallas_call(
        functools.partial(_dsa_step_scores_kernel, npg=npg),
        grid_spec=grid_spec,
        out_shape=jax.ShapeDtypeStruct((nb, ROWS8, n_pages * PAGE_SIZE), F32),
        compiler_params=_cparams("parallel", "arbitrary"),
        name="dsa_step_scores",
    )(page_table.reshape(-1), z8, z8, *([kidx_t] * npg))


STEP_SELECT_SEQS = 4


def _dsa_step_select_kernel(sc_ref, iq_ref, sm_ref, triu_ref, mp_ref, mn_ref, key_ref, *, past, l_new, topk, g):
    ck = DSA_CK
    nck = past // ck
    rows = g * ROWS8
    new_keys = []
    for j in range(g):
        iq = iq_ref[j]
        sm = sm_ref[j]
        lhs = [_idx_lhs(iq, h) for h in range(IDX_HEADS)]
        wcol = [sm[:, SM_IW + h:SM_IW + h + 1] for h in range(IDX_HEADS)]
        rhs_new = _idx_rhs(jnp.concatenate([sm[:, SM_IK:SM_IK + IDX_DIM],
                                            jnp.zeros((LANES - ROWS8, IDX_DIM), F32)], axis=0))
        new_keys.append(_sort_key(_idx_scores(lhs, rhs_new, wcol)))
    rloc = lax.broadcasted_iota(I32, (rows, LANES), 0) % ROWS8
    col = lax.broadcasted_iota(I32, (rows, LANES), 1)
    key_new = jnp.where(jnp.logical_and(col <= rloc, col < l_new), jnp.concatenate(new_keys, axis=0), INT_MIN)
    key_ref[...] = _sort_key(sc_ref[...].reshape(rows, past))
    kk = jnp.minimum(topk, past + 1 + lax.broadcasted_iota(I32, (rows, 1), 0) % ROWS8)

    def count(pred):
        return (jnp.sum(jnp.where(pred(key_ref[...]), 1, 0), axis=1, keepdims=True)
                + jnp.sum(jnp.where(pred(key_new), 1, 0), axis=1, keepdims=True))

    def bit_body(t, carry):
        thr, cge = carry
        cand = thr + lax.shift_left(jnp.int32(1), 31 - t)
        cnt = count(lambda x: x >= cand)
        ok = cnt >= kk
        return jnp.where(ok, cand, thr), jnp.where(ok, cnt, cge)

    thr, cge = lax.fori_loop(0, 32, bit_body, (jnp.full((rows, 1), INT_MIN, I32), jnp.zeros((rows, 1), I32)))
    cgt = count(lambda x: x > thr)
    need = kk - cgt
    needf = need.astype(F32)
    tied = jnp.max(jnp.where(need < cge - cgt, 1, 0))
    mn_ref[...] = jnp.where(key_new >= thr, 0.0, NEG).reshape(g, ROWS8, LANES)

    @pl.when(tied == 0)
    def _():
        mp_ref[...] = jnp.where(key_ref[...] >= thr, 0.0, NEG).reshape(g, ROWS8, past)

    @pl.when(tied > 0)
    def _():
        off = jnp.zeros((rows, 1), F32)
        triu = triu_ref[...]
        for c in range(nck):
            blk = key_ref[:, c * ck:(c + 1) * ck]
            eq = blk == thr
            pref = _dot(jnp.where(eq, 1.0, 0.0).astype(BF16), triu)
            keep = jnp.logical_or(blk > thr, jnp.logical_and(eq, pref + off <= needf))
            mp_ref[:, :, c * ck:(c + 1) * ck] = jnp.where(keep, 0.0, NEG).reshape(g, ROWS8, ck)
            off = off + pref[:, ck - 1:ck]
        eq = key_new == thr
        pref = _dot(jnp.where(eq, 1.0, 0.0).astype(BF16), triu[0:LANES, 0:LANES])
        keep = jnp.logical_or(key_new > thr, jnp.logical_and(eq, pref + off <= needf))
        mn_ref[...] = jnp.where(keep, 0.0, NEG).reshape(g, ROWS8, LANES)


def dsa_step_select(scores, z8, l_new):
    nb, _, past = scores.shape
    topk = min(DSA_TOPK_MAX, (past + l_new) // 4)
    iqw = IDX_HEADS * IDX_DIM
    g = math.gcd(STEP_SELECT_SEQS, nb)
    assert past % DSA_CK == 0
    triu = jnp.asarray(np.triu(np.ones((DSA_CK, DSA_CK), np.float32)), BF16)
    return pl.pallas_call(
        functools.partial(_dsa_step_select_kernel, past=past, l_new=l_new, topk=topk, g=g),
        grid=(nb // g,),
        in_specs=[pl.BlockSpec((g, ROWS8, past), lambda n: (n, 0, 0)),
                  pl.BlockSpec((g, ROWS8, iqw), lambda n: (n, 0, CD_COLS["iq"] // iqw)),
                  pl.BlockSpec((g, ROWS8, LANES), lambda n: (n, 0, SMALL_BLOCK)),
                  pl.BlockSpec((DSA_CK, DSA_CK), lambda n: (0, 0))],
        out_specs=[pl.BlockSpec((g, ROWS8, past), lambda n: (n, 0, 0)),
                   pl.BlockSpec((g, ROWS8, LANES), lambda n: (n, 0, 0))],
        out_shape=[jax.ShapeDtypeStruct((nb, ROWS8, past), F32),
                   jax.ShapeDtypeStruct((nb, ROWS8, LANES), F32)],
        scratch_shapes=[pltpu.VMEM((g * ROWS8, past), I32)],
        compiler_params=_cparams("parallel"),
        name="dsa_step_select",
    )(scores, z8, z8, triu)


def _dsa_step_attn_kernel(pt_ref, q_ref, kvn_ref, mp_ref, mn_ref, bp_ref, bn_ref, *rest, w, npg):
    kv_refs, o_ref = rest[:npg], rest[npg]
    qbd_ref, newpage_ref, m_ref, l_ref, acc_ref = rest[npg + 1:]
    n = pl.program_id(0)
    p = pl.program_id(1)
    rows = DSA_HEADS * ROWS8
    lane_h = lax.broadcasted_iota(I32, (ROWS8, w), 1) // HEAD_DIM

    @pl.when(jnp.logical_and(n == 0, p == 0))
    def _():
        newpage_ref[...] = jnp.zeros(newpage_ref.shape, F32)

    @pl.when(p == 0)
    def _():
        q = q_ref[0]
        for h in range(DSA_HEADS):
            qbd_ref[h * ROWS8:(h + 1) * ROWS8, :] = jnp.where(lane_h == h, q, jnp.zeros_like(q))
        m_ref[...] = jnp.full(m_ref.shape, NEG, F32)
        l_ref[...] = jnp.zeros(l_ref.shape, F32)
        acc_ref[...] = jnp.zeros(acc_ref.shape, F32)

    def accumulate(scores, madd8, bias, pv):
        s = scores + bias + jnp.concatenate([madd8] * DSA_HEADS, axis=0)
        m_old = m_ref[:, 0:1]
        m_new = jnp.maximum(m_old, jnp.max(s, axis=-1, keepdims=True))
        alpha = jnp.exp(m_old - m_new)
        pr = jnp.exp(s - m_new)
        l_ref[...] = jnp.broadcast_to(alpha * l_ref[:, 0:1] + jnp.sum(pr, axis=-1, keepdims=True), (rows, LANES))
        m_ref[...] = jnp.broadcast_to(m_new, (rows, LANES))
        acc_ref[...] = alpha * acc_ref[...] + pv(pr.astype(BF16))

    qbd = qbd_ref[...]
    scores = jnp.concatenate([_dot(qbd, kv_refs[j][0, 0, 0].reshape(w, PAGE_SIZE).astype(BF16))
                              for j in range(npg)], axis=1)

    def pv_pages(pr):
        out = None
        for j in range(npg):
            t = _nt(pr[:, j * PAGE_SIZE:(j + 1) * PAGE_SIZE], kv_refs[j][0, 0, 1].reshape(w, PAGE_SIZE).astype(BF16))
            out = t if out is None else out + t
        return out

    accumulate(scores, mp_ref[0], bp_ref[...], pv_pages)

    @pl.when(p == pl.num_programs(1) - 1)
    def _():
        newpage_ref[0:ROWS8, :] = kvn_ref[0]
        newp = newpage_ref[...]
        accumulate(_nt(qbd, newp[:, 0:w].astype(BF16)), mn_ref[0], bn_ref[...],
                   lambda pr: _dot(pr, newp[:, w:2 * w].astype(BF16)))
        for h in range(DSA_HEADS):
            rs = slice(h * ROWS8, (h + 1) * ROWS8)
            hs = slice(h * HEAD_DIM, (h + 1) * HEAD_DIM)
            o_ref[0, :, hs] = (acc_ref[rs, hs] / l_ref[rs, 0:1]).astype(BF16)


def _dsa_step_bias(tab_d, past, l_new):
    lq = np.minimum(np.arange(ROWS8), l_new - 1)[:, None]
    d_past = past + lq - np.arange(past)[None, :]
    d_new = lq - np.arange(LANES)[None, :]

    def table(d):
        return bias_lookup(tab_d, _rel_bucket_np(d)).reshape(DSA_HEADS * ROWS8, d.shape[1])

    return table(d_past), table(d_new)


def dsa_step_attend(dq8, kv_new8, kv_t, layer, page_table, mask_past, mask_new, tab_d, l_new):
    nb, _, w = dq8.shape
    n_pages = page_table.shape[1]
    npg = math.gcd(ATTN_PAGES, n_pages)
    past = n_pages * PAGE_SIZE
    rows = DSA_HEADS * ROWS8
    bp, bn = _dsa_step_bias(tab_d, past, l_new)

    def page_spec(j):
        return pl.BlockSpec((1, 1, 2, DSA_HEADS, HEAD_DIM, PAGE_SIZE),
                            lambda n, p, pt: (layer, pt[n * n_pages + p * npg + j], 0, 0, 0, 0))

    grid_spec = pltpu.PrefetchScalarGridSpec(
        num_scalar_prefetch=1,
        grid=(nb, n_pages // npg),
        in_specs=[pl.BlockSpec((1, ROWS8, w), lambda n, p, pt: (n, 0, 0)),
                  pl.BlockSpec((1, ROWS8, 2 * w), lambda n, p, pt: (n, 0, 0)),
                  pl.BlockSpec((1, ROWS8, npg * PAGE_SIZE), lambda n, p, pt: (n, 0, p)),
                  pl.BlockSpec((1, ROWS8, LANES), lambda n, p, pt: (n, 0, 0)),
                  pl.BlockSpec((rows, npg * PAGE_SIZE), lambda n, p, pt: (0, p)),
                  pl.BlockSpec((rows, LANES), lambda n, p, pt: (0, 0))]
                 + [page_spec(j) for j in range(npg)],
        out_specs=pl.BlockSpec((1, ROWS8, w), lambda n, p, pt: (n, 0, 0)),
        scratch_shapes=[pltpu.VMEM((rows, w), BF16), pltpu.VMEM((PAGE_SIZE, 2 * w), F32),
                        pltpu.VMEM((rows, LANES), F32), pltpu.VMEM((rows, LANES), F32), pltpu.VMEM((rows, w), F32)],
    )
    return pl.pallas_call(
        functools.partial(_dsa_step_attn_kernel, w=w, npg=npg),
        grid_spec=grid_spec,
        out_shape=jax.ShapeDtypeStruct((nb, ROWS8, w), BF16),
        compiler_params=_cparams("arbitrary", "arbitrary"),
        name="dsa_step_attend",
    )(page_table.reshape(-1), dq8, kv_new8, mask_past, mask_new, bp, bn, *([kv_t] * npg))


def _pad_rows(x3, rows):
    return jnp.pad(x3, ((0, 0), (0, rows - x3.shape[1]), (0, 0)))


def _trunk(x, is_step, conv_state, win_states, gla_state, dsa_kv, dsa_kidx, page_table, wts):
    (norm_mix, norm_ffn, w_in_ab, conv_w, conv_b, conv_ln_g, conv_ln_b, qn_ab, kn_ab, w_out_ab,
     w_in_cd, gla_wa2, gla_ba, gla_norm, qn_cd, kn_cd, w_out_cd, rel_bias, w_g, w_u, w_d) = wts
    nb, l, d = x.shape
    t = nb * l
    depth = norm_mix.shape[0]
    c = conv_w.shape[2]
    wq = B_HPG * HEAD_DIM
    x2 = x.reshape(t, d)
    conv_new, gla_new, kv_new, kidx_new = [], [], [], []
    win_new = [[] for _ in WINDOWS]
    for layer in range(depth):
        i = layer // 2
        if layer % 2 == 0:
            z = norm_matmul(x2, norm_mix[layer], w_in_ab[i])
            hist = conv_state[i] if is_step else jnp.zeros((nb, CONV_WIDTH - 1, c), F32)
            a_out, c_st = conv_module(z.reshape(nb, l, -1), hist, conv_w[i], conv_b[i], conv_ln_g[i], conv_ln_b[i])
            conv_new.append(c_st)
            if is_step:
                qs, kvs = ab_qkv(z, qn_ab[i], kn_ab[i], 2 * c, t, (1,) * len(WINDOWS))
            else:
                qs, kvs = ab_qkv(z, qn_ab[i], kn_ab[i], 2 * c, l, DILATIONS)
            outs, lses = [], []
            for g, window in enumerate(WINDOWS):
                tab_g = rel_bias[:, g * B_HPG:(g + 1) * B_HPG]
                if is_step:
                    o, lse, st = win_step(_pad_rows(qs[g].reshape(nb, l, wq), ROWS8),
                                          kvs[g].reshape(nb, l, 2 * wq), win_states[g], i, g, tab_g)
                else:
                    o, lse = win_prompt(qs[g], kvs[g], g, tab_g)
                    keep = min(window, l)
                    st = kvs[g][:, -(keep // DILATIONS[g]):].reshape(nb, keep, 2, B_HPG, HEAD_DIM)
                outs.append(o.reshape(1, t, wq) if is_step else o)
                lses.append(lse.reshape(1, t, wq) if is_step else lse)
                win_new[g].append(st)
            m2 = ab_merge(outs, lses, t, (1,) * len(WINDOWS)) if is_step else ab_merge(outs, lses, l, DILATIONS)
            m1 = a_out.reshape(t, c)
            wo = w_out_ab[i]
        else:
            z = norm_matmul(x2, norm_mix[layer], w_in_cd[i])
            pre = cd_pre(z, gla_wa2[i], gla_ba[i], qn_cd[i], kn_cd[i], None if is_step else l)
            gdec, dq, kv = pre[:3]
            z3 = z.reshape(nb, l, -1)
            wd = DSA_HEADS * HEAD_DIM
            ik = z3[:, :, CD_COLS["ik"]:CD_COLS["ik"] + IDX_DIM]
            tab_d = rel_bias[:, B_HEADS:]
            if is_step:
                lp = GLA_STEP_ROWS
                o_c, s_c = gla(_pad_rows(z3, lp), _pad_rows(gdec.reshape(nb, l, -1), lp), gla_state[i], gla_norm[i])
                o_c = o_c[:, :l]
                z8 = _pad_rows(z3, ROWS8)
                scores = dsa_step_scores(z8, dsa_kidx, i, page_table)
                mask_p, mask_n = dsa_step_select(scores, z8, l)
                o_d = dsa_step_attend(_pad_rows(dq.reshape(nb, l, wd), ROWS8), _pad_rows(kv.reshape(nb, l, 2 * wd), ROWS8),
                                      dsa_kv, i, page_table, mask_p, mask_n, tab_d, l)[:, :l]
                kv_st = kv.reshape(nb, l, 2, DSA_HEADS, HEAD_DIM)
                ki_st = ik
            else:
                s0 = jnp.zeros((nb, GLA_HEADS, GLA_DK, GLA_DV), F32)
                o_c, s_c = gla(z3, gdec.reshape(nb, l, -1), s0, gla_norm[i])
                kb, vt, smt = pre[3:]
                mask = dsa_select(z3, smt)
                o_d = dsa_attend(dq.reshape(nb, l, wd), kb.reshape(nb, l, wd), vt, mask, tab_d)
                n_pg = l // PAGE_SIZE
                kv_st = kv.reshape(nb, n_pg, PAGE_SIZE, 2, DSA_HEADS, HEAD_DIM)
                ki_st = ik.reshape(nb, n_pg, PAGE_SIZE, IDX_DIM)
            gla_new.append(s_c)
            kv_new.append(kv_st)
            kidx_new.append(ki_st)
            m1, m2 = o_c.reshape(t, -1), o_d.reshape(t, -1)
            wo = w_out_cd[i]
        d1 = m1.shape[1]
        x2 = mix_ffn(x2, m1, m2, wo[:d1], wo[d1:], norm_ffn[layer], w_g[layer], w_u[layer], w_d[layer])
    wins = [jnp.stack(ws) for ws in win_new]
    if is_step:
        wins = [ws.transpose(0, 1, 5, 2, 3, 4) for ws in wins]
    states = (jnp.stack(conv_new), wins[0], wins[1], wins[2],
              jnp.stack(gla_new), jnp.stack(kv_new), jnp.stack(kidx_new))
    return x2.reshape(nb, l, d), states


def kernel(x_prompt, x_sample, state_conv, cache_win128, cache_win512, cache_win2048, state_gla, cache_dsa_kv, cache_dsa_kidx, page_table, norm_mix, norm_ffn, w_in_ab, conv_w, conv_b, conv_ln_g, conv_ln_b, qn_ab, kn_ab, w_out_ab, w_in_cd, gla_wa2, gla_ba, gla_norm, qn_cd, kn_cd, w_out_cd, rel_bias, w_ffn_gate, w_ffn_up, w_ffn_down):
    bf = lambda a: a.astype(BF16)
    w_in_cd_r = jnp.stack([cd_reorder_w(w_in_cd[i]) for i in range(w_in_cd.shape[0])])
    wts = (norm_mix, norm_ffn, bf(w_in_ab), conv_w, conv_b, conv_ln_g, conv_ln_b, qn_ab, kn_ab, bf(w_out_ab),
           bf(w_in_cd_r), gla_wa2, gla_ba, gla_norm, qn_cd, kn_cd, bf(w_out_cd), rel_bias,
           bf(w_ffn_gate), bf(w_ffn_up), bf(w_ffn_down))
    y_p, sp = _trunk(x_prompt, False, None, None, None, None, None, None, wts)
    wins_t = tuple(cw.transpose(0, 1, 3, 4, 5, 2) for cw in (cache_win128, cache_win512, cache_win2048))
    y_s, ss = _trunk(x_sample, True, state_conv, wins_t, state_gla,
                     cache_dsa_kv.transpose(0, 1, 3, 4, 5, 2), cache_dsa_kidx.transpose(0, 1, 3, 2), page_table, wts)
    conv_p, win128_p, win512_p, win2048_p, gla_p, dsa_kv_p, dsa_kidx_p = sp
    conv_s, win128_s, win512_s, win2048_s, gla_s, dsa_kv_s, dsa_kidx_s = ss
    return (y_p, y_s, conv_p, conv_s, win128_p, win128_s, win512_p, win512_s, win2048_p, win2048_s,
            gla_p, gla_s, dsa_kv_p, dsa_kv_s, dsa_kidx_p, dsa_kidx_s)
```

```python
import functools
import math

import numpy as np
import jax
import jax.numpy as jnp
from jax import lax
from jax.experimental import pallas as pl
from jax.experimental.pallas import tpu as pltpu

F32 = jnp.float32
BF16 = jnp.bfloat16
I32 = jnp.int32

EPS = 1e-6
NEG = -1e30
LOG2E = math.log2(math.e)
F32_MIN_NORMAL = 2.0 ** -126
INT_MIN = -(2 ** 31)

V7X_VMEM_BYTES = 64 * 1024 * 1024
VMEM_LIMIT = V7X_VMEM_BYTES - 12 * 1024 * 1024
LANES = 128

HEAD_DIM = 64
CONV_WIDTH = 31
WINDOWS = (128, 512, 2048)
DILATIONS = (1, 4, 16)
B_HPG = 4
B_HEADS = B_HPG * len(WINDOWS)
SW_BLOCK = 128
GLA_HEADS = 4
GLA_DK = 64
GLA_DV = 128
GLA_GATE_RANK = 16
GLA_TAU = 16.0
DSA_HEADS = 8
IDX_HEADS = 4
IDX_DIM = 64
DSA_TOPK_MAX = 256
PAGE_SIZE = 128
REL_BUCKETS = 32
REL_MAX_DIST = 2048

GLA_CHUNK = 128
GLA_STEP_ROWS = 64
DSA_QB = 256
DSA_BT = 128
DSA_SQ = 256
DSA_CK = 512
HIST_PAD = 32


def _cparams(*sem):
    return pltpu.CompilerParams(dimension_semantics=sem, vmem_limit_bytes=VMEM_LIMIT)


def _nt(a, b):
    return lax.dot_general(a, b, (((1,), (1,)), ((), ())), preferred_element_type=F32)


def _dot(a, b):
    return jnp.dot(a, b, preferred_element_type=F32)


def _split2(x):
    hi = x.astype(BF16)
    lo = (x - hi.astype(F32)).astype(BF16)
    return hi, lo


def _split3(x):
    hi = x.astype(BF16)
    r = x - hi.astype(F32)
    mid = r.astype(BF16)
    lo = (r - mid.astype(F32)).astype(BF16)
    return hi, mid, lo


def _rel_bucket_np(dist):
    n = np.maximum(np.asarray(dist, np.int64), 0)
    max_exact = REL_BUCKETS // 2
    nf = np.maximum(n, max_exact).astype(np.float32)
    large = max_exact + (np.log(nf / np.float32(max_exact)) / np.float32(math.log(REL_MAX_DIST / max_exact))
                         * np.float32(REL_BUCKETS - max_exact)).astype(np.int32)
    large = np.minimum(large, REL_BUCKETS - 1)
    return np.where(n < max_exact, n, large).astype(np.int32)


def _bias_lookup_kernel(tab_ref, idx_ref, o_ref, *, nh):
    idx = idx_ref[...]
    for h in range(nh):
        acc = jnp.full(idx.shape, NEG, F32)
        for b in range(REL_BUCKETS):
            acc = jnp.where(idx == b, tab_ref[b, h], acc)
        o_ref[h] = acc


def bias_lookup(tab, idx_np):
    r, c = idx_np.shape
    nh = tab.shape[1]
    tr = 8 if (c >= 2048 and r % 8 == 0) else (128 if r % 128 == 0 else r)
    return pl.pallas_call(
        functools.partial(_bias_lookup_kernel, nh=nh),
        grid=(r // tr,),
        in_specs=[pl.BlockSpec(memory_space=pltpu.SMEM),
                  pl.BlockSpec((tr, c), lambda i: (i, 0))],
        out_specs=pl.BlockSpec((nh, tr, c), lambda i: (0, i, 0)),
        out_shape=jax.ShapeDtypeStruct((nh, r, c), F32),
        compiler_params=_cparams("parallel"),
        name="bias_lookup",
    )(tab, jnp.asarray(idx_np.astype(np.int32)))


def _norm_matmul_kernel(x_ref, g_ref, w_ref, o_ref):
    x = x_ref[...]
    y = x * lax.rsqrt(jnp.mean(x * x, axis=-1, keepdims=True) + EPS) * g_ref[...]
    o_ref[...] = _dot(y.astype(BF16), w_ref[...])


def norm_matmul(x, g, w):
    t, d = x.shape
    n = w.shape[1]
    tm = min(t, 512)
    return pl.pallas_call(
        _norm_matmul_kernel,
        grid=(t // tm,),
        in_specs=[pl.BlockSpec((tm, d), lambda i: (i, 0)),
                  pl.BlockSpec((1, d), lambda i: (0, 0)),
                  pl.BlockSpec((d, n), lambda i: (0, 0))],
        out_specs=pl.BlockSpec((tm, n), lambda i: (i, 0)),
        out_shape=jax.ShapeDtypeStruct((t, n), F32),
        compiler_params=_cparams("parallel"),
        name="norm_matmul",
    )(x, g.reshape(1, d), w)


def _mix_ffn_kernel(x_ref, m1_ref, m2_ref, wo1_ref, wo2_ref, g_ref, wg_ref, wu_ref, wd_ref, o_ref, *, th):
    x1 = x_ref[...] + _dot(m1_ref[...], wo1_ref[...]) + _dot(m2_ref[...], wo2_ref[...])
    hf = (x1 * lax.rsqrt(jnp.mean(x1 * x1, axis=-1, keepdims=True) + EPS) * g_ref[...]).astype(BF16)
    acc = x1
    for c0 in range(0, wg_ref.shape[1], th):
        a = _dot(hf, wg_ref[:, c0:c0 + th])
        u = _dot(hf, wu_ref[:, c0:c0 + th])
        act = (a * jax.nn.sigmoid(a) * u).astype(BF16)
        acc = acc + _dot(act, wd_ref[c0:c0 + th, :])
    o_ref[...] = acc


def mix_ffn(x, m1, m2, wo1, wo2, g, wg, wu, wd):
    t, d = x.shape
    hid = wg.shape[1]
    tm = min(t, 512)
    th = hid // 4 if hid % (4 * LANES) == 0 else hid
    d1, d2 = m1.shape[1], m2.shape[1]
    resident = lambda shape: pl.BlockSpec(shape, lambda i: (0, 0), pipeline_mode=pl.Buffered(1))
    return pl.pallas_call(
        functools.partial(_mix_ffn_kernel, th=th),
        grid=(t // tm,),
        in_specs=[pl.BlockSpec((tm, d), lambda i: (i, 0)),
                  pl.BlockSpec((tm, d1), lambda i: (i, 0)),
                  pl.BlockSpec((tm, d2), lambda i: (i, 0)),
                  resident((d1, d)), resident((d2, d)), resident((1, d)),
                  resident((d, hid)), resident((d, hid)), resident((hid, d))],
        out_specs=pl.BlockSpec((tm, d), lambda i: (i, 0)),
        out_shape=jax.ShapeDtypeStruct((t, d), F32),
        compiler_params=_cparams("parallel"),
        name="mix_ffn",
    )(x, m1, m2, wo1, wo2, g.reshape(1, d), wg, wu, wd)


def _conv_kernel(z_ref, hist_ref, w_ref, b_ref, lg_ref, lb_ref, o_ref, tail_ref, uh_ref, us_ref, *, ts, c):
    t = pl.program_id(1)
    sl = 8

    @pl.when(t == 0)
    def _():
        uh_ref[0:HIST_PAD, :] = hist_ref[0]

    z = z_ref[0]
    u = z[:, 0:c] * jax.nn.sigmoid(z[:, c:2 * c])
    uh_ref[HIST_PAD:HIST_PAD + ts, :] = u
    span = HIST_PAD + ts - sl
    for s in range(1, sl):
        us_ref[s - 1, 0:span, :] = uh_ref[s:s + span, :]
    acc = jnp.zeros((ts, c), F32) + b_ref[...]
    off = HIST_PAD - (CONV_WIDTH - 1)
    for j in range(CONV_WIDTH):
        a, s = divmod(off + j, sl)
        rows = uh_ref[a * sl:a * sl + ts, :] if s == 0 else us_ref[s - 1, a * sl:a * sl + ts, :]
        acc = acc + w_ref[j:j + 1, :] * rows
    mu = jnp.mean(acc, axis=-1, keepdims=True)
    var = jnp.mean(jnp.square(acc - mu), axis=-1, keepdims=True)
    yn = (acc - mu) * lax.rsqrt(var + EPS) * lg_ref[...] + lb_ref[...]
    o_ref[0] = (yn * jax.nn.sigmoid(yn)).astype(BF16)
    tail = uh_ref[ts:ts + HIST_PAD, :]
    uh_ref[0:HIST_PAD, :] = tail
    tail_ref[0] = tail


def conv_module(z3, hist, conv_w, conv_b, ln_g, ln_b):
    nb, l, _ = z3.shape
    c = conv_w.shape[1]
    ts = min(l, 512)
    hist_p = jnp.pad(hist, ((0, 0), (HIST_PAD - (CONV_WIDTH - 1), 0), (0, 0)))
    w_p = jnp.pad(conv_w, ((0, HIST_PAD - CONV_WIDTH), (0, 0)))
    a_out, tail = pl.pallas_call(
        functools.partial(_conv_kernel, ts=ts, c=c),
        grid=(nb, l // ts),
        in_specs=[pl.BlockSpec((1, ts, 2 * c), lambda n, t: (n, t, 0)),
                  pl.BlockSpec((1, HIST_PAD, c), lambda n, t: (n, 0, 0)),
                  pl.BlockSpec((HIST_PAD, c), lambda n, t: (0, 0)),
                  pl.BlockSpec((1, c), lambda n, t: (0, 0)),
                  pl.BlockSpec((1, c), lambda n, t: (0, 0)),
                  pl.BlockSpec((1, c), lambda n, t: (0, 0))],
        out_specs=[pl.BlockSpec((1, ts, c), lambda n, t: (n, t, 0)),
                   pl.BlockSpec((1, HIST_PAD, c), lambda n, t: (n, 0, 0))],
        out_shape=[jax.ShapeDtypeStruct((nb, l, c), BF16),
                   jax.ShapeDtypeStruct((nb, HIST_PAD, c), F32)],
        scratch_shapes=[pltpu.VMEM((HIST_PAD + ts, c), F32), pltpu.VMEM((7, HIST_PAD + ts, c), F32)],
        compiler_params=_cparams("parallel", "arbitrary"),
        name="conv_module",
    )(z3, hist_p, w_p, conv_b.reshape(1, c), ln_g.reshape(1, c), ln_b.reshape(1, c))
    return a_out, tail[:, HIST_PAD - (CONV_WIDTH - 1):]


def _seg_mean_sq(x, seg_ref):
    hi, lo = _split2(x * x)
    seg = seg_ref[...]
    return (_dot(hi, seg) + _dot(lo, seg)) * (1.0 / HEAD_DIM)


def _head_norm(x, g, seg_ref):
    return x * lax.rsqrt(_seg_mean_sq(x, seg_ref) + EPS) * g


def _seg_matrix(width):
    idx = np.arange(width) // HEAD_DIM
    return jnp.asarray((idx[:, None] == idx[None, :]).astype(np.float32), BF16)


def _ab_qkv_kernel(*refs, w, tm, dils):
    ng = len(dils)
    qkv_refs = refs[:3 * ng]
    qg_ref, kg_ref, seg_ref = refs[3 * ng:3 * ng + 3]
    qo_refs = refs[3 * ng + 3:4 * ng + 3]
    kvo_refs = refs[4 * ng + 3:5 * ng + 3]
    qs_ref, kvs_ref = refs[5 * ng + 3:]
    for g, dil in enumerate(dils):
        qn = _head_norm(qkv_refs[g][...], qg_ref[...], seg_ref) * HEAD_DIM ** -0.5
        kn = _head_norm(qkv_refs[ng + g][...], kg_ref[...], seg_ref)
        v = qkv_refs[2 * ng + g][...]
        if dil == 1:
            qo_refs[g][0] = qn.astype(BF16)
            kvo_refs[g][0, :, 0:w] = kn
            kvo_refs[g][0, :, w:2 * w] = v
        else:
            nq, nkv = w // LANES, 2 * w // LANES
            kv = jnp.concatenate([kn, v], axis=1)
            for j in range(nq):
                qs_ref[j] = qn[:, j * LANES:(j + 1) * LANES]
            for j in range(nkv):
                kvs_ref[j] = kv[:, j * LANES:(j + 1) * LANES]
            for r in range(dil):
                rows = pl.ds(r, tm // dil, stride=dil)
                for j in range(nq):
                    qo_refs[g][0, :, r * w + j * LANES:r * w + (j + 1) * LANES] = qs_ref[j, rows, :].astype(BF16)
                for j in range(nkv):
                    kvo_refs[g][0, :, r * 2 * w + j * LANES:r * 2 * w + (j + 1) * LANES] = kvs_ref[j, rows, :]


def ab_qkv(z, qn, kn, col0, seq, dils):
    t = z.shape[0]
    w = B_HPG * HEAD_DIM
    ng = len(dils)
    tm = min(seq, 512)
    nt = seq // tm
    cb = col0 // w
    assert all(tm % (8 * d) == 0 or d == 1 for d in dils)
    col_spec = lambda j: pl.BlockSpec((tm, w), lambda i: (i, cb + j))
    out_specs = ([pl.BlockSpec((1, tm // d, d * w), lambda i: (i // nt, i % nt, 0)) for d in dils]
                 + [pl.BlockSpec((1, tm // d, d * 2 * w), lambda i: (i // nt, i % nt, 0)) for d in dils])
    out_shape = ([jax.ShapeDtypeStruct((t // seq, seq // d, d * w), BF16) for d in dils]
                 + [jax.ShapeDtypeStruct((t // seq, seq // d, d * 2 * w), F32) for d in dils])
    outs = pl.pallas_call(
        functools.partial(_ab_qkv_kernel, w=w, tm=tm, dils=tuple(dils)),
        grid=(t // tm,),
        in_specs=[col_spec(j) for j in range(3 * ng)]
                 + [pl.BlockSpec((1, w), lambda i: (0, 0)),
                    pl.BlockSpec((1, w), lambda i: (0, 0)),
                    pl.BlockSpec((w, w), lambda i: (0, 0))],
        out_specs=out_specs,
        out_shape=out_shape,
        scratch_shapes=[pltpu.VMEM((w // LANES, tm, LANES), F32), pltpu.VMEM((2 * w // LANES, tm, LANES), F32)],
        compiler_params=_cparams("parallel"),
        name="ab_qkv",
    )(*([z] * (3 * ng)), jnp.tile(qn, B_HPG).reshape(1, w), jnp.tile(kn, B_HPG).reshape(1, w), _seg_matrix(w))
    return outs[:ng], outs[ng:]


WIN_BLOCKS = 4


def _win_prompt_kernel(q_ref, kvp_ref, kvc_ref, bias_ref, o_ref, lse_ref, *, w, nblk):
    step = pl.program_id(2)
    col = lax.broadcasted_iota(I32, (SW_BLOCK, 2 * SW_BLOCK), 1)
    first = jnp.logical_and(step == 0, col < SW_BLOCK)
    for j in range(nblk):
        rows = slice(j * SW_BLOCK, (j + 1) * SW_BLOCK)
        q = q_ref[0, rows, :]
        kvc = kvc_ref[0, rows, :]
        kvp = kvp_ref[0] if j == 0 else kvc_ref[0, (j - 1) * SW_BLOCK:j * SW_BLOCK, :]
        for h in range(B_HPG):
            hs = slice(h * HEAD_DIM, (h + 1) * HEAD_DIM)
            vs = slice(w + h * HEAD_DIM, w + (h + 1) * HEAD_DIM)
            k2 = jnp.concatenate([kvp[:, hs], kvc[:, hs]], axis=0).astype(BF16)
            v2 = jnp.concatenate([kvp[:, vs], kvc[:, vs]], axis=0).astype(BF16)
            s = _nt(q[:, hs], k2) + bias_ref[h]
            if j == 0:
                s = jnp.where(first, NEG, s)
            m = jnp.max(s, axis=-1, keepdims=True)
            p = jnp.exp(s - m)
            l = jnp.sum(p, axis=-1, keepdims=True)
            o_ref[0, rows, hs] = _dot(p.astype(BF16), v2) / l
            lse_ref[0, rows, hs] = jnp.broadcast_to(m + jnp.log(l), (SW_BLOCK, HEAD_DIM))


def _win_prompt_bias(tab_g, dil, reach):
    ql = np.arange(SW_BLOCK)[:, None]
    kl = np.arange(2 * SW_BLOCK)[None, :] - SW_BLOCK
    rel = ql - kl
    ok = (rel >= 0) & (rel <= reach)
    return bias_lookup(tab_g, np.where(ok, _rel_bucket_np(rel * dil), -1))


def win_prompt(qv, kvv, g, tab_g):
    w = B_HPG * HEAD_DIM
    dil = DILATIONS[g]
    b, n, _ = qv.shape
    s = n * dil
    nb = n // SW_BLOCK
    assert nb * SW_BLOCK * dil == s
    bias = _win_prompt_bias(tab_g, dil, WINDOWS[g] // dil)
    nblk = math.gcd(WIN_BLOCKS, nb)
    rows = nblk * SW_BLOCK
    o, lse = pl.pallas_call(
        functools.partial(_win_prompt_kernel, w=w, nblk=nblk),
        grid=(b, dil, nb // nblk),
        in_specs=[pl.BlockSpec((1, rows, w), lambda n_, r, k: (n_, k, r)),
                  pl.BlockSpec((1, SW_BLOCK, 2 * w), lambda n_, r, k: (n_, jnp.maximum(k * nblk - 1, 0), r)),
                  pl.BlockSpec((1, rows, 2 * w), lambda n_, r, k: (n_, k, r)),
                  pl.BlockSpec((B_HPG, SW_BLOCK, 2 * SW_BLOCK), lambda n_, r, k: (0, 0, 0))],
        out_specs=[pl.BlockSpec((1, rows, w), lambda n_, r, k: (n_, k, r)),
                   pl.BlockSpec((1, rows, w), lambda n_, r, k: (n_, k, r))],
        out_shape=[jax.ShapeDtypeStruct((b, n, dil * w), F32),
                   jax.ShapeDtypeStruct((b, n, dil * w), F32)],
        compiler_params=_cparams("parallel", "parallel", "arbitrary"),
        name=f"win_prompt_g{g}",
    )(qv, kvv, kvv, bias)
    return o, lse


ROWS8 = 8


def _win_step_kernel(q_ref, kvn_ref, newt_ref, buf_ref, tb_ref, tn_ref, o_ref, lse_ref, st_ref,
                     qbd_ref, newpage_ref, *, w, wb, l):
    n = pl.program_id(0)

    @pl.when(n == 0)
    def _():
        newpage_ref[...] = jnp.zeros(newpage_ref.shape, F32)

    q = q_ref[0]
    lane_h = lax.broadcasted_iota(I32, (ROWS8, w), 1) // HEAD_DIM
    for h in range(B_HPG):
        qbd_ref[h * ROWS8:(h + 1) * ROWS8, :] = jnp.where(lane_h == h, q, jnp.zeros_like(q))
    newpage_ref[0:l, :] = kvn_ref[0]
    qbd = qbd_ref[...]
    newp = newpage_ref[...]
    buf = buf_ref[0, 0]
    kt = buf[0].reshape(w, wb).astype(BF16)
    vt = buf[1].reshape(w, wb).astype(BF16)
    s1 = _dot(qbd, kt) + tb_ref[...]
    s2 = _nt(qbd, newp[:, 0:w].astype(BF16)) + tn_ref[...]
    m = jnp.maximum(jnp.max(s1, axis=-1, keepdims=True), jnp.max(s2, axis=-1, keepdims=True))
    p1 = jnp.exp(s1 - m)
    p2 = jnp.exp(s2 - m)
    den = jnp.sum(p1, axis=-1, keepdims=True) + jnp.sum(p2, axis=-1, keepdims=True)
    num = _nt(p1.astype(BF16), vt) + _dot(p2.astype(BF16), newp[:, w:2 * w].astype(BF16))
    o = num / den
    lse = m + jnp.log(den)
    for h in range(B_HPG):
        rs = slice(h * ROWS8, (h + 1) * ROWS8)
        hs = slice(h * HEAD_DIM, (h + 1) * HEAD_DIM)
        o_ref[0, :, hs] = o[rs, hs]
        lse_ref[0, :, hs] = jnp.broadcast_to(lse[rs], (ROWS8, HEAD_DIM))
    rolled = pltpu.roll(buf.reshape(2 * w, wb), wb - l, 1)
    lane = lax.broadcasted_iota(I32, (2 * w, LANES), 1)
    last = jnp.where(lane >= LANES - l, newt_ref[0], rolled[:, wb - LANES:wb])
    if wb > LANES:
        st_ref[0, :, :, :, 0:wb - LANES] = rolled[:, 0:wb - LANES].reshape(2, B_HPG, HEAD_DIM, wb - LANES)
    st_ref[0, :, :, :, wb - LANES:wb] = last.reshape(2, B_HPG, HEAD_DIM, LANES)


def _win_step_bias(tab_g, dil, window, wb, l):
    lq = np.minimum(np.arange(ROWS8), l - 1)[:, None]
    dist_b = wb + lq - np.arange(wb)[None, :]
    ok_b = (dist_b % dil == 0) & (dist_b // dil <= window // dil)
    cols = np.arange(LANES)[None, :]
    dist_n = lq - cols
    ok_n = (cols < l) & (dist_n >= 0) & (dist_n % dil == 0) & (dist_n // dil <= window // dil)
    rows = B_HPG * ROWS8
    tb = bias_lookup(tab_g, np.where(ok_b, _rel_bucket_np(dist_b), -1)).reshape(rows, wb)
    tn = bias_lookup(tab_g, np.where(ok_n, _rel_bucket_np(dist_n), -1)).reshape(rows, LANES)
    return tb, tn


def win_step(q8, kv_new, cache_t, layer, g, tab_g):
    nb = q8.shape[0]
    l = kv_new.shape[1]
    w = B_HPG * HEAD_DIM
    wb = cache_t.shape[-1]
    window, dil = WINDOWS[g], DILATIONS[g]
    assert wb == window, "the step kernel keeps a full window of rows"
    tb, tn = _win_step_bias(tab_g, dil, window, wb, l)
    rows = B_HPG * ROWS8
    new_t = jnp.pad(kv_new.transpose(0, 2, 1), ((0, 0), (0, 0), (LANES - l, 0)))
    o, lse, st = pl.pallas_call(
        functools.partial(_win_step_kernel, w=w, wb=wb, l=l),
        grid=(nb,),
        in_specs=[pl.BlockSpec((1, ROWS8, w), lambda n: (n, 0, 0)),
                  pl.BlockSpec((1, l, 2 * w), lambda n: (n, 0, 0)),
                  pl.BlockSpec((1, 2 * w, LANES), lambda n: (n, 0, 0)),
                  pl.BlockSpec((1, 1, 2, B_HPG, HEAD_DIM, wb), lambda n: (layer, n, 0, 0, 0, 0)),
                  pl.BlockSpec((rows, wb), lambda n: (0, 0)),
                  pl.BlockSpec((rows, LANES), lambda n: (0, 0))],
        out_specs=[pl.BlockSpec((1, ROWS8, w), lambda n: (n, 0, 0)),
                   pl.BlockSpec((1, ROWS8, w), lambda n: (n, 0, 0)),
                   pl.BlockSpec((1, 2, B_HPG, HEAD_DIM, wb), lambda n: (n, 0, 0, 0, 0))],
        out_shape=[jax.ShapeDtypeStruct((nb, ROWS8, w), F32),
                   jax.ShapeDtypeStruct((nb, ROWS8, w), F32),
                   jax.ShapeDtypeStruct((nb, 2, B_HPG, HEAD_DIM, wb), F32)],
        scratch_shapes=[pltpu.VMEM((rows, w), BF16), pltpu.VMEM((LANES, 2 * w), F32)],
        compiler_params=_cparams("arbitrary"),
        name=f"win_step_g{g}",
    )(q8, kv_new, new_t, cache_t, tb, tn)
    return o[:, :l], lse[:, :l], st


def _ab_merge_kernel(*refs, w, tm, dils):
    ng = len(dils)
    out_ref, stage_ref = refs[2 * ng], refs[2 * ng + 1]
    nl = w // LANES

    def token_rows(ref, dil):
        if dil == 1:
            return ref[0]
        for r in range(dil):
            for j in range(nl):
                stage_ref[j, pl.ds(r, tm // dil, stride=dil), :] = ref[0, :, r * w + j * LANES:r * w + (j + 1) * LANES]
        return jnp.concatenate([stage_ref[j] for j in range(nl)], axis=1)

    os_ = [token_rows(refs[g], dils[g]) for g in range(ng)]
    ls_ = [token_rows(refs[ng + g], dils[g]) for g in range(ng)]
    m = functools.reduce(jnp.maximum, ls_)
    es = [jnp.exp(l - m) for l in ls_]
    num = functools.reduce(lambda x, y: x + y, [e * o for e, o in zip(es, os_)])
    den = functools.reduce(lambda x, y: x + y, es)
    out_ref[...] = (num / den).astype(BF16)


def ab_merge(outs, lses, seq, dils):
    w = B_HPG * HEAD_DIM
    t = outs[0].shape[0] * seq
    tm = min(seq, 512)
    nt = seq // tm
    specs = [pl.BlockSpec((1, tm // d, d * w), lambda i: (i // nt, i % nt, 0)) for d in dils]
    return pl.pallas_call(
        functools.partial(_ab_merge_kernel, w=w, tm=tm, dils=tuple(dils)),
        grid=(t // tm,),
        in_specs=specs + specs,
        out_specs=pl.BlockSpec((tm, w), lambda i: (i, 0)),
        out_shape=jax.ShapeDtypeStruct((t, w), BF16),
        scratch_shapes=[pltpu.VMEM((w // LANES, tm, LANES), F32)],
        compiler_params=_cparams("parallel"),
        name="ab_merge",
    )(*outs, *lses)


CD_COLS = dict(cq=0, ck=256, cv=512, cgate=1024, dq=1536, dk=2048, dv=2560, iq=3072, ik=3328, clr=3392, iw=3408)
CD_PAD = 3456
SMALL_BLOCK = CD_COLS["ik"] // LANES
SM_IK, SM_CLR, SM_IW = 0, CD_COLS["clr"] - CD_COLS["ik"], CD_COLS["iw"] - CD_COLS["ik"]


def cd_reorder_w(w_in):
    sizes = (256, 256, 512, 16, 512, 512, 512, 512, 256, 64, 4)
    names = ("cq", "ck", "cv", "clr", "cgate", "dq", "dk", "dv", "iq", "ik", "iw")
    starts = np.concatenate([[0], np.cumsum(sizes)[:-1]])
    out = jnp.zeros((w_in.shape[0], CD_PAD), w_in.dtype)
    for nm, st, sz in zip(names, starts, sizes):
        out = lax.dynamic_update_slice(out, w_in[:, st:st + sz], (0, CD_COLS[nm]))
    return out


def _cd_pre_kernel(d_ref, sm_ref, wa_ref, ba_ref, qg_ref, kg_ref, seg_ref, g_ref, dq_ref, kv_ref, *t_refs, w):
    d = d_ref[...]
    sm = sm_ref[...]
    clr = sm[:, SM_CLR:SM_CLR + GLA_GATE_RANK]
    wa = wa_ref[...]
    pre = ba_ref[...]
    for part in _split3(clr):
        pre = pre + _dot(part, wa[0]) + _dot(part, wa[1])
    lsig = jnp.minimum(pre, 0.0) - jnp.log(1.0 + jnp.exp(-jnp.abs(pre)))
    g_ref[...] = lsig * (1.0 / GLA_TAU)
    q_scale = HEAD_DIM ** -0.5 * (LOG2E if t_refs else 1.0)
    dq_ref[...] = (_head_norm(d[:, 0:w], qg_ref[...], seg_ref) * q_scale).astype(BF16)
    kn = _head_norm(d[:, w:2 * w], kg_ref[...], seg_ref)
    v = d[:, 2 * w:3 * w]
    kv_ref[:, 0:w] = kn
    kv_ref[:, w:2 * w] = v
    if t_refs:
        kb_ref, vt_ref, smt_ref = t_refs
        kb_ref[...] = kn.astype(BF16)
        vt_ref[0] = v.T.astype(BF16)
        smt_ref[0] = sm.T


def cd_pre(z, wa2, ba, qn, kn, seq=None):
    t = z.shape[0]
    w = DSA_HEADS * HEAD_DIM
    gw = GLA_HEADS * GLA_DK
    tm = min(t, 512)
    wa_hi, wa_lo = _split2(wa2)
    out_specs = [pl.BlockSpec((tm, gw), lambda i: (i, 0)),
                 pl.BlockSpec((tm, w), lambda i: (i, 0)),
                 pl.BlockSpec((tm, 2 * w), lambda i: (i, 0))]
    out_shape = [jax.ShapeDtypeStruct((t, gw), F32),
                 jax.ShapeDtypeStruct((t, w), BF16),
                 jax.ShapeDtypeStruct((t, 2 * w), F32)]
    if seq is not None:
        nt = seq // tm
        out_specs += [pl.BlockSpec((tm, w), lambda i: (i, 0)),
                      pl.BlockSpec((1, w, tm), lambda i: (i // nt, 0, i % nt)),
                      pl.BlockSpec((1, LANES, tm), lambda i: (i // nt, 0, i % nt))]
        out_shape += [jax.ShapeDtypeStruct((t, w), BF16),
                      jax.ShapeDtypeStruct((t // seq, w, seq), BF16),
                      jax.ShapeDtypeStruct((t // seq, LANES, seq), F32)]
    return pl.pallas_call(
        functools.partial(_cd_pre_kernel, w=w),
        grid=(t // tm,),
        in_specs=[pl.BlockSpec((tm, 3 * w), lambda i: (i, CD_COLS["dq"] // (3 * w))),
                  pl.BlockSpec((tm, LANES), lambda i: (i, SMALL_BLOCK)),
                  pl.BlockSpec((2, GLA_GATE_RANK, gw), lambda i: (0, 0, 0)),
                  pl.BlockSpec((1, gw), lambda i: (0, 0)),
                  pl.BlockSpec((1, w), lambda i: (0, 0)),
                  pl.BlockSpec((1, w), lambda i: (0, 0)),
                  pl.BlockSpec((w, w), lambda i: (0, 0))],
        out_specs=out_specs,
        out_shape=out_shape,
        compiler_params=_cparams("parallel"),
        name="cd_pre",
    )(z, z, jnp.stack([wa_hi, wa_lo]), ba.reshape(1, gw), jnp.tile(qn, DSA_HEADS).reshape(1, w),
      jnp.tile(kn, DSA_HEADS).reshape(1, w), _seg_matrix(w))


def _gla_consts(c):
    levels = []
    s = c // 2
    while s >= 1:
        levels.append(s)
        s //= 2
    i = np.arange(c)
    tri = (i[None, :] <= i[:, None]).astype(np.float32)
    mats = [tri]
    masks = []
    for s in levels:
        ref = (i // (2 * s)) * (2 * s) + s - 1
        r = (i[None, :] <= ref[:, None]).astype(np.float32)
        mats.append(tri - r)
        same = (i[:, None] // (2 * s)) == (i[None, :] // (2 * s))
        masks.append(same & ((i[:, None] % (2 * s)) >= s) & ((i[None, :] % (2 * s)) < s))
    masks.append(i[:, None] == i[None, :])
    mstack = np.concatenate(mats, axis=0)
    mask = np.stack([np.tile(m.astype(np.float32), (1, GLA_HEADS)) for m in masks])
    return jnp.asarray(mstack, BF16), jnp.asarray(mask, F32), len(levels)


def _gla_kernel(qk_ref, v_ref, gate_ref, g_ref, s0_ref, mstack_ref, mask_ref, gn_ref, eye_ref,
                o_ref, sf_ref, st_ref, *, c, nl):
    t = pl.program_id(1)
    kw = GLA_HEADS * GLA_DK
    vw = GLA_HEADS * GLA_DV
    lane_k = lax.broadcasted_iota(I32, (1, kw), 1) // GLA_DK
    lane_v = lax.broadcasted_iota(I32, (1, vw), 1) // GLA_DV

    @pl.when(t == 0)
    def _():
        rows = []
        for h in range(GLA_HEADS):
            z = jnp.zeros((GLA_DK, GLA_DV), F32)
            rows.append(jnp.concatenate([s0_ref[0, h] if hh == h else z for hh in range(GLA_HEADS)], axis=1))
        st_ref[...] = jnp.concatenate(rows, axis=0).T

    qk = qk_ref[0]
    q = qk[:, 0:kw] * GLA_DK ** -0.5
    k = qk[:, kw:2 * kw]
    v = v_ref[0]
    mstack = mstack_ref[...]
    r = None
    for part in _split3(g_ref[0]):
        d = _dot(mstack, part)
        r = d if r is None else r + d
    b = r[0:c]

    def expand_k(x):
        return jnp.concatenate([jnp.where(lane_k == h, x, 0.0) for h in range(GLA_HEADS)], axis=0).astype(BF16)

    st = st_ref[...]
    o = _nt((q * jnp.exp(b)).astype(BF16), st.astype(BF16))
    a = mask_ref[nl] * _nt(q.astype(BF16), expand_k(k))
    for lv in range(nl):
        sc = jnp.exp(-jnp.abs(r[(lv + 1) * c:(lv + 2) * c]))
        a = a + mask_ref[lv] * _nt((q * sc).astype(BF16), expand_k(k * sc))
    vexp = jnp.concatenate([jnp.where(lane_v == h, v, 0.0) for h in range(GLA_HEADS)], axis=0).astype(BF16)
    o = o + _dot(a.astype(BF16), vexp)

    blast = b[c - 1:c]
    kt = (k * jnp.exp(blast - b)).astype(BF16)
    vt = _nt(eye_ref[...], v.astype(BF16)).astype(BF16)
    upd = _dot(vt, kt)
    row_h = lax.broadcasted_iota(I32, (vw, 1), 0) // GLA_DV
    st_new = st * jnp.exp(blast) + jnp.where(row_h == lane_k, upd, 0.0)
    st_ref[...] = st_new

    gate = gate_ref[0]
    gn = gn_ref[...]
    for h in range(GLA_HEADS):
        hs = slice(h * GLA_DV, (h + 1) * GLA_DV)
        oh = o[:, hs]
        y = oh * lax.rsqrt(jnp.mean(oh * oh, axis=-1, keepdims=True) + EPS) * gn
        gh = gate[:, hs]
        o_ref[0, :, hs] = (y * (gh * jax.nn.sigmoid(gh))).astype(BF16)

    @pl.when(t == pl.num_programs(1) - 1)
    def _():
        s_t = st_new.T
        for h in range(GLA_HEADS):
            sf_ref[0, h] = s_t[h * GLA_DK:(h + 1) * GLA_DK, h * GLA_DV:(h + 1) * GLA_DV]


def gla(z3, g3, s0, gla_norm):
    nb, l, _ = z3.shape
    c = min(l, GLA_CHUNK)
    assert l % c == 0 and c & (c - 1) == 0
    kw = GLA_HEADS * GLA_DK
    vw = GLA_HEADS * GLA_DV
    mstack, mask, nl = _gla_consts(c)
    eye = jnp.eye(vw, dtype=BF16)
    o, sf = pl.pallas_call(
        functools.partial(_gla_kernel, c=c, nl=nl),
        grid=(nb, l // c),
        in_specs=[pl.BlockSpec((1, c, 2 * kw), lambda n, t: (n, t, 0)),
                  pl.BlockSpec((1, c, vw), lambda n, t: (n, t, CD_COLS["cv"] // vw)),
                  pl.BlockSpec((1, c, vw), lambda n, t: (n, t, CD_COLS["cgate"] // vw)),
                  pl.BlockSpec((1, c, kw), lambda n, t: (n, t, 0)),
                  pl.BlockSpec((1, GLA_HEADS, GLA_DK, GLA_DV), lambda n, t: (n, 0, 0, 0)),
                  pl.BlockSpec(mstack.shape, lambda n, t: (0, 0)),
                  pl.BlockSpec(mask.shape, lambda n, t: (0, 0, 0)),
                  pl.BlockSpec((1, GLA_DV), lambda n, t: (0, 0)),
                  pl.BlockSpec((vw, vw), lambda n, t: (0, 0))],
        out_specs=[pl.BlockSpec((1, c, vw), lambda n, t: (n, t, 0)),
                   pl.BlockSpec((1, GLA_HEADS, GLA_DK, GLA_DV), lambda n, t: (n, 0, 0, 0))],
        out_shape=[jax.ShapeDtypeStruct((nb, l, vw), BF16),
                   jax.ShapeDtypeStruct((nb, GLA_HEADS, GLA_DK, GLA_DV), F32)],
        scratch_shapes=[pltpu.VMEM((vw, kw), F32)],
        compiler_params=_cparams("parallel", "arbitrary"),
        name="gla",
    )(z3, z3, z3, g3, s0, mstack, mask, gla_norm.reshape(1, GLA_DV), eye)
    return o, sf


def _sort_key(x):
    bits = pltpu.bitcast(x, I32)
    return jnp.where(bits < 0, (bits ^ 0x7FFFFFFF) + 1, bits)


def _idx_lhs(iq, h):
    hi, lo = _split2(iq[:, h * IDX_DIM:(h + 1) * IDX_DIM])
    return jnp.concatenate([hi, hi, lo, jnp.zeros_like(hi)], axis=1)


def _idx_rhs(ik):
    hi, lo = _split2(ik)
    return jnp.concatenate([hi, lo, hi, jnp.zeros_like(hi)], axis=1)


def _idx_scores(lhs, rhs, wcol):
    sc = None
    for h in range(IDX_HEADS):
        d = jnp.maximum(_nt(lhs[h], rhs) * IDX_DIM ** -0.5, 0.0) * (wcol[h] * IDX_HEADS ** -0.5)
        sc = d if sc is None else sc + d
    return sc


def _dsa_select_kernel(iq_ref, smt_ref, smk_ref, tril_ref, mask_ref, kb3_ref, hi_ref, d1_ref, d0_ref, *, nc, topk):
    i = pl.program_id(1)
    ck = DSA_CK
    qb = DSA_SQ
    bpc = ck // qb
    sub = ck // 8

    @pl.when(i == 0)
    def _():
        for c in range(nc):
            kb3_ref[c] = _idx_rhs(smk_ref[0, c * ck:(c + 1) * ck, SM_IK:SM_IK + IDX_DIM])

    iq = iq_ref[0]
    smt = smt_ref[0]
    lhs = jnp.concatenate([_idx_lhs(iq, h) for h in range(IDX_HEADS)], axis=0)
    wrow = [smt[SM_IW + h:SM_IW + h + 1, :] * (IDX_HEADS ** -0.5 * IDX_DIM ** -0.5) for h in range(IDX_HEADS)]
    nch = (i + bpc) // bpc

    def chunk_planes(c, causal):
        dots = _nt(kb3_ref[c], lhs)
        sc = None
        for h in range(IDX_HEADS):
            t = jnp.maximum(dots[:, h * qb:(h + 1) * qb], 0.0) * wrow[h]
            sc = t if sc is None else sc + t
        sc = jnp.where(jnp.abs(sc) < F32_MIN_NORMAL, 0.0, sc)
        bits = pltpu.bitcast(sc, I32)
        key = bits ^ (lax.shift_right_arithmetic(bits, 31) & 0x7FFFFFFF)
        hi = pltpu.bitcast(bits & -65536, F32)
        if causal:
            drc = lax.broadcasted_iota(I32, (ck, qb), 0) - lax.broadcasted_iota(I32, (ck, qb), 1)
            hi = jnp.where(drc <= i * qb - c * ck, hi, -jnp.inf)
        hi_ref[c] = hi.astype(BF16)
        d1_ref[c] = (lax.shift_right_logical(key, 8) & 0xFF).astype(F32).astype(BF16)
        d0_ref[c] = (key & 0xFF).astype(F32).astype(BF16)

    def score_body(c, carry):
        chunk_planes(c, False)
        return carry

    lax.fori_loop(0, nch - 1, score_body, 0)
    chunk_planes(nch - 1, True)

    one, zero = jnp.ones((), BF16), jnp.zeros((), BF16)

    def count(ref, cand, strict):
        def body(c, acc):
            blk = ref[c]
            m = jnp.where(blk > cand if strict else blk >= cand, one, zero)
            for j in range(8):
                acc = acc + m[j * sub:(j + 1) * sub, :]
            return acc
        acc = lax.fori_loop(0, nch, body, jnp.zeros((sub, qb), BF16))
        return jnp.sum(acc.astype(F32), axis=0, keepdims=True)

    qpos = i * qb + lax.broadcasted_iota(I32, (1, qb), 1)
    kk = jnp.minimum(topk, qpos + 1).astype(F32)

    def hi_value(s16):
        p = jnp.where(s16 >= 0, s16, s16 ^ 0x7FFF) & 0xFFFF
        return pltpu.bitcast(lax.shift_left(p, 16), F32).astype(BF16)

    def descend(ref, nbits, start, as_cand, want, c_start):
        def step(t, carry):
            thr, cge = carry
            cand = thr + lax.shift_left(jnp.int32(1), nbits - 1 - t)
            cnt = count(ref, as_cand(cand), False)
            ok = cnt >= want
            return jnp.where(ok, cand, thr), jnp.where(ok, cnt, cge)
        return lax.fori_loop(0, nbits, step, (jnp.full((1, qb), start, I32), c_start))

    digit = lambda t: t.astype(F32).astype(BF16)

    h16, cge1 = descend(hi_ref, 16, -(2 ** 15), hi_value, kk, jnp.zeros((1, qb), F32))
    hv = hi_value(h16)
    cgt1 = count(hi_ref, hv, True)
    need1 = kk - cgt1

    def restrict(dst_ref, cls_ref, cls_val):
        def body(c, carry):
            dst_ref[c] = jnp.where(cls_ref[c] == cls_val, dst_ref[c], -one)
            return carry
        lax.fori_loop(0, nch, body, 0)

    restrict(d1_ref, hi_ref, hv)
    b1, cge2 = descend(d1_ref, 8, 0, digit, need1, cge1 - cgt1)
    b1v = digit(b1)
    cgt2 = count(d1_ref, b1v, True)
    need2 = need1 - cgt2
    restrict(d0_ref, d1_ref, b1v)
    b0, cge = descend(d0_ref, 8, 0, digit, need2, cge2 - cgt2)
    b0v = digit(b0)
    tied = jnp.max(jnp.where(cge > need2, 1, 0))

    @pl.when(tied > 0)
    def _():
        need3 = need2 - count(d0_ref, b0v, True)
        b0f = b0.astype(F32)

        def tie_body(c, off):
            blk = d0_ref[c].astype(F32)
            eq = blk == b0f
            pref = _dot(tril_ref[...], jnp.where(eq, 1.0, 0.0).astype(BF16))
            drop = jnp.logical_and(eq, pref + off > need3)
            d0_ref[c] = jnp.where(drop, -1.0, blk).astype(BF16)
            return off + pref[ck - 1:ck, :]

        lax.fori_loop(0, nch, tie_body, jnp.zeros((1, qb), F32))

    neg = jnp.full((), NEG, BF16)

    def live_body(c, carry):
        keep3 = jnp.where(d0_ref[c] >= b0v, zero, neg)
        keep2 = jnp.where(d1_ref[c] > b1v, zero, keep3)
        mask_ref[0, 0, c] = jnp.where(hi_ref[c] > hv, zero, keep2)
        return carry

    def dead_body(c, carry):
        mask_ref[0, 0, c] = jnp.full((ck, qb), NEG, BF16)
        return carry

    lax.fori_loop(0, nch, live_body, 0)
    lax.fori_loop(nch, nc, dead_body, 0)


def dsa_select(z3, smt):
    b, s, _ = z3.shape
    nq, nc = s // DSA_SQ, s // DSA_CK
    topk = min(DSA_TOPK_MAX, s // 4)
    tril = jnp.asarray(np.tril(np.ones((DSA_CK, DSA_CK), np.float32)), BF16)
    return pl.pallas_call(
        functools.partial(_dsa_select_kernel, nc=nc, topk=topk),
        grid=(b, nq),
        in_specs=[pl.BlockSpec((1, DSA_SQ, IDX_HEADS * IDX_DIM), lambda n, i: (n, i, CD_COLS["iq"] // (IDX_HEADS * IDX_DIM))),
                  pl.BlockSpec((1, LANES, DSA_SQ), lambda n, i: (n, 0, i)),
                  pl.BlockSpec((1, s, LANES), lambda n, i: (n, 0, SMALL_BLOCK)),
                  pl.BlockSpec((DSA_CK, DSA_CK), lambda n, i: (0, 0))],
        out_specs=pl.BlockSpec((1, 1, nc, DSA_CK, DSA_SQ), lambda n, i: (n, i, 0, 0, 0)),
        out_shape=jax.ShapeDtypeStruct((b, nq, nc, DSA_CK, DSA_SQ), BF16),
        scratch_shapes=[pltpu.VMEM((nc, DSA_CK, 4 * IDX_DIM), BF16)] + [pltpu.VMEM((nc, DSA_CK, DSA_SQ), BF16)] * 3,
        compiler_params=_cparams("parallel", "arbitrary"),
        name="dsa_select",
    )(z3, smt, z3, tril)


def _dsa_bias_tiles(tab_d):
    o = 0
    while _rel_bucket_np(max(o * DSA_BT - (DSA_BT - 1), 0)) < REL_BUCKETS - 1:
        o += 1
    offs = np.arange(o + 1)[:, None, None] * DSA_BT
    d = offs + np.arange(DSA_BT)[None, None, :] - np.arange(DSA_BT)[None, :, None]
    tiles = bias_lookup(tab_d, _rel_bucket_np(d).reshape((o + 1) * DSA_BT, DSA_BT))
    return tiles.reshape(DSA_HEADS, o + 1, DSA_BT, DSA_BT)


def _dsa_attn_kernel(qi_ref, kc_ref, q_ref, k_ref, vt_ref, mask_ref, bt_ref, o_ref, m_ref, l_ref, acc_ref, s_ref,
                     *, n_off):
    s_id = pl.program_id(1)
    i = qi_ref[s_id]
    c = kc_ref[s_id]
    qb, ck = DSA_QB, DSA_CK
    tk, tq = ck // DSA_BT, qb // DSA_BT

    @pl.when(c == 0)
    def _():
        m_ref[...] = jnp.full(m_ref.shape, NEG, F32)
        l_ref[...] = jnp.zeros(l_ref.shape, F32)
        acc_ref[...] = jnp.zeros(acc_ref.shape, F32)

    madd = mask_ref[0, 0, 0].astype(F32)
    q = q_ref[0]
    k = k_ref[0]
    vt = vt_ref[0]
    offs = [[jnp.clip((i * tq + u) - (c * tk + t), 0, n_off - 1) for u in range(tq)] for t in range(tk)]
    m_all = m_ref[...]
    l_all = l_ref[...]
    m_rows, l_rows = [], []
    for h in range(DSA_HEADS):
        hs = slice(h * HEAD_DIM, (h + 1) * HEAD_DIM)
        bias = jnp.concatenate([jnp.concatenate([bt_ref[h, offs[t][u]] for u in range(tq)], axis=1)
                                for t in range(tk)], axis=0)
        s = _nt(k[:, hs], q[:, hs]) + bias + madd
        s_ref[h] = s
        m_rows.append(jnp.maximum(m_all[h:h + 1, :], jnp.max(s, axis=0, keepdims=True)))
    ones = jnp.ones((16, ck), BF16)
    for h in range(DSA_HEADS):
        hs = slice(h * HEAD_DIM, (h + 1) * HEAD_DIM)
        alpha = jnp.exp2(m_all[h:h + 1, :] - m_rows[h])
        p = jnp.exp2(s_ref[h] - m_rows[h]).astype(BF16)
        pv = _dot(jnp.concatenate([vt[hs, :], ones], axis=0), p)
        l_rows.append(alpha * l_all[h:h + 1, :] + pv[HEAD_DIM:HEAD_DIM + 1, :])
        acc_ref[hs, :] = alpha * acc_ref[hs, :] + pv[0:HEAD_DIM, :]
    m_ref[...] = jnp.concatenate(m_rows, axis=0)
    l_ref[...] = jnp.concatenate(l_rows, axis=0)

    @pl.when(c == ((i + 1) * qb - 1) // ck)
    def _():
        inv = 1.0 / l_ref[...]
        ot = jnp.concatenate([acc_ref[h * HEAD_DIM:(h + 1) * HEAD_DIM, :] * inv[h:h + 1, :]
                              for h in range(DSA_HEADS)], axis=0)
        o_ref[0] = ot.T.astype(BF16)


def dsa_attend(dq, kb, vt, mask, tab_d):
    b, s, w = dq.shape
    nq = s // DSA_QB
    last_c = lambda i: ((i + 1) * DSA_QB - 1) // DSA_CK
    qps = DSA_SQ // DSA_QB
    bt = _dsa_bias_tiles(tab_d) * LOG2E
    n_off = bt.shape[1]
    qi = np.concatenate([np.full(last_c(i) + 1, i) for i in range(nq)]).astype(np.int32)
    kc = np.concatenate([np.arange(last_c(i) + 1) for i in range(nq)]).astype(np.int32)
    grid_spec = pltpu.PrefetchScalarGridSpec(
        num_scalar_prefetch=2,
        grid=(b, len(qi)),
        in_specs=[pl.BlockSpec((1, DSA_QB, w), lambda n, t, qi_, kc_: (n, qi_[t], 0)),
                  pl.BlockSpec((1, DSA_CK, w), lambda n, t, qi_, kc_: (n, kc_[t], 0)),
                  pl.BlockSpec((1, w, DSA_CK), lambda n, t, qi_, kc_: (n, 0, kc_[t])),
                  pl.BlockSpec((1, 1, 1, DSA_CK, DSA_QB),
                               lambda n, t, qi_, kc_: (n, qi_[t] // qps, kc_[t], 0, qi_[t] % qps)),
                  pl.BlockSpec(bt.shape, lambda n, t, qi_, kc_: (0, 0, 0, 0))],
        out_specs=pl.BlockSpec((1, DSA_QB, w), lambda n, t, qi_, kc_: (n, qi_[t], 0)),
        scratch_shapes=[pltpu.VMEM((DSA_HEADS, DSA_QB), F32), pltpu.VMEM((DSA_HEADS, DSA_QB), F32),
                        pltpu.VMEM((w, DSA_QB), F32), pltpu.VMEM((DSA_HEADS, DSA_CK, DSA_QB), F32)],
    )
    return pl.pallas_call(
        functools.partial(_dsa_attn_kernel, n_off=n_off),
        grid_spec=grid_spec,
        out_shape=jax.ShapeDtypeStruct((b, s, w), BF16),
        compiler_params=_cparams("parallel", "arbitrary"),
        name="dsa_attend",
    )(jnp.asarray(qi), jnp.asarray(kc), dq, kb, vt, mask, bt)


SCORE_PAGES = 32
ATTN_PAGES = 16


def _dsa_step_scores_kernel(pt_ref, iq_ref, sm_ref, *rest, npg):
    k_refs, o_ref = rest[:npg], rest[npg]
    iq = iq_ref[0]
    sm = sm_ref[0]
    lhs = jnp.concatenate([_idx_lhs(iq, h) for h in range(IDX_HEADS)], axis=0)
    wcol = [sm[:, SM_IW + h:SM_IW + h + 1] * IDX_HEADS ** -0.5 for h in range(IDX_HEADS)]
    for j in range(npg):
        hi, lo = _split2(k_refs[j][0, 0])
        d = _dot(lhs, jnp.concatenate([hi, lo, hi, jnp.zeros_like(hi)], axis=0))
        sc = None
        for h in range(IDX_HEADS):
            t = jnp.maximum(d[h * ROWS8:(h + 1) * ROWS8] * IDX_DIM ** -0.5, 0.0) * wcol[h]
            sc = t if sc is None else sc + t
        o_ref[0, :, j * PAGE_SIZE:(j + 1) * PAGE_SIZE] = sc


def dsa_step_scores(z8, kidx_t, layer, page_table):
    nb = z8.shape[0]
    n_pages = page_table.shape[1]
    npg = math.gcd(SCORE_PAGES, n_pages)
    iqw = IDX_HEADS * IDX_DIM

    def page_spec(j):
        return pl.BlockSpec((1, 1, IDX_DIM, PAGE_SIZE),
                            lambda n, p, pt: (layer, pt[n * n_pages + p * npg + j], 0, 0))

    grid_spec = pltpu.PrefetchScalarGridSpec(
        num_scalar_prefetch=1,
        grid=(nb, n_pages // npg),
        in_specs=[pl.BlockSpec((1, ROWS8, iqw), lambda n, p, pt: (n, 0, CD_COLS["iq"] // iqw)),
                  pl.BlockSpec((1, ROWS8, LANES), lambda n, p, pt: (n, 0, SMALL_BLOCK))]
                 + [page_spec(j) for j in range(npg)],
        out_specs=pl.BlockSpec((1, ROWS8, npg * PAGE_SIZE), lambda n, p, pt: (n, 0, p)),
    )
    return pl.pallas_call(
        functools.partial(_dsa_step_scores_kernel, npg=npg),
        grid_spec=grid_spec,
        out_shape=jax.ShapeDtypeStruct((nb, ROWS8, n_pages * PAGE_SIZE), F32),
        compiler_params=_cparams("parallel", "arbitrary"),
        name="dsa_step_scores",
    )(page_table.reshape(-1), z8, z8, *([kidx_t] * npg))


STEP_SELECT_SEQS = 4


def _dsa_step_select_kernel(sc_ref, iq_ref, sm_ref, triu_ref, mp_ref, mn_ref, key_ref, *, past, l_new, topk, g):
    ck = DSA_CK
    nck = past // ck
    rows = g * ROWS8
    new_keys = []
    for j in range(g):
        iq = iq_ref[j]
        sm = sm_ref[j]
        lhs = [_idx_lhs(iq, h) for h in range(IDX_HEADS)]
        wcol = [sm[:, SM_IW + h:SM_IW + h + 1] for h in range(IDX_HEADS)]
        rhs_new = _idx_rhs(jnp.concatenate([sm[:, SM_IK:SM_IK + IDX_DIM],
                                            jnp.zeros((LANES - ROWS8, IDX_DIM), F32)], axis=0))
        new_keys.append(_sort_key(_idx_scores(lhs, rhs_new, wcol)))
    rloc = lax.broadcasted_iota(I32, (rows, LANES), 0) % ROWS8
    col = lax.broadcasted_iota(I32, (rows, LANES), 1)
    key_new = jnp.where(jnp.logical_and(col <= rloc, col < l_new), jnp.concatenate(new_keys, axis=0), INT_MIN)
    key_ref[...] = _sort_key(sc_ref[...].reshape(rows, past))
    kk = jnp.minimum(topk, past + 1 + lax.broadcasted_iota(I32, (rows, 1), 0) % ROWS8)

    def count(pred):
        return (jnp.sum(jnp.where(pred(key_ref[...]), 1, 0), axis=1, keepdims=True)
                + jnp.sum(jnp.where(pred(key_new), 1, 0), axis=1, keepdims=True))

    def bit_body(t, carry):
        thr, cge = carry
        cand = thr + lax.shift_left(jnp.int32(1), 31 - t)
        cnt = count(lambda x: x >= cand)
        ok = cnt >= kk
        return jnp.where(ok, cand, thr), jnp.where(ok, cnt, cge)

    thr, cge = lax.fori_loop(0, 32, bit_body, (jnp.full((rows, 1), INT_MIN, I32), jnp.zeros((rows, 1), I32)))
    cgt = count(lambda x: x > thr)
    need = kk - cgt
    needf = need.astype(F32)
    tied = jnp.max(jnp.where(need < cge - cgt, 1, 0))
    mn_ref[...] = jnp.where(key_new >= thr, 0.0, NEG).reshape(g, ROWS8, LANES)

    @pl.when(tied == 0)
    def _():
        mp_ref[...] = jnp.where(key_ref[...] >= thr, 0.0, NEG).reshape(g, ROWS8, past)

    @pl.when(tied > 0)
    def _():
        off = jnp.zeros((rows, 1), F32)
        triu = triu_ref[...]
        for c in range(nck):
            blk = key_ref[:, c * ck:(c + 1) * ck]
            eq = blk == thr
            pref = _dot(jnp.where(eq, 1.0, 0.0).astype(BF16), triu)
            keep = jnp.logical_or(blk > thr, jnp.logical_and(eq, pref + off <= needf))
            mp_ref[:, :, c * ck:(c + 1) * ck] = jnp.where(keep, 0.0, NEG).reshape(g, ROWS8, ck)
            off = off + pref[:, ck - 1:ck]
        eq = key_new == thr
        pref = _dot(jnp.where(eq, 1.0, 0.0).astype(BF16), triu[0:LANES, 0:LANES])
        keep = jnp.logical_or(key_new > thr, jnp.logical_and(eq, pref + off <= needf))
        mn_ref[...] = jnp.where(keep, 0.0, NEG).reshape(g, ROWS8, LANES)


def dsa_step_select(scores, z8, l_new):
    nb, _, past = scores.shape
    topk = min(DSA_TOPK_MAX, (past + l_new) // 4)
    iqw = IDX_HEADS * IDX_DIM
    g = math.gcd(STEP_SELECT_SEQS, nb)
    assert past % DSA_CK == 0
    triu = jnp.asarray(np.triu(np.ones((DSA_CK, DSA_CK), np.float32)), BF16)
    return pl.pallas_call(
        functools.partial(_dsa_step_select_kernel, past=past, l_new=l_new, topk=topk, g=g),
        grid=(nb // g,),
        in_specs=[pl.BlockSpec((g, ROWS8, past), lambda n: (n, 0, 0)),
                  pl.BlockSpec((g, ROWS8, iqw), lambda n: (n, 0, CD_COLS["iq"] // iqw)),
                  pl.BlockSpec((g, ROWS8, LANES), lambda n: (n, 0, SMALL_BLOCK)),
                  pl.BlockSpec((DSA_CK, DSA_CK), lambda n: (0, 0))],
        out_specs=[pl.BlockSpec((g, ROWS8, past), lambda n: (n, 0, 0)),
                   pl.BlockSpec((g, ROWS8, LANES), lambda n: (n, 0, 0))],
        out_shape=[jax.ShapeDtypeStruct((nb, ROWS8, past), F32),
                   jax.ShapeDtypeStruct((nb, ROWS8, LANES), F32)],
        scratch_shapes=[pltpu.VMEM((g * ROWS8, past), I32)],
        compiler_params=_cparams("parallel"),
        name="dsa_step_select",
    )(scores, z8, z8, triu)


def _dsa_step_attn_kernel(pt_ref, q_ref, kvn_ref, mp_ref, mn_ref, bp_ref, bn_ref, *rest, w, npg):
    kv_refs, o_ref = rest[:npg], rest[npg]
    qbd_ref, newpage_ref, m_ref, l_ref, acc_ref = rest[npg + 1:]
    n = pl.program_id(0)
    p = pl.program_id(1)
    rows = DSA_HEADS * ROWS8
    lane_h = lax.broadcasted_iota(I32, (ROWS8, w), 1) // HEAD_DIM

    @pl.when(jnp.logical_and(n == 0, p == 0))
    def _():
        newpage_ref[...] = jnp.zeros(newpage_ref.shape, F32)

    @pl.when(p == 0)
    def _():
        q = q_ref[0]
        for h in range(DSA_HEADS):
            qbd_ref[h * ROWS8:(h + 1) * ROWS8, :] = jnp.where(lane_h == h, q, jnp.zeros_like(q))
        m_ref[...] = jnp.full(m_ref.shape, NEG, F32)
        l_ref[...] = jnp.zeros(l_ref.shape, F32)
        acc_ref[...] = jnp.zeros(acc_ref.shape, F32)

    def accumulate(scores, madd8, bias, pv):
        s = scores + bias + jnp.concatenate([madd8] * DSA_HEADS, axis=0)
        m_old = m_ref[:, 0:1]
        m_new = jnp.maximum(m_old, jnp.max(s, axis=-1, keepdims=True))
        alpha = jnp.exp(m_old - m_new)
        pr = jnp.exp(s - m_new)
        l_ref[...] = jnp.broadcast_to(alpha * l_ref[:, 0:1] + jnp.sum(pr, axis=-1, keepdims=True), (rows, LANES))
        m_ref[...] = jnp.broadcast_to(m_new, (rows, LANES))
        acc_ref[...] = alpha * acc_ref[...] + pv(pr.astype(BF16))

    qbd = qbd_ref[...]
    scores = jnp.concatenate([_dot(qbd, kv_refs[j][0, 0, 0].reshape(w, PAGE_SIZE).astype(BF16))
                              for j in range(npg)], axis=1)

    def pv_pages(pr):
        out = None
        for j in range(npg):
            t = _nt(pr[:, j * PAGE_SIZE:(j + 1) * PAGE_SIZE], kv_refs[j][0, 0, 1].reshape(w, PAGE_SIZE).astype(BF16))
            out = t if out is None else out + t
        return out

    accumulate(scores, mp_ref[0], bp_ref[...], pv_pages)

    @pl.when(p == pl.num_programs(1) - 1)
    def _():
        newpage_ref[0:ROWS8, :] = kvn_ref[0]
        newp = newpage_ref[...]
        accumulate(_nt(qbd, newp[:, 0:w].astype(BF16)), mn_ref[0], bn_ref[...],
                   lambda pr: _dot(pr, newp[:, w:2 * w].astype(BF16)))
        for h in range(DSA_HEADS):
            rs = slice(h * ROWS8, (h + 1) * ROWS8)
            hs = slice(h * HEAD_DIM, (h + 1) * HEAD_DIM)
            o_ref[0, :, hs] = (acc_ref[rs, hs] / l_ref[rs, 0:1]).astype(BF16)


def _dsa_step_bias(tab_d, past, l_new):
    lq = np.minimum(np.arange(ROWS8), l_new - 1)[:, None]
    d_past = past + lq - np.arange(past)[None, :]
    d_new = lq - np.arange(LANES)[None, :]

    def table(d):
        return bias_lookup(tab_d, _rel_bucket_np(d)).reshape(DSA_HEADS * ROWS8, d.shape[1])

    return table(d_past), table(d_new)


def dsa_step_attend(dq8, kv_new8, kv_t, layer, page_table, mask_past, mask_new, tab_d, l_new):
    nb, _, w = dq8.shape
    n_pages = page_table.shape[1]
    npg = math.gcd(ATTN_PAGES, n_pages)
    past = n_pages * PAGE_SIZE
    rows = DSA_HEADS * ROWS8
    bp, bn = _dsa_step_bias(tab_d, past, l_new)

    def page_spec(j):
        return pl.BlockSpec((1, 1, 2, DSA_HEADS, HEAD_DIM, PAGE_SIZE),
                            lambda n, p, pt: (layer, pt[n * n_pages + p * npg + j], 0, 0, 0, 0))

    grid_spec = pltpu.PrefetchScalarGridSpec(
        num_scalar_prefetch=1,
        grid=(nb, n_pages // npg),
        in_specs=[pl.BlockSpec((1, ROWS8, w), lambda n, p, pt: (n, 0, 0)),
                  pl.BlockSpec((1, ROWS8, 2 * w), lambda n, p, pt: (n, 0, 0)),
                  pl.BlockSpec((1, ROWS8, npg * PAGE_SIZE), lambda n, p, pt: (n, 0, p)),
                  pl.BlockSpec((1, ROWS8, LANES), lambda n, p, pt: (n, 0, 0)),
                  pl.BlockSpec((rows, npg * PAGE_SIZE), lambda n, p, pt: (0, p)),
                  pl.BlockSpec((rows, LANES), lambda n, p, pt: (0, 0))]
                 + [page_spec(j) for j in range(npg)],
        out_specs=pl.BlockSpec((1, ROWS8, w), lambda n, p, pt: (n, 0, 0)),
        scratch_shapes=[pltpu.VMEM((rows, w), BF16), pltpu.VMEM((PAGE_SIZE, 2 * w), F32),
                        pltpu.VMEM((rows, LANES), F32), pltpu.VMEM((rows, LANES), F32), pltpu.VMEM((rows, w), F32)],
    )
    return pl.pallas_call(
        functools.partial(_dsa_step_attn_kernel, w=w, npg=npg),
        grid_spec=grid_spec,
        out_shape=jax.ShapeDtypeStruct((nb, ROWS8, w), BF16),
        compiler_params=_cparams("arbitrary", "arbitrary"),
        name="dsa_step_attend",
    )(page_table.reshape(-1), dq8, kv_new8, mask_past, mask_new, bp, bn, *([kv_t] * npg))


def _pad_rows(x3, rows):
    return jnp.pad(x3, ((0, 0), (0, rows - x3.shape[1]), (0, 0)))


def _trunk(x, is_step, conv_state, win_states, gla_state, dsa_kv, dsa_kidx, page_table, wts):
    (norm_mix, norm_ffn, w_in_ab, conv_w, conv_b, conv_ln_g, conv_ln_b, qn_ab, kn_ab, w_out_ab,
     w_in_cd, gla_wa2, gla_ba, gla_norm, qn_cd, kn_cd, w_out_cd, rel_bias, w_g, w_u, w_d) = wts
    nb, l, d = x.shape
    t = nb * l
    depth = norm_mix.shape[0]
    c = conv_w.shape[2]
    wq = B_HPG * HEAD_DIM
    x2 = x.reshape(t, d)
    conv_new, gla_new, kv_new, kidx_new = [], [], [], []
    win_new = [[] for _ in WINDOWS]
    for layer in range(depth):
        i = layer // 2
        if layer % 2 == 0:
            z = norm_matmul(x2, norm_mix[layer], w_in_ab[i])
            hist = conv_state[i] if is_step else jnp.zeros((nb, CONV_WIDTH - 1, c), F32)
            a_out, c_st = conv_module(z.reshape(nb, l, -1), hist, conv_w[i], conv_b[i], conv_ln_g[i], conv_ln_b[i])
            conv_new.append(c_st)
            if is_step:
                qs, kvs = ab_qkv(z, qn_ab[i], kn_ab[i], 2 * c, t, (1,) * len(WINDOWS))
            else:
                qs, kvs = ab_qkv(z, qn_ab[i], kn_ab[i], 2 * c, l, DILATIONS)
            outs, lses = [], []
            for g, window in enumerate(WINDOWS):
                tab_g = rel_bias[:, g * B_HPG:(g + 1) * B_HPG]
                if is_step:
                    o, lse, st = win_step(_pad_rows(qs[g].reshape(nb, l, wq), ROWS8),
                                          kvs[g].reshape(nb, l, 2 * wq), win_states[g], i, g, tab_g)
                else:
                    o, lse = win_prompt(qs[g], kvs[g], g, tab_g)
                    keep = min(window, l)
                    st = kvs[g][:, -(keep // DILATIONS[g]):].reshape(nb, keep, 2, B_HPG, HEAD_DIM)
                outs.append(o.reshape(1, t, wq) if is_step else o)
                lses.append(lse.reshape(1, t, wq) if is_step else lse)
                win_new[g].append(st)
            m2 = ab_merge(outs, lses, t, (1,) * len(WINDOWS)) if is_step else ab_merge(outs, lses, l, DILATIONS)
            m1 = a_out.reshape(t, c)
            wo = w_out_ab[i]
        else:
            z = norm_matmul(x2, norm_mix[layer], w_in_cd[i])
            pre = cd_pre(z, gla_wa2[i], gla_ba[i], qn_cd[i], kn_cd[i], None if is_step else l)
            gdec, dq, kv = pre[:3]
            z3 = z.reshape(nb, l, -1)
            wd = DSA_HEADS * HEAD_DIM
            ik = z3[:, :, CD_COLS["ik"]:CD_COLS["ik"] + IDX_DIM]
            tab_d = rel_bias[:, B_HEADS:]
            if is_step:
                lp = GLA_STEP_ROWS
                o_c, s_c = gla(_pad_rows(z3, lp), _pad_rows(gdec.reshape(nb, l, -1), lp), gla_state[i], gla_norm[i])
                o_c = o_c[:, :l]
                z8 = _pad_rows(z3, ROWS8)
                scores = dsa_step_scores(z8, dsa_kidx, i, page_table)
                mask_p, mask_n = dsa_step_select(scores, z8, l)
                o_d = dsa_step_attend(_pad_rows(dq.reshape(nb, l, wd), ROWS8), _pad_rows(kv.reshape(nb, l, 2 * wd), ROWS8),
                                      dsa_kv, i, page_table, mask_p, mask_n, tab_d, l)[:, :l]
                kv_st = kv.reshape(nb, l, 2, DSA_HEADS, HEAD_DIM)
                ki_st = ik
            else:
                s0 = jnp.zeros((nb, GLA_HEADS, GLA_DK, GLA_DV), F32)
                o_c, s_c = gla(z3, gdec.reshape(nb, l, -1), s0, gla_norm[i])
                kb, vt, smt = pre[3:]
                mask = dsa_select(z3, smt)
                o_d = dsa_attend(dq.reshape(nb, l, wd), kb.reshape(nb, l, wd), vt, mask, tab_d)
                n_pg = l // PAGE_SIZE
                kv_st = kv.reshape(nb, n_pg, PAGE_SIZE, 2, DSA_HEADS, HEAD_DIM)
                ki_st = ik.reshape(nb, n_pg, PAGE_SIZE, IDX_DIM)
            gla_new.append(s_c)
            kv_new.append(kv_st)
            kidx_new.append(ki_st)
            m1, m2 = o_c.reshape(t, -1), o_d.reshape(t, -1)
            wo = w_out_cd[i]
        d1 = m1.shape[1]
        x2 = mix_ffn(x2, m1, m2, wo[:d1], wo[d1:], norm_ffn[layer], w_g[layer], w_u[layer], w_d[layer])
    wins = [jnp.stack(ws) for ws in win_new]
    if is_step:
        wins = [ws.transpose(0, 1, 5, 2, 3, 4) for ws in wins]
    states = (jnp.stack(conv_new), wins[0], wins[1], wins[2],
              jnp.stack(gla_new), jnp.stack(kv_new), jnp.stack(kidx_new))
    return x2.reshape(nb, l, d), states


def kernel(x_prompt, x_sample, state_conv, cache_win128, cache_win512, cache_win2048, state_gla, cache_dsa_kv, cache_dsa_kidx, page_table, norm_mix, norm_ffn, w_in_ab, conv_w, conv_b, conv_ln_g, conv_ln_b, qn_ab, kn_ab, w_out_ab, w_in_cd, gla_wa2, gla_ba, gla_norm, qn_cd, kn_cd, w_out_cd, rel_bias, w_ffn_gate, w_ffn_up, w_ffn_down):
    bf = lambda a: a.astype(BF16)
    w_in_cd_r = jnp.stack([cd_reorder_w(w_in_cd[i]) for i in range(w_in_cd.shape[0])])
    wts = (norm_mix, norm_ffn, bf(w_in_ab), conv_w, conv_b, conv_ln_g, conv_ln_b, qn_ab, kn_ab, bf(w_out_ab),
           bf(w_in_cd_r), gla_wa2, gla_ba, gla_norm, qn_cd, kn_cd, bf(w_out_cd), rel_bias,
           bf(w_ffn_gate), bf(w_ffn_up), bf(w_ffn_down))
    y_p, sp = _trunk(x_prompt, False, None, None, None, None, None, None, wts)
    wins_t = tuple(cw.transpose(0, 1, 3, 4, 5, 2) for cw in (cache_win128, cache_win512, cache_win2048))
    y_s, ss = _trunk(x_sample, True, state_conv, wins_t, state_gla,
                     cache_dsa_kv.transpose(0, 1, 3, 4, 5, 2), cache_dsa_kidx.transpose(0, 1, 3, 2), page_table, wts)
    conv_p, win128_p, win512_p, win2048_p, gla_p, dsa_kv_p, dsa_kidx_p = sp
    conv_s, win128_s, win512_s, win2048_s, gla_s, dsa_kv_s, dsa_kidx_s = ss
    return (y_p, y_s, conv_p, conv_s, win128_p, win128_s, win512_p, win512_s, win2048_p, win2048_s,
            gla_p, gla_s, dsa_kv_p, dsa_kv_s, dsa_kidx_p, dsa_kidx_s)
```

```python
import functools
import math

import numpy as np
import jax
import jax.numpy as jnp
from jax import lax
from jax.experimental import pallas as pl
from jax.experimental.pallas import tpu as pltpu

F32 = jnp.float32
BF16 = jnp.bfloat16
I32 = jnp.int32

EPS = 1e-6
NEG = -1e30
LOG2E = math.log2(math.e)
F32_MIN_NORMAL = 2.0 ** -126
INT_MIN = -(2 ** 31)

V7X_VMEM_BYTES = 64 * 1024 * 1024
VMEM_LIMIT = V7X_VMEM_BYTES - 12 * 1024 * 1024
LANES = 128

HEAD_DIM = 64
CONV_WIDTH = 31
WINDOWS = (128, 512, 2048)
DILATIONS = (1, 4, 16)
B_HPG = 4
B_HEADS = B_HPG * len(WINDOWS)
SW_BLOCK = 128
GLA_HEADS = 4
GLA_DK = 64
GLA_DV = 128
GLA_GATE_RANK = 16
GLA_TAU = 16.0
DSA_HEADS = 8
IDX_HEADS = 4
IDX_DIM = 64
DSA_TOPK_MAX = 256
PAGE_SIZE = 128
REL_BUCKETS = 32
REL_MAX_DIST = 2048

GLA_CHUNK = 128
GLA_STEP_ROWS = 64
DSA_QB = 256
DSA_BT = 128
DSA_SQ = 256
DSA_CK = 512
HIST_PAD = 32


def _cparams(*sem):
    return pltpu.CompilerParams(dimension_semantics=sem, vmem_limit_bytes=VMEM_LIMIT)


def _nt(a, b):
    return lax.dot_general(a, b, (((1,), (1,)), ((), ())), preferred_element_type=F32)


def _dot(a, b):
    return jnp.dot(a, b, preferred_element_type=F32)


def _split2(x):
    hi = x.astype(BF16)
    lo = (x - hi.astype(F32)).astype(BF16)
    return hi, lo


def _split3(x):
    hi = x.astype(BF16)
    r = x - hi.astype(F32)
    mid = r.astype(BF16)
    lo = (r - mid.astype(F32)).astype(BF16)
    return hi, mid, lo


def _rel_bucket_np(dist):
    n = np.maximum(np.asarray(dist, np.int64), 0)
    max_exact = REL_BUCKETS // 2
    nf = np.maximum(n, max_exact).astype(np.float32)
    large = max_exact + (np.log(nf / np.float32(max_exact)) / np.float32(math.log(REL_MAX_DIST / max_exact))
                         * np.float32(REL_BUCKETS - max_exact)).astype(np.int32)
    large = np.minimum(large, REL_BUCKETS - 1)
    return np.where(n < max_exact, n, large).astype(np.int32)


def _bias_lookup_kernel(tab_ref, idx_ref, o_ref, *, nh):
    idx = idx_ref[...]
    for h in range(nh):
        acc = jnp.full(idx.shape, NEG, F32)
        for b in range(REL_BUCKETS):
            acc = jnp.where(idx == b, tab_ref[b, h], acc)
        o_ref[h] = acc


def bias_lookup(tab, idx_np):
    r, c = idx_np.shape
    nh = tab.shape[1]
    tr = 8 if (c >= 2048 and r % 8 == 0) else (128 if r % 128 == 0 else r)
    return pl.pallas_call(
        functools.partial(_bias_lookup_kernel, nh=nh),
        grid=(r // tr,),
        in_specs=[pl.BlockSpec(memory_space=pltpu.SMEM),
                  pl.BlockSpec((tr, c), lambda i: (i, 0))],
        out_specs=pl.BlockSpec((nh, tr, c), lambda i: (0, i, 0)),
        out_shape=jax.ShapeDtypeStruct((nh, r, c), F32),
        compiler_params=_cparams("parallel"),
        name="bias_lookup",
    )(tab, jnp.asarray(idx_np.astype(np.int32)))


def _norm_matmul_kernel(x_ref, g_ref, w_ref, o_ref):
    x = x_ref[...]
    y = x * lax.rsqrt(jnp.mean(x * x, axis=-1, keepdims=True) + EPS) * g_ref[...]
    o_ref[...] = _dot(y.astype(BF16), w_ref[...])


def norm_matmul(x, g, w):
    t, d = x.shape
    n = w.shape[1]
    tm = min(t, 512)
    return pl.pallas_call(
        _norm_matmul_kernel,
        grid=(t // tm,),
        in_specs=[pl.BlockSpec((tm, d), lambda i: (i, 0)),
                  pl.BlockSpec((1, d), lambda i: (0, 0)),
                  pl.BlockSpec((d, n), lambda i: (0, 0))],
        out_specs=pl.BlockSpec((tm, n), lambda i: (i, 0)),
        out_shape=jax.ShapeDtypeStruct((t, n), F32),
        compiler_params=_cparams("parallel"),
        name="norm_matmul",
    )(x, g.reshape(1, d), w)


def _mix_ffn_kernel(x_ref, m1_ref, m2_ref, wo1_ref, wo2_ref, g_ref, wg_ref, wu_ref, wd_ref, o_ref, *, th):
    x1 = x_ref[...] + _dot(m1_ref[...], wo1_ref[...]) + _dot(m2_ref[...], wo2_ref[...])
    hf = (x1 * lax.rsqrt(jnp.mean(x1 * x1, axis=-1, keepdims=True) + EPS) * g_ref[...]).astype(BF16)
    acc = x1
    for c0 in range(0, wg_ref.shape[1], th):
        a = _dot(hf, wg_ref[:, c0:c0 + th])
        u = _dot(hf, wu_ref[:, c0:c0 + th])
        act = (a * jax.nn.sigmoid(a) * u).astype(BF16)
        acc = acc + _dot(act, wd_ref[c0:c0 + th, :])
    o_ref[...] = acc


def mix_ffn(x, m1, m2, wo1, wo2, g, wg, wu, wd):
    t, d = x.shape
    hid = wg.shape[1]
    tm = min(t, 512)
    th = hid // 4 if hid % (4 * LANES) == 0 else hid
    d1, d2 = m1.shape[1], m2.shape[1]
    resident = lambda shape: pl.BlockSpec(shape, lambda i: (0, 0), pipeline_mode=pl.Buffered(1))
    return pl.pallas_call(
        functools.partial(_mix_ffn_kernel, th=th),
        grid=(t // tm,),
        in_specs=[pl.BlockSpec((tm, d), lambda i: (i, 0)),
                  pl.BlockSpec((tm, d1), lambda i: (i, 0)),
                  pl.BlockSpec((tm, d2), lambda i: (i, 0)),
                  resident((d1, d)), resident((d2, d)), resident((1, d)),
                  resident((d, hid)), resident((d, hid)), resident((hid, d))],
        out_specs=pl.BlockSpec((tm, d), lambda i: (i, 0)),
        out_shape=jax.ShapeDtypeStruct((t, d), F32),
        compiler_params=_cparams("parallel"),
        name="mix_ffn",
    )(x, m1, m2, wo1, wo2, g.reshape(1, d), wg, wu, wd)


def _conv_kernel(z_ref, hist_ref, w_ref, b_ref, lg_ref, lb_ref, o_ref, tail_ref, uh_ref, us_ref, *, ts, c):
    t = pl.program_id(1)
    sl = 8

    @pl.when(t == 0)
    def _():
        uh_ref[0:HIST_PAD, :] = hist_ref[0]

    z = z_ref[0]
    u = z[:, 0:c] * jax.nn.sigmoid(z[:, c:2 * c])
    uh_ref[HIST_PAD:HIST_PAD + ts, :] = u
    span = HIST_PAD + ts - sl
    for s in range(1, sl):
        us_ref[s - 1, 0:span, :] = uh_ref[s:s + span, :]
    acc = jnp.zeros((ts, c), F32) + b_ref[...]
    off = HIST_PAD - (CONV_WIDTH - 1)
    for j in range(CONV_WIDTH):
        a, s = divmod(off + j, sl)
        rows = uh_ref[a * sl:a * sl + ts, :] if s == 0 else us_ref[s - 1, a * sl:a * sl + ts, :]
        acc = acc + w_ref[j:j + 1, :] * rows
    mu = jnp.mean(acc, axis=-1, keepdims=True)
    var = jnp.mean(jnp.square(acc - mu), axis=-1, keepdims=True)
    yn = (acc - mu) * lax.rsqrt(var + EPS) * lg_ref[...] + lb_ref[...]
    o_ref[0] = (yn * jax.nn.sigmoid(yn)).astype(BF16)
    tail = uh_ref[ts:ts + HIST_PAD, :]
    uh_ref[0:HIST_PAD, :] = tail
    tail_ref[0] = tail


def conv_module(z3, hist, conv_w, conv_b, ln_g, ln_b):
    nb, l, _ = z3.shape
    c = conv_w.shape[1]
    ts = min(l, 512)
    hist_p = jnp.pad(hist, ((0, 0), (HIST_PAD - (CONV_WIDTH - 1), 0), (0, 0)))
    w_p = jnp.pad(conv_w, ((0, HIST_PAD - CONV_WIDTH), (0, 0)))
    a_out, tail = pl.pallas_call(
        functools.partial(_conv_kernel, ts=ts, c=c),
        grid=(nb, l // ts),
        in_specs=[pl.BlockSpec((1, ts, 2 * c), lambda n, t: (n, t, 0)),
                  pl.BlockSpec((1, HIST_PAD, c), lambda n, t: (n, 0, 0)),
                  pl.BlockSpec((HIST_PAD, c), lambda n, t: (0, 0)),
                  pl.BlockSpec((1, c), lambda n, t: (0, 0)),
                  pl.BlockSpec((1, c), lambda n, t: (0, 0)),
                  pl.BlockSpec((1, c), lambda n, t: (0, 0))],
        out_specs=[pl.BlockSpec((1, ts, c), lambda n, t: (n, t, 0)),
                   pl.BlockSpec((1, HIST_PAD, c), lambda n, t: (n, 0, 0))],
        out_shape=[jax.ShapeDtypeStruct((nb, l, c), BF16),
                   jax.ShapeDtypeStruct((nb, HIST_PAD, c), F32)],
        scratch_shapes=[pltpu.VMEM((HIST_PAD + ts, c), F32), pltpu.VMEM((7, HIST_PAD + ts, c), F32)],
        compiler_params=_cparams("parallel", "arbitrary"),
        name="conv_module",
    )(z3, hist_p, w_p, conv_b.reshape(1, c), ln_g.reshape(1, c), ln_b.reshape(1, c))
    return a_out, tail[:, HIST_PAD - (CONV_WIDTH - 1):]


def _seg_mean_sq(x, seg_ref):
    hi, lo = _split2(x * x)
    seg = seg_ref[...]
    return (_dot(hi, seg) + _dot(lo, seg)) * (1.0 / HEAD_DIM)


def _head_norm(x, g, seg_ref):
    return x * lax.rsqrt(_seg_mean_sq(x, seg_ref) + EPS) * g


def _seg_matrix(width):
    idx = np.arange(width) // HEAD_DIM
    return jnp.asarray((idx[:, None] == idx[None, :]).astype(np.float32), BF16)


def _ab_qkv_kernel(*refs, w, tm, dils):
    ng = len(dils)
    qkv_refs = refs[:3 * ng]
    qg_ref, kg_ref, seg_ref = refs[3 * ng:3 * ng + 3]
    qo_refs = refs[3 * ng + 3:4 * ng + 3]
    kvo_refs = refs[4 * ng + 3:5 * ng + 3]
    qs_ref, kvs_ref = refs[5 * ng + 3:]
    for g, dil in enumerate(dils):
        qn = _head_norm(qkv_refs[g][...], qg_ref[...], seg_ref) * HEAD_DIM ** -0.5
        kn = _head_norm(qkv_refs[ng + g][...], kg_ref[...], seg_ref)
        v = qkv_refs[2 * ng + g][...]
        if dil == 1:
            qo_refs[g][0] = qn.astype(BF16)
            kvo_refs[g][0, :, 0:w] = kn
            kvo_refs[g][0, :, w:2 * w] = v
        else:
            nq, nkv = w // LANES, 2 * w // LANES
            kv = jnp.concatenate([kn, v], axis=1)
            for j in range(nq):
                qs_ref[j] = qn[:, j * LANES:(j + 1) * LANES]
            for j in range(nkv):
                kvs_ref[j] = kv[:, j * LANES:(j + 1) * LANES]
            for r in range(dil):
                rows = pl.ds(r, tm // dil, stride=dil)
                for j in range(nq):
                    qo_refs[g][0, :, r * w + j * LANES:r * w + (j + 1) * LANES] = qs_ref[j, rows, :].astype(BF16)
                for j in range(nkv):
                    kvo_refs[g][0, :, r * 2 * w + j * LANES:r * 2 * w + (j + 1) * LANES] = kvs_ref[j, rows, :]


def ab_qkv(z, qn, kn, col0, seq, dils):
    t = z.shape[0]
    w = B_HPG * HEAD_DIM
    ng = len(dils)
    tm = min(seq, 512)
    nt = seq // tm
    cb = col0 // w
    assert all(tm % (8 * d) == 0 or d == 1 for d in dils)
    col_spec = lambda j: pl.BlockSpec((tm, w), lambda i: (i, cb + j))
    out_specs = ([pl.BlockSpec((1, tm // d, d * w), lambda i: (i // nt, i % nt, 0)) for d in dils]
                 + [pl.BlockSpec((1, tm // d, d * 2 * w), lambda i: (i // nt, i % nt, 0)) for d in dils])
    out_shape = ([jax.ShapeDtypeStruct((t // seq, seq // d, d * w), BF16) for d in dils]
                 + [jax.ShapeDtypeStruct((t // seq, seq // d, d * 2 * w), F32) for d in dils])
    outs = pl.pallas_call(
        functools.partial(_ab_qkv_kernel, w=w, tm=tm, dils=tuple(dils)),
        grid=(t // tm,),
        in_specs=[col_spec(j) for j in range(3 * ng)]
                 + [pl.BlockSpec((1, w), lambda i: (0, 0)),
                    pl.BlockSpec((1, w), lambda i: (0, 0)),
                    pl.BlockSpec((w, w), lambda i: (0, 0))],
        out_specs=out_specs,
        out_shape=out_shape,
        scratch_shapes=[pltpu.VMEM((w // LANES, tm, LANES), F32), pltpu.VMEM((2 * w // LANES, tm, LANES), F32)],
        compiler_params=_cparams("parallel"),
        name="ab_qkv",
    )(*([z] * (3 * ng)), jnp.tile(qn, B_HPG).reshape(1, w), jnp.tile(kn, B_HPG).reshape(1, w), _seg_matrix(w))
    return outs[:ng], outs[ng:]


WIN_BLOCKS = 8


def _win_prompt_kernel(q_ref, kvp_ref, kvc_ref, bias_ref, o_ref, lse_ref, *, w, nblk):
    step = pl.program_id(2)
    col = lax.broadcasted_iota(I32, (SW_BLOCK, 2 * SW_BLOCK), 1)
    first = jnp.logical_and(step == 0, col < SW_BLOCK)
    for j in range(nblk):
        rows = slice(j * SW_BLOCK, (j + 1) * SW_BLOCK)
        q = q_ref[0, rows, :]
        kvc = kvc_ref[0, rows, :]
        kvp = kvp_ref[0] if j == 0 else kvc_ref[0, (j - 1) * SW_BLOCK:j * SW_BLOCK, :]
        for h in range(B_HPG):
            hs = slice(h * HEAD_DIM, (h + 1) * HEAD_DIM)
            vs = slice(w + h * HEAD_DIM, w + (h + 1) * HEAD_DIM)
            k2 = jnp.concatenate([kvp[:, hs], kvc[:, hs]], axis=0).astype(BF16)
            v2 = jnp.concatenate([kvp[:, vs], kvc[:, vs]], axis=0).astype(BF16)
            s = _nt(q[:, hs], k2) + bias_ref[h]
            if j == 0:
                s = jnp.where(first, NEG, s)
            m = jnp.max(s, axis=-1, keepdims=True)
            p = jnp.exp(s - m)
            l = jnp.sum(p, axis=-1, keepdims=True)
            o_ref[0, rows, hs] = _dot(p.astype(BF16), v2) / l
            lse_ref[0, rows, hs] = jnp.broadcast_to(m + jnp.log(l), (SW_BLOCK, HEAD_DIM))


def _win_prompt_bias(tab_g, dil, reach):
    ql = np.arange(SW_BLOCK)[:, None]
    kl = np.arange(2 * SW_BLOCK)[None, :] - SW_BLOCK
    rel = ql - kl
    ok = (rel >= 0) & (rel <= reach)
    return bias_lookup(tab_g, np.where(ok, _rel_bucket_np(rel * dil), -1))


def win_prompt(qv, kvv, g, tab_g):
    w = B_HPG * HEAD_DIM
    dil = DILATIONS[g]
    b, n, _ = qv.shape
    s = n * dil
    nb = n // SW_BLOCK
    assert nb * SW_BLOCK * dil == s
    bias = _win_prompt_bias(tab_g, dil, WINDOWS[g] // dil)
    nblk = math.gcd(WIN_BLOCKS, nb)
    rows = nblk * SW_BLOCK
    o, lse = pl.pallas_call(
        functools.partial(_win_prompt_kernel, w=w, nblk=nblk),
        grid=(b, dil, nb // nblk),
        in_specs=[pl.BlockSpec((1, rows, w), lambda n_, r, k: (n_, k, r)),
                  pl.BlockSpec((1, SW_BLOCK, 2 * w), lambda n_, r, k: (n_, jnp.maximum(k * nblk - 1, 0), r)),
                  pl.BlockSpec((1, rows, 2 * w), lambda n_, r, k: (n_, k, r)),
                  pl.BlockSpec((B_HPG, SW_BLOCK, 2 * SW_BLOCK), lambda n_, r, k: (0, 0, 0))],
        out_specs=[pl.BlockSpec((1, rows, w), lambda n_, r, k: (n_, k, r)),
                   pl.BlockSpec((1, rows, w), lambda n_, r, k: (n_, k, r))],
        out_shape=[jax.ShapeDtypeStruct((b, n, dil * w), F32),
                   jax.ShapeDtypeStruct((b, n, dil * w), F32)],
        compiler_params=_cparams("parallel", "parallel", "arbitrary"),
        name=f"win_prompt_g{g}",
    )(qv, kvv, kvv, bias)
    return o, lse


ROWS8 = 8


def _win_step_kernel(q_ref, kvn_ref, newt_ref, buf_ref, tb_ref, tn_ref, o_ref, lse_ref, st_ref,
                     qbd_ref, newpage_ref, *, w, wb, l):
    n = pl.program_id(0)

    @pl.when(n == 0)
    def _():
        newpage_ref[...] = jnp.zeros(newpage_ref.shape, F32)

    q = q_ref[0]
    lane_h = lax.broadcasted_iota(I32, (ROWS8, w), 1) // HEAD_DIM
    for h in range(B_HPG):
        qbd_ref[h * ROWS8:(h + 1) * ROWS8, :] = jnp.where(lane_h == h, q, jnp.zeros_like(q))
    newpage_ref[0:l, :] = kvn_ref[0]
    qbd = qbd_ref[...]
    newp = newpage_ref[...]
    buf = buf_ref[0, 0]
    kt = buf[0].reshape(w, wb).astype(BF16)
    vt = buf[1].reshape(w, wb).astype(BF16)
    s1 = _dot(qbd, kt) + tb_ref[...]
    s2 = _nt(qbd, newp[:, 0:w].astype(BF16)) + tn_ref[...]
    m = jnp.maximum(jnp.max(s1, axis=-1, keepdims=True), jnp.max(s2, axis=-1, keepdims=True))
    p1 = jnp.exp(s1 - m)
    p2 = jnp.exp(s2 - m)
    den = jnp.sum(p1, axis=-1, keepdims=True) + jnp.sum(p2, axis=-1, keepdims=True)
    num = _nt(p1.astype(BF16), vt) + _dot(p2.astype(BF16), newp[:, w:2 * w].astype(BF16))
    o = num / den
    lse = m + jnp.log(den)
    for h in range(B_HPG):
        rs = slice(h * ROWS8, (h + 1) * ROWS8)
        hs = slice(h * HEAD_DIM, (h + 1) * HEAD_DIM)
        o_ref[0, :, hs] = o[rs, hs]
        lse_ref[0, :, hs] = jnp.broadcast_to(lse[rs], (ROWS8, HEAD_DIM))
    rolled = pltpu.roll(buf.reshape(2 * w, wb), wb - l, 1)
    lane = lax.broadcasted_iota(I32, (2 * w, LANES), 1)
    last = jnp.where(lane >= LANES - l, newt_ref[0], rolled[:, wb - LANES:wb])
    if wb > LANES:
        st_ref[0, :, :, :, 0:wb - LANES] = rolled[:, 0:wb - LANES].reshape(2, B_HPG, HEAD_DIM, wb - LANES)
    st_ref[0, :, :, :, wb - LANES:wb] = last.reshape(2, B_HPG, HEAD_DIM, LANES)


def _win_step_bias(tab_g, dil, window, wb, l):
    lq = np.minimum(np.arange(ROWS8), l - 1)[:, None]
    dist_b = wb + lq - np.arange(wb)[None, :]
    ok_b = (dist_b % dil == 0) & (dist_b // dil <= window // dil)
    cols = np.arange(LANES)[None, :]
    dist_n = lq - cols
    ok_n = (cols < l) & (dist_n >= 0) & (dist_n % dil == 0) & (dist_n // dil <= window // dil)
    rows = B_HPG * ROWS8
    tb = bias_lookup(tab_g, np.where(ok_b, _rel_bucket_np(dist_b), -1)).reshape(rows, wb)
    tn = bias_lookup(tab_g, np.where(ok_n, _rel_bucket_np(dist_n), -1)).reshape(rows, LANES)
    return tb, tn


def win_step(q8, kv_new, cache_t, layer, g, tab_g):
    nb = q8.shape[0]
    l = kv_new.shape[1]
    w = B_HPG * HEAD_DIM
    wb = cache_t.shape[-1]
    window, dil = WINDOWS[g], DILATIONS[g]
    assert wb == window, "the step kernel keeps a full window of rows"
    tb, tn = _win_step_bias(tab_g, dil, window, wb, l)
    rows = B_HPG * ROWS8
    new_t = jnp.pad(kv_new.transpose(0, 2, 1), ((0, 0), (0, 0), (LANES - l, 0)))
    o, lse, st = pl.pallas_call(
        functools.partial(_win_step_kernel, w=w, wb=wb, l=l),
        grid=(nb,),
        in_specs=[pl.BlockSpec((1, ROWS8, w), lambda n: (n, 0, 0)),
                  pl.BlockSpec((1, l, 2 * w), lambda n: (n, 0, 0)),
                  pl.BlockSpec((1, 2 * w, LANES), lambda n: (n, 0, 0)),
                  pl.BlockSpec((1, 1, 2, B_HPG, HEAD_DIM, wb), lambda n: (layer, n, 0, 0, 0, 0)),
                  pl.BlockSpec((rows, wb), lambda n: (0, 0)),
                  pl.BlockSpec((rows, LANES), lambda n: (0, 0))],
        out_specs=[pl.BlockSpec((1, ROWS8, w), lambda n: (n, 0, 0)),
                   pl.BlockSpec((1, ROWS8, w), lambda n: (n, 0, 0)),
                   pl.BlockSpec((1, 2, B_HPG, HEAD_DIM, wb), lambda n: (n, 0, 0, 0, 0))],
        out_shape=[jax.ShapeDtypeStruct((nb, ROWS8, w), F32),
                   jax.ShapeDtypeStruct((nb, ROWS8, w), F32),
                   jax.ShapeDtypeStruct((nb, 2, B_HPG, HEAD_DIM, wb), F32)],
        scratch_shapes=[pltpu.VMEM((rows, w), BF16), pltpu.VMEM((LANES, 2 * w), F32)],
        compiler_params=_cparams("arbitrary"),
        name=f"win_step_g{g}",
    )(q8, kv_new, new_t, cache_t, tb, tn)
    return o[:, :l], lse[:, :l], st


def _ab_merge_kernel(*refs, w, tm, dils):
    ng = len(dils)
    out_ref, stage_ref = refs[2 * ng], refs[2 * ng + 1]
    nl = w // LANES

    def token_rows(ref, dil):
        if dil == 1:
            return ref[0]
        for r in range(dil):
            for j in range(nl):
                stage_ref[j, pl.ds(r, tm // dil, stride=dil), :] = ref[0, :, r * w + j * LANES:r * w + (j + 1) * LANES]
        return jnp.concatenate([stage_ref[j] for j in range(nl)], axis=1)

    os_ = [token_rows(refs[g], dils[g]) for g in range(ng)]
    ls_ = [token_rows(refs[ng + g], dils[g]) for g in range(ng)]
    m = functools.reduce(jnp.maximum, ls_)
    es = [jnp.exp(l - m) for l in ls_]
    num = functools.reduce(lambda x, y: x + y, [e * o for e, o in zip(es, os_)])
    den = functools.reduce(lambda x, y: x + y, es)
    out_ref[...] = (num / den).astype(BF16)


def ab_merge(outs, lses, seq, dils):
    w = B_HPG * HEAD_DIM
    t = outs[0].shape[0] * seq
    tm = min(seq, 512)
    nt = seq // tm
    specs = [pl.BlockSpec((1, tm // d, d * w), lambda i: (i // nt, i % nt, 0)) for d in dils]
    return pl.pallas_call(
        functools.partial(_ab_merge_kernel, w=w, tm=tm, dils=tuple(dils)),
        grid=(t // tm,),
        in_specs=specs + specs,
        out_specs=pl.BlockSpec((tm, w), lambda i: (i, 0)),
        out_shape=jax.ShapeDtypeStruct((t, w), BF16),
        scratch_shapes=[pltpu.VMEM((w // LANES, tm, LANES), F32)],
        compiler_params=_cparams("parallel"),
        name="ab_merge",
    )(*outs, *lses)


CD_COLS = dict(cq=0, ck=256, cv=512, cgate=1024, dq=1536, dk=2048, dv=2560, iq=3072, ik=3328, clr=3392, iw=3408)
CD_PAD = 3456
SMALL_BLOCK = CD_COLS["ik"] // LANES
SM_IK, SM_CLR, SM_IW = 0, CD_COLS["clr"] - CD_COLS["ik"], CD_COLS["iw"] - CD_COLS["ik"]


def cd_reorder_w(w_in):
    sizes = (256, 256, 512, 16, 512, 512, 512, 512, 256, 64, 4)
    names = ("cq", "ck", "cv", "clr", "cgate", "dq", "dk", "dv", "iq", "ik", "iw")
    starts = np.concatenate([[0], np.cumsum(sizes)[:-1]])
    out = jnp.zeros((w_in.shape[0], CD_PAD), w_in.dtype)
    for nm, st, sz in zip(names, starts, sizes):
        out = lax.dynamic_update_slice(out, w_in[:, st:st + sz], (0, CD_COLS[nm]))
    return out


def _cd_pre_kernel(d_ref, sm_ref, wa_ref, ba_ref, qg_ref, kg_ref, seg_ref, g_ref, dq_ref, kv_ref, *t_refs, w):
    d = d_ref[...]
    sm = sm_ref[...]
    clr = sm[:, SM_CLR:SM_CLR + GLA_GATE_RANK]
    wa = wa_ref[...]
    pre = ba_ref[...]
    for part in _split3(clr):
        pre = pre + _dot(part, wa[0]) + _dot(part, wa[1])
    lsig = jnp.minimum(pre, 0.0) - jnp.log(1.0 + jnp.exp(-jnp.abs(pre)))
    g_ref[...] = lsig * (1.0 / GLA_TAU)
    q_scale = HEAD_DIM ** -0.5 * (LOG2E if t_refs else 1.0)
    dq_ref[...] = (_head_norm(d[:, 0:w], qg_ref[...], seg_ref) * q_scale).astype(BF16)
    kn = _head_norm(d[:, w:2 * w], kg_ref[...], seg_ref)
    v = d[:, 2 * w:3 * w]
    kv_ref[:, 0:w] = kn
    kv_ref[:, w:2 * w] = v
    if t_refs:
        kb_ref, vt_ref, smt_ref = t_refs
        kb_ref[...] = kn.astype(BF16)
        vt_ref[0] = v.T.astype(BF16)
        smt_ref[0] = sm.T


def cd_pre(z, wa2, ba, qn, kn, seq=None):
    t = z.shape[0]
    w = DSA_HEADS * HEAD_DIM
    gw = GLA_HEADS * GLA_DK
    tm = min(t, 512)
    wa_hi, wa_lo = _split2(wa2)
    out_specs = [pl.BlockSpec((tm, gw), lambda i: (i, 0)),
                 pl.BlockSpec((tm, w), lambda i: (i, 0)),
                 pl.BlockSpec((tm, 2 * w), lambda i: (i, 0))]
    out_shape = [jax.ShapeDtypeStruct((t, gw), F32),
                 jax.ShapeDtypeStruct((t, w), BF16),
                 jax.ShapeDtypeStruct((t, 2 * w), F32)]
    if seq is not None:
        nt = seq // tm
        out_specs += [pl.BlockSpec((tm, w), lambda i: (i, 0)),
                      pl.BlockSpec((1, w, tm), lambda i: (i // nt, 0, i % nt)),
                      pl.BlockSpec((1, LANES, tm), lambda i: (i // nt, 0, i % nt))]
        out_shape += [jax.ShapeDtypeStruct((t, w), BF16),
                      jax.ShapeDtypeStruct((t // seq, w, seq), BF16),
                      jax.ShapeDtypeStruct((t // seq, LANES, seq), F32)]
    return pl.pallas_call(
        functools.partial(_cd_pre_kernel, w=w),
        grid=(t // tm,),
        in_specs=[pl.BlockSpec((tm, 3 * w), lambda i: (i, CD_COLS["dq"] // (3 * w))),
                  pl.BlockSpec((tm, LANES), lambda i: (i, SMALL_BLOCK)),
                  pl.BlockSpec((2, GLA_GATE_RANK, gw), lambda i: (0, 0, 0)),
                  pl.BlockSpec((1, gw), lambda i: (0, 0)),
                  pl.BlockSpec((1, w), lambda i: (0, 0)),
                  pl.BlockSpec((1, w), lambda i: (0, 0)),
                  pl.BlockSpec((w, w), lambda i: (0, 0))],
        out_specs=out_specs,
        out_shape=out_shape,
        compiler_params=_cparams("parallel"),
        name="cd_pre",
    )(z, z, jnp.stack([wa_hi, wa_lo]), ba.reshape(1, gw), jnp.tile(qn, DSA_HEADS).reshape(1, w),
      jnp.tile(kn, DSA_HEADS).reshape(1, w), _seg_matrix(w))


def _gla_consts(c):
    levels = []
    s = c // 2
    while s >= 1:
        levels.append(s)
        s //= 2
    i = np.arange(c)
    tri = (i[None, :] <= i[:, None]).astype(np.float32)
    mats = [tri]
    masks = []
    for s in levels:
        ref = (i // (2 * s)) * (2 * s) + s - 1
        r = (i[None, :] <= ref[:, None]).astype(np.float32)
        mats.append(tri - r)
        same = (i[:, None] // (2 * s)) == (i[None, :] // (2 * s))
        masks.append(same & ((i[:, None] % (2 * s)) >= s) & ((i[None, :] % (2 * s)) < s))
    masks.append(i[:, None] == i[None, :])
    mstack = np.concatenate(mats, axis=0)
    mask = np.stack([np.tile(m.astype(np.float32), (1, GLA_HEADS)) for m in masks])
    return jnp.asarray(mstack, BF16), jnp.asarray(mask, F32), len(levels)


def _gla_kernel(qk_ref, v_ref, gate_ref, g_ref, s0_ref, mstack_ref, mask_ref, gn_ref, eye_ref,
                o_ref, sf_ref, st_ref, *, c, nl):
    t = pl.program_id(1)
    kw = GLA_HEADS * GLA_DK
    vw = GLA_HEADS * GLA_DV
    lane_k = lax.broadcasted_iota(I32, (1, kw), 1) // GLA_DK
    lane_v = lax.broadcasted_iota(I32, (1, vw), 1) // GLA_DV

    @pl.when(t == 0)
    def _():
        rows = []
        for h in range(GLA_HEADS):
            z = jnp.zeros((GLA_DK, GLA_DV), F32)
            rows.append(jnp.concatenate([s0_ref[0, h] if hh == h else z for hh in range(GLA_HEADS)], axis=1))
        st_ref[...] = jnp.concatenate(rows, axis=0).T

    qk = qk_ref[0]
    q = qk[:, 0:kw] * GLA_DK ** -0.5
    k = qk[:, kw:2 * kw]
    v = v_ref[0]
    mstack = mstack_ref[...]
    r = None
    for part in _split3(g_ref[0]):
        d = _dot(mstack, part)
        r = d if r is None else r + d
    b = r[0:c]

    def expand_k(x):
        return jnp.concatenate([jnp.where(lane_k == h, x, 0.0) for h in range(GLA_HEADS)], axis=0).astype(BF16)

    st = st_ref[...]
    o = _nt((q * jnp.exp(b)).astype(BF16), st.astype(BF16))
    a = mask_ref[nl] * _nt(q.astype(BF16), expand_k(k))
    for lv in range(nl):
        sc = jnp.exp(-jnp.abs(r[(lv + 1) * c:(lv + 2) * c]))
        a = a + mask_ref[lv] * _nt((q * sc).astype(BF16), expand_k(k * sc))
    vexp = jnp.concatenate([jnp.where(lane_v == h, v, 0.0) for h in range(GLA_HEADS)], axis=0).astype(BF16)
    o = o + _dot(a.astype(BF16), vexp)

    blast = b[c - 1:c]
    kt = (k * jnp.exp(blast - b)).astype(BF16)
    vt = _nt(eye_ref[...], v.astype(BF16)).astype(BF16)
    upd = _dot(vt, kt)
    row_h = lax.broadcasted_iota(I32, (vw, 1), 0) // GLA_DV
    st_new = st * jnp.exp(blast) + jnp.where(row_h == lane_k, upd, 0.0)
    st_ref[...] = st_new

    gate = gate_ref[0]
    gn = gn_ref[...]
    for h in range(GLA_HEADS):
        hs = slice(h * GLA_DV, (h + 1) * GLA_DV)
        oh = o[:, hs]
        y = oh * lax.rsqrt(jnp.mean(oh * oh, axis=-1, keepdims=True) + EPS) * gn
        gh = gate[:, hs]
        o_ref[0, :, hs] = (y * (gh * jax.nn.sigmoid(gh))).astype(BF16)

    @pl.when(t == pl.num_programs(1) - 1)
    def _():
        s_t = st_new.T
        for h in range(GLA_HEADS):
            sf_ref[0, h] = s_t[h * GLA_DK:(h + 1) * GLA_DK, h * GLA_DV:(h + 1) * GLA_DV]


def gla(z3, g3, s0, gla_norm):
    nb, l, _ = z3.shape
    c = min(l, GLA_CHUNK)
    assert l % c == 0 and c & (c - 1) == 0
    kw = GLA_HEADS * GLA_DK
    vw = GLA_HEADS * GLA_DV
    mstack, mask, nl = _gla_consts(c)
    eye = jnp.eye(vw, dtype=BF16)
    o, sf = pl.pallas_call(
        functools.partial(_gla_kernel, c=c, nl=nl),
        grid=(nb, l // c),
        in_specs=[pl.BlockSpec((1, c, 2 * kw), lambda n, t: (n, t, 0)),
                  pl.BlockSpec((1, c, vw), lambda n, t: (n, t, CD_COLS["cv"] // vw)),
                  pl.BlockSpec((1, c, vw), lambda n, t: (n, t, CD_COLS["cgate"] // vw)),
                  pl.BlockSpec((1, c, kw), lambda n, t: (n, t, 0)),
                  pl.BlockSpec((1, GLA_HEADS, GLA_DK, GLA_DV), lambda n, t: (n, 0, 0, 0)),
                  pl.BlockSpec(mstack.shape, lambda n, t: (0, 0)),
                  pl.BlockSpec(mask.shape, lambda n, t: (0, 0, 0)),
                  pl.BlockSpec((1, GLA_DV), lambda n, t: (0, 0)),
                  pl.BlockSpec((vw, vw), lambda n, t: (0, 0))],
        out_specs=[pl.BlockSpec((1, c, vw), lambda n, t: (n, t, 0)),
                   pl.BlockSpec((1, GLA_HEADS, GLA_DK, GLA_DV), lambda n, t: (n, 0, 0, 0))],
        out_shape=[jax.ShapeDtypeStruct((nb, l, vw), BF16),
                   jax.ShapeDtypeStruct((nb, GLA_HEADS, GLA_DK, GLA_DV), F32)],
        scratch_shapes=[pltpu.VMEM((vw, kw), F32)],
        compiler_params=_cparams("parallel", "arbitrary"),
        name="gla",
    )(z3, z3, z3, g3, s0, mstack, mask, gla_norm.reshape(1, GLA_DV), eye)
    return o, sf


def _sort_key(x):
    bits = pltpu.bitcast(x, I32)
    return jnp.where(bits < 0, (bits ^ 0x7FFFFFFF) + 1, bits)


def _idx_lhs(iq, h):
    hi, lo = _split2(iq[:, h * IDX_DIM:(h + 1) * IDX_DIM])
    return jnp.concatenate([hi, hi, lo, jnp.zeros_like(hi)], axis=1)


def _idx_rhs(ik):
    hi, lo = _split2(ik)
    return jnp.concatenate([hi, lo, hi, jnp.zeros_like(hi)], axis=1)


def _idx_scores(lhs, rhs, wcol):
    sc = None
    for h in range(IDX_HEADS):
        d = jnp.maximum(_nt(lhs[h], rhs) * IDX_DIM ** -0.5, 0.0) * (wcol[h] * IDX_HEADS ** -0.5)
        sc = d if sc is None else sc + d
    return sc


def _dsa_select_kernel(iq_ref, smt_ref, smk_ref, tril_ref, mask_ref, kb3_ref, hi_ref, d1_ref, d0_ref, *, nc, topk):
    i = pl.program_id(1)
    ck = DSA_CK
    qb = DSA_SQ
    bpc = ck // qb
    sub = ck // 8

    @pl.when(i == 0)
    def _():
        for c in range(nc):
            kb3_ref[c] = _idx_rhs(smk_ref[0, c * ck:(c + 1) * ck, SM_IK:SM_IK + IDX_DIM])

    iq = iq_ref[0]
    smt = smt_ref[0]
    lhs = jnp.concatenate([_idx_lhs(iq, h) for h in range(IDX_HEADS)], axis=0)
    wrow = [smt[SM_IW + h:SM_IW + h + 1, :] * (IDX_HEADS ** -0.5 * IDX_DIM ** -0.5) for h in range(IDX_HEADS)]
    nch = (i + bpc) // bpc

    def chunk_planes(c, causal):
        dots = _nt(kb3_ref[c], lhs)
        sc = None
        for h in range(IDX_HEADS):
            t = jnp.maximum(dots[:, h * qb:(h + 1) * qb], 0.0) * wrow[h]
            sc = t if sc is None else sc + t
        sc = jnp.where(jnp.abs(sc) < F32_MIN_NORMAL, 0.0, sc)
        bits = pltpu.bitcast(sc, I32)
        key = bits ^ (lax.shift_right_arithmetic(bits, 31) & 0x7FFFFFFF)
        hi = pltpu.bitcast(bits & -65536, F32)
        if causal:
            drc = lax.broadcasted_iota(I32, (ck, qb), 0) - lax.broadcasted_iota(I32, (ck, qb), 1)
            hi = jnp.where(drc <= i * qb - c * ck, hi, -jnp.inf)
        hi_ref[c] = hi.astype(BF16)
        d1_ref[c] = (lax.shift_right_logical(key, 8) & 0xFF).astype(F32).astype(BF16)
        d0_ref[c] = (key & 0xFF).astype(F32).astype(BF16)

    def score_body(c, carry):
        chunk_planes(c, False)
        return carry

    lax.fori_loop(0, nch - 1, score_body, 0)
    chunk_planes(nch - 1, True)

    one, zero = jnp.ones((), BF16), jnp.zeros((), BF16)

    def count(ref, cand, strict):
        def body(c, acc):
            blk = ref[c]
            m = jnp.where(blk > cand if strict else blk >= cand, one, zero)
            for j in range(8):
                acc = acc + m[j * sub:(j + 1) * sub, :]
            return acc
        acc = lax.fori_loop(0, nch, body, jnp.zeros((sub, qb), BF16))
        return jnp.sum(acc.astype(F32), axis=0, keepdims=True)

    qpos = i * qb + lax.broadcasted_iota(I32, (1, qb), 1)
    kk = jnp.minimum(topk, qpos + 1).astype(F32)

    def hi_value(s16):
        p = jnp.where(s16 >= 0, s16, s16 ^ 0x7FFF) & 0xFFFF
        return pltpu.bitcast(lax.shift_left(p, 16), F32).astype(BF16)

    def descend(ref, nbits, start, as_cand, want, c_start):
        def step(t, carry):
            thr, cge = carry
            cand = thr + lax.shift_left(jnp.int32(1), nbits - 1 - t)
            cnt = count(ref, as_cand(cand), False)
            ok = cnt >= want
            return jnp.where(ok, cand, thr), jnp.where(ok, cnt, cge)
        return lax.fori_loop(0, nbits, step, (jnp.full((1, qb), start, I32), c_start))

    digit = lambda t: t.astype(F32).astype(BF16)

    h16, cge1 = descend(hi_ref, 16, -(2 ** 15), hi_value, kk, jnp.zeros((1, qb), F32))
    hv = hi_value(h16)
    cgt1 = count(hi_ref, hv, True)
    need1 = kk - cgt1

    def restrict(dst_ref, cls_ref, cls_val):
        def body(c, carry):
            dst_ref[c] = jnp.where(cls_ref[c] == cls_val, dst_ref[c], -one)
            return carry
        lax.fori_loop(0, nch, body, 0)

    restrict(d1_ref, hi_ref, hv)
    b1, cge2 = descend(d1_ref, 8, 0, digit, need1, cge1 - cgt1)
    b1v = digit(b1)
    cgt2 = count(d1_ref, b1v, True)
    need2 = need1 - cgt2
    restrict(d0_ref, d1_ref, b1v)
    b0, cge = descend(d0_ref, 8, 0, digit, need2, cge2 - cgt2)
    b0v = digit(b0)
    tied = jnp.max(jnp.where(cge > need2, 1, 0))

    @pl.when(tied > 0)
    def _():
        need3 = need2 - count(d0_ref, b0v, True)
        b0f = b0.astype(F32)

        def tie_body(c, off):
            blk = d0_ref[c].astype(F32)
            eq = blk == b0f
            pref = _dot(tril_ref[...], jnp.where(eq, 1.0, 0.0).astype(BF16))
            drop = jnp.logical_and(eq, pref + off > need3)
            d0_ref[c] = jnp.where(drop, -1.0, blk).astype(BF16)
            return off + pref[ck - 1:ck, :]

        lax.fori_loop(0, nch, tie_body, jnp.zeros((1, qb), F32))

    neg = jnp.full((), NEG, BF16)

    def live_body(c, carry):
        keep3 = jnp.where(d0_ref[c] >= b0v, zero, neg)
        keep2 = jnp.where(d1_ref[c] > b1v, zero, keep3)
        mask_ref[0, 0, c] = jnp.where(hi_ref[c] > hv, zero, keep2)
        return carry

    def dead_body(c, carry):
        mask_ref[0, 0, c] = jnp.full((ck, qb), NEG, BF16)
        return carry

    lax.fori_loop(0, nch, live_body, 0)
    lax.fori_loop(nch, nc, dead_body, 0)


def dsa_select(z3, smt):
    b, s, _ = z3.shape
    nq, nc = s // DSA_SQ, s // DSA_CK
    topk = min(DSA_TOPK_MAX, s // 4)
    tril = jnp.asarray(np.tril(np.ones((DSA_CK, DSA_CK), np.float32)), BF16)
    return pl.pallas_call(
        functools.partial(_dsa_select_kernel, nc=nc, topk=topk),
        grid=(b, nq),
        in_specs=[pl.BlockSpec((1, DSA_SQ, IDX_HEADS * IDX_DIM), lambda n, i: (n, i, CD_COLS["iq"] // (IDX_HEADS * IDX_DIM))),
                  pl.BlockSpec((1, LANES, DSA_SQ), lambda n, i: (n, 0, i)),
                  pl.BlockSpec((1, s, LANES), lambda n, i: (n, 0, SMALL_BLOCK)),
                  pl.BlockSpec((DSA_CK, DSA_CK), lambda n, i: (0, 0))],
        out_specs=pl.BlockSpec((1, 1, nc, DSA_CK, DSA_SQ), lambda n, i: (n, i, 0, 0, 0)),
        out_shape=jax.ShapeDtypeStruct((b, nq, nc, DSA_CK, DSA_SQ), BF16),
        scratch_shapes=[pltpu.VMEM((nc, DSA_CK, 4 * IDX_DIM), BF16)] + [pltpu.VMEM((nc, DSA_CK, DSA_SQ), BF16)] * 3,
        compiler_params=_cparams("parallel", "arbitrary"),
        name="dsa_select",
    )(z3, smt, z3, tril)


def _dsa_bias_tiles(tab_d):
    o = 0
    while _rel_bucket_np(max(o * DSA_BT - (DSA_BT - 1), 0)) < REL_BUCKETS - 1:
        o += 1
    offs = np.arange(o + 1)[:, None, None] * DSA_BT
    d = offs + np.arange(DSA_BT)[None, None, :] - np.arange(DSA_BT)[None, :, None]
    tiles = bias_lookup(tab_d, _rel_bucket_np(d).reshape((o + 1) * DSA_BT, DSA_BT))
    return tiles.reshape(DSA_HEADS, o + 1, DSA_BT, DSA_BT)


def _dsa_attn_kernel(qi_ref, kc_ref, q_ref, k_ref, vt_ref, mask_ref, bt_ref, o_ref, m_ref, l_ref, acc_ref, s_ref,
                     *, n_off):
    s_id = pl.program_id(1)
    i = qi_ref[s_id]
    c = kc_ref[s_id]
    qb, ck = DSA_QB, DSA_CK
    tk, tq = ck // DSA_BT, qb // DSA_BT

    @pl.when(c == 0)
    def _():
        m_ref[...] = jnp.full(m_ref.shape, NEG, F32)
        l_ref[...] = jnp.zeros(l_ref.shape, F32)
        acc_ref[...] = jnp.zeros(acc_ref.shape, F32)

    madd = mask_ref[0, 0, 0].astype(F32)
    q = q_ref[0]
    k = k_ref[0]
    vt = vt_ref[0]
    offs = [[jnp.clip((i * tq + u) - (c * tk + t), 0, n_off - 1) for u in range(tq)] for t in range(tk)]
    m_all = m_ref[...]
    l_all = l_ref[...]
    m_rows, l_rows = [], []
    for h in range(DSA_HEADS):
        hs = slice(h * HEAD_DIM, (h + 1) * HEAD_DIM)
        bias = jnp.concatenate([jnp.concatenate([bt_ref[h, offs[t][u]] for u in range(tq)], axis=1)
                                for t in range(tk)], axis=0)
        s = _nt(k[:, hs], q[:, hs]) + bias + madd
        s_ref[h] = s
        m_rows.append(jnp.maximum(m_all[h:h + 1, :], jnp.max(s, axis=0, keepdims=True)))
    ones = jnp.ones((16, ck), BF16)
    for h in range(DSA_HEADS):
        hs = slice(h * HEAD_DIM, (h + 1) * HEAD_DIM)
        alpha = jnp.exp2(m_all[h:h + 1, :] - m_rows[h])
        p = jnp.exp2(s_ref[h] - m_rows[h]).astype(BF16)
        pv = _dot(jnp.concatenate([vt[hs, :], ones], axis=0), p)
        l_rows.append(alpha * l_all[h:h + 1, :] + pv[HEAD_DIM:HEAD_DIM + 1, :])
        acc_ref[hs, :] = alpha * acc_ref[hs, :] + pv[0:HEAD_DIM, :]
    m_ref[...] = jnp.concatenate(m_rows, axis=0)
    l_ref[...] = jnp.concatenate(l_rows, axis=0)

    @pl.when(c == ((i + 1) * qb - 1) // ck)
    def _():
        inv = 1.0 / l_ref[...]
        ot = jnp.concatenate([acc_ref[h * HEAD_DIM:(h + 1) * HEAD_DIM, :] * inv[h:h + 1, :]
                              for h in range(DSA_HEADS)], axis=0)
        o_ref[0] = ot.T.astype(BF16)


def dsa_attend(dq, kb, vt, mask, tab_d):
    b, s, w = dq.shape
    nq = s // DSA_QB
    last_c = lambda i: ((i + 1) * DSA_QB - 1) // DSA_CK
    qps = DSA_SQ // DSA_QB
    bt = _dsa_bias_tiles(tab_d) * LOG2E
    n_off = bt.shape[1]
    qi = np.concatenate([np.full(last_c(i) + 1, i) for i in range(nq)]).astype(np.int32)
    kc = np.concatenate([np.arange(last_c(i) + 1) for i in range(nq)]).astype(np.int32)
    grid_spec = pltpu.PrefetchScalarGridSpec(
        num_scalar_prefetch=2,
        grid=(b, len(qi)),
        in_specs=[pl.BlockSpec((1, DSA_QB, w), lambda n, t, qi_, kc_: (n, qi_[t], 0)),
                  pl.BlockSpec((1, DSA_CK, w), lambda n, t, qi_, kc_: (n, kc_[t], 0)),
                  pl.BlockSpec((1, w, DSA_CK), lambda n, t, qi_, kc_: (n, 0, kc_[t])),
                  pl.BlockSpec((1, 1, 1, DSA_CK, DSA_QB),
                               lambda n, t, qi_, kc_: (n, qi_[t] // qps, kc_[t], 0, qi_[t] % qps)),
                  pl.BlockSpec(bt.shape, lambda n, t, qi_, kc_: (0, 0, 0, 0))],
        out_specs=pl.BlockSpec((1, DSA_QB, w), lambda n, t, qi_, kc_: (n, qi_[t], 0)),
        scratch_shapes=[pltpu.VMEM((DSA_HEADS, DSA_QB), F32), pltpu.VMEM((DSA_HEADS, DSA_QB), F32),
                        pltpu.VMEM((w, DSA_QB), F32), pltpu.VMEM((DSA_HEADS, DSA_CK, DSA_QB), F32)],
    )
    return pl.pallas_call(
        functools.partial(_dsa_attn_kernel, n_off=n_off),
        grid_spec=grid_spec,
        out_shape=jax.ShapeDtypeStruct((b, s, w), BF16),
        compiler_params=_cparams("parallel", "arbitrary"),
        name="dsa_attend",
    )(jnp.asarray(qi), jnp.asarray(kc), dq, kb, vt, mask, bt)


SCORE_PAGES = 32
ATTN_PAGES = 16


def _dsa_step_scores_kernel(pt_ref, iq_ref, sm_ref, *rest, npg):
    k_refs, o_ref = rest[:npg], rest[npg]
    iq = iq_ref[0]
    sm = sm_ref[0]
    lhs = jnp.concatenate([_idx_lhs(iq, h) for h in range(IDX_HEADS)], axis=0)
    wcol = [sm[:, SM_IW + h:SM_IW + h + 1] * IDX_HEADS ** -0.5 for h in range(IDX_HEADS)]
    for j in range(npg):
        hi, lo = _split2(k_refs[j][0, 0])
        d = _dot(lhs, jnp.concatenate([hi, lo, hi, jnp.zeros_like(hi)], axis=0))
        sc = None
        for h in range(IDX_HEADS):
            t = jnp.maximum(d[h * ROWS8:(h + 1) * ROWS8] * IDX_DIM ** -0.5, 0.0) * wcol[h]
            sc = t if sc is None else sc + t
        o_ref[0, :, j * PAGE_SIZE:(j + 1) * PAGE_SIZE] = sc


def dsa_step_scores(z8, kidx_t, layer, page_table):
    nb = z8.shape[0]
    n_pages = page_table.shape[1]
    npg = math.gcd(SCORE_PAGES, n_pages)
    iqw = IDX_HEADS * IDX_DIM

    def page_spec(j):
        return pl.BlockSpec((1, 1, IDX_DIM, PAGE_SIZE),
                            lambda n, p, pt: (layer, pt[n * n_pages + p * npg + j], 0, 0))

    grid_spec = pltpu.PrefetchScalarGridSpec(
        num_scalar_prefetch=1,
        grid=(nb, n_pages // npg),
        in_specs=[pl.BlockSpec((1, ROWS8, iqw), lambda n, p, pt: (n, 0, CD_COLS["iq"] // iqw)),
                  pl.BlockSpec((1, ROWS8, LANES), lambda n, p, pt: (n, 0, SMALL_BLOCK))]
                 + [page_spec(j) for j in range(npg)],
        out_specs=pl.BlockSpec((1, ROWS8, npg * PAGE_SIZE), lambda n, p, pt: (n, 0, p)),
    )
    return pl.pallas_call(
        functools.partial(_dsa_step_scores_kernel, npg=npg),
        grid_spec=grid_spec,
        out_shape=jax.ShapeDtypeStruct((nb, ROWS8, n_pages * PAGE_SIZE), F32),
        compiler_params=_cparams("parallel", "arbitrary"),
        name="dsa_step_scores",
    )(page_table.reshape(-1), z8, z8, *([kidx_t] * npg))


STEP_SELECT_SEQS = 8


def _dsa_step_select_kernel(sc_ref, iq_ref, sm_ref, triu_ref, mp_ref, mn_ref, key_ref, *, past, l_new, topk, g):
    ck = DSA_CK
    nck = past // ck
    rows = g * ROWS8
    new_keys = []
    for j in range(g):
        iq = iq_ref[j]
        sm = sm_ref[j]
        lhs = [_idx_lhs(iq, h) for h in range(IDX_HEADS)]
        wcol = [sm[:, SM_IW + h:SM_IW + h + 1] for h in range(IDX_HEADS)]
        rhs_new = _idx_rhs(jnp.concatenate([sm[:, SM_IK:SM_IK + IDX_DIM],
                                            jnp.zeros((LANES - ROWS8, IDX_DIM), F32)], axis=0))
        new_keys.append(_sort_key(_idx_scores(lhs, rhs_new, wcol)))
    rloc = lax.broadcasted_iota(I32, (rows, LANES), 0) % ROWS8
    col = lax.broadcasted_iota(I32, (rows, LANES), 1)
    key_new = jnp.where(jnp.logical_and(col <= rloc, col < l_new), jnp.concatenate(new_keys, axis=0), INT_MIN)
    key_ref[...] = _sort_key(sc_ref[...].reshape(rows, past))
    kk = jnp.minimum(topk, past + 1 + lax.broadcasted_iota(I32, (rows, 1), 0) % ROWS8)

    def count(pred):
        return (jnp.sum(jnp.where(pred(key_ref[...]), 1, 0), axis=1, keepdims=True)
                + jnp.sum(jnp.where(pred(key_new), 1, 0), axis=1, keepdims=True))

    def bit_body(t, carry):
        thr, cge = carry
        cand = thr + lax.shift_left(jnp.int32(1), 31 - t)
        cnt = count(lambda x: x >= cand)
        ok = cnt >= kk
        return jnp.where(ok, cand, thr), jnp.where(ok, cnt, cge)

    thr, cge = lax.fori_loop(0, 32, bit_body, (jnp.full((rows, 1), INT_MIN, I32), jnp.zeros((rows, 1), I32)))
    cgt = count(lambda x: x > thr)
    need = kk - cgt
    needf = need.astype(F32)
    tied = jnp.max(jnp.where(need < cge - cgt, 1, 0))
    mn_ref[...] = jnp.where(key_new >= thr, 0.0, NEG).reshape(g, ROWS8, LANES)

    @pl.when(tied == 0)
    def _():
        mp_ref[...] = jnp.where(key_ref[...] >= thr, 0.0, NEG).reshape(g, ROWS8, past)

    @pl.when(tied > 0)
    def _():
        off = jnp.zeros((rows, 1), F32)
        triu = triu_ref[...]
        for c in range(nck):
            blk = key_ref[:, c * ck:(c + 1) * ck]
            eq = blk == thr
            pref = _dot(jnp.where(eq, 1.0, 0.0).astype(BF16), triu)
            keep = jnp.logical_or(blk > thr, jnp.logical_and(eq, pref + off <= needf))
            mp_ref[:, :, c * ck:(c + 1) * ck] = jnp.where(keep, 0.0, NEG).reshape(g, ROWS8, ck)
            off = off + pref[:, ck - 1:ck]
        eq = key_new == thr
        pref = _dot(jnp.where(eq, 1.0, 0.0).astype(BF16), triu[0:LANES, 0:LANES])
        keep = jnp.logical_or(key_new > thr, jnp.logical_and(eq, pref + off <= needf))
        mn_ref[...] = jnp.where(keep, 0.0, NEG).reshape(g, ROWS8, LANES)


def dsa_step_select(scores, z8, l_new):
    nb, _, past = scores.shape
    topk = min(DSA_TOPK_MAX, (past + l_new) // 4)
    iqw = IDX_HEADS * IDX_DIM
    g = math.gcd(STEP_SELECT_SEQS, nb)
    assert past % DSA_CK == 0
    triu = jnp.asarray(np.triu(np.ones((DSA_CK, DSA_CK), np.float32)), BF16)
    return pl.pallas_call(
        functools.partial(_dsa_step_select_kernel, past=past, l_new=l_new, topk=topk, g=g),
        grid=(nb // g,),
        in_specs=[pl.BlockSpec((g, ROWS8, past), lambda n: (n, 0, 0)),
                  pl.BlockSpec((g, ROWS8, iqw), lambda n: (n, 0, CD_COLS["iq"] // iqw)),
                  pl.BlockSpec((g, ROWS8, LANES), lambda n: (n, 0, SMALL_BLOCK)),
                  pl.BlockSpec((DSA_CK, DSA_CK), lambda n: (0, 0))],
        out_specs=[pl.BlockSpec((g, ROWS8, past), lambda n: (n, 0, 0)),
                   pl.BlockSpec((g, ROWS8, LANES), lambda n: (n, 0, 0))],
        out_shape=[jax.ShapeDtypeStruct((nb, ROWS8, past), F32),
                   jax.ShapeDtypeStruct((nb, ROWS8, LANES), F32)],
        scratch_shapes=[pltpu.VMEM((g * ROWS8, past), I32)],
        compiler_params=_cparams("parallel"),
        name="dsa_step_select",
    )(scores, z8, z8, triu)


def _dsa_step_attn_kernel(pt_ref, q_ref, kvn_ref, mp_ref, mn_ref, bp_ref, bn_ref, *rest, w, npg):
    kv_refs, o_ref = rest[:npg], rest[npg]
    qbd_ref, newpage_ref, m_ref, l_ref, acc_ref = rest[npg + 1:]
    n = pl.program_id(0)
    p = pl.program_id(1)
    rows = DSA_HEADS * ROWS8
    lane_h = lax.broadcasted_iota(I32, (ROWS8, w), 1) // HEAD_DIM

    @pl.when(jnp.logical_and(n == 0, p == 0))
    def _():
        newpage_ref[...] = jnp.zeros(newpage_ref.shape, F32)

    @pl.when(p == 0)
    def _():
        q = q_ref[0]
        for h in range(DSA_HEADS):
            qbd_ref[h * ROWS8:(h + 1) * ROWS8, :] = jnp.where(lane_h == h, q, jnp.zeros_like(q))
        m_ref[...] = jnp.full(m_ref.shape, NEG, F32)
        l_ref[...] = jnp.zeros(l_ref.shape, F32)
        acc_ref[...] = jnp.zeros(acc_ref.shape, F32)

    def accumulate(scores, madd8, bias, pv):
        s = scores + bias + jnp.concatenate([madd8] * DSA_HEADS, axis=0)
        m_old = m_ref[:, 0:1]
        m_new = jnp.maximum(m_old, jnp.max(s, axis=-1, keepdims=True))
        alpha = jnp.exp(m_old - m_new)
        pr = jnp.exp(s - m_new)
        l_ref[...] = jnp.broadcast_to(alpha * l_ref[:, 0:1] + jnp.sum(pr, axis=-1, keepdims=True), (rows, LANES))
        m_ref[...] = jnp.broadcast_to(m_new, (rows, LANES))
        acc_ref[...] = alpha * acc_ref[...] + pv(pr.astype(BF16))

    qbd = qbd_ref[...]
    scores = jnp.concatenate([_dot(qbd, kv_refs[j][0, 0, 0].reshape(w, PAGE_SIZE).astype(BF16))
                              for j in range(npg)], axis=1)

    def pv_pages(pr):
        out = None
        for j in range(npg):
            t = _nt(pr[:, j * PAGE_SIZE:(j + 1) * PAGE_SIZE], kv_refs[j][0, 0, 1].reshape(w, PAGE_SIZE).astype(BF16))
            out = t if out is None else out + t
        return out

    accumulate(scores, mp_ref[0], bp_ref[...], pv_pages)

    @pl.when(p == pl.num_programs(1) - 1)
    def _():
        newpage_ref[0:ROWS8, :] = kvn_ref[0]
        newp = newpage_ref[...]
        accumulate(_nt(qbd, newp[:, 0:w].astype(BF16)), mn_ref[0], bn_ref[...],
                   lambda pr: _dot(pr, newp[:, w:2 * w].astype(BF16)))
        for h in range(DSA_HEADS):
            rs = slice(h * ROWS8, (h + 1) * ROWS8)
            hs = slice(h * HEAD_DIM, (h + 1) * HEAD_DIM)
            o_ref[0, :, hs] = (acc_ref[rs, hs] / l_ref[rs, 0:1]).astype(BF16)


def _dsa_step_bias(tab_d, past, l_new):
    lq = np.minimum(np.arange(ROWS8), l_new - 1)[:, None]
    d_past = past + lq - np.arange(past)[None, :]
    d_new = lq - np.arange(LANES)[None, :]

    def table(d):
        return bias_lookup(tab_d, _rel_bucket_np(d)).reshape(DSA_HEADS * ROWS8, d.shape[1])

    return table(d_past), table(d_new)


def dsa_step_attend(dq8, kv_new8, kv_t, layer, page_table, mask_past, mask_new, tab_d, l_new):
    nb, _, w = dq8.shape
    n_pages = page_table.shape[1]
    npg = math.gcd(ATTN_PAGES, n_pages)
    past = n_pages * PAGE_SIZE
    rows = DSA_HEADS * ROWS8
    bp, bn = _dsa_step_bias(tab_d, past, l_new)

    def page_spec(j):
        return pl.BlockSpec((1, 1, 2, DSA_HEADS, HEAD_DIM, PAGE_SIZE),
                            lambda n, p, pt: (layer, pt[n * n_pages + p * npg + j], 0, 0, 0, 0))

    grid_spec = pltpu.PrefetchScalarGridSpec(
        num_scalar_prefetch=1,
        grid=(nb, n_pages // npg),
        in_specs=[pl.BlockSpec((1, ROWS8, w), lambda n, p, pt: (n, 0, 0)),
                  pl.BlockSpec((1, ROWS8, 2 * w), lambda n, p, pt: (n, 0, 0)),
                  pl.BlockSpec((1, ROWS8, npg * PAGE_SIZE), lambda n, p, pt: (n, 0, p)),
                  pl.BlockSpec((1, ROWS8, LANES), lambda n, p, pt: (n, 0, 0)),
                  pl.BlockSpec((rows, npg * PAGE_SIZE), lambda n, p, pt: (0, p)),
                  pl.BlockSpec((rows, LANES), lambda n, p, pt: (0, 0))]
                 + [page_spec(j) for j in range(npg)],
        out_specs=pl.BlockSpec((1, ROWS8, w), lambda n, p, pt: (n, 0, 0)),
        scratch_shapes=[pltpu.VMEM((rows, w), BF16), pltpu.VMEM((PAGE_SIZE, 2 * w), F32),
                        pltpu.VMEM((rows, LANES), F32), pltpu.VMEM((rows, LANES), F32), pltpu.VMEM((rows, w), F32)],
    )
    return pl.pallas_call(
        functools.partial(_dsa_step_attn_kernel, w=w, npg=npg),
        grid_spec=grid_spec,
        out_shape=jax.ShapeDtypeStruct((nb, ROWS8, w), BF16),
        compiler_params=_cparams("arbitrary", "arbitrary"),
        name="dsa_step_attend",
    )(page_table.reshape(-1), dq8, kv_new8, mask_past, mask_new, bp, bn, *([kv_t] * npg))


def _pad_rows(x3, rows):
    return jnp.pad(x3, ((0, 0), (0, rows - x3.shape[1]), (0, 0)))


def _trunk(x, is_step, conv_state, win_states, gla_state, dsa_kv, dsa_kidx, page_table, wts):
    (norm_mix, norm_ffn, w_in_ab, conv_w, conv_b, conv_ln_g, conv_ln_b, qn_ab, kn_ab, w_out_ab,
     w_in_cd, gla_wa2, gla_ba, gla_norm, qn_cd, kn_cd, w_out_cd, rel_bias, w_g, w_u, w_d) = wts
    nb, l, d = x.shape
    t = nb * l
    depth = norm_mix.shape[0]
    c = conv_w.shape[2]
    wq = B_HPG * HEAD_DIM
    x2 = x.reshape(t, d)
    conv_new, gla_new, kv_new, kidx_new = [], [], [], []
    win_new = [[] for _ in WINDOWS]
    for layer in range(depth):
        i = layer // 2
        if layer % 2 == 0:
            z = norm_matmul(x2, norm_mix[layer], w_in_ab[i])
            hist = conv_state[i] if is_step else jnp.zeros((nb, CONV_WIDTH - 1, c), F32)
            a_out, c_st = conv_module(z.reshape(nb, l, -1), hist, conv_w[i], conv_b[i], conv_ln_g[i], conv_ln_b[i])
            conv_new.append(c_st)
            if is_step:
                qs, kvs = ab_qkv(z, qn_ab[i], kn_ab[i], 2 * c, t, (1,) * len(WINDOWS))
            else:
                qs, kvs = ab_qkv(z, qn_ab[i], kn_ab[i], 2 * c, l, DILATIONS)
            outs, lses = [], []
            for g, window in enumerate(WINDOWS):
                tab_g = rel_bias[:, g * B_HPG:(g + 1) * B_HPG]
                if is_step:
                    o, lse, st = win_step(_pad_rows(qs[g].reshape(nb, l, wq), ROWS8),
                                          kvs[g].reshape(nb, l, 2 * wq), win_states[g], i, g, tab_g)
                else:
                    o, lse = win_prompt(qs[g], kvs[g], g, tab_g)
                    keep = min(window, l)
                    st = kvs[g][:, -(keep // DILATIONS[g]):].reshape(nb, keep, 2, B_HPG, HEAD_DIM)
                outs.append(o.reshape(1, t, wq) if is_step else o)
                lses.append(lse.reshape(1, t, wq) if is_step else lse)
                win_new[g].append(st)
            m2 = ab_merge(outs, lses, t, (1,) * len(WINDOWS)) if is_step else ab_merge(outs, lses, l, DILATIONS)
            m1 = a_out.reshape(t, c)
            wo = w_out_ab[i]
        else:
            z = norm_matmul(x2, norm_mix[layer], w_in_cd[i])
            pre = cd_pre(z, gla_wa2[i], gla_ba[i], qn_cd[i], kn_cd[i], None if is_step else l)
            gdec, dq, kv = pre[:3]
            z3 = z.reshape(nb, l, -1)
            wd = DSA_HEADS * HEAD_DIM
            ik = z3[:, :, CD_COLS["ik"]:CD_COLS["ik"] + IDX_DIM]
            tab_d = rel_bias[:, B_HEADS:]
            if is_step:
                lp = GLA_STEP_ROWS
                o_c, s_c = gla(_pad_rows(z3, lp), _pad_rows(gdec.reshape(nb, l, -1), lp), gla_state[i], gla_norm[i])
                o_c = o_c[:, :l]
                z8 = _pad_rows(z3, ROWS8)
                scores = dsa_step_scores(z8, dsa_kidx, i, page_table)
                mask_p, mask_n = dsa_step_select(scores, z8, l)
                o_d = dsa_step_attend(_pad_rows(dq.reshape(nb, l, wd), ROWS8), _pad_rows(kv.reshape(nb, l, 2 * wd), ROWS8),
                                      dsa_kv, i, page_table, mask_p, mask_n, tab_d, l)[:, :l]
                kv_st = kv.reshape(nb, l, 2, DSA_HEADS, HEAD_DIM)
                ki_st = ik
            else:
                s0 = jnp.zeros((nb, GLA_HEADS, GLA_DK, GLA_DV), F32)
                o_c, s_c = gla(z3, gdec.reshape(nb, l, -1), s0, gla_norm[i])
                kb, vt, smt = pre[3:]
                mask = dsa_select(z3, smt)
                o_d = dsa_attend(dq.reshape(nb, l, wd), kb.reshape(nb, l, wd), vt, mask, tab_d)
                n_pg = l // PAGE_SIZE
                kv_st = kv.reshape(nb, n_pg, PAGE_SIZE, 2, DSA_HEADS, HEAD_DIM)
                ki_st = ik.reshape(nb, n_pg, PAGE_SIZE, IDX_DIM)
            gla_new.append(s_c)
            kv_new.append(kv_st)
            kidx_new.append(ki_st)
            m1, m2 = o_c.reshape(t, -1), o_d.reshape(t, -1)
            wo = w_out_cd[i]
        d1 = m1.shape[1]
        x2 = mix_ffn(x2, m1, m2, wo[:d1], wo[d1:], norm_ffn[layer], w_g[layer], w_u[layer], w_d[layer])
    wins = [jnp.stack(ws) for ws in win_new]
    if is_step:
        wins = [ws.transpose(0, 1, 5, 2, 3, 4) for ws in wins]
    states = (jnp.stack(conv_new), wins[0], wins[1], wins[2],
              jnp.stack(gla_new), jnp.stack(kv_new), jnp.stack(kidx_new))
    return x2.reshape(nb, l, d), states


def kernel(x_prompt, x_sample, state_conv, cache_win128, cache_win512, cache_win2048, state_gla, cache_dsa_kv, cache_dsa_kidx, page_table, norm_mix, norm_ffn, w_in_ab, conv_w, conv_b, conv_ln_g, conv_ln_b, qn_ab, kn_ab, w_out_ab, w_in_cd, gla_wa2, gla_ba, gla_norm, qn_cd, kn_cd, w_out_cd, rel_bias, w_ffn_gate, w_ffn_up, w_ffn_down):
    bf = lambda a: a.astype(BF16)
    w_in_cd_r = jnp.stack([cd_reorder_w(w_in_cd[i]) for i in range(w_in_cd.shape[0])])
    wts = (norm_mix, norm_ffn, bf(w_in_ab), conv_w, conv_b, conv_ln_g, conv_ln_b, qn_ab, kn_ab, bf(w_out_ab),
           bf(w_in_cd_r), gla_wa2, gla_ba, gla_norm, qn_cd, kn_cd, bf(w_out_cd), rel_bias,
           bf(w_ffn_gate), bf(w_ffn_up), bf(w_ffn_down))
    y_p, sp = _trunk(x_prompt, False, None, None, None, None, None, None, wts)
    wins_t = tuple(cw.transpose(0, 1, 3, 4, 5, 2) for cw in (cache_win128, cache_win512, cache_win2048))
    y_s, ss = _trunk(x_sample, True, state_conv, wins_t, state_gla,
                     cache_dsa_kv.transpose(0, 1, 3, 4, 5, 2), cache_dsa_kidx.transpose(0, 1, 3, 2), page_table, wts)
    conv_p, win128_p, win512_p, win2048_p, gla_p, dsa_kv_p, dsa_kidx_p = sp
    conv_s, win128_s, win512_s, win2048_s, gla_s, dsa_kv_s, dsa_kidx_s = ss
    return (y_p, y_s, conv_p, conv_s, win128_p, win128_s, win512_p, win512_s, win2048_p, win2048_s,
            gla_p, gla_s, dsa_kv_p, dsa_kv_s, dsa_kidx_p, dsa_kidx_s)
```
